```python
import jax, jax.numpy as jnp
from jax import lax
import numpy as np

D_MODEL = 1024
BATCH = 4
SEQ = 4096
DEPTH = 1

N_HEADS = 16
HEAD_DIM = 64
N_KV = 4
HPG = N_HEADS // N_KV
ATTN_DIM = N_HEADS * HEAD_DIM
KV_DIM = N_KV * HEAD_DIM
CMP_BLOCK = 32
CMP_STRIDE = 16
CMP_HIDDEN = 256
SEL_BLOCK = 64
N_SEL = 16
WINDOW = 512
Q_BLOCK = 128
FORCE_BONUS = 1e4
LRU_WIDTH = 1024
LRU_BLOCKS = 16
LRU_BW = LRU_WIDTH // LRU_BLOCKS
CONV_W = 4
LRU_C = 8.0
D_FF = -(-8 * D_MODEL // 768) * 256
IN_DIM = ATTN_DIM + 6 * KV_DIM + 3 * N_HEADS + 2 * LRU_WIDTH + 2 * D_MODEL
EPS = 1e-6

kernel_name = "hybrid_nsa_rglru_gated_block"


def _rmsnorm(x, g):
    xf = x.astype(jnp.float32)
    y = xf * lax.rsqrt(jnp.mean(xf * xf, axis=-1, keepdims=True) + EPS)
    return (y * g.astype(jnp.float32)).astype(x.dtype)


def _masked_softmax(s, mask):
    s = jnp.where(mask, s.astype(jnp.float32), -1e30)
    m = jnp.max(s, axis=-1, keepdims=True)
    e = jnp.exp(s - m) * mask
    return e / jnp.maximum(jnp.sum(e, axis=-1, keepdims=True), 1e-30)


def _alibi_slopes():
    return np.power(2.0, -8.0 * np.arange(1, N_HEADS + 1) / N_HEADS).astype(np.float32)


def _cmp_to_sel_map(n_cmp, n_blk):
    cs = np.arange(n_cmp) * CMP_STRIDE
    ce = cs + CMP_BLOCK - 1
    bs = np.arange(n_blk) * SEL_BLOCK
    be = bs + SEL_BLOCK - 1
    return ((cs[:, None] <= be[None, :]) & (ce[:, None] >= bs[None, :])).astype(np.float32)


def _compress(kv, pos, w1, b1, w2, b2):
    B, T, G, D = kv.shape
    chunks = kv.reshape(B, T // CMP_STRIDE, CMP_STRIDE, G, D)
    blocks = jnp.concatenate([chunks[:, :-1], chunks[:, 1:]], axis=2) + pos[None, None, :, None, :]
    hid = jax.nn.gelu(jnp.einsum('bnlgd,ldf->bngf', blocks, w1) + b1)
    return jnp.einsum('bngf,fd->bngd', hid, w2) + b2


def _nsa(q, k_cmp, v_cmp, k_slc, v_slc, k_win, v_win, gates):
    B, T = q.shape[0], q.shape[1]
    n_cmp = k_cmp.shape[1]
    n_blk = T // SEL_BLOCK
    n_sel = min(N_SEL, n_blk)
    n_tok = n_sel * SEL_BLOCK
    scale = HEAD_DIM ** -0.5
    slopes = jnp.asarray(_alibi_slopes()).reshape(1, N_KV, HPG, 1, 1)
    cmp_end = jnp.arange(n_cmp) * CMP_STRIDE + CMP_BLOCK - 1
    blk_map = jnp.asarray(_cmp_to_sel_map(n_cmp, n_blk))
    kb = k_slc.reshape(B, n_blk, SEL_BLOCK, N_KV, HEAD_DIM).transpose(0, 3, 1, 2, 4)
    vb = v_slc.reshape(B, n_blk, SEL_BLOCK, N_KV, HEAD_DIM).transpose(0, 3, 1, 2, 4)
    k_pad = jnp.pad(k_win, ((0, 0), (WINDOW, 0), (0, 0), (0, 0)))
    v_pad = jnp.pad(v_win, ((0, 0), (WINDOW, 0), (0, 0), (0, 0)))
    b_ix = jnp.arange(B)[:, None, None, None]
    g_ix = jnp.arange(N_KV)[None, :, None, None]
    blk_ids = jnp.arange(n_blk)

    def block(i):
        qs = i * Q_BLOCK
        qb = lax.dynamic_slice_in_dim(q, qs, Q_BLOCK, axis=1)
        gb = lax.dynamic_slice_in_dim(gates, qs, Q_BLOCK, axis=1)
        t = qs + jnp.arange(Q_BLOCK)
        dist = (t[:, None] - cmp_end[None, :]).astype(jnp.float32)
        s = jnp.einsum('bqghd,bngd->bghqn', qb, k_cmp) * scale
        p_cmp = _masked_softmax(s - slopes * dist, dist >= 0)
        o_cmp = jnp.einsum('bghqn,bngd->bqghd', p_cmp.astype(v_cmp.dtype), v_cmp)
        imp = jnp.einsum('bghqn,nj->bgqj', p_cmp, blk_map)
        cur = (t // SEL_BLOCK)[:, None]
        forced = ((blk_ids == 0) | (blk_ids == cur) | (blk_ids == cur - 1)).astype(jnp.float32)
        valid = blk_ids[None, :] * SEL_BLOCK <= t[:, None]
        imp = jnp.where(valid, imp + FORCE_BONUS * forced, -1e30)
        _, idx = lax.top_k(imp, n_sel)
        k_sel = kb[b_ix, g_ix, idx].reshape(B, N_KV, Q_BLOCK, n_tok, HEAD_DIM)
        v_sel = vb[b_ix, g_ix, idx].reshape(B, N_KV, Q_BLOCK, n_tok, HEAD_DIM)
        s_pos = (idx[..., None] * SEL_BLOCK + jnp.arange(SEL_BLOCK)).reshape(B, N_KV, 1, Q_BLOCK, n_tok)
        dist = (t[None, None, None, :, None] - s_pos).astype(jnp.float32)
        s = jnp.einsum('bqghd,bgqmd->bghqm', qb, k_sel) * scale
        p = _masked_softmax(s - slopes * dist, dist >= 0)
        o_slc = jnp.einsum('bghqm,bgqmd->bqghd', p.astype(v_sel.dtype), v_sel)
        kw = lax.dynamic_slice_in_dim(k_pad, qs, WINDOW + Q_BLOCK, axis=1)
        vw = lax.dynamic_slice_in_dim(v_pad, qs, WINDOW + Q_BLOCK, axis=1)
        w_pos = qs - WINDOW + jnp.arange(WINDOW + Q_BLOCK)
        wd = t[:, None] - w_pos[None, :]
        wmask = (wd >= 0) & (wd < WINDOW) & (w_pos[None, :] >= 0)
        s = jnp.einsum('bqghd,bkgd->bghqk', qb, kw) * scale
        p = _masked_softmax(s - slopes * wd.astype(jnp.float32), wmask)
        o_win = jnp.einsum('bghqk,bkgd->bqghd', p.astype(vw.dtype), vw)
        o = gb[..., 0:1] * o_cmp + gb[..., 1:2] * o_slc + gb[..., 2:3] * o_win
        return o.reshape(B, Q_BLOCK, ATTN_DIM)

    out = lax.map(block, jnp.arange(T // Q_BLOCK))
    return out.transpose(1, 0, 2, 3).reshape(B, T, ATTN_DIM)


def _lin_combine(c1, c2):
    a1, b1 = c1
    a2, b2 = c2
    return a1 * a2, a2 * b1 + b2


def _rglru_branch(xb, gate, conv_w, conv_b, wa, ba, wi, bi, lam):
    B, T, W = xb.shape
    xc = lax.conv_general_dilated(xb, conv_w, window_strides=(1,), padding=[(CONV_W - 1, 0)],
                                  dimension_numbers=('NWC', 'WIO', 'NWC'),
                                  feature_group_count=LRU_WIDTH) + conv_b
    xr = xc.reshape(B, T, LRU_BLOCKS, LRU_BW)
    r = jax.nn.sigmoid(jnp.einsum('btnk,nkj->btnj', xr, wa).reshape(B, T, W) + ba)
    i = jax.nn.sigmoid(jnp.einsum('btnk,nkj->btnj', xr, wi).reshape(B, T, W) + bi)
    log_a = -LRU_C * r.astype(jnp.float32) * jax.nn.softplus(-lam.astype(jnp.float32))
    a = jnp.exp(log_a)
    b = jnp.sqrt(-jnp.expm1(2.0 * log_a)) * (i * xc).astype(jnp.float32)
    _, h = lax.associative_scan(_lin_combine, (a, b), axis=1)
    return h.astype(xb.dtype) * jax.nn.gelu(gate)


def _layer(x, norm1_g, w_in, q_norm_g, k_norm_g, cmp_pos_k, cmp_w1_k, cmp_b1_k, cmp_w2_k, cmp_b2_k,
           cmp_pos_v, cmp_w1_v, cmp_b1_v, cmp_w2_v, cmp_b2_v, conv_w, conv_b, lru_wa, lru_ba,
           lru_wi, lru_bi, lru_lambda, w_o_attn, w_o_lru, w_out, norm2_g, w_gate, w_up, w_down):
    B, T, _ = x.shape
    h = _rmsnorm(x, norm1_g)
    proj = h @ w_in
    o1 = ATTN_DIM
    o2 = o1 + 6 * KV_DIM
    o3 = o2 + 3 * N_HEADS
    o4 = o3 + LRU_WIDTH
    o5 = o4 + LRU_WIDTH
    q = _rmsnorm(proj[..., :o1].reshape(B, T, N_KV, HPG, HEAD_DIM), q_norm_g)
    kv = proj[..., o1:o2].reshape(B, T, 6, N_KV, HEAD_DIM)
    nsa_g = jax.nn.sigmoid(proj[..., o2:o3].reshape(B, T, N_KV, HPG, 3))
    lru_x = proj[..., o3:o4]
    lru_gate = proj[..., o4:o5]
    merge_g = jax.nn.sigmoid(proj[..., o5:].reshape(B, T, 2, D_MODEL))
    k_cmp = _rmsnorm(_compress(kv[:, :, 0], cmp_pos_k, cmp_w1_k, cmp_b1_k, cmp_w2_k, cmp_b2_k), k_norm_g[0])
    v_cmp = _compress(kv[:, :, 1], cmp_pos_v, cmp_w1_v, cmp_b1_v, cmp_w2_v, cmp_b2_v)
    k_slc = _rmsnorm(kv[:, :, 2], k_norm_g[1])
    k_win = _rmsnorm(kv[:, :, 4], k_norm_g[2])
    attn = _nsa(q, k_cmp, v_cmp, k_slc, kv[:, :, 3], k_win, kv[:, :, 5], nsa_g)
    lru = _rglru_branch(lru_x, lru_gate, conv_w, conv_b, lru_wa, lru_ba, lru_wi, lru_bi, lru_lambda)
    merged = merge_g[:, :, 0] * (attn @ w_o_attn) + merge_g[:, :, 1] * (lru @ w_o_lru)
    x = x + merged @ w_out
    h2 = _rmsnorm(x, norm2_g)
    return x + (jax.nn.silu(h2 @ w_gate) * (h2 @ w_up)) @ w_down


def setup_inputs(seed: int = 0) -> dict:
    key = jax.random.key(seed)
    ks = jax.random.split(key, 32)
    L = DEPTH
    nrm = lambda k, shape, fan: jax.random.normal(k, shape, jnp.float32) * (fan ** -0.5)
    small = lambda k, shape, s: jax.random.normal(k, shape, jnp.float32) * s
    u = jax.random.uniform(ks[31], (L, LRU_WIDTH), jnp.float32, minval=0.9, maxval=0.999)
    a0 = u ** (1.0 / LRU_C)
    return {
        "x": jax.random.normal(ks[0], (BATCH, SEQ, D_MODEL), jnp.float32),
        "norm1_g": 1.0 + small(ks[1], (L, D_MODEL), 0.02),
        "w_in": nrm(ks[2], (L, D_MODEL, IN_DIM), D_MODEL),
        "q_norm_g": 1.0 + small(ks[3], (L, HEAD_DIM), 0.02),
        "k_norm_g": 1.0 + small(ks[4], (L, 3, HEAD_DIM), 0.02),
        "cmp_pos_k": small(ks[5], (L, CMP_BLOCK, HEAD_DIM), 0.1),
        "cmp_w1_k": nrm(ks[6], (L, CMP_BLOCK, HEAD_DIM, CMP_HIDDEN), CMP_BLOCK * HEAD_DIM),
        "cmp_b1_k": small(ks[7], (L, CMP_HIDDEN), 0.01),
        "cmp_w2_k": nrm(ks[8], (L, CMP_HIDDEN, HEAD_DIM), CMP_HIDDEN),
        "cmp_b2_k": small(ks[9], (L, HEAD_DIM), 0.01),
        "cmp_pos_v": small(ks[10], (L, CMP_BLOCK, HEAD_DIM), 0.1),
        "cmp_w1_v": nrm(ks[11], (L, CMP_BLOCK, HEAD_DIM, CMP_HIDDEN), CMP_BLOCK * HEAD_DIM),
        "cmp_b1_v": small(ks[12], (L, CMP_HIDDEN), 0.01),
        "cmp_w2_v": nrm(ks[13], (L, CMP_HIDDEN, HEAD_DIM), CMP_HIDDEN),
        "cmp_b2_v": small(ks[14], (L, HEAD_DIM), 0.01),
        "conv_w": nrm(ks[15], (L, CONV_W, 1, LRU_WIDTH), CONV_W),
        "conv_b": small(ks[16], (L, LRU_WIDTH), 0.01),
        "lru_wa": nrm(ks[17], (L, LRU_BLOCKS, LRU_BW, LRU_BW), LRU_BW),
        "lru_ba": small(ks[18], (L, LRU_WIDTH), 0.01),
        "lru_wi": nrm(ks[19], (L, LRU_BLOCKS, LRU_BW, LRU_BW), LRU_BW),
        "lru_bi": small(ks[20], (L, LRU_WIDTH), 0.01),
        "lru_lambda": jnp.log(a0) - jnp.log1p(-a0),
        "w_o_attn": nrm(ks[21], (L, ATTN_DIM, D_MODEL), ATTN_DIM),
        "w_o_lru": nrm(ks[22], (L, LRU_WIDTH, D_MODEL), LRU_WIDTH),
        "w_out": nrm(ks[23], (L, D_MODEL, D_MODEL), D_MODEL),
        "norm2_g": 1.0 + small(ks[24], (L, D_MODEL), 0.02),
        "w_gate": nrm(ks[25], (L, D_MODEL, D_FF), D_MODEL),
        "w_up": nrm(ks[26], (L, D_MODEL, D_FF), D_MODEL),
        "w_down": nrm(ks[27], (L, D_FF, D_MODEL), D_FF),
    }


def reference(x, norm1_g, w_in, q_norm_g, k_norm_g, cmp_pos_k, cmp_w1_k, cmp_b1_k, cmp_w2_k, cmp_b2_k,
              cmp_pos_v, cmp_w1_v, cmp_b1_v, cmp_w2_v, cmp_b2_v, conv_w, conv_b, lru_wa, lru_ba,
              lru_wi, lru_bi, lru_lambda, w_o_attn, w_o_lru, w_out, norm2_g, w_gate, w_up, w_down):
    for l in range(DEPTH):
        x = _layer(x, norm1_g[l], w_in[l], q_norm_g[l], k_norm_g[l], cmp_pos_k[l], cmp_w1_k[l],
                   cmp_b1_k[l], cmp_w2_k[l], cmp_b2_k[l], cmp_pos_v[l], cmp_w1_v[l], cmp_b1_v[l],
                   cmp_w2_v[l], cmp_b2_v[l], conv_w[l], conv_b[l], lru_wa[l], lru_ba[l], lru_wi[l],
                   lru_bi[l], lru_lambda[l], w_o_attn[l], w_o_lru[l], w_out[l], norm2_g[l],
                   w_gate[l], w_up[l], w_down[l])
    return x
```

```python
import functools

import numpy as np
import jax
import jax.numpy as jnp
from jax import lax
from jax.experimental import pallas as pl
from jax.experimental.pallas import tpu as pltpu

N_HEADS = 16
HEAD_DIM = 64
N_KV = 4
HPG = N_HEADS // N_KV
CMP_BLOCK = 32
CMP_STRIDE = 16
CMP_HIDDEN = 256
SEL_BLOCK = 64
N_SEL = 16
WINDOW = 512
FORCE_BONUS = 1e4
LRU_BLOCKS = 16
CONV_W = 4
LRU_C = 8.0
EPS = 1e-6

MXU_DTYPE = jnp.bfloat16
F32 = jnp.float32

V7X_LANES = 128
V7X_SUBLANES = 8
V7X_MXU_DIM = 256
V7X_VMEM_LIMIT_BYTES = 48 * 1024 * 1024

TQ = 128
TK_SEL = 256
TK_WIN = 128
FEAT = V7X_MXU_DIM
F_SEL = HEAD_DIM
F_POS = 2 * HEAD_DIM
F_CMP = F_POS + 6
NEG_MASK = -1e30
NEG_BLOCK = -(2.0 ** 100)


def _params(*sem):
    return pltpu.CompilerParams(dimension_semantics=sem, vmem_limit_bytes=V7X_VMEM_LIMIT_BYTES)


def _gelu_tanh(x):
    return 0.5 * x * (1.0 + jnp.tanh(0.7978845608028654 * (x + 0.044715 * (x * x * x))))


def _sigmoid(x):
    return 1.0 / (1.0 + jnp.exp(-x))


def _neg_expm1(y):
    c = [1.0 / 2, 1.0 / 6, 1.0 / 24, 1.0 / 120, 1.0 / 720, 1.0 / 5040, 1.0 / 40320]
    poly = c[-1]
    for ck in reversed(c[:-1]):
        poly = poly * y + ck
    series = -y * (1.0 + y * poly)
    return jnp.where(y > -0.25, series, 1.0 - jnp.exp(y))


def _rmsnorm_kernel(x_ref, g_ref, o_ref):
    x = x_ref[...]
    ms = jnp.mean(x * x, axis=-1, keepdims=True)
    o_ref[...] = (x * lax.rsqrt(ms + EPS) * g_ref[...]).astype(o_ref.dtype)


def _rmsnorm(x2d, g, tm=512):
    m, d = x2d.shape
    return pl.pallas_call(
        _rmsnorm_kernel,
        grid=(m // tm,),
        in_specs=[pl.BlockSpec((tm, d), lambda i: (i, 0)),
                  pl.BlockSpec((1, d), lambda i: (0, 0))],
        out_specs=pl.BlockSpec((tm, d), lambda i: (i, 0)),
        out_shape=jax.ShapeDtypeStruct((m, d), MXU_DTYPE),
        compiler_params=_params("parallel"),
        name="rmsnorm",
    )(x2d, g.reshape(1, d))


def _nn_kernel(h_ref, w_ref, o_ref, *, act):
    r = jnp.dot(h_ref[...], w_ref[...], preferred_element_type=F32)
    if act == "sigmoid":
        r = _sigmoid(r)
    o_ref[...] = r.astype(o_ref.dtype)


def _nn_proj(h2d, w, act=None, out_dtype=F32, tm=512, tn=512, name="nn_proj"):
    m, k = h2d.shape
    n = w.shape[1]
    return pl.pallas_call(
        functools.partial(_nn_kernel, act=act),
        grid=(m // tm, n // tn),
        in_specs=[pl.BlockSpec((tm, k), lambda i, j: (i, 0)),
                  pl.BlockSpec((k, tn), lambda i, j: (0, j))],
        out_specs=pl.BlockSpec((tm, tn), lambda i, j: (i, j)),
        out_shape=jax.ShapeDtypeStruct((m, n), out_dtype),
        compiler_params=_params("parallel", "parallel"),
        name=name,
    )(h2d, w)


def _nt_dot(wt, h):
    return lax.dot_general(wt, h, (((1,), (1,)), ((), ())), preferred_element_type=F32)


def _qt_kernel(h_ref, wt_ref, g_ref, o_ref):
    r = _nt_dot(wt_ref[...], h_ref[0])
    tm = r.shape[1]
    r3 = r.reshape(HPG, HEAD_DIM, tm)
    ms = jnp.mean(r3 * r3, axis=1, keepdims=True)
    qn = (r3 * lax.rsqrt(ms + EPS)).reshape(HPG * HEAD_DIM, tm)
    o_ref[0] = (qn * g_ref[...]).astype(o_ref.dtype)


def _q_proj_t(h3d, wq_t, q_gain_col, tm=512):
    b, t, k = h3d.shape
    rows = HPG * HEAD_DIM
    return pl.pallas_call(
        _qt_kernel,
        grid=(b, t // tm, N_KV),
        in_specs=[pl.BlockSpec((1, tm, k), lambda bi, i, g: (bi, i, 0)),
                  pl.BlockSpec((rows, k), lambda bi, i, g: (g, 0)),
                  pl.BlockSpec((rows, 1), lambda bi, i, g: (0, 0))],
        out_specs=pl.BlockSpec((1, rows, tm), lambda bi, i, g: (bi, g, i)),
        out_shape=jax.ShapeDtypeStruct((b, N_HEADS * HEAD_DIM, t), MXU_DTYPE),
        compiler_params=_params("parallel", "parallel", "parallel"),
        name="q_proj_t",
    )(h3d, wq_t, q_gain_col)


def _nt_kernel(h_ref, wt_ref, o_ref, *, act):
    r = _nt_dot(wt_ref[...], h_ref[0])
    if act == "sigmoid":
        r = _sigmoid(r)
    o_ref[0] = r.astype(o_ref.dtype)


def _nt_proj(h3d, w_t, act=None, out_dtype=F32, tm=512, name="nt_proj"):
    b, t, k = h3d.shape
    n = w_t.shape[0]
    return pl.pallas_call(
        functools.partial(_nt_kernel, act=act),
        grid=(b, t // tm),
        in_specs=[pl.BlockSpec((1, tm, k), lambda bi, i: (bi, i, 0)),
                  pl.BlockSpec((n, k), lambda bi, i: (0, 0))],
        out_specs=pl.BlockSpec((1, n, tm), lambda bi, i: (bi, 0, i)),
        out_shape=jax.ShapeDtypeStruct((b, n, t), out_dtype),
        compiler_params=_params("parallel", "parallel"),
        name=name,
    )(h3d, w_t)


def _token_features(pos, col, with_block_mask):
    blk = pos >> 6
    off = pos & (SEL_BLOCK - 1)
    feat = jnp.where((col >= F_POS) & (col < F_POS + 3), blk.astype(F32),
                     jnp.where((col >= F_POS + 3) & (col < F_POS + 6), off.astype(F32), 0.0))
    if with_block_mask:
        feat = jnp.where((col >= F_SEL) & (col - F_SEL == blk) & (col < F_POS), NEG_BLOCK, feat)
    return feat


def _kfeat_kernel(h_ref, w_ref, g_ref, o_ref, *, with_block_mask):
    r = jnp.dot(h_ref[0], w_ref[...], preferred_element_type=F32)
    tm = r.shape[0]
    ms = jnp.sum(r * r, axis=-1, keepdims=True) * (1.0 / HEAD_DIM)
    kn = r * lax.rsqrt(ms + EPS) * g_ref[...]
    pos = pl.program_id(1) * tm + lax.broadcasted_iota(jnp.int32, (tm, FEAT), 0)
    col = lax.broadcasted_iota(jnp.int32, (tm, FEAT), 1)
    o_ref[0, 0] = (kn + _token_features(pos, col, with_block_mask)).astype(o_ref.dtype)


def _key_features(h3d, w_pad, gain_pad, with_block_mask, tm=512, name="key_features"):
    b, t, k = h3d.shape
    return pl.pallas_call(
        functools.partial(_kfeat_kernel, with_block_mask=with_block_mask),
        grid=(b, t // tm, N_KV),
        in_specs=[pl.BlockSpec((1, tm, k), lambda bi, i, g: (bi, i, 0)),
                  pl.BlockSpec((k, FEAT), lambda bi, i, g: (0, g)),
                  pl.BlockSpec((1, FEAT), lambda bi, i, g: (0, 0))],
        out_specs=pl.BlockSpec((1, 1, tm, FEAT), lambda bi, i, g: (bi, g, i, 0)),
        out_shape=jax.ShapeDtypeStruct((b, N_KV, t, FEAT), MXU_DTYPE),
        compiler_params=_params("parallel", "parallel", "parallel"),
        name=name,
    )(h3d, w_pad, gain_pad)


def _cmp_hidden(c_ref, pos_ref, w1_ref, b1_ref):
    c = c_ref[0, 0]
    ncp = c.shape[0]
    ca = (c + pos_ref[0:1, :]).astype(MXU_DTYPE)
    cb = (c + pos_ref[1:2, :]).astype(MXU_DTYPE)
    first = jnp.dot(ca, w1_ref[0], preferred_element_type=F32)
    second = jnp.dot(cb, w1_ref[1], preferred_element_type=F32)
    hid = first + pltpu.roll(second, ncp - 1, 0) + b1_ref[...]
    return _gelu_tanh(hid).astype(MXU_DTYPE)


def _cmp_k_kernel(c_ref, pos_ref, w1_ref, b1_ref, w2_ref, b2_ref, g_ref, o_ref):
    hid = _cmp_hidden(c_ref, pos_ref, w1_ref, b1_ref)
    r = jnp.dot(hid, w2_ref[...], preferred_element_type=F32) + b2_ref[...]
    ncp = r.shape[0]
    ms = jnp.sum(r * r, axis=-1, keepdims=True) * (1.0 / HEAD_DIM)
    kn = r * lax.rsqrt(ms + EPS) * g_ref[...]
    idx = lax.broadcasted_iota(jnp.int32, (ncp, FEAT), 0)
    col = lax.broadcasted_iota(jnp.int32, (ncp, FEAT), 1)
    feat = jnp.where((col >= F_CMP) & (col < F_CMP + 3), (idx >> 6).astype(F32),
                     jnp.where((col >= F_CMP + 3) & (col < F_CMP + 6), (idx & 63).astype(F32), 0.0))
    o_ref[0, 0] = (kn + feat).astype(o_ref.dtype)


def _cmp_v_kernel(c_ref, pos_ref, w1_ref, b1_ref, w2t_ref, b2_ref, o_ref):
    hid = _cmp_hidden(c_ref, pos_ref, w1_ref, b1_ref)
    r = _nt_dot(w2t_ref[...], hid) + b2_ref[...]
    o_ref[0, 0] = r.astype(o_ref.dtype)


def _compress(chunks, pos2, w1, b1, w2, b2, gain_pad=None):
    b, g, ncp, ck = chunks.shape
    common = [pl.BlockSpec((1, 1, ncp, ck), lambda bi, gi: (bi, gi, 0, 0)),
              pl.BlockSpec((2, ck), lambda bi, gi: (0, 0)),
              pl.BlockSpec((2, ck, CMP_HIDDEN), lambda bi, gi: (0, 0, 0)),
              pl.BlockSpec((1, CMP_HIDDEN), lambda bi, gi: (0, 0))]
    if gain_pad is not None:
        return pl.pallas_call(
            _cmp_k_kernel,
            grid=(b, g),
            in_specs=common + [pl.BlockSpec((CMP_HIDDEN, FEAT), lambda bi, gi: (0, 0)),
                               pl.BlockSpec((1, FEAT), lambda bi, gi: (0, 0)),
                               pl.BlockSpec((1, FEAT), lambda bi, gi: (0, 0))],
            out_specs=pl.BlockSpec((1, 1, ncp, FEAT), lambda bi, gi: (bi, gi, 0, 0)),
            out_shape=jax.ShapeDtypeStruct((b, g, ncp, FEAT), MXU_DTYPE),
            compiler_params=_params("parallel", "parallel"),
            name="compress_k",
        )(chunks, pos2, w1, b1, w2, b2, gain_pad)
    return pl.pallas_call(
        _cmp_v_kernel,
        grid=(b, g),
        in_specs=common + [pl.BlockSpec((HEAD_DIM, CMP_HIDDEN), lambda bi, gi: (0, 0)),
                           pl.BlockSpec((HEAD_DIM, 1), lambda bi, gi: (0, 0))],
        out_specs=pl.BlockSpec((1, 1, HEAD_DIM, ncp), lambda bi, gi: (bi, gi, 0, 0)),
        out_shape=jax.ShapeDtypeStruct((b, g, HEAD_DIM, ncp), MXU_DTYPE),
        compiler_params=_params("parallel", "parallel"),
        name="compress_v",
    )(chunks, pos2, w1, b1, w2, b2)


def _split3(v):
    parts = []
    rest = np.asarray(v, np.float64)
    for _ in range(3):
        p = rest.astype(np.float32).astype(jnp.bfloat16).astype(np.float64)
        parts.append(p)
        rest = rest - p
    return parts


def _alibi_query_features():
    tab = np.zeros((N_KV, FEAT - F_POS, HPG * TQ), np.float64)
    for g in range(N_KV):
        for h in range(HPG):
            slope = 2.0 ** (-8.0 * (g * HPG + h + 1) / N_HEADS)
            parts = _split3(slope)
            lanes = slice(h * TQ, (h + 1) * TQ)
            for i, p in enumerate(parts):
                tab[g, i, lanes] = SEL_BLOCK * p
                tab[g, 3 + i, lanes] = p
                tab[g, 6 + i, lanes] = CMP_STRIDE * 64 * p
                tab[g, 9 + i, lanes] = CMP_STRIDE * p
    return jnp.asarray(tab, F32).astype(MXU_DTYPE)


def _block_map_t(n_cmp_pad, n_blk):
    cs = np.arange(n_cmp_pad) * CMP_STRIDE
    ce = cs + CMP_BLOCK - 1
    bs = np.arange(n_blk) * SEL_BLOCK
    be = bs + SEL_BLOCK - 1
    return jnp.asarray(((cs[None, :] <= be[:, None]) & (ce[None, :] >= bs[:, None])).astype(np.float32))


def _attn_kernel(qt_ref, gate_ref, kc_ref, vc_ref, ks_ref, vs_ref, kw_ref, vw_ref, alibi_ref, map_ref,
                 o_ref, qp_ref, imp_ref, *, n_blk, n_sel):
    lanes = HPG * TQ
    qi = pl.program_id(2)
    q0 = qi * TQ
    ncp = kc_ref.shape[2]

    for h in range(HPG):
        qp_ref[0:HEAD_DIM, h * TQ:(h + 1) * TQ] = qt_ref[0, h * HEAD_DIM:(h + 1) * HEAD_DIM, :]
    qp_ref[F_SEL:F_POS, :] = jnp.zeros((F_POS - F_SEL, lanes), qp_ref.dtype)
    qp_ref[F_POS:FEAT, :] = alibi_ref[0]

    t_lane = q0 + (lax.broadcasted_iota(jnp.int32, (1, lanes), 1) & (TQ - 1))

    sc = jnp.dot(kc_ref[0, 0], qp_ref[...], preferred_element_type=F32)
    cmp_end = lax.broadcasted_iota(jnp.int32, (ncp, lanes), 0) * CMP_STRIDE + (CMP_BLOCK - 1)
    cmask = t_lane >= cmp_end
    sc = jnp.where(cmask, sc, NEG_MASK)
    mc = jnp.max(sc, axis=0, keepdims=True)
    ec = jnp.where(cmask, jnp.exp(sc - mc), 0.0)
    lc = jnp.sum(ec, axis=0, keepdims=True)
    pc = ec * (1.0 / jnp.maximum(lc, 1e-30))
    o_cmp = jnp.dot(vc_ref[0, 0], pc.astype(MXU_DTYPE), preferred_element_type=F32)

    psum = pc[:, 0:TQ]
    for h in range(1, HPG):
        psum = psum + pc[:, h * TQ:(h + 1) * TQ]
    imp = jnp.dot(map_ref[...], psum, preferred_element_type=F32)
    t_q = q0 + lax.broadcasted_iota(jnp.int32, (n_blk, TQ), 1)
    blk = lax.broadcasted_iota(jnp.int32, (n_blk, TQ), 0)
    cur = t_q >> 6
    forced = (blk == 0) | (blk == cur) | (blk == cur - 1)
    valid = blk <= cur
    imp = jnp.where(valid, imp + jnp.where(forced, FORCE_BONUS, 0.0), NEG_MASK)
    imp_ref[...] = imp
    n_chunks = n_blk // V7X_SUBLANES
    chunks = [imp[c * V7X_SUBLANES:(c + 1) * V7X_SUBLANES] for c in range(n_chunks)]
    ranks = [jnp.zeros((V7X_SUBLANES, TQ), jnp.int32) for _ in range(n_chunks)]
    sub = lax.broadcasted_iota(jnp.int32, (V7X_SUBLANES, TQ), 0)
    for k in range(n_blk):
        row = imp_ref[k:k + 1, :]
        for c in range(n_chunks):
            lo = c * V7X_SUBLANES
            if lo > k:
                one = jnp.where(row >= chunks[c], 1, 0)
            elif lo + V7X_SUBLANES - 1 <= k:
                one = jnp.where(row > chunks[c], 1, 0)
            else:
                one = jnp.where(sub + lo > k, jnp.where(row >= chunks[c], 1, 0),
                                jnp.where(row > chunks[c], 1, 0))
            ranks[c] = ranks[c] + one
    rank = jnp.concatenate(ranks, axis=0)
    not_sel = jnp.where(rank < n_sel, 0.0, 1.0).astype(qp_ref.dtype)
    for h in range(HPG):
        qp_ref[F_SEL:F_SEL + n_blk, h * TQ:(h + 1) * TQ] = not_sel

    def step(k_ref, v_ref, k0, tk, carry, mask_fn):
        m, l, acc = carry
        kt = k_ref[0, 0, pl.ds(k0, tk), :]
        s = jnp.dot(kt, qp_ref[...], preferred_element_type=F32)
        if mask_fn is not None:
            kpos = k0 + lax.broadcasted_iota(jnp.int32, (tk, lanes), 0)
            msk = mask_fn(kpos)
            s = jnp.where(msk, s, NEG_MASK)
        m_new = jnp.maximum(m, jnp.max(s, axis=0, keepdims=True))
        alpha = jnp.exp(m - m_new)
        p = jnp.exp(s - m_new)
        if mask_fn is not None:
            p = jnp.where(msk, p, 0.0)
        l = alpha * l + jnp.sum(p, axis=0, keepdims=True)
        vt = v_ref[0, :, pl.ds(k0, tk)]
        acc = alpha * acc + jnp.dot(vt, p.astype(MXU_DTYPE), preferred_element_type=F32)
        return m_new, l, acc

    init = (jnp.full((1, lanes), NEG_MASK, F32), jnp.zeros((1, lanes), F32),
            jnp.zeros((HEAD_DIM, lanes), F32))

    n_full = q0 // TK_SEL

    def sel_body(kt, carry):
        return step(ks_ref, vs_ref, pl.multiple_of(kt * TK_SEL, TK_SEL), TK_SEL, carry, None)

    carry = lax.fori_loop(0, n_full, sel_body, init)
    m_s, l_s, acc_s = step(ks_ref, vs_ref, pl.multiple_of(n_full * TK_SEL, TK_SEL), TK_SEL, carry,
                           lambda kpos: kpos <= t_lane)
    o_sel = acc_s / l_s

    def win_mask(kpos):
        d = t_lane - kpos
        return (d >= 0) & (d < WINDOW)

    def win_body(kt, carry):
        return step(kw_ref, vw_ref, pl.multiple_of(kt * TK_WIN, TK_WIN), TK_WIN, carry, win_mask)

    first_tile = jnp.maximum(qi - WINDOW // TK_WIN, 0)
    m_w, l_w, acc_w = lax.fori_loop(first_tile, qi + 1, win_body, init)
    o_win = acc_w / l_w

    gates = gate_ref[0]
    def gate_row(j):
        return jnp.concatenate([gates[j * HPG + h:j * HPG + h + 1, :] for h in range(HPG)], axis=1)
    o_t = gate_row(0) * o_cmp + gate_row(1) * o_sel + gate_row(2) * o_win
    for hp in range(HPG // 2):
        pair = jnp.concatenate([o_t[:, (2 * hp) * TQ:(2 * hp + 1) * TQ],
                                o_t[:, (2 * hp + 1) * TQ:(2 * hp + 2) * TQ]], axis=0)
        o_ref[0, :, hp * 2 * HEAD_DIM:(hp + 1) * 2 * HEAD_DIM] = pair.T.astype(o_ref.dtype)


def _attention(q_t, gates_t, k_cmp, v_cmp_t, k_sel, k_win, v_t):
    b, _, t = q_t.shape
    ncp = k_cmp.shape[2]
    n_blk = t // SEL_BLOCK
    assert t % TK_SEL == 0 and F_SEL + n_blk <= F_POS, "sequence length not supported by the feature layout"
    n_sel = min(N_SEL, n_blk)
    lanes = HPG * TQ
    rows = HPG * HEAD_DIM
    alibi = _alibi_query_features()
    blk_map_t = _block_map_t(ncp, n_blk)
    kernel = functools.partial(_attn_kernel, n_blk=n_blk, n_sel=n_sel)
    return pl.pallas_call(
        kernel,
        grid=(b, N_KV, t // TQ),
        in_specs=[
            pl.BlockSpec((1, rows, TQ), lambda bi, g, i: (bi, g, i)),
            pl.BlockSpec((1, 16, TQ), lambda bi, g, i: (bi, g, i)),
            pl.BlockSpec((1, 1, ncp, FEAT), lambda bi, g, i: (bi, g, 0, 0)),
            pl.BlockSpec((1, 1, HEAD_DIM, ncp), lambda bi, g, i: (bi, g, 0, 0)),
            pl.BlockSpec((1, 1, t, FEAT), lambda bi, g, i: (bi, g, 0, 0)),
            pl.BlockSpec((1, HEAD_DIM, t), lambda bi, g, i: (bi, g, 0)),
            pl.BlockSpec((1, 1, t, FEAT), lambda bi, g, i: (bi, g, 0, 0)),
            pl.BlockSpec((1, HEAD_DIM, t), lambda bi, g, i: (bi, N_KV + g, 0)),
            pl.BlockSpec((1, FEAT - F_POS, lanes), lambda bi, g, i: (g, 0, 0)),
            pl.BlockSpec((n_blk, ncp), lambda bi, g, i: (0, 0)),
        ],
        out_specs=pl.BlockSpec((1, TQ, rows), lambda bi, g, i: (bi, i, g)),
        out_shape=jax.ShapeDtypeStruct((b, t, N_HEADS * HEAD_DIM), MXU_DTYPE),
        scratch_shapes=[pltpu.VMEM((FEAT, lanes), MXU_DTYPE),
                        pltpu.VMEM((n_blk, TQ), F32)],
        compiler_params=_params("parallel", "parallel", "arbitrary"),
        name="nsa_attention",
    )(q_t, gates_t, k_cmp, v_cmp_t, k_sel, v_t, k_win, v_t, alibi, blk_map_t)


def _lru_kernel(x_ref, gate_ref, cw_ref, cb_ref, wa_ref, ba_ref, wi_ref, bi_ref, lam_ref, o_ref,
                tail_ref, h_ref):
    tt, w = x_ref.shape[1], x_ref.shape[2]

    @pl.when(pl.program_id(1) == 0)
    def _():
        tail_ref[...] = jnp.zeros_like(tail_ref)
        h_ref[...] = jnp.zeros_like(h_ref)

    x = x_ref[0]
    prev = tail_ref[...]
    row8 = lax.broadcasted_iota(jnp.int32, (V7X_SUBLANES, w), 0)
    xc = x * cw_ref[CONV_W - 1:CONV_W, :] + cb_ref[...]
    for s in range(1, CONV_W):
        xs = pltpu.roll(x, s, 0)
        head = jnp.where(row8 < s, pltpu.roll(prev, s, 0), xs[0:V7X_SUBLANES])
        xs = jnp.concatenate([head, xs[V7X_SUBLANES:]], axis=0)
        xc = xc + xs * cw_ref[CONV_W - 1 - s:CONV_W - s, :]
    tail_ref[...] = x[tt - V7X_SUBLANES:tt]

    xb = xc.astype(MXU_DTYPE)
    r = _sigmoid(jnp.dot(xb, wa_ref[...], preferred_element_type=F32) + ba_ref[...])
    i = _sigmoid(jnp.dot(xb, wi_ref[...], preferred_element_type=F32) + bi_ref[...])
    z = -lam_ref[...]
    softplus = jnp.maximum(z, 0.0) + jnp.log1p(jnp.exp(-jnp.abs(z)))
    log_a = -LRU_C * r * softplus
    a = jnp.exp(log_a)
    bb = jnp.sqrt(_neg_expm1(2.0 * log_a)) * (i * xc)

    row = lax.broadcasted_iota(jnp.int32, (tt, w), 0)
    d = 1
    while d < tt:
        ok = row >= d
        a_sh = pltpu.roll(a, d, 0)
        b_sh = pltpu.roll(bb, d, 0)
        bb = jnp.where(ok, a * b_sh + bb, bb)
        a = jnp.where(ok, a * a_sh, a)
        d *= 2
    hcur = bb + a * h_ref[0:1, :]
    h_ref[...] = jnp.broadcast_to(hcur[tt - 1:tt, :], h_ref.shape)
    o_ref[0] = (hcur * _gelu_tanh(gate_ref[0])).astype(o_ref.dtype)


def _rglru(xg, conv_w, conv_b, wa_bd, ba, wi_bd, bi, lam, tt=256):
    b, t, w2 = xg.shape
    w = w2 // 2
    vec = lambda: pl.BlockSpec((1, w), lambda bi_, i: (0, 0))
    return pl.pallas_call(
        _lru_kernel,
        grid=(b, t // tt),
        in_specs=[pl.BlockSpec((1, tt, w), lambda bi_, i: (bi_, i, 0)),
                  pl.BlockSpec((1, tt, w), lambda bi_, i: (bi_, i, 1)),
                  pl.BlockSpec((CONV_W, w), lambda bi_, i: (0, 0)), vec(),
                  pl.BlockSpec((w, w), lambda bi_, i: (0, 0)), vec(),
                  pl.BlockSpec((w, w), lambda bi_, i: (0, 0)), vec(), vec()],
        out_specs=pl.BlockSpec((1, tt, w), lambda bi_, i: (bi_, i, 0)),
        out_shape=jax.ShapeDtypeStruct((b, t, w), MXU_DTYPE),
        scratch_shapes=[pltpu.VMEM((V7X_SUBLANES, w), F32), pltpu.VMEM((V7X_SUBLANES, w), F32)],
        compiler_params=_params("parallel", "arbitrary"),
        name="rglru",
    )(xg, xg, conv_w, conv_b, wa_bd, ba, wi_bd, bi, lam)


def _merge_kernel(attn_ref, lru_ref, mg0_ref, mg1_ref, x_ref, wa_ref, wl_ref, wo_ref, g2_ref,
                  x1_ref, h2_ref):
    ya = jnp.dot(attn_ref[...], wa_ref[...], preferred_element_type=F32)
    yl = jnp.dot(lru_ref[...], wl_ref[...], preferred_element_type=F32)
    merged = mg0_ref[...] * ya + mg1_ref[...] * yl
    x1 = x_ref[...] + jnp.dot(merged.astype(MXU_DTYPE), wo_ref[...], preferred_element_type=F32)
    x1_ref[...] = x1
    ms = jnp.mean(x1 * x1, axis=-1, keepdims=True)
    h2_ref[...] = (x1 * lax.rsqrt(ms + EPS) * g2_ref[...]).astype(h2_ref.dtype)


def _merge(attn, lru, mg, x2d, wa, wl, wo, g2, tm=256):
    m, d = x2d.shape
    row = lambda j: pl.BlockSpec((tm, d), lambda i: (i, j))
    full = lambda r: pl.BlockSpec((r, d), lambda i: (0, 0))
    return pl.pallas_call(
        _merge_kernel,
        grid=(m // tm,),
        in_specs=[row(0), row(0), row(0), row(1), row(0), full(d), full(d), full(d), full(1)],
        out_specs=[row(0), row(0)],
        out_shape=[jax.ShapeDtypeStruct((m, d), F32), jax.ShapeDtypeStruct((m, d), MXU_DTYPE)],
        compiler_params=_params("parallel"),
        name="merge_out",
    )(attn, lru, mg, mg, x2d, wa, wl, wo, g2.reshape(1, d))


def _ffn_kernel(h_ref, x1_ref, wg_ref, wu_ref, wd_ref, o_ref):
    @pl.when(pl.program_id(1) == 0)
    def _():
        o_ref[...] = x1_ref[...]

    h = h_ref[...]
    g = jnp.dot(h, wg_ref[...], preferred_element_type=F32)
    u = jnp.dot(h, wu_ref[...], preferred_element_type=F32)
    act = (g * _sigmoid(g) * u).astype(MXU_DTYPE)
    o_ref[...] += jnp.dot(act, wd_ref[...], preferred_element_type=F32)


def _ffn(h2, x1, wg, wu, wd, tm=1024, tf=256):
    m, d = x1.shape
    f = wg.shape[1]
    return pl.pallas_call(
        _ffn_kernel,
        grid=(m // tm, f // tf),
        in_specs=[pl.BlockSpec((tm, d), lambda i, j: (i, 0)),
                  pl.BlockSpec((tm, d), lambda i, j: (i, 0)),
                  pl.BlockSpec((d, tf), lambda i, j: (0, j)),
                  pl.BlockSpec((d, tf), lambda i, j: (0, j)),
                  pl.BlockSpec((tf, d), lambda i, j: (j, 0))],
        out_specs=pl.BlockSpec((tm, d), lambda i, j: (i, 0)),
        out_shape=jax.ShapeDtypeStruct((m, d), F32),
        compiler_params=_params("parallel", "arbitrary"),
        name="swiglu_ffn",
    )(h2, x1, wg, wu, wd)


def _pad_last(a, n):
    return jnp.pad(a, [(0, 0)] * (a.ndim - 1) + [(0, n - a.shape[-1])])


def _layer(x, norm1_g, w_in, q_norm_g, k_norm_g, cmp_pos_k, cmp_w1_k, cmp_b1_k, cmp_w2_k, cmp_b2_k,
           cmp_pos_v, cmp_w1_v, cmp_b1_v, cmp_w2_v, cmp_b2_v, conv_w, conv_b, lru_wa, lru_ba,
           lru_wi, lru_bi, lru_lambda, w_o_attn, w_o_lru, w_out, norm2_g, w_gate, w_up, w_down):
    b, t, d = x.shape
    m = b * t
    attn_dim = N_HEADS * HEAD_DIM
    kv_dim = N_KV * HEAD_DIM
    lru_w = lru_lambda.shape[0]
    assert t % 512 == 0
    o1 = attn_dim
    o2 = o1 + 6 * kv_dim
    o3 = o2 + 3 * N_HEADS
    o4 = o3 + lru_w
    o5 = o4 + lru_w
    cast = lambda a: a.astype(MXU_DTYPE)

    wq_t = cast(w_in[:, :o1].T)
    w_kv = w_in[:, o1:o2].reshape(d, 6, N_KV, HEAD_DIM)
    w_cmp_src = cast(w_kv[:, 0:2].reshape(d, 2 * kv_dim))
    w_ksel_pad = cast(_pad_last(w_kv[:, 2], FEAT).reshape(d, N_KV * FEAT))
    w_kwin_pad = cast(_pad_last(w_kv[:, 4], FEAT).reshape(d, N_KV * FEAT))
    w_v_t = cast(jnp.concatenate([w_kv[:, 3].reshape(d, kv_dim), w_kv[:, 5].reshape(d, kv_dim)], axis=1).T)
    w_g = w_in[:, o2:o3].reshape(d, N_KV, HPG, 3).transpose(0, 1, 3, 2).reshape(d, N_KV, 3 * HPG)
    w_g_t = cast(_pad_last(w_g, 16).reshape(d, N_KV * 16).T)
    w_lru = cast(w_in[:, o3:o5])
    w_mg = cast(w_in[:, o5:])
    q_gain_col = jnp.tile(q_norm_g * (HEAD_DIM ** -0.5), HPG).reshape(HPG * HEAD_DIM, 1)
    gain_pad = lambda g: _pad_last(g.reshape(1, HEAD_DIM), FEAT)

    h2d = _rmsnorm(x.reshape(m, d), norm1_g)
    h3d = h2d.reshape(b, t, d)
    q_t = _q_proj_t(h3d, wq_t, q_gain_col)
    v_t = _nt_proj(h3d, w_v_t, out_dtype=MXU_DTYPE, name="v_proj_t")
    gates_t = _nt_proj(h3d, w_g_t, act="sigmoid", name="gate_proj_t")
    k_sel = _key_features(h3d, w_ksel_pad, gain_pad(k_norm_g[1]), True, name="k_sel_features")
    k_win = _key_features(h3d, w_kwin_pad, gain_pad(k_norm_g[2]), False, name="k_win_features")
    cmp_src = _nn_proj(h2d, w_cmp_src, name="cmp_src_proj")
    xg = _nn_proj(h2d, w_lru, name="lru_proj").reshape(b, t, 2 * lru_w)
    mg = _nn_proj(h2d, w_mg, act="sigmoid", name="merge_gate_proj")

    ncp = t // CMP_STRIDE
    chunks = cmp_src.reshape(b, ncp, CMP_STRIDE, 2, N_KV, HEAD_DIM).transpose(3, 0, 4, 1, 2, 5)
    chunks = chunks.reshape(2, b, N_KV, ncp, CMP_STRIDE * HEAD_DIM)
    ck = CMP_STRIDE * HEAD_DIM
    k_cmp = _compress(chunks[0], cmp_pos_k.reshape(2, ck), cast(cmp_w1_k.reshape(2, ck, CMP_HIDDEN)),
                      cmp_b1_k.reshape(1, CMP_HIDDEN), cast(_pad_last(cmp_w2_k, FEAT)),
                      _pad_last(cmp_b2_k.reshape(1, HEAD_DIM), FEAT), gain_pad(k_norm_g[0]))
    v_cmp_t = _compress(chunks[1], cmp_pos_v.reshape(2, ck), cast(cmp_w1_v.reshape(2, ck, CMP_HIDDEN)),
                        cmp_b1_v.reshape(1, CMP_HIDDEN), cast(cmp_w2_v.T), cmp_b2_v.reshape(HEAD_DIM, 1))

    attn = _attention(q_t, gates_t, k_cmp, v_cmp_t, k_sel, k_win, v_t)

    eye = jnp.eye(LRU_BLOCKS, dtype=F32)
    bd = lambda wgt: cast(jnp.einsum('nkj,nm->nkmj', wgt, eye).reshape(lru_w, lru_w))
    vec = lambda v: v.reshape(1, lru_w)
    lru = _rglru(xg, conv_w.reshape(CONV_W, lru_w), vec(conv_b), bd(lru_wa), vec(lru_ba),
                 bd(lru_wi), vec(lru_bi), vec(lru_lambda))

    x1, h2 = _merge(attn.reshape(m, attn_dim), lru.reshape(m, lru_w), mg, x.reshape(m, d),
                    cast(w_o_attn), cast(w_o_lru), cast(w_out), norm2_g)
    out = _ffn(h2, x1, cast(w_gate), cast(w_up), cast(w_down))
    return out.reshape(b, t, d)


def kernel(x, norm1_g, w_in, q_norm_g, k_norm_g, cmp_pos_k, cmp_w1_k, cmp_b1_k, cmp_w2_k, cmp_b2_k,
           cmp_pos_v, cmp_w1_v, cmp_b1_v, cmp_w2_v, cmp_b2_v, conv_w, conv_b, lru_wa, lru_ba,
           lru_wi, lru_bi, lru_lambda, w_o_attn, w_o_lru, w_out, norm2_g, w_gate, w_up, w_down):
    for l in range(norm1_g.shape[0]):
        x = _layer(x, norm1_g[l], w_in[l], q_norm_g[l], k_norm_g[l], cmp_pos_k[l], cmp_w1_k[l],
                   cmp_b1_k[l], cmp_w2_k[l], cmp_b2_k[l], cmp_pos_v[l], cmp_w1_v[l], cmp_b1_v[l],
                   cmp_w2_v[l], cmp_b2_v[l], conv_w[l], conv_b[l], lru_wa[l], lru_ba[l], lru_wi[l],
                   lru_bi[l], lru_lambda[l], w_o_attn[l], w_o_lru[l], w_out[l], norm2_g[l],
                   w_gate[l], w_up[l], w_down[l])
    return x
```

```python
import functools

import numpy as np
import jax
import jax.numpy as jnp
from jax import lax
from jax.experimental import pallas as pl
from jax.experimental.pallas import tpu as pltpu

N_HEADS = 16
HEAD_DIM = 64
N_KV = 4
HPG = N_HEADS // N_KV
CMP_BLOCK = 32
CMP_STRIDE = 16
CMP_HIDDEN = 256
SEL_BLOCK = 64
N_SEL = 16
WINDOW = 512
FORCE_BONUS = 1e4
LRU_BLOCKS = 16
CONV_W = 4
LRU_C = 8.0
EPS = 1e-6

MXU_DTYPE = jnp.bfloat16
F32 = jnp.float32

V7X_LANES = 128
V7X_SUBLANES = 8
V7X_MXU_DIM = 256
V7X_VMEM_LIMIT_BYTES = 48 * 1024 * 1024

TQ = 128
TK_SEL = 512
WIN_KEYS = WINDOW + TQ
V_ROWS = 80
LOG2E = 1.4426950408889634
FEAT = V7X_MXU_DIM
F_SEL = HEAD_DIM
F_POS = 2 * HEAD_DIM
F_CMP = F_POS + 6
NEG_MASK = -1e30
NEG_BLOCK = -(2.0 ** 100)


def _params(*sem):
    return pltpu.CompilerParams(dimension_semantics=sem, vmem_limit_bytes=V7X_VMEM_LIMIT_BYTES)


def _gelu_tanh(x):
    return 0.5 * x * (1.0 + jnp.tanh(0.7978845608028654 * (x + 0.044715 * (x * x * x))))


def _sigmoid(x):
    return 1.0 / (1.0 + jnp.exp(-x))


def _neg_expm1(y):
    c = [1.0 / 2, 1.0 / 6, 1.0 / 24, 1.0 / 120, 1.0 / 720, 1.0 / 5040, 1.0 / 40320]
    poly = c[-1]
    for ck in reversed(c[:-1]):
        poly = poly * y + ck
    series = -y * (1.0 + y * poly)
    return jnp.where(y > -0.25, series, 1.0 - jnp.exp(y))


def _rmsnorm_kernel(x_ref, g_ref, o_ref):
    x = x_ref[...]
    ms = jnp.mean(x * x, axis=-1, keepdims=True)
    o_ref[...] = (x * lax.rsqrt(ms + EPS) * g_ref[...]).astype(o_ref.dtype)


def _rmsnorm(x2d, g, tm=512):
    m, d = x2d.shape
    return pl.pallas_call(
        _rmsnorm_kernel,
        grid=(m // tm,),
        in_specs=[pl.BlockSpec((tm, d), lambda i: (i, 0)),
                  pl.BlockSpec((1, d), lambda i: (0, 0))],
        out_specs=pl.BlockSpec((tm, d), lambda i: (i, 0)),
        out_shape=jax.ShapeDtypeStruct((m, d), MXU_DTYPE),
        compiler_params=_params("parallel"),
        name="rmsnorm",
    )(x2d, g.reshape(1, d))


def _nn_kernel(h_ref, w_ref, o_ref, *, act):
    r = jnp.dot(h_ref[...], w_ref[...], preferred_element_type=F32)
    if act == "sigmoid":
        r = _sigmoid(r)
    o_ref[...] = r.astype(o_ref.dtype)


def _nn_proj(h2d, w, act=None, out_dtype=F32, tm=512, tn=512, name="nn_proj"):
    m, k = h2d.shape
    n = w.shape[1]
    return pl.pallas_call(
        functools.partial(_nn_kernel, act=act),
        grid=(m // tm, n // tn),
        in_specs=[pl.BlockSpec((tm, k), lambda i, j: (i, 0)),
                  pl.BlockSpec((k, tn), lambda i, j: (0, j))],
        out_specs=pl.BlockSpec((tm, tn), lambda i, j: (i, j)),
        out_shape=jax.ShapeDtypeStruct((m, n), out_dtype),
        compiler_params=_params("parallel", "parallel"),
        name=name,
    )(h2d, w)


def _nt_dot(wt, h):
    return lax.dot_general(wt, h, (((1,), (1,)), ((), ())), preferred_element_type=F32)


def _qt_kernel(h_ref, wt_ref, g_ref, o_ref):
    r = _nt_dot(wt_ref[...], h_ref[0])
    tm = r.shape[1]
    r3 = r.reshape(HPG, HEAD_DIM, tm)
    ms = jnp.mean(r3 * r3, axis=1, keepdims=True)
    qn = (r3 * lax.rsqrt(ms + EPS)).reshape(HPG * HEAD_DIM, tm)
    o_ref[0] = (qn * g_ref[...]).astype(o_ref.dtype)


def _q_proj_t(h3d, wq_t, q_gain_col, tm=512):
    b, t, k = h3d.shape
    rows = HPG * HEAD_DIM
    return pl.pallas_call(
        _qt_kernel,
        grid=(b, t // tm, N_KV),
        in_specs=[pl.BlockSpec((1, tm, k), lambda bi, i, g: (bi, i, 0)),
                  pl.BlockSpec((rows, k), lambda bi, i, g: (g, 0)),
                  pl.BlockSpec((rows, 1), lambda bi, i, g: (0, 0))],
        out_specs=pl.BlockSpec((1, rows, tm), lambda bi, i, g: (bi, g, i)),
        out_shape=jax.ShapeDtypeStruct((b, N_HEADS * HEAD_DIM, t), MXU_DTYPE),
        compiler_params=_params("parallel", "parallel", "parallel"),
        name="q_proj_t",
    )(h3d, wq_t, q_gain_col)


def _nt_kernel(h_ref, wt_ref, bias_ref, o_ref, *, act):
    r = _nt_dot(wt_ref[...], h_ref[0]) + bias_ref[...]
    if act == "sigmoid":
        r = _sigmoid(r)
    o_ref[0] = r.astype(o_ref.dtype)


def _nt_proj(h3d, w_t, bias_col, act=None, out_dtype=F32, tm=512, name="nt_proj"):
    b, t, k = h3d.shape
    n = w_t.shape[0]
    return pl.pallas_call(
        functools.partial(_nt_kernel, act=act),
        grid=(b, t // tm),
        in_specs=[pl.BlockSpec((1, tm, k), lambda bi, i: (bi, i, 0)),
                  pl.BlockSpec((n, k), lambda bi, i: (0, 0)),
                  pl.BlockSpec((n, 1), lambda bi, i: (0, 0))],
        out_specs=pl.BlockSpec((1, n, tm), lambda bi, i: (bi, 0, i)),
        out_shape=jax.ShapeDtypeStruct((b, n, t), out_dtype),
        compiler_params=_params("parallel", "parallel"),
        name=name,
    )(h3d, w_t, bias_col)


def _token_features(pos, col, with_block_mask):
    blk = pos >> 6
    off = pos & (SEL_BLOCK - 1)
    feat = jnp.where((col >= F_POS) & (col < F_POS + 3), blk.astype(F32),
                     jnp.where((col >= F_POS + 3) & (col < F_POS + 6), off.astype(F32), 0.0))
    if with_block_mask:
        feat = jnp.where((col >= F_SEL) & (col - F_SEL == blk) & (col < F_POS), NEG_BLOCK, feat)
    return feat


def _kfeat_kernel(h_ref, w_ref, g_ref, o_ref, *, with_block_mask):
    r = jnp.dot(h_ref[0], w_ref[...], preferred_element_type=F32)
    tm = r.shape[0]
    ms = jnp.sum(r * r, axis=-1, keepdims=True) * (1.0 / HEAD_DIM)
    kn = r * lax.rsqrt(ms + EPS) * g_ref[...]
    pos = pl.program_id(1) * tm + lax.broadcasted_iota(jnp.int32, (tm, FEAT), 0)
    col = lax.broadcasted_iota(jnp.int32, (tm, FEAT), 1)
    o_ref[0, 0] = (kn + _token_features(pos, col, with_block_mask)).astype(o_ref.dtype)


def _key_features(h3d, w_pad, gain_pad, with_block_mask, tm=512, name="key_features"):
    b, t, k = h3d.shape
    return pl.pallas_call(
        functools.partial(_kfeat_kernel, with_block_mask=with_block_mask),
        grid=(b, t // tm, N_KV),
        in_specs=[pl.BlockSpec((1, tm, k), lambda bi, i, g: (bi, i, 0)),
                  pl.BlockSpec((k, FEAT), lambda bi, i, g: (0, g)),
                  pl.BlockSpec((1, FEAT), lambda bi, i, g: (0, 0))],
        out_specs=pl.BlockSpec((1, 1, tm, FEAT), lambda bi, i, g: (bi, g, i, 0)),
        out_shape=jax.ShapeDtypeStruct((b, N_KV, t, FEAT), MXU_DTYPE),
        compiler_params=_params("parallel", "parallel", "parallel"),
        name=name,
    )(h3d, w_pad, gain_pad)


def _cmp_hidden(c_ref, pos_ref, w1_ref, b1_ref):
    c = c_ref[0, 0]
    ncp = c.shape[0]
    ca = (c + pos_ref[0:1, :]).astype(MXU_DTYPE)
    cb = (c + pos_ref[1:2, :]).astype(MXU_DTYPE)
    first = jnp.dot(ca, w1_ref[0], preferred_element_type=F32)
    second = jnp.dot(cb, w1_ref[1], preferred_element_type=F32)
    hid = first + pltpu.roll(second, ncp - 1, 0) + b1_ref[...]
    return _gelu_tanh(hid).astype(MXU_DTYPE)


def _cmp_k_kernel(c_ref, pos_ref, w1_ref, b1_ref, w2_ref, b2_ref, g_ref, o_ref):
    hid = _cmp_hidden(c_ref, pos_ref, w1_ref, b1_ref)
    r = jnp.dot(hid, w2_ref[...], preferred_element_type=F32) + b2_ref[...]
    ncp = r.shape[0]
    ms = jnp.sum(r * r, axis=-1, keepdims=True) * (1.0 / HEAD_DIM)
    kn = r * lax.rsqrt(ms + EPS) * g_ref[...]
    idx = lax.broadcasted_iota(jnp.int32, (ncp, FEAT), 0)
    col = lax.broadcasted_iota(jnp.int32, (ncp, FEAT), 1)
    feat = jnp.where((col >= F_CMP) & (col < F_CMP + 3), (idx >> 6).astype(F32),
                     jnp.where((col >= F_CMP + 3) & (col < F_CMP + 6), (idx & 63).astype(F32), 0.0))
    o_ref[0, 0] = (kn + feat).astype(o_ref.dtype)


def _cmp_v_kernel(c_ref, pos_ref, w1_ref, b1_ref, w2t_ref, b2_ref, o_ref):
    hid = _cmp_hidden(c_ref, pos_ref, w1_ref, b1_ref)
    r = _nt_dot(w2t_ref[...], hid) + b2_ref[...]
    o_ref[0, 0] = r.astype(o_ref.dtype)


def _compress(chunks, pos2, w1, b1, w2, b2, gain_pad=None):
    b, g, ncp, ck = chunks.shape
    common = [pl.BlockSpec((1, 1, ncp, ck), lambda bi, gi: (bi, gi, 0, 0)),
              pl.BlockSpec((2, ck), lambda bi, gi: (0, 0)),
              pl.BlockSpec((2, ck, CMP_HIDDEN), lambda bi, gi: (0, 0, 0)),
              pl.BlockSpec((1, CMP_HIDDEN), lambda bi, gi: (0, 0))]
    if gain_pad is not None:
        return pl.pallas_call(
            _cmp_k_kernel,
            grid=(b, g),
            in_specs=common + [pl.BlockSpec((CMP_HIDDEN, FEAT), lambda bi, gi: (0, 0)),
                               pl.BlockSpec((1, FEAT), lambda bi, gi: (0, 0)),
                               pl.BlockSpec((1, FEAT), lambda bi, gi: (0, 0))],
            out_specs=pl.BlockSpec((1, 1, ncp, FEAT), lambda bi, gi: (bi, gi, 0, 0)),
            out_shape=jax.ShapeDtypeStruct((b, g, ncp, FEAT), MXU_DTYPE),
            compiler_params=_params("parallel", "parallel"),
            name="compress_k",
        )(chunks, pos2, w1, b1, w2, b2, gain_pad)
    return pl.pallas_call(
        _cmp_v_kernel,
        grid=(b, g),
        in_specs=common + [pl.BlockSpec((HEAD_DIM, CMP_HIDDEN), lambda bi, gi: (0, 0)),
                           pl.BlockSpec((HEAD_DIM, 1), lambda bi, gi: (0, 0))],
        out_specs=pl.BlockSpec((1, 1, HEAD_DIM, ncp), lambda bi, gi: (bi, gi, 0, 0)),
        out_shape=jax.ShapeDtypeStruct((b, g, HEAD_DIM, ncp), MXU_DTYPE),
        compiler_params=_params("parallel", "parallel"),
        name="compress_v",
    )(chunks, pos2, w1, b1, w2, b2)


def _split3(v):
    parts = []
    rest = np.asarray(v, np.float64)
    for _ in range(3):
        p = rest.astype(np.float32).astype(jnp.bfloat16).astype(np.float64)
        parts.append(p)
        rest = rest - p
    return parts


def _alibi_query_features():
    tab = np.zeros((N_KV, FEAT - F_POS, HPG * TQ), np.float64)
    for g in range(N_KV):
        for h in range(HPG):
            slope = 2.0 ** (-8.0 * (g * HPG + h + 1) / N_HEADS)
            parts = _split3(slope * LOG2E)
            lanes = slice(h * TQ, (h + 1) * TQ)
            for i, p in enumerate(parts):
                tab[g, i, lanes] = SEL_BLOCK * p
                tab[g, 3 + i, lanes] = p
                tab[g, 6 + i, lanes] = CMP_STRIDE * 64 * p
                tab[g, 9 + i, lanes] = CMP_STRIDE * p
    return jnp.asarray(tab, F32).astype(MXU_DTYPE)


def _block_map_t(n_cmp_pad, n_blk):
    cs = np.arange(n_cmp_pad) * CMP_STRIDE
    ce = cs + CMP_BLOCK - 1
    bs = np.arange(n_blk) * SEL_BLOCK
    be = bs + SEL_BLOCK - 1
    return jnp.asarray(((cs[None, :] <= be[:, None]) & (ce[None, :] >= bs[:, None])).astype(np.float32))


def _attn_kernel(qt_ref, gate_ref, kc_ref, vc_ref, ks_ref, vs_ref, kw_ref, vw_ref, alibi_ref, map_ref,
                 o_ref, qb_ref, qs_ref, imp_ref, s0_ref, s1_ref, p_ref, *, n_blk, n_sel):
    lanes = HPG * TQ
    qi = pl.program_id(2)
    q0 = pl.multiple_of(qi * TQ, TQ)
    ncp = kc_ref.shape[2]

    for h in range(HPG):
        qb_ref[0:HEAD_DIM, h * TQ:(h + 1) * TQ] = qt_ref[0, h * HEAD_DIM:(h + 1) * HEAD_DIM, :]
    qb_ref[F_SEL:F_POS, :] = jnp.zeros((F_POS - F_SEL, lanes), qb_ref.dtype)
    qb_ref[F_POS:FEAT, :] = alibi_ref[0]
    qb = qb_ref[...]

    t_lane = q0 + (lax.broadcasted_iota(jnp.int32, (1, lanes), 1) & (TQ - 1))

    sc = jnp.dot(kc_ref[0, 0], qb, preferred_element_type=F32)
    cmp_end = lax.broadcasted_iota(jnp.int32, (ncp, lanes), 0) * CMP_STRIDE + (CMP_BLOCK - 1)
    cmask = t_lane >= cmp_end
    sc = jnp.where(cmask, sc, NEG_MASK)
    mc = jnp.max(sc, axis=0, keepdims=True)
    ec = jnp.where(cmask, jnp.exp2(sc - mc), 0.0)
    lc = jnp.sum(ec, axis=0, keepdims=True)
    pc = ec * (1.0 / jnp.maximum(lc, 1e-30))
    o_cmp = jnp.dot(vc_ref[0, 0], pc.astype(MXU_DTYPE), preferred_element_type=F32)

    psum = pc[:, 0:TQ]
    for h in range(1, HPG):
        psum = psum + pc[:, h * TQ:(h + 1) * TQ]
    imp = jnp.dot(map_ref[...], psum, preferred_element_type=F32)
    t_q = q0 + lax.broadcasted_iota(jnp.int32, (n_blk, TQ), 1)
    blk = lax.broadcasted_iota(jnp.int32, (n_blk, TQ), 0)
    cur = t_q >> 6
    forced = (blk == 0) | (blk == cur) | (blk == cur - 1)
    valid = blk <= cur
    imp = jnp.where(valid, imp + jnp.where(forced, FORCE_BONUS, 0.0), NEG_MASK)
    imp_ref[...] = imp
    n_chunks = n_blk // V7X_SUBLANES
    chunks = [imp[c * V7X_SUBLANES:(c + 1) * V7X_SUBLANES] for c in range(n_chunks)]
    ranks = [jnp.zeros((V7X_SUBLANES, TQ), jnp.int32) for _ in range(n_chunks)]
    sub = lax.broadcasted_iota(jnp.int32, (V7X_SUBLANES, TQ), 0)
    for k in range(n_blk):
        row = imp_ref[k:k + 1, :]
        for c in range(n_chunks):
            lo = c * V7X_SUBLANES
            if lo > k:
                one = jnp.where(row >= chunks[c], 1, 0)
            elif lo + V7X_SUBLANES - 1 <= k:
                one = jnp.where(row > chunks[c], 1, 0)
            else:
                one = jnp.where(sub + lo > k, jnp.where(row >= chunks[c], 1, 0),
                                jnp.where(row > chunks[c], 1, 0))
            ranks[c] = ranks[c] + one
    rank = jnp.concatenate(ranks, axis=0)
    first_own_blk = qi * (TQ // SEL_BLOCK)
    not_sel = jnp.where((rank < n_sel) & (blk < first_own_blk), 0.0, 1.0).astype(qs_ref.dtype)
    qs_ref[...] = qb
    for h in range(HPG):
        qs_ref[F_SEL:F_SEL + n_blk, h * TQ:(h + 1) * TQ] = not_sel

    w0 = pl.multiple_of(jnp.maximum(q0 - WINDOW, 0), TQ)
    sw = jnp.dot(kw_ref[0, 0, pl.ds(w0, WIN_KEYS), :], qb, preferred_element_type=F32)
    dw = t_lane - (w0 + lax.broadcasted_iota(jnp.int32, (WIN_KEYS, lanes), 0))
    sw = jnp.where(dw >= 0, jnp.where(dw < WINDOW, sw, NEG_MASK), NEG_MASK)
    pw = jnp.exp2(sw - jnp.max(sw, axis=0, keepdims=True))
    acc_w = jnp.dot(vw_ref[0, :, pl.ds(w0, WIN_KEYS)], pw.astype(MXU_DTYPE), preferred_element_type=F32)
    o_win = acc_w[0:HEAD_DIM] * (1.0 / acc_w[HEAD_DIM:HEAD_DIM + 1])

    def qk_chunk(c):
        k0 = pl.multiple_of(c * TK_SEL, TK_SEL)
        return jnp.dot(ks_ref[0, 0, pl.ds(k0, TK_SEL), :], qs_ref[...], preferred_element_type=F32)

    def pv_chunk(c):
        k0 = pl.multiple_of(c * TK_SEL, TK_SEL)
        return jnp.dot(vs_ref[0, :, pl.ds(k0, TK_SEL)], p_ref[...], preferred_element_type=F32)

    def stage(k, carry, s_cur_ref, s_next_ref):
        m, acc = carry
        pv_prev = pv_chunk(jnp.maximum(k - 1, 0))
        if s_next_ref is not None:
            s_next_ref[...] = qk_chunk(k + 1)
        s = s_cur_ref[...]
        m_new = jnp.maximum(m, jnp.max(s, axis=0, keepdims=True))
        p_ref[...] = jnp.exp2(s - m_new).astype(p_ref.dtype)
        return m_new, jnp.exp2(m - m_new) * (acc + pv_prev)

    n_pairs = jnp.maximum((qi * TQ + 2 * TK_SEL - 1) // (2 * TK_SEL), 1)
    s0_ref[...] = qk_chunk(0)
    p_ref[...] = jnp.zeros(p_ref.shape, p_ref.dtype)

    def pair_body(j, carry):
        carry = stage(2 * j, carry, s0_ref, s1_ref)
        return stage(2 * j + 1, carry, s1_ref, s0_ref)

    carry = (jnp.full((1, lanes), NEG_MASK, F32), jnp.zeros((V_ROWS, lanes), F32))
    carry = lax.fori_loop(0, n_pairs - 1, pair_body, carry)
    last = 2 * n_pairs - 1
    carry = stage(last - 1, carry, s0_ref, s1_ref)
    m_s, acc_s = stage(last, carry, s1_ref, None)
    acc_s = acc_s + pv_chunk(last)

    sd = jnp.dot(ks_ref[0, 0, pl.ds(q0, TQ), :], qb, preferred_element_type=F32)
    kd = q0 + lax.broadcasted_iota(jnp.int32, (TQ, lanes), 0)
    sd = jnp.where(kd <= t_lane, sd, NEG_MASK)
    m_d = jnp.maximum(m_s, jnp.max(sd, axis=0, keepdims=True))
    pd = jnp.exp2(sd - m_d).astype(MXU_DTYPE)
    acc_s = jnp.exp2(m_s - m_d) * acc_s + jnp.dot(vs_ref[0, :, pl.ds(q0, TQ)], pd, preferred_element_type=F32)
    o_sel = acc_s[0:HEAD_DIM] * (1.0 / acc_s[HEAD_DIM:HEAD_DIM + 1])

    gates = gate_ref[0]
    def gate_row(j):
        return jnp.concatenate([gates[j * HPG + h:j * HPG + h + 1, :] for h in range(HPG)], axis=1)
    o_t = gate_row(0) * o_cmp + gate_row(1) * o_sel + gate_row(2) * o_win
    for hp in range(HPG // 2):
        pair = jnp.concatenate([o_t[:, (2 * hp) * TQ:(2 * hp + 1) * TQ],
                                o_t[:, (2 * hp + 1) * TQ:(2 * hp + 2) * TQ]], axis=0)
        o_ref[0, :, hp * 2 * HEAD_DIM:(hp + 1) * 2 * HEAD_DIM] = pair.T.astype(o_ref.dtype)


def _attention(q_t, gates_t, k_cmp, v_cmp_t, k_sel, k_win, v_t):
    b, _, t = q_t.shape
    ncp = k_cmp.shape[2]
    n_blk = t // SEL_BLOCK
    assert t % (2 * TK_SEL) == 0 and F_SEL + n_blk <= F_POS, "unsupported sequence length"
    n_sel = min(N_SEL, n_blk)
    lanes = HPG * TQ
    rows = HPG * HEAD_DIM
    alibi = _alibi_query_features()
    blk_map_t = _block_map_t(ncp, n_blk)
    kernel = functools.partial(_attn_kernel, n_blk=n_blk, n_sel=n_sel)
    return pl.pallas_call(
        kernel,
        grid=(b, N_KV, t // TQ),
        in_specs=[
            pl.BlockSpec((1, rows, TQ), lambda bi, g, i: (bi, g, i)),
            pl.BlockSpec((1, 16, TQ), lambda bi, g, i: (bi, g, i)),
            pl.BlockSpec((1, 1, ncp, FEAT), lambda bi, g, i: (bi, g, 0, 0)),
            pl.BlockSpec((1, 1, HEAD_DIM, ncp), lambda bi, g, i: (bi, g, 0, 0)),
            pl.BlockSpec((1, 1, t, FEAT), lambda bi, g, i: (bi, g, 0, 0)),
            pl.BlockSpec((1, V_ROWS, t), lambda bi, g, i: (bi, g, 0)),
            pl.BlockSpec((1, 1, t, FEAT), lambda bi, g, i: (bi, g, 0, 0)),
            pl.BlockSpec((1, V_ROWS, t), lambda bi, g, i: (bi, N_KV + g, 0)),
            pl.BlockSpec((1, FEAT - F_POS, lanes), lambda bi, g, i: (g, 0, 0)),
            pl.BlockSpec((n_blk, ncp), lambda bi, g, i: (0, 0)),
        ],
        out_specs=pl.BlockSpec((1, TQ, rows), lambda bi, g, i: (bi, i, g)),
        out_shape=jax.ShapeDtypeStruct((b, t, N_HEADS * HEAD_DIM), MXU_DTYPE),
        scratch_shapes=[pltpu.VMEM((FEAT, lanes), MXU_DTYPE),
                        pltpu.VMEM((FEAT, lanes), MXU_DTYPE),
                        pltpu.VMEM((n_blk, TQ), F32),
                        pltpu.VMEM((TK_SEL, lanes), F32),
                        pltpu.VMEM((TK_SEL, lanes), F32),
                        pltpu.VMEM((TK_SEL, lanes), MXU_DTYPE)],
        compiler_params=_params("parallel", "parallel", "arbitrary"),
        name="nsa_attention",
    )(q_t, gates_t, k_cmp, v_cmp_t, k_sel, v_t, k_win, v_t, alibi, blk_map_t)


def _lru_kernel(x_ref, gate_ref, cw_ref, cb_ref, wa_ref, ba_ref, wi_ref, bi_ref, lam_ref, o_ref,
                tail_ref, h_ref):
    tt, w = x_ref.shape[1], x_ref.shape[2]

    @pl.when(pl.program_id(1) == 0)
    def _():
        tail_ref[...] = jnp.zeros_like(tail_ref)
        h_ref[...] = jnp.zeros_like(h_ref)

    x = x_ref[0]
    prev = tail_ref[...]
    row8 = lax.broadcasted_iota(jnp.int32, (V7X_SUBLANES, w), 0)
    xc = x * cw_ref[CONV_W - 1:CONV_W, :] + cb_ref[...]
    for s in range(1, CONV_W):
        xs = pltpu.roll(x, s, 0)
        head = jnp.where(row8 < s, pltpu.roll(prev, s, 0), xs[0:V7X_SUBLANES])
        xs = jnp.concatenate([head, xs[V7X_SUBLANES:]], axis=0)
        xc = xc + xs * cw_ref[CONV_W - 1 - s:CONV_W - s, :]
    tail_ref[...] = x[tt - V7X_SUBLANES:tt]

    xb = xc.astype(MXU_DTYPE)
    r = _sigmoid(jnp.dot(xb, wa_ref[...], preferred_element_type=F32) + ba_ref[...])
    i = _sigmoid(jnp.dot(xb, wi_ref[...], preferred_element_type=F32) + bi_ref[...])
    z = -lam_ref[...]
    softplus = jnp.maximum(z, 0.0) + jnp.log1p(jnp.exp(-jnp.abs(z)))
    log_a = -LRU_C * r * softplus
    a = jnp.exp(log_a)
    bb = jnp.sqrt(_neg_expm1(2.0 * log_a)) * (i * xc)

    row = lax.broadcasted_iota(jnp.int32, (tt, w), 0)
    d = 1
    while d < tt:
        ok = row >= d
        a_sh = pltpu.roll(a, d, 0)
        b_sh = pltpu.roll(bb, d, 0)
        bb = jnp.where(ok, a * b_sh + bb, bb)
        a = jnp.where(ok, a * a_sh, a)
        d *= 2
    hcur = bb + a * h_ref[0:1, :]
    h_ref[...] = jnp.broadcast_to(hcur[tt - 1:tt, :], h_ref.shape)
    o_ref[0] = (hcur * _gelu_tanh(gate_ref[0])).astype(o_ref.dtype)


def _rglru(xg, conv_w, conv_b, wa_bd, ba, wi_bd, bi, lam, tt=256):
    b, t, w2 = xg.shape
    w = w2 // 2
    vec = lambda: pl.BlockSpec((1, w), lambda bi_, i: (0, 0))
    return pl.pallas_call(
        _lru_kernel,
        grid=(b, t // tt),
        in_specs=[pl.BlockSpec((1, tt, w), lambda bi_, i: (bi_, i, 0)),
                  pl.BlockSpec((1, tt, w), lambda bi_, i: (bi_, i, 1)),
                  pl.BlockSpec((CONV_W, w), lambda bi_, i: (0, 0)), vec(),
                  pl.BlockSpec((w, w), lambda bi_, i: (0, 0)), vec(),
                  pl.BlockSpec((w, w), lambda bi_, i: (0, 0)), vec(), vec()],
        out_specs=pl.BlockSpec((1, tt, w), lambda bi_, i: (bi_, i, 0)),
        out_shape=jax.ShapeDtypeStruct((b, t, w), MXU_DTYPE),
        scratch_shapes=[pltpu.VMEM((V7X_SUBLANES, w), F32), pltpu.VMEM((V7X_SUBLANES, w), F32)],
        compiler_params=_params("parallel", "arbitrary"),
        name="rglru",
    )(xg, xg, conv_w, conv_b, wa_bd, ba, wi_bd, bi, lam)


def _merge_kernel(attn_ref, lru_ref, mg0_ref, mg1_ref, x_ref, wa_ref, wl_ref, wo_ref, g2_ref,
                  x1_ref, h2_ref):
    ya = jnp.dot(attn_ref[...], wa_ref[...], preferred_element_type=F32)
    yl = jnp.dot(lru_ref[...], wl_ref[...], preferred_element_type=F32)
    merged = mg0_ref[...] * ya + mg1_ref[...] * yl
    x1 = x_ref[...] + jnp.dot(merged.astype(MXU_DTYPE), wo_ref[...], preferred_element_type=F32)
    x1_ref[...] = x1
    ms = jnp.mean(x1 * x1, axis=-1, keepdims=True)
    h2_ref[...] = (x1 * lax.rsqrt(ms + EPS) * g2_ref[...]).astype(h2_ref.dtype)


def _merge(attn, lru, mg, x2d, wa, wl, wo, g2, tm=256):
    m, d = x2d.shape
    row = lambda j: pl.BlockSpec((tm, d), lambda i: (i, j))
    full = lambda r: pl.BlockSpec((r, d), lambda i: (0, 0))
    return pl.pallas_call(
        _merge_kernel,
        grid=(m // tm,),
        in_specs=[row(0), row(0), row(0), row(1), row(0), full(d), full(d), full(d), full(1)],
        out_specs=[row(0), row(0)],
        out_shape=[jax.ShapeDtypeStruct((m, d), F32), jax.ShapeDtypeStruct((m, d), MXU_DTYPE)],
        compiler_params=_params("parallel"),
        name="merge_out",
    )(attn, lru, mg, mg, x2d, wa, wl, wo, g2.reshape(1, d))


def _ffn_kernel(h_ref, x1_ref, wg_ref, wu_ref, wd_ref, o_ref):
    @pl.when(pl.program_id(1) == 0)
    def _():
        o_ref[...] = x1_ref[...]

    h = h_ref[...]
    g = jnp.dot(h, wg_ref[...], preferred_element_type=F32)
    u = jnp.dot(h, wu_ref[...], preferred_element_type=F32)
    act = (g * _sigmoid(g) * u).astype(MXU_DTYPE)
    o_ref[...] += jnp.dot(act, wd_ref[...], preferred_element_type=F32)


def _ffn(h2, x1, wg, wu, wd, tm=1024, tf=256):
    m, d = x1.shape
    f = wg.shape[1]
    return pl.pallas_call(
        _ffn_kernel,
        grid=(m // tm, f // tf),
        in_specs=[pl.BlockSpec((tm, d), lambda i, j: (i, 0)),
                  pl.BlockSpec((tm, d), lambda i, j: (i, 0)),
                  pl.BlockSpec((d, tf), lambda i, j: (0, j)),
                  pl.BlockSpec((d, tf), lambda i, j: (0, j)),
                  pl.BlockSpec((tf, d), lambda i, j: (j, 0))],
        out_specs=pl.BlockSpec((tm, d), lambda i, j: (i, 0)),
        out_shape=jax.ShapeDtypeStruct((m, d), F32),
        compiler_params=_params("parallel", "arbitrary"),
        name="swiglu_ffn",
    )(h2, x1, wg, wu, wd)


def _pad_last(a, n):
    return jnp.pad(a, [(0, 0)] * (a.ndim - 1) + [(0, n - a.shape[-1])])


def _layer(x, norm1_g, w_in, q_norm_g, k_norm_g, cmp_pos_k, cmp_w1_k, cmp_b1_k, cmp_w2_k, cmp_b2_k,
           cmp_pos_v, cmp_w1_v, cmp_b1_v, cmp_w2_v, cmp_b2_v, conv_w, conv_b, lru_wa, lru_ba,
           lru_wi, lru_bi, lru_lambda, w_o_attn, w_o_lru, w_out, norm2_g, w_gate, w_up, w_down):
    b, t, d = x.shape
    m = b * t
    attn_dim = N_HEADS * HEAD_DIM
    kv_dim = N_KV * HEAD_DIM
    lru_w = lru_lambda.shape[0]
    assert t % 512 == 0
    o1 = attn_dim
    o2 = o1 + 6 * kv_dim
    o3 = o2 + 3 * N_HEADS
    o4 = o3 + lru_w
    o5 = o4 + lru_w
    cast = lambda a: a.astype(MXU_DTYPE)

    wq_t = cast(w_in[:, :o1].T)
    w_kv = w_in[:, o1:o2].reshape(d, 6, N_KV, HEAD_DIM)
    w_cmp_src = cast(w_kv[:, 0:2].reshape(d, 2 * kv_dim))
    w_ksel_pad = cast(_pad_last(w_kv[:, 2], FEAT).reshape(d, N_KV * FEAT))
    w_kwin_pad = cast(_pad_last(w_kv[:, 4], FEAT).reshape(d, N_KV * FEAT))
    w_v = jnp.stack([w_kv[:, 3], w_kv[:, 5]], axis=1)
    w_v_t = cast(_pad_last(w_v, V_ROWS).reshape(d, 2 * N_KV * V_ROWS).T)
    v_ones_col = jnp.tile(jnp.arange(V_ROWS) == HEAD_DIM, 2 * N_KV).astype(F32).reshape(-1, 1)
    w_g = w_in[:, o2:o3].reshape(d, N_KV, HPG, 3).transpose(0, 1, 3, 2).reshape(d, N_KV, 3 * HPG)
    w_g_t = cast(_pad_last(w_g, 16).reshape(d, N_KV * 16).T)
    w_lru = cast(w_in[:, o3:o5])
    w_mg = cast(w_in[:, o5:])
    q_gain_col = jnp.tile(q_norm_g * (HEAD_DIM ** -0.5 * LOG2E), HPG).reshape(HPG * HEAD_DIM, 1)
    gain_pad = lambda g: _pad_last(g.reshape(1, HEAD_DIM), FEAT)

    h2d = _rmsnorm(x.reshape(m, d), norm1_g)
    h3d = h2d.reshape(b, t, d)
    q_t = _q_proj_t(h3d, wq_t, q_gain_col)
    v_t = _nt_proj(h3d, w_v_t, v_ones_col, out_dtype=MXU_DTYPE, name="v_proj_t")
    gates_t = _nt_proj(h3d, w_g_t, jnp.zeros((w_g_t.shape[0], 1), F32), act="sigmoid",
                       name="gate_proj_t")
    k_sel = _key_features(h3d, w_ksel_pad, gain_pad(k_norm_g[1]), True, name="k_sel_features")
    k_win = _key_features(h3d, w_kwin_pad, gain_pad(k_norm_g[2]), False, name="k_win_features")
    cmp_src = _nn_proj(h2d, w_cmp_src, name="cmp_src_proj")
    xg = _nn_proj(h2d, w_lru, name="lru_proj").reshape(b, t, 2 * lru_w)
    mg = _nn_proj(h2d, w_mg, act="sigmoid", name="merge_gate_proj")

    ncp = t // CMP_STRIDE
    chunks = cmp_src.reshape(b, ncp, CMP_STRIDE, 2, N_KV, HEAD_DIM).transpose(3, 0, 4, 1, 2, 5)
    chunks = chunks.reshape(2, b, N_KV, ncp, CMP_STRIDE * HEAD_DIM)
    ck = CMP_STRIDE * HEAD_DIM
    k_cmp = _compress(chunks[0], cmp_pos_k.reshape(2, ck), cast(cmp_w1_k.reshape(2, ck, CMP_HIDDEN)),
                      cmp_b1_k.reshape(1, CMP_HIDDEN), cast(_pad_last(cmp_w2_k, FEAT)),
                      _pad_last(cmp_b2_k.reshape(1, HEAD_DIM), FEAT), gain_pad(k_norm_g[0]))
    v_cmp_t = _compress(chunks[1], cmp_pos_v.reshape(2, ck), cast(cmp_w1_v.reshape(2, ck, CMP_HIDDEN)),
                        cmp_b1_v.reshape(1, CMP_HIDDEN), cast(cmp_w2_v.T), cmp_b2_v.reshape(HEAD_DIM, 1))

    attn = _attention(q_t, gates_t, k_cmp, v_cmp_t, k_sel, k_win, v_t)

    eye = jnp.eye(LRU_BLOCKS, dtype=F32)
    bd = lambda wgt: cast(jnp.einsum('nkj,nm->nkmj', wgt, eye).reshape(lru_w, lru_w))
    vec = lambda v: v.reshape(1, lru_w)
    lru = _rglru(xg, conv_w.reshape(CONV_W, lru_w), vec(conv_b), bd(lru_wa), vec(lru_ba),
                 bd(lru_wi), vec(lru_bi), vec(lru_lambda))

    x1, h2 = _merge(attn.reshape(m, attn_dim), lru.reshape(m, lru_w), mg, x.reshape(m, d),
                    cast(w_o_attn), cast(w_o_lru), cast(w_out), norm2_g)
    out = _ffn(h2, x1, cast(w_gate), cast(w_up), cast(w_down))
    return out.reshape(b, t, d)


def kernel(x, norm1_g, w_in, q_norm_g, k_norm_g, cmp_pos_k, cmp_w1_k, cmp_b1_k, cmp_w2_k, cmp_b2_k,
           cmp_pos_v, cmp_w1_v, cmp_b1_v, cmp_w2_v, cmp_b2_v, conv_w, conv_b, lru_wa, lru_ba,
           lru_wi, lru_bi, lru_lambda, w_o_attn, w_o_lru, w_out, norm2_g, w_gate, w_up, w_down):
    for l in range(norm1_g.shape[0]):
        x = _layer(x, norm1_g[l], w_in[l], q_norm_g[l], k_norm_g[l], cmp_pos_k[l], cmp_w1_k[l],
                   cmp_b1_k[l], cmp_w2_k[l], cmp_b2_k[l], cmp_pos_v[l], cmp_w1_v[l], cmp_b1_v[l],
                   cmp_w2_v[l], cmp_b2_v[l], conv_w[l], conv_b[l], lru_wa[l], lru_ba[l], lru_wi[l],
                   lru_bi[l], lru_lambda[l], w_o_attn[l], w_o_lru[l], w_out[l], norm2_g[l],
                   w_gate[l], w_up[l], w_down[l])
    return x
```

```python
import functools

import numpy as np
import jax
import jax.numpy as jnp
from jax import lax
from jax.experimental import pallas as pl
from jax.experimental.pallas import tpu as pltpu

N_HEADS = 16
HEAD_DIM = 64
N_KV = 4
HPG = N_HEADS // N_KV
CMP_BLOCK = 32
CMP_STRIDE = 16
CMP_HIDDEN = 256
SEL_BLOCK = 64
N_SEL = 16
WINDOW = 512
FORCE_BONUS = 1e4
LRU_BLOCKS = 16
CONV_W = 4
LRU_C = 8.0
EPS = 1e-6

MXU_DTYPE = jnp.bfloat16
F32 = jnp.float32

V7X_LANES = 128
V7X_SUBLANES = 8
V7X_MXU_DIM = 256
V7X_VMEM_LIMIT_BYTES = 48 * 1024 * 1024

TQ = 256
TK_SEL = 512
WIN_KEYS = WINDOW + TQ
V_ROWS = 80
LOG2E = 1.4426950408889634
FEAT = V7X_MXU_DIM
F_SEL = HEAD_DIM
F_POS = 2 * HEAD_DIM
F_CMP = F_POS + 6
NEG_MASK = -1e30
NEG_BLOCK = -(2.0 ** 100)


def _params(*sem):
    return pltpu.CompilerParams(dimension_semantics=sem, vmem_limit_bytes=V7X_VMEM_LIMIT_BYTES)


def _gelu_tanh(x):
    return 0.5 * x * (1.0 + jnp.tanh(0.7978845608028654 * (x + 0.044715 * (x * x * x))))


def _sigmoid(x):
    return 1.0 / (1.0 + jnp.exp(-x))


def _neg_expm1(y):
    c = [1.0 / 2, 1.0 / 6, 1.0 / 24, 1.0 / 120, 1.0 / 720, 1.0 / 5040, 1.0 / 40320]
    poly = c[-1]
    for ck in reversed(c[:-1]):
        poly = poly * y + ck
    series = -y * (1.0 + y * poly)
    return jnp.where(y > -0.25, series, 1.0 - jnp.exp(y))


def _nt_dot(wt, h):
    return lax.dot_general(wt, h, (((1,), (1,)), ((), ())), preferred_element_type=F32)


def _token_features(pos, col, with_block_mask):
    blk = pos >> 6
    off = pos & (SEL_BLOCK - 1)
    feat = jnp.where((col >= F_POS) & (col < F_POS + 3), blk.astype(F32),
                     jnp.where((col >= F_POS + 3) & (col < F_POS + 6), off.astype(F32), 0.0))
    if with_block_mask:
        feat = jnp.where((col >= F_SEL) & (col - F_SEL == blk) & (col < F_POS), NEG_BLOCK, feat)
    return feat


def _in_proj_kernel(x_ref, g1_ref, wq_ref, qg_ref, wv_ref, vb_ref, wg_ref, wk_ref, kg_ref, grp_ref,
                    place_ref, wc_ref, h_ref, qt_ref, vt_ref, gt_ref, ks_ref, kw_ref, cs_ref):
    x = x_ref[0]
    tm = x.shape[0]
    ms = jnp.mean(x * x, axis=-1, keepdims=True)
    h = (x * lax.rsqrt(ms + EPS) * g1_ref[...]).astype(MXU_DTYPE)
    h_ref[0] = h

    r3 = _nt_dot(wq_ref[...], h).reshape(N_HEADS, HEAD_DIM, tm)
    qn = r3 * lax.rsqrt(jnp.mean(r3 * r3, axis=1, keepdims=True) + EPS)
    qt_ref[0] = (qn.reshape(N_HEADS * HEAD_DIM, tm) * qg_ref[...]).astype(qt_ref.dtype)

    vt_ref[0] = (_nt_dot(wv_ref[...], h) + vb_ref[...]).astype(vt_ref.dtype)
    gt_ref[0] = _sigmoid(_nt_dot(wg_ref[...], h))

    k = jnp.dot(h, wk_ref[...], preferred_element_type=F32)
    kk = k * k
    kk_hi = kk.astype(MXU_DTYPE)
    kk_lo = (kk - kk_hi.astype(F32)).astype(MXU_DTYPE)
    pos = pl.program_id(1) * tm + lax.broadcasted_iota(jnp.int32, (tm, FEAT), 0)
    col = lax.broadcasted_iota(jnp.int32, (tm, FEAT), 1)
    kv_dim = N_KV * HEAD_DIM
    for branch, (o_ref, with_block_mask) in enumerate(((ks_ref, True), (kw_ref, False))):
        sl = slice(branch * kv_dim, (branch + 1) * kv_dim)
        msq = (jnp.dot(kk_hi[:, sl], grp_ref[...], preferred_element_type=F32)
               + jnp.dot(kk_lo[:, sl], grp_ref[...], preferred_element_type=F32))
        kn = (k[:, sl] * lax.rsqrt(msq + EPS) * kg_ref[branch:branch + 1, :]).astype(MXU_DTYPE)
        placed = jnp.dot(kn, place_ref[...], preferred_element_type=F32)
        feat = _token_features(pos, col, with_block_mask)
        for g in range(N_KV):
            o_ref[0, g] = (placed[:, g * FEAT:(g + 1) * FEAT] + feat).astype(o_ref.dtype)

    cs_ref[0] = jnp.dot(h, wc_ref[...], preferred_element_type=F32)


def _in_proj(x, g1, wq_t, q_gain_col, w_v_t, v_bias_col, w_g_t, w_k2, k_gain2, grp_avg, place, w_cmp_src,
             tm=512):
    b, t, d = x.shape
    kv_dim = N_KV * HEAD_DIM
    full = lambda a: pl.BlockSpec(a.shape, lambda bi, i: (0,) * a.ndim)
    rowblk = lambda n: pl.BlockSpec((1, tm, n), lambda bi, i: (bi, i, 0))
    colblk = lambda n: pl.BlockSpec((1, n, tm), lambda bi, i: (bi, 0, i))
    kblk = pl.BlockSpec((1, N_KV, tm, FEAT), lambda bi, i: (bi, 0, i, 0))
    weights = (g1, wq_t, q_gain_col, w_v_t, v_bias_col, w_g_t, w_k2, k_gain2, grp_avg, place, w_cmp_src)
    return pl.pallas_call(
        _in_proj_kernel,
        grid=(b, t // tm),
        in_specs=[rowblk(d)] + [full(a) for a in weights],
        out_specs=[rowblk(d), colblk(wq_t.shape[0]), colblk(w_v_t.shape[0]), colblk(w_g_t.shape[0]),
                   kblk, kblk, rowblk(2 * kv_dim)],
        out_shape=[jax.ShapeDtypeStruct((b, t, d), MXU_DTYPE),
                   jax.ShapeDtypeStruct((b, wq_t.shape[0], t), MXU_DTYPE),
                   jax.ShapeDtypeStruct((b, w_v_t.shape[0], t), MXU_DTYPE),
                   jax.ShapeDtypeStruct((b, w_g_t.shape[0], t), F32),
                   jax.ShapeDtypeStruct((b, N_KV, t, FEAT), MXU_DTYPE),
                   jax.ShapeDtypeStruct((b, N_KV, t, FEAT), MXU_DTYPE),
                   jax.ShapeDtypeStruct((b, t, 2 * kv_dim), F32)],
        compiler_params=_params("parallel", "parallel"),
        name="in_proj",
    )(x, *weights)


def _cmp_hidden(c_ref, pos_ref, w1_ref, b1_ref):
    c = c_ref[0, 0]
    ncp = c.shape[0]
    ca = (c + pos_ref[0:1, :]).astype(MXU_DTYPE)
    cb = (c + pos_ref[1:2, :]).astype(MXU_DTYPE)
    first = jnp.dot(ca, w1_ref[0], preferred_element_type=F32)
    second = jnp.dot(cb, w1_ref[1], preferred_element_type=F32)
    hid = first + pltpu.roll(second, ncp - 1, 0) + b1_ref[...]
    return _gelu_tanh(hid).astype(MXU_DTYPE)


def _cmp_k_kernel(c_ref, pos_ref, w1_ref, b1_ref, w2_ref, b2_ref, g_ref, o_ref):
    hid = _cmp_hidden(c_ref, pos_ref, w1_ref, b1_ref)
    r = jnp.dot(hid, w2_ref[...], preferred_element_type=F32) + b2_ref[...]
    ncp = r.shape[0]
    ms = jnp.sum(r * r, axis=-1, keepdims=True) * (1.0 / HEAD_DIM)
    kn = r * lax.rsqrt(ms + EPS) * g_ref[...]
    idx = lax.broadcasted_iota(jnp.int32, (ncp, FEAT), 0)
    col = lax.broadcasted_iota(jnp.int32, (ncp, FEAT), 1)
    feat = jnp.where((col >= F_CMP) & (col < F_CMP + 3), (idx >> 6).astype(F32),
                     jnp.where((col >= F_CMP + 3) & (col < F_CMP + 6), (idx & 63).astype(F32), 0.0))
    o_ref[0, 0] = (kn + feat).astype(o_ref.dtype)


def _cmp_v_kernel(c_ref, pos_ref, w1_ref, b1_ref, w2t_ref, b2_ref, o_ref):
    hid = _cmp_hidden(c_ref, pos_ref, w1_ref, b1_ref)
    r = _nt_dot(w2t_ref[...], hid) + b2_ref[...]
    o_ref[0, 0] = r.astype(o_ref.dtype)


def _compress(chunks, pos2, w1, b1, w2, b2, gain_pad=None):
    b, g, ncp, ck = chunks.shape
    common = [pl.BlockSpec((1, 1, ncp, ck), lambda bi, gi: (bi, gi, 0, 0)),
              pl.BlockSpec((2, ck), lambda bi, gi: (0, 0)),
              pl.BlockSpec((2, ck, CMP_HIDDEN), lambda bi, gi: (0, 0, 0)),
              pl.BlockSpec((1, CMP_HIDDEN), lambda bi, gi: (0, 0))]
    if gain_pad is not None:
        return pl.pallas_call(
            _cmp_k_kernel,
            grid=(b, g),
            in_specs=common + [pl.BlockSpec((CMP_HIDDEN, FEAT), lambda bi, gi: (0, 0)),
                               pl.BlockSpec((1, FEAT), lambda bi, gi: (0, 0)),
                               pl.BlockSpec((1, FEAT), lambda bi, gi: (0, 0))],
            out_specs=pl.BlockSpec((1, 1, ncp, FEAT), lambda bi, gi: (bi, gi, 0, 0)),
            out_shape=jax.ShapeDtypeStruct((b, g, ncp, FEAT), MXU_DTYPE),
            compiler_params=_params("parallel", "parallel"),
            name="compress_k",
        )(chunks, pos2, w1, b1, w2, b2, gain_pad)
    return pl.pallas_call(
        _cmp_v_kernel,
        grid=(b, g),
        in_specs=common + [pl.BlockSpec((V_ROWS, CMP_HIDDEN), lambda bi, gi: (0, 0)),
                           pl.BlockSpec((V_ROWS, 1), lambda bi, gi: (0, 0))],
        out_specs=pl.BlockSpec((1, 1, V_ROWS, ncp), lambda bi, gi: (bi, gi, 0, 0)),
        out_shape=jax.ShapeDtypeStruct((b, g, V_ROWS, ncp), MXU_DTYPE),
        compiler_params=_params("parallel", "parallel"),
        name="compress_v",
    )(chunks, pos2, w1, b1, w2, b2)


def _split3(v):
    parts = []
    rest = np.asarray(v, np.float64)
    for _ in range(3):
        p = rest.astype(np.float32).astype(jnp.bfloat16).astype(np.float64)
        parts.append(p)
        rest = rest - p
    return parts


def _alibi_query_features():
    tab = np.zeros((N_KV, FEAT - F_POS, HPG * TQ), np.float64)
    for g in range(N_KV):
        for h in range(HPG):
            slope = 2.0 ** (-8.0 * (g * HPG + h + 1) / N_HEADS)
            parts = _split3(slope * LOG2E)
            lanes = slice(h * TQ, (h + 1) * TQ)
            for i, p in enumerate(parts):
                tab[g, i, lanes] = SEL_BLOCK * p
                tab[g, 3 + i, lanes] = p
                tab[g, 6 + i, lanes] = CMP_STRIDE * 64 * p
                tab[g, 9 + i, lanes] = CMP_STRIDE * p
    return jnp.asarray(tab, F32).astype(MXU_DTYPE)


def _block_map_t(n_cmp_pad, n_blk):
    cs = np.arange(n_cmp_pad) * CMP_STRIDE
    ce = cs + CMP_BLOCK - 1
    bs = np.arange(n_blk) * SEL_BLOCK
    be = bs + SEL_BLOCK - 1
    return jnp.asarray(((cs[None, :] <= be[:, None]) & (ce[None, :] >= bs[:, None])).astype(np.float32))


def _attn_kernel(qt_ref, gate_ref, kc_ref, vc_ref, ks_ref, vs_ref, kw_ref, vw_ref, alibi_ref, map_ref,
                 o_ref, qb_ref, qs_ref, imp_ref, s0_ref, s1_ref, p_ref, *, n_blk, n_sel):
    lanes = HPG * TQ
    qi = pl.program_id(2)
    q0 = pl.multiple_of(qi * TQ, TQ)
    ncp = kc_ref.shape[2]

    for h in range(HPG):
        qb_ref[0:HEAD_DIM, h * TQ:(h + 1) * TQ] = qt_ref[0, h * HEAD_DIM:(h + 1) * HEAD_DIM, :]
    qb_ref[F_SEL:F_POS, :] = jnp.zeros((F_POS - F_SEL, lanes), qb_ref.dtype)
    qb_ref[F_POS:FEAT, :] = alibi_ref[0]
    qb = qb_ref[...]

    t_lane = q0 + (lax.broadcasted_iota(jnp.int32, (1, lanes), 1) & (TQ - 1))

    sc = jnp.dot(kc_ref[0, 0], qb, preferred_element_type=F32)
    last_cmp = (t_lane - (CMP_BLOCK - 1)) >> 4
    sc = jnp.where(lax.broadcasted_iota(jnp.int32, (ncp, lanes), 0) <= last_cmp, sc, NEG_MASK)
    ec = jnp.exp2(sc - jnp.max(sc, axis=0, keepdims=True))
    acc_c = jnp.dot(vc_ref[0, 0], ec.astype(MXU_DTYPE), preferred_element_type=F32)
    inv_c = jnp.where(last_cmp >= 0, 1.0 / jnp.maximum(acc_c[HEAD_DIM:HEAD_DIM + 1], 1e-30), 0.0)
    o_cmp = acc_c[0:HEAD_DIM] * inv_c

    psum = ec[:, 0:TQ] * inv_c[:, 0:TQ]
    for h in range(1, HPG):
        psum = psum + ec[:, h * TQ:(h + 1) * TQ] * inv_c[:, h * TQ:(h + 1) * TQ]
    imp = jnp.dot(map_ref[...], psum, preferred_element_type=F32)
    t_q = q0 + lax.broadcasted_iota(jnp.int32, (n_blk, TQ), 1)
    blk = lax.broadcasted_iota(jnp.int32, (n_blk, TQ), 0)
    cur = t_q >> 6
    forced = (blk == 0) | (blk == cur) | (blk == cur - 1)
    valid = blk <= cur
    imp = jnp.where(valid, imp + jnp.where(forced, FORCE_BONUS, 0.0), NEG_MASK)
    imp_ref[...] = imp
    n_chunks = n_blk // V7X_SUBLANES
    chunks = [imp[c * V7X_SUBLANES:(c + 1) * V7X_SUBLANES] for c in range(n_chunks)]
    ranks = [jnp.zeros((V7X_SUBLANES, TQ), jnp.int32) for _ in range(n_chunks)]
    sub = lax.broadcasted_iota(jnp.int32, (V7X_SUBLANES, TQ), 0)
    for k in range(n_blk):
        row = imp_ref[k:k + 1, :]
        for c in range(n_chunks):
            lo = c * V7X_SUBLANES
            if lo > k:
                one = jnp.where(row >= chunks[c], 1, 0)
            elif lo + V7X_SUBLANES - 1 <= k:
                one = jnp.where(row > chunks[c], 1, 0)
            else:
                one = jnp.where(sub + lo > k, jnp.where(row >= chunks[c], 1, 0),
                                jnp.where(row > chunks[c], 1, 0))
            ranks[c] = ranks[c] + one
    rank = jnp.concatenate(ranks, axis=0)
    first_own_blk = qi * (TQ // SEL_BLOCK)
    not_sel = jnp.where((rank < n_sel) & (blk < first_own_blk), 0.0, 1.0).astype(qs_ref.dtype)
    qs_ref[...] = qb
    for h in range(HPG):
        qs_ref[F_SEL:F_SEL + n_blk, h * TQ:(h + 1) * TQ] = not_sel

    w0 = pl.multiple_of(jnp.maximum(q0 - WINDOW, 0), TQ)
    sw = jnp.dot(kw_ref[0, 0, pl.ds(w0, WIN_KEYS), :], qb, preferred_element_type=F32)
    dw = (t_lane - w0) - lax.broadcasted_iota(jnp.int32, (WIN_KEYS, lanes), 0)
    sw = jnp.where(lax.bitcast_convert_type(dw, jnp.uint32) < WINDOW, sw, NEG_MASK)
    pw = jnp.exp2(sw - jnp.max(sw, axis=0, keepdims=True))
    acc_w = jnp.dot(vw_ref[0, :, pl.ds(w0, WIN_KEYS)], pw.astype(MXU_DTYPE), preferred_element_type=F32)
    o_win = acc_w[0:HEAD_DIM] * (1.0 / acc_w[HEAD_DIM:HEAD_DIM + 1])

    def qk_chunk(c):
        k0 = pl.multiple_of(c * TK_SEL, TK_SEL)
        return jnp.dot(ks_ref[0, 0, pl.ds(k0, TK_SEL), :], qs_ref[...], preferred_element_type=F32)

    def pv_chunk(c):
        k0 = pl.multiple_of(c * TK_SEL, TK_SEL)
        return jnp.dot(vs_ref[0, :, pl.ds(k0, TK_SEL)], p_ref[...], preferred_element_type=F32)

    def stage(k, carry, s_cur_ref, s_next_ref):
        m, acc = carry
        pv_prev = pv_chunk(jnp.maximum(k - 1, 0))
        if s_next_ref is not None:
            s_next_ref[...] = qk_chunk(k + 1)
        s = s_cur_ref[...]
        m_new = jnp.maximum(m, jnp.max(s, axis=0, keepdims=True))
        p_ref[...] = jnp.exp2(s - m_new).astype(p_ref.dtype)
        return m_new, jnp.exp2(m - m_new) * (acc + pv_prev)

    n_pairs = jnp.maximum((qi * TQ + 2 * TK_SEL - 1) // (2 * TK_SEL), 1)
    s0_ref[...] = qk_chunk(0)
    p_ref[...] = jnp.zeros(p_ref.shape, p_ref.dtype)

    def pair_body(j, carry):
        carry = stage(2 * j, carry, s0_ref, s1_ref)
        return stage(2 * j + 1, carry, s1_ref, s0_ref)

    carry = (jnp.full((1, lanes), NEG_MASK, F32), jnp.zeros((V_ROWS, lanes), F32))
    carry = lax.fori_loop(0, n_pairs - 1, pair_body, carry)
    last = 2 * n_pairs - 1
    carry = stage(last - 1, carry, s0_ref, s1_ref)
    m_s, acc_s = stage(last, carry, s1_ref, None)
    acc_s = acc_s + pv_chunk(last)

    sd = jnp.dot(ks_ref[0, 0, pl.ds(q0, TQ), :], qb, preferred_element_type=F32)
    kd = q0 + lax.broadcasted_iota(jnp.int32, (TQ, lanes), 0)
    sd = jnp.where(kd <= t_lane, sd, NEG_MASK)
    m_d = jnp.maximum(m_s, jnp.max(sd, axis=0, keepdims=True))
    pd = jnp.exp2(sd - m_d).astype(MXU_DTYPE)
    acc_s = jnp.exp2(m_s - m_d) * acc_s + jnp.dot(vs_ref[0, :, pl.ds(q0, TQ)], pd, preferred_element_type=F32)
    o_sel = acc_s[0:HEAD_DIM] * (1.0 / acc_s[HEAD_DIM:HEAD_DIM + 1])

    gates = gate_ref[0]
    def gate_row(j):
        return jnp.concatenate([gates[j * HPG + h:j * HPG + h + 1, :] for h in range(HPG)], axis=1)
    o_t = gate_row(0) * o_cmp + gate_row(1) * o_sel + gate_row(2) * o_win
    for hp in range(HPG // 2):
        pair = jnp.concatenate([o_t[:, (2 * hp) * TQ:(2 * hp + 1) * TQ],
                                o_t[:, (2 * hp + 1) * TQ:(2 * hp + 2) * TQ]], axis=0)
        o_ref[0, :, hp * 2 * HEAD_DIM:(hp + 1) * 2 * HEAD_DIM] = pair.T.astype(o_ref.dtype)


def _attention(q_t, gates_t, k_cmp, v_cmp_t, k_sel, k_win, v_t):
    b, _, t = q_t.shape
    ncp = k_cmp.shape[2]
    n_blk = t // SEL_BLOCK
    assert t % (2 * TK_SEL) == 0 and F_SEL + n_blk <= F_POS, "unsupported sequence length"
    n_sel = min(N_SEL, n_blk)
    lanes = HPG * TQ
    rows = HPG * HEAD_DIM
    alibi = _alibi_query_features()
    blk_map_t = _block_map_t(ncp, n_blk)
    kernel = functools.partial(_attn_kernel, n_blk=n_blk, n_sel=n_sel)
    return pl.pallas_call(
        kernel,
        grid=(b, N_KV, t // TQ),
        in_specs=[
            pl.BlockSpec((1, rows, TQ), lambda bi, g, i: (bi, g, i)),
            pl.BlockSpec((1, 16, TQ), lambda bi, g, i: (bi, g, i)),
            pl.BlockSpec((1, 1, ncp, FEAT), lambda bi, g, i: (bi, g, 0, 0)),
            pl.BlockSpec((1, 1, V_ROWS, ncp), lambda bi, g, i: (bi, g, 0, 0)),
            pl.BlockSpec((1, 1, t, FEAT), lambda bi, g, i: (bi, g, 0, 0)),
            pl.BlockSpec((1, V_ROWS, t), lambda bi, g, i: (bi, g, 0)),
            pl.BlockSpec((1, 1, t, FEAT), lambda bi, g, i: (bi, g, 0, 0)),
            pl.BlockSpec((1, V_ROWS, t), lambda bi, g, i: (bi, N_KV + g, 0)),
            pl.BlockSpec((1, FEAT - F_POS, lanes), lambda bi, g, i: (g, 0, 0)),
            pl.BlockSpec((n_blk, ncp), lambda bi, g, i: (0, 0)),
        ],
        out_specs=pl.BlockSpec((1, TQ, rows), lambda bi, g, i: (bi, i, g)),
        out_shape=jax.ShapeDtypeStruct((b, t, N_HEADS * HEAD_DIM), MXU_DTYPE),
        scratch_shapes=[pltpu.VMEM((FEAT, lanes), MXU_DTYPE),
                        pltpu.VMEM((FEAT, lanes), MXU_DTYPE),
                        pltpu.VMEM((n_blk, TQ), F32),
                        pltpu.VMEM((TK_SEL, lanes), F32),
                        pltpu.VMEM((TK_SEL, lanes), F32),
                        pltpu.VMEM((TK_SEL, lanes), MXU_DTYPE)],
        compiler_params=_params("parallel", "parallel", "arbitrary"),
        name="nsa_attention",
    )(q_t, gates_t, k_cmp, v_cmp_t, k_sel, v_t, k_win, v_t, alibi, blk_map_t)


def _lru_kernel(hin_ref, wl_ref, cw_ref, cb_ref, wa_ref, ba_ref, wi_ref, bi_ref, lam_ref, o_ref,
                tail_ref, h_ref):
    tt = hin_ref.shape[1]
    w = lam_ref.shape[1]

    @pl.when(pl.program_id(1) == 0)
    def _():
        tail_ref[...] = jnp.zeros_like(tail_ref)
        h_ref[...] = jnp.zeros_like(h_ref)

    hin = hin_ref[0]
    x = jnp.dot(hin, wl_ref[:, 0:w], preferred_element_type=F32)
    gate = jnp.dot(hin, wl_ref[:, w:2 * w], preferred_element_type=F32)
    prev = tail_ref[...]
    row8 = lax.broadcasted_iota(jnp.int32, (V7X_SUBLANES, w), 0)
    xc = x * cw_ref[CONV_W - 1:CONV_W, :] + cb_ref[...]
    for s in range(1, CONV_W):
        xs = pltpu.roll(x, s, 0)
        head = jnp.where(row8 < s, pltpu.roll(prev, s, 0), xs[0:V7X_SUBLANES])
        xs = jnp.concatenate([head, xs[V7X_SUBLANES:]], axis=0)
        xc = xc + xs * cw_ref[CONV_W - 1 - s:CONV_W - s, :]
    tail_ref[...] = x[tt - V7X_SUBLANES:tt]

    xb = xc.astype(MXU_DTYPE)
    r = _sigmoid(jnp.dot(xb, wa_ref[...], preferred_element_type=F32) + ba_ref[...])
    i = _sigmoid(jnp.dot(xb, wi_ref[...], preferred_element_type=F32) + bi_ref[...])
    z = -lam_ref[...]
    softplus = jnp.maximum(z, 0.0) + jnp.log1p(jnp.exp(-jnp.abs(z)))
    log_a = -LRU_C * r * softplus
    a = jnp.exp(log_a)
    bb = jnp.sqrt(_neg_expm1(2.0 * log_a)) * (i * xc)

    row = lax.broadcasted_iota(jnp.int32, (tt, w), 0) & (V7X_SUBLANES - 1)
    for d in (1, 2, 4):
        ok = row >= d
        a_sh = pltpu.roll(a, d, 0)
        b_sh = pltpu.roll(bb, d, 0)
        bb = jnp.where(ok, a * b_sh + bb, bb)
        a = jnp.where(ok, a * a_sh, a)
    carry = h_ref[0:1, :]
    groups = []
    for r in range(tt // V7X_SUBLANES):
        rows = slice(r * V7X_SUBLANES, (r + 1) * V7X_SUBLANES)
        hg = bb[rows] + a[rows] * carry
        groups.append(hg)
        carry = hg[V7X_SUBLANES - 1:V7X_SUBLANES, :]
    hcur = jnp.concatenate(groups, axis=0)
    h_ref[...] = jnp.broadcast_to(carry, h_ref.shape)
    o_ref[0] = (hcur * _gelu_tanh(gate)).astype(o_ref.dtype)


def _rglru(h3d, w_lru, conv_w, conv_b, wa_bd, ba, wi_bd, bi, lam, tt=256):
    b, t, d = h3d.shape
    w = w_lru.shape[1] // 2
    vec = lambda: pl.BlockSpec((1, w), lambda bi_, i: (0, 0))
    return pl.pallas_call(
        _lru_kernel,
        grid=(b, t // tt),
        in_specs=[pl.BlockSpec((1, tt, d), lambda bi_, i: (bi_, i, 0)),
                  pl.BlockSpec((d, 2 * w), lambda bi_, i: (0, 0)),
                  pl.BlockSpec((CONV_W, w), lambda bi_, i: (0, 0)), vec(),
                  pl.BlockSpec((w, w), lambda bi_, i: (0, 0)), vec(),
                  pl.BlockSpec((w, w), lambda bi_, i: (0, 0)), vec(), vec()],
        out_specs=pl.BlockSpec((1, tt, w), lambda bi_, i: (bi_, i, 0)),
        out_shape=jax.ShapeDtypeStruct((b, t, w), MXU_DTYPE),
        scratch_shapes=[pltpu.VMEM((V7X_SUBLANES, w), F32), pltpu.VMEM((V7X_SUBLANES, w), F32)],
        compiler_params=_params("parallel", "arbitrary"),
        name="rglru",
    )(h3d, w_lru, conv_w, conv_b, wa_bd, ba, wi_bd, bi, lam)


def _merge_kernel(attn_ref, lru_ref, h_ref, x_ref, wm_ref, wa_ref, wl_ref, wo_ref, g2_ref,
                  x1_ref, h2_ref):
    d = x_ref.shape[1]
    h = h_ref[...]
    mg0 = _sigmoid(jnp.dot(h, wm_ref[:, 0:d], preferred_element_type=F32))
    mg1 = _sigmoid(jnp.dot(h, wm_ref[:, d:2 * d], preferred_element_type=F32))
    ya = jnp.dot(attn_ref[...], wa_ref[...], preferred_element_type=F32)
    yl = jnp.dot(lru_ref[...], wl_ref[...], preferred_element_type=F32)
    merged = mg0 * ya + mg1 * yl
    x1 = x_ref[...] + jnp.dot(merged.astype(MXU_DTYPE), wo_ref[...], preferred_element_type=F32)
    x1_ref[...] = x1
    ms = jnp.mean(x1 * x1, axis=-1, keepdims=True)
    h2_ref[...] = (x1 * lax.rsqrt(ms + EPS) * g2_ref[...]).astype(h2_ref.dtype)


def _merge(attn, lru, h2d, x2d, w_mg, wa, wl, wo, g2, tm=256):
    m, d = x2d.shape
    row = pl.BlockSpec((tm, d), lambda i: (i, 0))
    full = lambda a: pl.BlockSpec(a.shape, lambda i: (0, 0))
    g2 = g2.reshape(1, d)
    return pl.pallas_call(
        _merge_kernel,
        grid=(m // tm,),
        in_specs=[row, row, row, row, full(w_mg), full(wa), full(wl), full(wo), full(g2)],
        out_specs=[row, row],
        out_shape=[jax.ShapeDtypeStruct((m, d), F32), jax.ShapeDtypeStruct((m, d), MXU_DTYPE)],
        compiler_params=_params("parallel"),
        name="merge_out",
    )(attn, lru, h2d, x2d, w_mg, wa, wl, wo, g2)


def _ffn_kernel(h_ref, x1_ref, wg_ref, wu_ref, wd_ref, o_ref):
    @pl.when(pl.program_id(1) == 0)
    def _():
        o_ref[...] = x1_ref[...]

    h = h_ref[...]
    g = jnp.dot(h, wg_ref[...], preferred_element_type=F32)
    u = jnp.dot(h, wu_ref[...], preferred_element_type=F32)
    act = (g * _sigmoid(g) * u).astype(MXU_DTYPE)
    o_ref[...] += jnp.dot(act, wd_ref[...], preferred_element_type=F32)


def _ffn(h2, x1, wg, wu, wd, tm=1024, tf=256):
    m, d = x1.shape
    f = wg.shape[1]
    return pl.pallas_call(
        _ffn_kernel,
        grid=(m // tm, f // tf),
        in_specs=[pl.BlockSpec((tm, d), lambda i, j: (i, 0)),
                  pl.BlockSpec((tm, d), lambda i, j: (i, 0)),
                  pl.BlockSpec((d, tf), lambda i, j: (0, j)),
                  pl.BlockSpec((d, tf), lambda i, j: (0, j)),
                  pl.BlockSpec((tf, d), lambda i, j: (j, 0))],
        out_specs=pl.BlockSpec((tm, d), lambda i, j: (i, 0)),
        out_shape=jax.ShapeDtypeStruct((m, d), F32),
        compiler_params=_params("parallel", "arbitrary"),
        name="swiglu_ffn",
    )(h2, x1, wg, wu, wd)


def _pad_last(a, n):
    return jnp.pad(a, [(0, 0)] * (a.ndim - 1) + [(0, n - a.shape[-1])])


def _layer(x, norm1_g, w_in, q_norm_g, k_norm_g, cmp_pos_k, cmp_w1_k, cmp_b1_k, cmp_w2_k, cmp_b2_k,
           cmp_pos_v, cmp_w1_v, cmp_b1_v, cmp_w2_v, cmp_b2_v, conv_w, conv_b, lru_wa, lru_ba,
           lru_wi, lru_bi, lru_lambda, w_o_attn, w_o_lru, w_out, norm2_g, w_gate, w_up, w_down):
    b, t, d = x.shape
    m = b * t
    attn_dim = N_HEADS * HEAD_DIM
    kv_dim = N_KV * HEAD_DIM
    lru_w = lru_lambda.shape[0]
    assert t % 512 == 0
    o1 = attn_dim
    o2 = o1 + 6 * kv_dim
    o3 = o2 + 3 * N_HEADS
    o4 = o3 + lru_w
    o5 = o4 + lru_w
    cast = lambda a: a.astype(MXU_DTYPE)

    wq_t = cast(w_in[:, :o1].T)
    w_kv = w_in[:, o1:o2].reshape(d, 6, N_KV, HEAD_DIM)
    w_cmp_src = cast(w_kv[:, 0:2].reshape(d, 2 * kv_dim))
    w_k2 = cast(jnp.stack([w_kv[:, 2], w_kv[:, 4]], axis=1).reshape(d, 2 * kv_dim))
    k_gain2 = jnp.stack([jnp.tile(k_norm_g[1], N_KV), jnp.tile(k_norm_g[2], N_KV)])
    lane_grp = np.arange(kv_dim) // HEAD_DIM
    grp_avg = cast(jnp.asarray((lane_grp[:, None] == lane_grp[None, :]) / HEAD_DIM, F32))
    place_np = np.zeros((kv_dim, N_KV * FEAT), np.float32)
    place_np[np.arange(kv_dim), lane_grp * FEAT + np.arange(kv_dim) % HEAD_DIM] = 1.0
    place = cast(jnp.asarray(place_np))
    w_v = jnp.stack([w_kv[:, 3], w_kv[:, 5]], axis=1)
    w_v_t = cast(_pad_last(w_v, V_ROWS).reshape(d, 2 * N_KV * V_ROWS).T)
    v_ones_col = jnp.tile(jnp.arange(V_ROWS) == HEAD_DIM, 2 * N_KV).astype(F32).reshape(-1, 1)
    w_g = w_in[:, o2:o3].reshape(d, N_KV, HPG, 3).transpose(0, 1, 3, 2).reshape(d, N_KV, 3 * HPG)
    w_g_t = cast(_pad_last(w_g, 16).reshape(d, N_KV * 16).T)
    w_lru = cast(w_in[:, o3:o5])
    w_mg = cast(w_in[:, o5:])
    q_gain_col = jnp.tile(q_norm_g * (HEAD_DIM ** -0.5 * LOG2E), N_HEADS).reshape(attn_dim, 1)
    gain_pad = lambda g: _pad_last(g.reshape(1, HEAD_DIM), FEAT)

    h3d, q_t, v_t, gates_t, k_sel, k_win, cmp_src = _in_proj(
        x, norm1_g.reshape(1, d), wq_t, q_gain_col, w_v_t, v_ones_col, w_g_t, w_k2, k_gain2, grp_avg, place,
        w_cmp_src)
    h2d = h3d.reshape(m, d)

    ncp = t // CMP_STRIDE
    chunks = cmp_src.reshape(b, ncp, CMP_STRIDE, 2, N_KV, HEAD_DIM).transpose(3, 0, 4, 1, 2, 5)
    chunks = chunks.reshape(2, b, N_KV, ncp, CMP_STRIDE * HEAD_DIM)
    ck = CMP_STRIDE * HEAD_DIM
    k_cmp = _compress(chunks[0], cmp_pos_k.reshape(2, ck), cast(cmp_w1_k.reshape(2, ck, CMP_HIDDEN)),
                      cmp_b1_k.reshape(1, CMP_HIDDEN), cast(_pad_last(cmp_w2_k, FEAT)),
                      _pad_last(cmp_b2_k.reshape(1, HEAD_DIM), FEAT), gain_pad(k_norm_g[0]))
    v_cmp_t = _compress(chunks[1], cmp_pos_v.reshape(2, ck), cast(cmp_w1_v.reshape(2, ck, CMP_HIDDEN)),
                        cmp_b1_v.reshape(1, CMP_HIDDEN), cast(_pad_last(cmp_w2_v, V_ROWS).T),
                        jnp.concatenate([cmp_b2_v, v_ones_col[HEAD_DIM:V_ROWS, 0]]).reshape(V_ROWS, 1))

    attn = _attention(q_t, gates_t, k_cmp, v_cmp_t, k_sel, k_win, v_t)

    eye = jnp.eye(LRU_BLOCKS, dtype=F32)
    bd = lambda wgt: cast(jnp.einsum('nkj,nm->nkmj', wgt, eye).reshape(lru_w, lru_w))
    vec = lambda v: v.reshape(1, lru_w)
    lru = _rglru(h3d, w_lru, conv_w.reshape(CONV_W, lru_w), vec(conv_b), bd(lru_wa), vec(lru_ba),
                 bd(lru_wi), vec(lru_bi), vec(lru_lambda))

    x1, h2 = _merge(attn.reshape(m, attn_dim), lru.reshape(m, lru_w), h2d, x.reshape(m, d), w_mg,
                    cast(w_o_attn), cast(w_o_lru), cast(w_out), norm2_g)
    out = _ffn(h2, x1, cast(w_gate), cast(w_up), cast(w_down))
    return out.reshape(b, t, d)


def kernel(x, norm1_g, w_in, q_norm_g, k_norm_g, cmp_pos_k, cmp_w1_k, cmp_b1_k, cmp_w2_k, cmp_b2_k,
           cmp_pos_v, cmp_w1_v, cmp_b1_v, cmp_w2_v, cmp_b2_v, conv_w, conv_b, lru_wa, lru_ba,
           lru_wi, lru_bi, lru_lambda, w_o_attn, w_o_lru, w_out, norm2_g, w_gate, w_up, w_down):
    for l in range(norm1_g.shape[0]):
        x = _layer(x, norm1_g[l], w_in[l], q_norm_g[l], k_norm_g[l], cmp_pos_k[l], cmp_w1_k[l],
                   cmp_b1_k[l], cmp_w2_k[l], cmp_b2_k[l], cmp_pos_v[l], cmp_w1_v[l], cmp_b1_v[l],
                   cmp_w2_v[l], cmp_b2_v[l], conv_w[l], conv_b[l], lru_wa[l], lru_ba[l], lru_wi[l],
                   lru_bi[l], lru_lambda[l], w_o_attn[l], w_o_lru[l], w_out[l], norm2_g[l],
                   w_gate[l], w_up[l], w_down[l])
    return x
```

```python
import functools

import numpy as np
import jax
import jax.numpy as jnp
from jax import lax
from jax.experimental import pallas as pl
from jax.experimental.pallas import tpu as pltpu

N_HEADS = 16
HEAD_DIM = 64
N_KV = 4
HPG = N_HEADS // N_KV
CMP_BLOCK = 32
CMP_STRIDE = 16
CMP_HIDDEN = 256
SEL_BLOCK = 64
N_SEL = 16
WINDOW = 512
FORCE_BONUS = 1e4
LRU_BLOCKS = 16
CONV_W = 4
LRU_C = 8.0
EPS = 1e-6

MXU_DTYPE = jnp.bfloat16
F32 = jnp.float32

V7X_LANES = 128
V7X_SUBLANES = 8
V7X_MXU_DIM = 256
V7X_VMEM_LIMIT_BYTES = 48 * 1024 * 1024

TQ = 256
TK_SEL = 512
WIN_KEYS = WINDOW + TQ
V_ROWS = 80
LOG2E = 1.4426950408889634
FEAT = V7X_MXU_DIM
F_SEL = HEAD_DIM
F_POS = 2 * HEAD_DIM
F_CMP = F_POS + 6
NEG_MASK = -1e30
NEG_BLOCK = -(2.0 ** 100)


def _params(*sem):
    return pltpu.CompilerParams(dimension_semantics=sem, vmem_limit_bytes=V7X_VMEM_LIMIT_BYTES)


def _gelu_tanh(x):
    return 0.5 * x * (1.0 + jnp.tanh(0.7978845608028654 * (x + 0.044715 * (x * x * x))))


def _sigmoid(x):
    return 1.0 / (1.0 + jnp.exp(-x))


def _one_minus_sq(a, log_a):
    y = 2.0 * log_a
    series = -y * (1.0 + y * (0.5 + y * (1.0 / 6 + y * (1.0 / 24))))
    return jnp.where(y > -1.0 / 64, series, 1.0 - a * a)


def _nt_dot(wt, h):
    return lax.dot_general(wt, h, (((1,), (1,)), ((), ())), preferred_element_type=F32)


def _token_features(pos, col, with_block_mask):
    blk = pos >> 6
    off = pos & (SEL_BLOCK - 1)
    feat = jnp.where((col >= F_POS) & (col < F_POS + 3), blk.astype(F32),
                     jnp.where((col >= F_POS + 3) & (col < F_POS + 6), off.astype(F32), 0.0))
    if with_block_mask:
        feat = jnp.where((col >= F_SEL) & (col - F_SEL == blk) & (col < F_POS), NEG_BLOCK, feat)
    return feat


def _in_proj_kernel(x_ref, g1_ref, wq_ref, qg_ref, wv_ref, vb_ref, wg_ref, wk_ref, kg_ref, grp_ref,
                    place_ref, wc_ref, h_ref, qt_ref, vt_ref, gt_ref, ks_ref, kw_ref, cs_ref):
    x = x_ref[0]
    tm = x.shape[0]
    ms = jnp.mean(x * x, axis=-1, keepdims=True)
    h = (x * lax.rsqrt(ms + EPS) * g1_ref[...]).astype(MXU_DTYPE)
    h_ref[0] = h

    r3 = _nt_dot(wq_ref[...], h).reshape(N_HEADS, HEAD_DIM, tm)
    qn = r3 * lax.rsqrt(jnp.mean(r3 * r3, axis=1, keepdims=True) + EPS)
    qt_ref[0] = (qn.reshape(N_HEADS * HEAD_DIM, tm) * qg_ref[...]).astype(qt_ref.dtype)

    vt_ref[0] = (_nt_dot(wv_ref[...], h) + vb_ref[...]).astype(vt_ref.dtype)
    gt_ref[0] = _sigmoid(_nt_dot(wg_ref[...], h))

    k = jnp.dot(h, wk_ref[...], preferred_element_type=F32)
    kk = k * k
    kk_hi = kk.astype(MXU_DTYPE)
    kk_lo = (kk - kk_hi.astype(F32)).astype(MXU_DTYPE)
    pos = pl.program_id(1) * tm + lax.broadcasted_iota(jnp.int32, (tm, FEAT), 0)
    col = lax.broadcasted_iota(jnp.int32, (tm, FEAT), 1)
    kv_dim = N_KV * HEAD_DIM
    for branch, (o_ref, with_block_mask) in enumerate(((ks_ref, True), (kw_ref, False))):
        sl = slice(branch * kv_dim, (branch + 1) * kv_dim)
        msq = (jnp.dot(kk_hi[:, sl], grp_ref[...], preferred_element_type=F32)
               + jnp.dot(kk_lo[:, sl], grp_ref[...], preferred_element_type=F32))
        kn = (k[:, sl] * lax.rsqrt(msq + EPS) * kg_ref[branch:branch + 1, :]).astype(MXU_DTYPE)
        placed = jnp.dot(kn, place_ref[...], preferred_element_type=F32)
        feat = _token_features(pos, col, with_block_mask)
        for g in range(N_KV):
            o_ref[0, g] = (placed[:, g * FEAT:(g + 1) * FEAT] + feat).astype(o_ref.dtype)

    cs_ref[0] = jnp.dot(h, wc_ref[...], preferred_element_type=F32)


def _in_proj(x, g1, wq_t, q_gain_col, w_v_t, v_bias_col, w_g_t, w_k2, k_gain2, grp_avg, place, w_cmp_src,
             tm=512):
    b, t, d = x.shape
    kv_dim = N_KV * HEAD_DIM
    full = lambda a: pl.BlockSpec(a.shape, lambda bi, i: (0,) * a.ndim)
    rowblk = lambda n: pl.BlockSpec((1, tm, n), lambda bi, i: (bi, i, 0))
    colblk = lambda n: pl.BlockSpec((1, n, tm), lambda bi, i: (bi, 0, i))
    kblk = pl.BlockSpec((1, N_KV, tm, FEAT), lambda bi, i: (bi, 0, i, 0))
    weights = (g1, wq_t, q_gain_col, w_v_t, v_bias_col, w_g_t, w_k2, k_gain2, grp_avg, place, w_cmp_src)
    return pl.pallas_call(
        _in_proj_kernel,
        grid=(b, t // tm),
        in_specs=[rowblk(d)] + [full(a) for a in weights],
        out_specs=[rowblk(d), colblk(wq_t.shape[0]), colblk(w_v_t.shape[0]), colblk(w_g_t.shape[0]),
                   kblk, kblk, rowblk(2 * kv_dim)],
        out_shape=[jax.ShapeDtypeStruct((b, t, d), MXU_DTYPE),
                   jax.ShapeDtypeStruct((b, wq_t.shape[0], t), MXU_DTYPE),
                   jax.ShapeDtypeStruct((b, w_v_t.shape[0], t), MXU_DTYPE),
                   jax.ShapeDtypeStruct((b, w_g_t.shape[0], t), F32),
                   jax.ShapeDtypeStruct((b, N_KV, t, FEAT), MXU_DTYPE),
                   jax.ShapeDtypeStruct((b, N_KV, t, FEAT), MXU_DTYPE),
                   jax.ShapeDtypeStruct((b, t, 2 * kv_dim), F32)],
        compiler_params=_params("parallel", "parallel"),
        name="in_proj",
    )(x, *weights)


CMP_GROUPS = V7X_LANES // HEAD_DIM


def _cmp_hidden(src_ref, pos_ref, w1_ref, b1_ref):
    ncp = src_ref.shape[1] // CMP_STRIDE
    first = jnp.zeros((ncp, CMP_GROUPS * CMP_HIDDEN), F32)
    second = jnp.zeros((ncp, CMP_GROUPS * CMP_HIDDEN), F32)
    for l in range(CMP_STRIDE):
        x = src_ref[0, pl.ds(l, ncp, stride=CMP_STRIDE), :]
        lo = (x + pos_ref[l:l + 1, :]).astype(MXU_DTYPE)
        hi = (x + pos_ref[CMP_STRIDE + l:CMP_STRIDE + l + 1, :]).astype(MXU_DTYPE)
        first = first + jnp.dot(lo, w1_ref[l], preferred_element_type=F32)
        second = second + jnp.dot(hi, w1_ref[CMP_STRIDE + l], preferred_element_type=F32)
    hid = first + pltpu.roll(second, ncp - 1, 0) + b1_ref[...]
    return _gelu_tanh(hid).astype(MXU_DTYPE)


def _cmp_k_kernel(src_ref, pos_ref, w1_ref, b1_ref, w2_ref, b2_ref, g_ref, o_ref):
    hid = _cmp_hidden(src_ref, pos_ref, w1_ref, b1_ref)
    ncp = hid.shape[0]
    idx = lax.broadcasted_iota(jnp.int32, (ncp, FEAT), 0)
    col = lax.broadcasted_iota(jnp.int32, (ncp, FEAT), 1)
    feat = jnp.where((col >= F_CMP) & (col < F_CMP + 3), (idx >> 6).astype(F32),
                     jnp.where((col >= F_CMP + 3) & (col < F_CMP + 6), (idx & 63).astype(F32), 0.0))
    for gl in range(CMP_GROUPS):
        r = jnp.dot(hid[:, gl * CMP_HIDDEN:(gl + 1) * CMP_HIDDEN], w2_ref[...],
                    preferred_element_type=F32) + b2_ref[...]
        ms = jnp.sum(r * r, axis=-1, keepdims=True) * (1.0 / HEAD_DIM)
        o_ref[0, gl] = (r * lax.rsqrt(ms + EPS) * g_ref[...] + feat).astype(o_ref.dtype)


def _cmp_v_kernel(src_ref, pos_ref, w1_ref, b1_ref, w2t_ref, b2_ref, o_ref):
    hid = _cmp_hidden(src_ref, pos_ref, w1_ref, b1_ref)
    for gl in range(CMP_GROUPS):
        r = _nt_dot(w2t_ref[...], hid[:, gl * CMP_HIDDEN:(gl + 1) * CMP_HIDDEN]) + b2_ref[...]
        o_ref[0, gl] = r.astype(o_ref.dtype)


def _compress(cmp_src, lane_block0, pos, w1, b1, w2, b2, gain_pad=None):
    b, t, _ = cmp_src.shape
    ncp = t // CMP_STRIDE
    hid_w = CMP_GROUPS * CMP_HIDDEN
    eye = jnp.eye(CMP_GROUPS, dtype=w1.dtype)
    w1_bd = jnp.einsum('ldf,gh->lgdhf', w1, eye).reshape(CMP_BLOCK, V7X_LANES, hid_w).astype(MXU_DTYPE)
    pos_t = jnp.tile(pos, (1, CMP_GROUPS))
    b1_t = jnp.tile(b1.reshape(1, CMP_HIDDEN), (1, CMP_GROUPS))
    full = lambda a: pl.BlockSpec(a.shape, lambda bi, p: (0,) * a.ndim)
    src_spec = pl.BlockSpec((1, t, V7X_LANES), lambda bi, p: (bi, 0, lane_block0 + p))
    grid = (b, N_KV // CMP_GROUPS)
    if gain_pad is not None:
        args = (pos_t, w1_bd, b1_t, w2, b2, gain_pad)
        return pl.pallas_call(
            _cmp_k_kernel,
            grid=grid,
            in_specs=[src_spec] + [full(a) for a in args],
            out_specs=pl.BlockSpec((1, CMP_GROUPS, ncp, FEAT), lambda bi, p: (bi, p, 0, 0)),
            out_shape=jax.ShapeDtypeStruct((b, N_KV, ncp, FEAT), MXU_DTYPE),
            compiler_params=_params("parallel", "parallel"),
            name="compress_k",
        )(cmp_src, *args)
    args = (pos_t, w1_bd, b1_t, w2, b2)
    return pl.pallas_call(
        _cmp_v_kernel,
        grid=grid,
        in_specs=[src_spec] + [full(a) for a in args],
        out_specs=pl.BlockSpec((1, CMP_GROUPS, V_ROWS, ncp), lambda bi, p: (bi, p, 0, 0)),
        out_shape=jax.ShapeDtypeStruct((b, N_KV, V_ROWS, ncp), MXU_DTYPE),
        compiler_params=_params("parallel", "parallel"),
        name="compress_v",
    )(cmp_src, *args)


def _split3(v):
    parts = []
    rest = np.asarray(v, np.float64)
    for _ in range(3):
        p = rest.astype(np.float32).astype(jnp.bfloat16).astype(np.float64)
        parts.append(p)
        rest = rest - p
    return parts


def _alibi_query_features():
    tab = np.zeros((N_KV, FEAT - F_POS, HPG * TQ), np.float64)
    for g in range(N_KV):
        for h in range(HPG):
            slope = 2.0 ** (-8.0 * (g * HPG + h + 1) / N_HEADS)
            parts = _split3(slope * LOG2E)
            lanes = slice(h * TQ, (h + 1) * TQ)
            for i, p in enumerate(parts):
                tab[g, i, lanes] = SEL_BLOCK * p
                tab[g, 3 + i, lanes] = p
                tab[g, 6 + i, lanes] = CMP_STRIDE * 64 * p
                tab[g, 9 + i, lanes] = CMP_STRIDE * p
    return jnp.asarray(tab, F32).astype(MXU_DTYPE)


def _block_map_t(n_cmp_pad, n_blk):
    cs = np.arange(n_cmp_pad) * CMP_STRIDE
    ce = cs + CMP_BLOCK - 1
    bs = np.arange(n_blk) * SEL_BLOCK
    be = bs + SEL_BLOCK - 1
    return jnp.asarray(((cs[None, :] <= be[:, None]) & (ce[None, :] >= bs[:, None])).astype(np.float32))


def _attn_kernel(qt_ref, gate_ref, kc_ref, vc_ref, ks_ref, vs_ref, kw_ref, vw_ref, alibi_ref, map_ref,
                 o_ref, qb_ref, qs_ref, imp_ref, s0_ref, s1_ref, p_ref, *, n_blk, n_sel):
    lanes = HPG * TQ
    qi = pl.program_id(2)
    q0 = pl.multiple_of(qi * TQ, TQ)
    ncp = kc_ref.shape[2]

    for h in range(HPG):
        qb_ref[0:HEAD_DIM, h * TQ:(h + 1) * TQ] = qt_ref[0, h * HEAD_DIM:(h + 1) * HEAD_DIM, :]
    qb_ref[F_SEL:F_POS, :] = jnp.zeros((F_POS - F_SEL, lanes), qb_ref.dtype)
    qb_ref[F_POS:FEAT, :] = alibi_ref[0]
    qb = qb_ref[...]

    t_lane = q0 + (lax.broadcasted_iota(jnp.int32, (1, lanes), 1) & (TQ - 1))

    sc = jnp.dot(kc_ref[0, 0], qb, preferred_element_type=F32)
    last_cmp = (t_lane - (CMP_BLOCK - 1)) >> 4
    sc = jnp.where(lax.broadcasted_iota(jnp.int32, (ncp, lanes), 0) <= last_cmp, sc, NEG_MASK)
    ec = jnp.exp2(sc - jnp.max(sc, axis=0, keepdims=True))
    acc_c = jnp.dot(vc_ref[0, 0], ec.astype(MXU_DTYPE), preferred_element_type=F32)
    inv_c = jnp.where(last_cmp >= 0, 1.0 / jnp.maximum(acc_c[HEAD_DIM:HEAD_DIM + 1], 1e-30), 0.0)
    o_cmp = acc_c[0:HEAD_DIM] * inv_c

    psum = ec[:, 0:TQ] * inv_c[:, 0:TQ]
    for h in range(1, HPG):
        psum = psum + ec[:, h * TQ:(h + 1) * TQ] * inv_c[:, h * TQ:(h + 1) * TQ]
    imp = jnp.dot(map_ref[...], psum, preferred_element_type=F32)
    t_q = q0 + lax.broadcasted_iota(jnp.int32, (n_blk, TQ), 1)
    blk = lax.broadcasted_iota(jnp.int32, (n_blk, TQ), 0)
    cur = t_q >> 6
    forced = (blk == 0) | (blk == cur) | (blk == cur - 1)
    valid = blk <= cur
    imp = jnp.where(valid, imp + jnp.where(forced, FORCE_BONUS, 0.0), NEG_MASK)
    imp_ref[...] = imp
    first_own_blk = qi * (TQ // SEL_BLOCK)
    n_chunks = n_blk // V7X_SUBLANES
    chunks = [imp[c * V7X_SUBLANES:(c + 1) * V7X_SUBLANES] for c in range(n_chunks)]
    ranks = [jnp.zeros((V7X_SUBLANES, TQ), jnp.int32) for _ in range(n_chunks)]
    sub = lax.broadcasted_iota(jnp.int32, (V7X_SUBLANES, TQ), 0)
    for k in range(n_blk):
        row = imp_ref[k:k + 1, :]
        for c in range(n_chunks):
            lo = c * V7X_SUBLANES
            if lo > k:
                one = jnp.where(row >= chunks[c], 1, 0)
            elif lo + V7X_SUBLANES - 1 <= k:
                one = jnp.where(row > chunks[c], 1, 0)
            else:
                one = jnp.where(sub + lo > k, jnp.where(row >= chunks[c], 1, 0),
                                jnp.where(row > chunks[c], 1, 0))
            ranks[c] = ranks[c] + one
    rank = jnp.concatenate(ranks, axis=0)
    not_sel = jnp.where((rank < n_sel) & (blk < first_own_blk), 0.0, 1.0).astype(qs_ref.dtype)
    qs_ref[...] = qb
    for h in range(HPG):
        qs_ref[F_SEL:F_SEL + n_blk, h * TQ:(h + 1) * TQ] = not_sel

    w0 = pl.multiple_of(jnp.maximum(q0 - WINDOW, 0), TQ)
    sw = jnp.dot(kw_ref[0, 0, pl.ds(w0, WIN_KEYS), :], qb, preferred_element_type=F32)
    dw = (t_lane - w0) - lax.broadcasted_iota(jnp.int32, (WIN_KEYS, lanes), 0)
    sw = jnp.where(lax.bitcast_convert_type(dw, jnp.uint32) < WINDOW, sw, NEG_MASK)
    pw = jnp.exp2(sw - jnp.max(sw, axis=0, keepdims=True))
    acc_w = jnp.dot(vw_ref[0, :, pl.ds(w0, WIN_KEYS)], pw.astype(MXU_DTYPE), preferred_element_type=F32)
    o_win = acc_w[0:HEAD_DIM] * (1.0 / acc_w[HEAD_DIM:HEAD_DIM + 1])

    def qk_chunk(c):
        k0 = pl.multiple_of(c * TK_SEL, TK_SEL)
        return jnp.dot(ks_ref[0, 0, pl.ds(k0, TK_SEL), :], qs_ref[...], preferred_element_type=F32)

    def pv_chunk(c):
        k0 = pl.multiple_of(c * TK_SEL, TK_SEL)
        return jnp.dot(vs_ref[0, :, pl.ds(k0, TK_SEL)], p_ref[...], preferred_element_type=F32)

    def stage(k, carry, s_cur_ref, s_next_ref):
        m, acc = carry
        if s_next_ref is not None:
            s_next_ref[...] = qk_chunk(k + 1)
        pv_prev = pv_chunk(jnp.maximum(k - 1, 0))
        s = s_cur_ref[...]
        m_new = jnp.maximum(m, jnp.max(s, axis=0, keepdims=True))
        p_ref[...] = jnp.exp2(s - m_new).astype(p_ref.dtype)
        return m_new, jnp.exp2(m - m_new) * (acc + pv_prev)

    n_pairs = jnp.maximum((qi * TQ + 2 * TK_SEL - 1) // (2 * TK_SEL), 1)
    s0_ref[...] = qk_chunk(0)
    p_ref[...] = jnp.zeros(p_ref.shape, p_ref.dtype)

    def pair_body(j, carry):
        carry = stage(2 * j, carry, s0_ref, s1_ref)
        return stage(2 * j + 1, carry, s1_ref, s0_ref)

    carry = (jnp.full((1, lanes), NEG_MASK, F32), jnp.zeros((V_ROWS, lanes), F32))
    carry = lax.fori_loop(0, n_pairs - 1, pair_body, carry)
    last = 2 * n_pairs - 1
    carry = stage(last - 1, carry, s0_ref, s1_ref)
    m_s, acc_s = stage(last, carry, s1_ref, None)
    acc_s = acc_s + pv_chunk(last)

    sd = jnp.dot(ks_ref[0, 0, pl.ds(q0, TQ), :], qb, preferred_element_type=F32)
    kd = q0 + lax.broadcasted_iota(jnp.int32, (TQ, lanes), 0)
    sd = jnp.where(kd <= t_lane, sd, NEG_MASK)
    m_d = jnp.max(sd, axis=0, keepdims=True)
    pd = jnp.exp2(sd - m_d).astype(MXU_DTYPE)
    acc_d = jnp.dot(vs_ref[0, :, pl.ds(q0, TQ)], pd, preferred_element_type=F32)
    m_all = jnp.maximum(m_s, m_d)
    acc_s = jnp.exp2(m_s - m_all) * acc_s + jnp.exp2(m_d - m_all) * acc_d
    o_sel = acc_s[0:HEAD_DIM] * (1.0 / acc_s[HEAD_DIM:HEAD_DIM + 1])

    gates = gate_ref[0]
    def gate_row(j):
        return jnp.concatenate([gates[j * HPG + h:j * HPG + h + 1, :] for h in range(HPG)], axis=1)
    o_t = gate_row(0) * o_cmp + gate_row(1) * o_sel + gate_row(2) * o_win
    for hp in range(HPG // 2):
        pair = jnp.concatenate([o_t[:, (2 * hp) * TQ:(2 * hp + 1) * TQ],
                                o_t[:, (2 * hp + 1) * TQ:(2 * hp + 2) * TQ]], axis=0)
        o_ref[0, :, hp * 2 * HEAD_DIM:(hp + 1) * 2 * HEAD_DIM] = pair.T.astype(o_ref.dtype)


def _attention(q_t, gates_t, k_cmp, v_cmp_t, k_sel, k_win, v_t):
    b, _, t = q_t.shape
    ncp = k_cmp.shape[2]
    n_blk = t // SEL_BLOCK
    assert t % (2 * TK_SEL) == 0 and F_SEL + n_blk <= F_POS, "unsupported sequence length"
    n_sel = min(N_SEL, n_blk)
    lanes = HPG * TQ
    rows = HPG * HEAD_DIM
    alibi = _alibi_query_features()
    blk_map_t = _block_map_t(ncp, n_blk)
    kernel = functools.partial(_attn_kernel, n_blk=n_blk, n_sel=n_sel)
    return pl.pallas_call(
        kernel,
        grid=(b, N_KV, t // TQ),
        in_specs=[
            pl.BlockSpec((1, rows, TQ), lambda bi, g, i: (bi, g, i)),
            pl.BlockSpec((1, 16, TQ), lambda bi, g, i: (bi, g, i)),
            pl.BlockSpec((1, 1, ncp, FEAT), lambda bi, g, i: (bi, g, 0, 0)),
            pl.BlockSpec((1, 1, V_ROWS, ncp), lambda bi, g, i: (bi, g, 0, 0)),
            pl.BlockSpec((1, 1, t, FEAT), lambda bi, g, i: (bi, g, 0, 0)),
            pl.BlockSpec((1, V_ROWS, t), lambda bi, g, i: (bi, g, 0)),
            pl.BlockSpec((1, 1, t, FEAT), lambda bi, g, i: (bi, g, 0, 0)),
            pl.BlockSpec((1, V_ROWS, t), lambda bi, g, i: (bi, N_KV + g, 0)),
            pl.BlockSpec((1, FEAT - F_POS, lanes), lambda bi, g, i: (g, 0, 0)),
            pl.BlockSpec((n_blk, ncp), lambda bi, g, i: (0, 0)),
        ],
        out_specs=pl.BlockSpec((1, TQ, rows), lambda bi, g, i: (bi, i, g)),
        out_shape=jax.ShapeDtypeStruct((b, t, N_HEADS * HEAD_DIM), MXU_DTYPE),
        scratch_shapes=[pltpu.VMEM((FEAT, lanes), MXU_DTYPE),
                        pltpu.VMEM((FEAT, lanes), MXU_DTYPE),
                        pltpu.VMEM((n_blk, TQ), F32),
                        pltpu.VMEM((TK_SEL, lanes), F32),
                        pltpu.VMEM((TK_SEL, lanes), F32),
                        pltpu.VMEM((TK_SEL, lanes), MXU_DTYPE)],
        compiler_params=_params("parallel", "parallel", "arbitrary"),
        name="nsa_attention",
    )(q_t, gates_t, k_cmp, v_cmp_t, k_sel, v_t, k_win, v_t, alibi, blk_map_t)


def _lru_kernel(hin_ref, wl_ref, cw_ref, cb_ref, wa_ref, ba_ref, wi_ref, bi_ref, lam_ref, o_ref,
                tail_ref, h_ref):
    tt = hin_ref.shape[1]
    w = lam_ref.shape[1]

    @pl.when(pl.program_id(1) == 0)
    def _():
        tail_ref[...] = jnp.zeros_like(tail_ref)
        h_ref[...] = jnp.zeros_like(h_ref)

    hin = hin_ref[0]
    x = jnp.dot(hin, wl_ref[:, 0:w], preferred_element_type=F32)
    gate = jnp.dot(hin, wl_ref[:, w:2 * w], preferred_element_type=F32)
    ng = tt // V7X_SUBLANES
    sub = lax.broadcasted_iota(jnp.int32, (ng, V7X_SUBLANES, w), 1)
    x3 = x.reshape(ng, V7X_SUBLANES, w)
    xprev3 = jnp.concatenate([tail_ref[...][None], x3], axis=0)
    tail_ref[...] = x[tt - V7X_SUBLANES:tt]
    xc = x * cw_ref[CONV_W - 1:CONV_W, :] + cb_ref[...]
    for s in range(1, CONV_W):
        rot = pltpu.roll(xprev3, s, 1)
        xs = jnp.where(sub >= s, rot[1:], rot[:-1])
        xc = xc + xs.reshape(tt, w) * cw_ref[CONV_W - 1 - s:CONV_W - s, :]

    xb = xc.astype(MXU_DTYPE)
    r = _sigmoid(jnp.dot(xb, wa_ref[...], preferred_element_type=F32) + ba_ref[...])
    i = _sigmoid(jnp.dot(xb, wi_ref[...], preferred_element_type=F32) + bi_ref[...])
    z = -lam_ref[...]
    softplus = jnp.maximum(z, 0.0) + jnp.log1p(jnp.exp(-jnp.abs(z)))
    log_a = -LRU_C * r * softplus
    a = jnp.exp(log_a)
    bb = jnp.sqrt(_one_minus_sq(a, log_a)) * (i * xc)

    a3 = a.reshape(ng, V7X_SUBLANES, w)
    b3 = bb.reshape(ng, V7X_SUBLANES, w)
    for d in (1, 2, 4):
        ok = sub >= d
        a_sh = pltpu.roll(a3, d, 1)
        b_sh = pltpu.roll(b3, d, 1)
        b3 = jnp.where(ok, a3 * b_sh + b3, b3)
        a3 = jnp.where(ok, a3 * a_sh, a3)
    carry = h_ref[0:1, :]
    groups = []
    for g in range(ng):
        hg = b3[g] + a3[g] * carry
        groups.append(hg)
        carry = hg[V7X_SUBLANES - 1:V7X_SUBLANES, :]
    hcur = jnp.concatenate(groups, axis=0)
    h_ref[...] = jnp.broadcast_to(carry, h_ref.shape)
    o_ref[0] = (hcur * _gelu_tanh(gate)).astype(o_ref.dtype)


def _rglru(h3d, w_lru, conv_w, conv_b, wa_bd, ba, wi_bd, bi, lam, tt=256):
    b, t, d = h3d.shape
    w = w_lru.shape[1] // 2
    vec = lambda: pl.BlockSpec((1, w), lambda bi_, i: (0, 0))
    return pl.pallas_call(
        _lru_kernel,
        grid=(b, t // tt),
        in_specs=[pl.BlockSpec((1, tt, d), lambda bi_, i: (bi_, i, 0)),
                  pl.BlockSpec((d, 2 * w), lambda bi_, i: (0, 0)),
                  pl.BlockSpec((CONV_W, w), lambda bi_, i: (0, 0)), vec(),
                  pl.BlockSpec((w, w), lambda bi_, i: (0, 0)), vec(),
                  pl.BlockSpec((w, w), lambda bi_, i: (0, 0)), vec(), vec()],
        out_specs=pl.BlockSpec((1, tt, w), lambda bi_, i: (bi_, i, 0)),
        out_shape=jax.ShapeDtypeStruct((b, t, w), MXU_DTYPE),
        scratch_shapes=[pltpu.VMEM((V7X_SUBLANES, w), F32), pltpu.VMEM((V7X_SUBLANES, w), F32)],
        compiler_params=_params("parallel", "arbitrary"),
        name="rglru",
    )(h3d, w_lru, conv_w, conv_b, wa_bd, ba, wi_bd, bi, lam)


def _merge_kernel(attn_ref, lru_ref, h_ref, x_ref, wm_ref, wa_ref, wl_ref, wo_ref, g2_ref,
                  x1_ref, h2_ref):
    d = x_ref.shape[1]
    h = h_ref[...]
    mg0 = _sigmoid(jnp.dot(h, wm_ref[:, 0:d], preferred_element_type=F32))
    mg1 = _sigmoid(jnp.dot(h, wm_ref[:, d:2 * d], preferred_element_type=F32))
    ya = jnp.dot(attn_ref[...], wa_ref[...], preferred_element_type=F32)
    yl = jnp.dot(lru_ref[...], wl_ref[...], preferred_element_type=F32)
    merged = mg0 * ya + mg1 * yl
    x1 = x_ref[...] + jnp.dot(merged.astype(MXU_DTYPE), wo_ref[...], preferred_element_type=F32)
    x1_ref[...] = x1
    ms = jnp.mean(x1 * x1, axis=-1, keepdims=True)
    h2_ref[...] = (x1 * lax.rsqrt(ms + EPS) * g2_ref[...]).astype(h2_ref.dtype)


def _merge(attn, lru, h2d, x2d, w_mg, wa, wl, wo, g2, tm=256):
    m, d = x2d.shape
    row = pl.BlockSpec((tm, d), lambda i: (i, 0))
    full = lambda a: pl.BlockSpec(a.shape, lambda i: (0, 0))
    g2 = g2.reshape(1, d)
    return pl.pallas_call(
        _merge_kernel,
        grid=(m // tm,),
        in_specs=[row, row, row, row, full(w_mg), full(wa), full(wl), full(wo), full(g2)],
        out_specs=[row, row],
        out_shape=[jax.ShapeDtypeStruct((m, d), F32), jax.ShapeDtypeStruct((m, d), MXU_DTYPE)],
        compiler_params=_params("parallel"),
        name="merge_out",
    )(attn, lru, h2d, x2d, w_mg, wa, wl, wo, g2)


def _ffn_kernel(h_ref, x1_ref, wg_ref, wu_ref, wd_ref, o_ref):
    @pl.when(pl.program_id(1) == 0)
    def _():
        o_ref[...] = x1_ref[...]

    h = h_ref[...]
    g = jnp.dot(h, wg_ref[...], preferred_element_type=F32)
    u = jnp.dot(h, wu_ref[...], preferred_element_type=F32)
    act = (g * _sigmoid(g) * u).astype(MXU_DTYPE)
    o_ref[...] += jnp.dot(act, wd_ref[...], preferred_element_type=F32)


def _ffn(h2, x1, wg, wu, wd, tm=1024, tf=256):
    m, d = x1.shape
    f = wg.shape[1]
    return pl.pallas_call(
        _ffn_kernel,
        grid=(m // tm, f // tf),
        in_specs=[pl.BlockSpec((tm, d), lambda i, j: (i, 0)),
                  pl.BlockSpec((tm, d), lambda i, j: (i, 0)),
                  pl.BlockSpec((d, tf), lambda i, j: (0, j)),
                  pl.BlockSpec((d, tf), lambda i, j: (0, j)),
                  pl.BlockSpec((tf, d), lambda i, j: (j, 0))],
        out_specs=pl.BlockSpec((tm, d), lambda i, j: (i, 0)),
        out_shape=jax.ShapeDtypeStruct((m, d), F32),
        compiler_params=_params("parallel", "arbitrary"),
        name="swiglu_ffn",
    )(h2, x1, wg, wu, wd)


def _pad_last(a, n):
    return jnp.pad(a, [(0, 0)] * (a.ndim - 1) + [(0, n - a.shape[-1])])


def _layer(x, norm1_g, w_in, q_norm_g, k_norm_g, cmp_pos_k, cmp_w1_k, cmp_b1_k, cmp_w2_k, cmp_b2_k,
           cmp_pos_v, cmp_w1_v, cmp_b1_v, cmp_w2_v, cmp_b2_v, conv_w, conv_b, lru_wa, lru_ba,
           lru_wi, lru_bi, lru_lambda, w_o_attn, w_o_lru, w_out, norm2_g, w_gate, w_up, w_down):
    b, t, d = x.shape
    m = b * t
    attn_dim = N_HEADS * HEAD_DIM
    kv_dim = N_KV * HEAD_DIM
    lru_w = lru_lambda.shape[0]
    assert t % 512 == 0
    o1 = attn_dim
    o2 = o1 + 6 * kv_dim
    o3 = o2 + 3 * N_HEADS
    o4 = o3 + lru_w
    o5 = o4 + lru_w
    cast = lambda a: a.astype(MXU_DTYPE)

    wq_t = cast(w_in[:, :o1].T)
    w_kv = w_in[:, o1:o2].reshape(d, 6, N_KV, HEAD_DIM)
    w_cmp_src = cast(w_kv[:, 0:2].reshape(d, 2 * kv_dim))
    w_k2 = cast(jnp.stack([w_kv[:, 2], w_kv[:, 4]], axis=1).reshape(d, 2 * kv_dim))
    k_gain2 = jnp.stack([jnp.tile(k_norm_g[1], N_KV), jnp.tile(k_norm_g[2], N_KV)])
    lane_grp = np.arange(kv_dim) // HEAD_DIM
    grp_avg = cast(jnp.asarray((lane_grp[:, None] == lane_grp[None, :]) / HEAD_DIM, F32))
    place_np = np.zeros((kv_dim, N_KV * FEAT), np.float32)
    place_np[np.arange(kv_dim), lane_grp * FEAT + np.arange(kv_dim) % HEAD_DIM] = 1.0
    place = cast(jnp.asarray(place_np))
    w_v = jnp.stack([w_kv[:, 3], w_kv[:, 5]], axis=1)
    w_v_t = cast(_pad_last(w_v, V_ROWS).reshape(d, 2 * N_KV * V_ROWS).T)
    v_ones_col = jnp.tile(jnp.arange(V_ROWS) == HEAD_DIM, 2 * N_KV).astype(F32).reshape(-1, 1)
    w_g = w_in[:, o2:o3].reshape(d, N_KV, HPG, 3).transpose(0, 1, 3, 2).reshape(d, N_KV, 3 * HPG)
    w_g_t = cast(_pad_last(w_g, 16).reshape(d, N_KV * 16).T)
    w_lru = cast(w_in[:, o3:o5])
    w_mg = cast(w_in[:, o5:])
    q_gain_col = jnp.tile(q_norm_g * (HEAD_DIM ** -0.5 * LOG2E), N_HEADS).reshape(attn_dim, 1)
    gain_pad = lambda g: _pad_last(g.reshape(1, HEAD_DIM), FEAT)

    h3d, q_t, v_t, gates_t, k_sel, k_win, cmp_src = _in_proj(
        x, norm1_g.reshape(1, d), wq_t, q_gain_col, w_v_t, v_ones_col, w_g_t, w_k2, k_gain2, grp_avg, place,
        w_cmp_src)
    h2d = h3d.reshape(m, d)

    k_cmp = _compress(cmp_src, 0, cmp_pos_k, cmp_w1_k, cmp_b1_k, cast(_pad_last(cmp_w2_k, FEAT)),
                      _pad_last(cmp_b2_k.reshape(1, HEAD_DIM), FEAT), gain_pad(k_norm_g[0]))
    v_cmp_t = _compress(cmp_src, N_KV // CMP_GROUPS, cmp_pos_v, cmp_w1_v, cmp_b1_v,
                        cast(_pad_last(cmp_w2_v, V_ROWS).T),
                        jnp.concatenate([cmp_b2_v, v_ones_col[HEAD_DIM:V_ROWS, 0]]).reshape(V_ROWS, 1))

    attn = _attention(q_t, gates_t, k_cmp, v_cmp_t, k_sel, k_win, v_t)

    eye = jnp.eye(LRU_BLOCKS, dtype=F32)
    bd = lambda wgt: cast(jnp.einsum('nkj,nm->nkmj', wgt, eye).reshape(lru_w, lru_w))
    vec = lambda v: v.reshape(1, lru_w)
    lru = _rglru(h3d, w_lru, conv_w.reshape(CONV_W, lru_w), vec(conv_b), bd(lru_wa), vec(lru_ba),
                 bd(lru_wi), vec(lru_bi), vec(lru_lambda))

    x1, h2 = _merge(attn.reshape(m, attn_dim), lru.reshape(m, lru_w), h2d, x.reshape(m, d), w_mg,
                    cast(w_o_attn), cast(w_o_lru), cast(w_out), norm2_g)
    out = _ffn(h2, x1, cast(w_gate), cast(w_up), cast(w_down))
    return out.reshape(b, t, d)


def kernel(x, norm1_g, w_in, q_norm_g, k_norm_g, cmp_pos_k, cmp_w1_k, cmp_b1_k, cmp_w2_k, cmp_b2_k,
           cmp_pos_v, cmp_w1_v, cmp_b1_v, cmp_w2_v, cmp_b2_v, conv_w, conv_b, lru_wa, lru_ba,
           lru_wi, lru_bi, lru_lambda, w_o_attn, w_o_lru, w_out, norm2_g, w_gate, w_up, w_down):
    for l in range(norm1_g.shape[0]):
        x = _layer(x, norm1_g[l], w_in[l], q_norm_g[l], k_norm_g[l], cmp_pos_k[l], cmp_w1_k[l],
                   cmp_b1_k[l], cmp_w2_k[l], cmp_b2_k[l], cmp_pos_v[l], cmp_w1_v[l], cmp_b1_v[l],
                   cmp_w2_v[l], cmp_b2_v[l], conv_w[l], conv_b[l], lru_wa[l], lru_ba[l], lru_wi[l],
                   lru_bi[l], lru_lambda[l], w_o_attn[l], w_o_lru[l], w_out[l], norm2_g[l],
                   w_gate[l], w_up[l], w_down[l])
    return x
```

```python
import functools

import numpy as np
import jax
import jax.numpy as jnp
from jax import lax
from jax.experimental import pallas as pl
from jax.experimental.pallas import tpu as pltpu

N_HEADS = 16
HEAD_DIM = 64
N_KV = 4
HPG = N_HEADS // N_KV
CMP_BLOCK = 32
CMP_STRIDE = 16
CMP_HIDDEN = 256
SEL_BLOCK = 64
N_SEL = 16
WINDOW = 512
FORCE_BONUS = 1e4
LRU_BLOCKS = 16
CONV_W = 4
LRU_C = 8.0
EPS = 1e-6

MXU_DTYPE = jnp.bfloat16
F32 = jnp.float32

V7X_LANES = 128
V7X_SUBLANES = 8
V7X_MXU_DIM = 256
V7X_VMEM_LIMIT_BYTES = 48 * 1024 * 1024

TQ = 256
TK_SEL = 512
WIN_KEYS = WINDOW + TQ
V_ROWS = 80
LOG2E = 1.4426950408889634
FEAT = V7X_MXU_DIM
F_SEL = HEAD_DIM
F_POS = 2 * HEAD_DIM
F_CMP = F_POS + 6
NEG_MASK = -1e30
NEG_BLOCK = -(2.0 ** 100)


def _params(*sem):
    return pltpu.CompilerParams(dimension_semantics=sem, vmem_limit_bytes=V7X_VMEM_LIMIT_BYTES)


def _gelu_tanh(x):
    return 0.5 * x * (1.0 + jnp.tanh(0.7978845608028654 * (x + 0.044715 * (x * x * x))))


def _sigmoid(x):
    return 1.0 / (1.0 + jnp.exp(-x))


def _one_minus_sq(a, log_a):
    y = 2.0 * log_a
    series = -y * (1.0 + y * (0.5 + y * (1.0 / 6 + y * (1.0 / 24))))
    return jnp.where(y > -1.0 / 64, series, 1.0 - a * a)


def _nt_dot(wt, h):
    return lax.dot_general(wt, h, (((1,), (1,)), ((), ())), preferred_element_type=F32)


def _token_features(pos, col, with_block_mask):
    blk = pos >> 6
    off = pos & (SEL_BLOCK - 1)
    feat = jnp.where((col >= F_POS) & (col < F_POS + 3), blk.astype(F32),
                     jnp.where((col >= F_POS + 3) & (col < F_POS + 6), off.astype(F32), 0.0))
    if with_block_mask:
        feat = jnp.where((col >= F_SEL) & (col - F_SEL == blk) & (col < F_POS), NEG_BLOCK, feat)
    return feat


def _in_proj_kernel(x_ref, g1_ref, wq_ref, qg_ref, wv_ref, vb_ref, wg_ref, wk_ref, kg_ref, grp_ref,
                    place_ref, wc_ref, h_ref, qt_ref, vt_ref, gt_ref, ks_ref, kw_ref, cs_ref):
    x = x_ref[0]
    tm = x.shape[0]
    ms = jnp.mean(x * x, axis=-1, keepdims=True)
    h = (x * lax.rsqrt(ms + EPS) * g1_ref[...]).astype(MXU_DTYPE)
    h_ref[0] = h

    r3 = _nt_dot(wq_ref[...], h).reshape(N_HEADS, HEAD_DIM, tm)
    qn = r3 * lax.rsqrt(jnp.mean(r3 * r3, axis=1, keepdims=True) + EPS)
    qt_ref[0] = (qn.reshape(N_HEADS * HEAD_DIM, tm) * qg_ref[...]).astype(qt_ref.dtype)

    vt_ref[0] = (_nt_dot(wv_ref[...], h) + vb_ref[...]).astype(vt_ref.dtype)
    gt_ref[0] = _sigmoid(_nt_dot(wg_ref[...], h))

    k = jnp.dot(h, wk_ref[...], preferred_element_type=F32)
    kk = k * k
    kk_hi = kk.astype(MXU_DTYPE)
    kk_lo = (kk - kk_hi.astype(F32)).astype(MXU_DTYPE)
    pos = pl.program_id(1) * tm + lax.broadcasted_iota(jnp.int32, (tm, FEAT), 0)
    col = lax.broadcasted_iota(jnp.int32, (tm, FEAT), 1)
    kv_dim = N_KV * HEAD_DIM
    for branch, (o_ref, with_block_mask) in enumerate(((ks_ref, True), (kw_ref, False))):
        sl = slice(branch * kv_dim, (branch + 1) * kv_dim)
        msq = (jnp.dot(kk_hi[:, sl], grp_ref[...], preferred_element_type=F32)
               + jnp.dot(kk_lo[:, sl], grp_ref[...], preferred_element_type=F32))
        kn = (k[:, sl] * lax.rsqrt(msq + EPS) * kg_ref[branch:branch + 1, :]).astype(MXU_DTYPE)
        placed = jnp.dot(kn, place_ref[...], preferred_element_type=F32)
        feat = _token_features(pos, col, with_block_mask)
        for g in range(N_KV):
            o_ref[0, g] = (placed[:, g * FEAT:(g + 1) * FEAT] + feat).astype(o_ref.dtype)

    cs_ref[0] = jnp.dot(h, wc_ref[...], preferred_element_type=F32)


def _in_proj(x, g1, wq_t, q_gain_col, w_v_t, v_bias_col, w_g_t, w_k2, k_gain2, grp_avg, place, w_cmp_src,
             tm=512):
    b, t, d = x.shape
    kv_dim = N_KV * HEAD_DIM
    full = lambda a: pl.BlockSpec(a.shape, lambda bi, i: (0,) * a.ndim)
    rowblk = lambda n: pl.BlockSpec((1, tm, n), lambda bi, i: (bi, i, 0))
    colblk = lambda n: pl.BlockSpec((1, n, tm), lambda bi, i: (bi, 0, i))
    kblk = pl.BlockSpec((1, N_KV, tm, FEAT), lambda bi, i: (bi, 0, i, 0))
    weights = (g1, wq_t, q_gain_col, w_v_t, v_bias_col, w_g_t, w_k2, k_gain2, grp_avg, place, w_cmp_src)
    return pl.pallas_call(
        _in_proj_kernel,
        grid=(b, t // tm),
        in_specs=[rowblk(d)] + [full(a) for a in weights],
        out_specs=[rowblk(d), colblk(wq_t.shape[0]), colblk(w_v_t.shape[0]), colblk(w_g_t.shape[0]),
                   kblk, kblk, rowblk(2 * kv_dim)],
        out_shape=[jax.ShapeDtypeStruct((b, t, d), MXU_DTYPE),
                   jax.ShapeDtypeStruct((b, wq_t.shape[0], t), MXU_DTYPE),
                   jax.ShapeDtypeStruct((b, w_v_t.shape[0], t), MXU_DTYPE),
                   jax.ShapeDtypeStruct((b, w_g_t.shape[0], t), F32),
                   jax.ShapeDtypeStruct((b, N_KV, t, FEAT), MXU_DTYPE),
                   jax.ShapeDtypeStruct((b, N_KV, t, FEAT), MXU_DTYPE),
                   jax.ShapeDtypeStruct((b, t, 2 * kv_dim), F32)],
        compiler_params=_params("parallel", "parallel"),
        name="in_proj",
    )(x, *weights)


CMP_GROUPS = V7X_LANES // HEAD_DIM


def _cmp_hidden(src_ref, pos_ref, w1_ref, b1_ref):
    ncp = src_ref.shape[1] // CMP_STRIDE
    first = jnp.zeros((ncp, CMP_GROUPS * CMP_HIDDEN), F32)
    second = jnp.zeros((ncp, CMP_GROUPS * CMP_HIDDEN), F32)
    for l in range(CMP_STRIDE):
        x = src_ref[0, pl.ds(l, ncp, stride=CMP_STRIDE), :]
        lo = (x + pos_ref[l:l + 1, :]).astype(MXU_DTYPE)
        hi = (x + pos_ref[CMP_STRIDE + l:CMP_STRIDE + l + 1, :]).astype(MXU_DTYPE)
        first = first + jnp.dot(lo, w1_ref[l], preferred_element_type=F32)
        second = second + jnp.dot(hi, w1_ref[CMP_STRIDE + l], preferred_element_type=F32)
    hid = first + pltpu.roll(second, ncp - 1, 0) + b1_ref[...]
    return _gelu_tanh(hid).astype(MXU_DTYPE)


def _cmp_k_kernel(src_ref, pos_ref, w1_ref, b1_ref, w2_ref, b2_ref, g_ref, o_ref):
    hid = _cmp_hidden(src_ref, pos_ref, w1_ref, b1_ref)
    ncp = hid.shape[0]
    idx = lax.broadcasted_iota(jnp.int32, (ncp, FEAT), 0)
    col = lax.broadcasted_iota(jnp.int32, (ncp, FEAT), 1)
    feat = jnp.where((col >= F_CMP) & (col < F_CMP + 3), (idx >> 6).astype(F32),
                     jnp.where((col >= F_CMP + 3) & (col < F_CMP + 6), (idx & 63).astype(F32), 0.0))
    for gl in range(CMP_GROUPS):
        r = jnp.dot(hid[:, gl * CMP_HIDDEN:(gl + 1) * CMP_HIDDEN], w2_ref[...],
                    preferred_element_type=F32) + b2_ref[...]
        ms = jnp.sum(r * r, axis=-1, keepdims=True) * (1.0 / HEAD_DIM)
        o_ref[0, gl] = (r * lax.rsqrt(ms + EPS) * g_ref[...] + feat).astype(o_ref.dtype)


def _cmp_v_kernel(src_ref, pos_ref, w1_ref, b1_ref, w2t_ref, b2_ref, o_ref):
    hid = _cmp_hidden(src_ref, pos_ref, w1_ref, b1_ref)
    for gl in range(CMP_GROUPS):
        r = _nt_dot(w2t_ref[...], hid[:, gl * CMP_HIDDEN:(gl + 1) * CMP_HIDDEN]) + b2_ref[...]
        o_ref[0, gl] = r.astype(o_ref.dtype)


def _compress(cmp_src, lane_block0, pos, w1, b1, w2, b2, gain_pad=None):
    b, t, _ = cmp_src.shape
    ncp = t // CMP_STRIDE
    hid_w = CMP_GROUPS * CMP_HIDDEN
    eye = jnp.eye(CMP_GROUPS, dtype=w1.dtype)
    w1_bd = jnp.einsum('ldf,gh->lgdhf', w1, eye).reshape(CMP_BLOCK, V7X_LANES, hid_w).astype(MXU_DTYPE)
    pos_t = jnp.tile(pos, (1, CMP_GROUPS))
    b1_t = jnp.tile(b1.reshape(1, CMP_HIDDEN), (1, CMP_GROUPS))
    full = lambda a: pl.BlockSpec(a.shape, lambda bi, p: (0,) * a.ndim)
    src_spec = pl.BlockSpec((1, t, V7X_LANES), lambda bi, p: (bi, 0, lane_block0 + p))
    grid = (b, N_KV // CMP_GROUPS)
    if gain_pad is not None:
        args = (pos_t, w1_bd, b1_t, w2, b2, gain_pad)
        return pl.pallas_call(
            _cmp_k_kernel,
            grid=grid,
            in_specs=[src_spec] + [full(a) for a in args],
            out_specs=pl.BlockSpec((1, CMP_GROUPS, ncp, FEAT), lambda bi, p: (bi, p, 0, 0)),
            out_shape=jax.ShapeDtypeStruct((b, N_KV, ncp, FEAT), MXU_DTYPE),
            compiler_params=_params("parallel", "parallel"),
            name="compress_k",
        )(cmp_src, *args)
    args = (pos_t, w1_bd, b1_t, w2, b2)
    return pl.pallas_call(
        _cmp_v_kernel,
        grid=grid,
        in_specs=[src_spec] + [full(a) for a in args],
        out_specs=pl.BlockSpec((1, CMP_GROUPS, V_ROWS, ncp), lambda bi, p: (bi, p, 0, 0)),
        out_shape=jax.ShapeDtypeStruct((b, N_KV, V_ROWS, ncp), MXU_DTYPE),
        compiler_params=_params("parallel", "parallel"),
        name="compress_v",
    )(cmp_src, *args)


def _split3(v):
    parts = []
    rest = np.asarray(v, np.float64)
    for _ in range(3):
        p = rest.astype(np.float32).astype(jnp.bfloat16).astype(np.float64)
        parts.append(p)
        rest = rest - p
    return parts


def _alibi_query_features():
    tab = np.zeros((N_KV, FEAT - F_POS, HPG * TQ), np.float64)
    for g in range(N_KV):
        for h in range(HPG):
            slope = 2.0 ** (-8.0 * (g * HPG + h + 1) / N_HEADS)
            parts = _split3(slope * LOG2E)
            lanes = slice(h * TQ, (h + 1) * TQ)
            for i, p in enumerate(parts):
                tab[g, i, lanes] = SEL_BLOCK * p
                tab[g, 3 + i, lanes] = p
                tab[g, 6 + i, lanes] = CMP_STRIDE * 64 * p
                tab[g, 9 + i, lanes] = CMP_STRIDE * p
    return jnp.asarray(tab, F32).astype(MXU_DTYPE)


def _block_map_t(n_cmp_pad, n_blk):
    cs = np.arange(n_cmp_pad) * CMP_STRIDE
    ce = cs + CMP_BLOCK - 1
    bs = np.arange(n_blk) * SEL_BLOCK
    be = bs + SEL_BLOCK - 1
    return jnp.asarray(((cs[None, :] <= be[:, None]) & (ce[None, :] >= bs[:, None])).astype(np.float32))


def _attn_kernel(qt_ref, gate_ref, kc_ref, vc_ref, ks_ref, vs_ref, kw_ref, vw_ref, alibi_ref, map_ref,
                 o_ref, qb_ref, qs_ref, imp_ref, s0_ref, s1_ref, p_ref, m_ref, acc_ref, *, n_blk, n_sel):
    lanes = HPG * TQ
    qi = pl.program_id(2)
    q0 = pl.multiple_of(qi * TQ, TQ)
    ncp = kc_ref.shape[2]

    for h in range(HPG):
        qb_ref[0:HEAD_DIM, h * TQ:(h + 1) * TQ] = qt_ref[0, h * HEAD_DIM:(h + 1) * HEAD_DIM, :]
    qb_ref[F_SEL:F_POS, :] = jnp.zeros((F_POS - F_SEL, lanes), qb_ref.dtype)
    qb_ref[F_POS:FEAT, :] = alibi_ref[0]
    qb = qb_ref[...]

    t_lane = q0 + (lax.broadcasted_iota(jnp.int32, (1, lanes), 1) & (TQ - 1))

    sc = jnp.dot(kc_ref[0, 0], qb, preferred_element_type=F32)
    last_cmp = (t_lane - (CMP_BLOCK - 1)) >> 4
    sc = jnp.where(lax.broadcasted_iota(jnp.int32, (ncp, lanes), 0) <= last_cmp, sc, NEG_MASK)
    ec = jnp.exp2(sc - jnp.max(sc, axis=0, keepdims=True))
    acc_c = jnp.dot(vc_ref[0, 0], ec.astype(MXU_DTYPE), preferred_element_type=F32)
    inv_c = jnp.where(last_cmp >= 0, 1.0 / jnp.maximum(acc_c[HEAD_DIM:HEAD_DIM + 1], 1e-30), 0.0)
    o_cmp = acc_c[0:HEAD_DIM] * inv_c

    psum = ec[:, 0:TQ] * inv_c[:, 0:TQ]
    for h in range(1, HPG):
        psum = psum + ec[:, h * TQ:(h + 1) * TQ] * inv_c[:, h * TQ:(h + 1) * TQ]
    imp = jnp.dot(map_ref[...], psum, preferred_element_type=F32)
    t_q = q0 + lax.broadcasted_iota(jnp.int32, (n_blk, TQ), 1)
    blk = lax.broadcasted_iota(jnp.int32, (n_blk, TQ), 0)
    cur = t_q >> 6
    forced = (blk == 0) | (blk == cur) | (blk == cur - 1)
    valid = blk <= cur
    imp = jnp.where(valid, imp + jnp.where(forced, FORCE_BONUS, 0.0), NEG_MASK)
    imp_ref[...] = imp
    first_own_blk = qi * (TQ // SEL_BLOCK)
    n_chunks = n_blk // V7X_SUBLANES
    chunks = [imp[c * V7X_SUBLANES:(c + 1) * V7X_SUBLANES] for c in range(n_chunks)]
    ranks = [jnp.zeros((V7X_SUBLANES, TQ), jnp.int32) for _ in range(n_chunks)]
    sub = lax.broadcasted_iota(jnp.int32, (V7X_SUBLANES, TQ), 0)
    for k in range(n_blk):
        row = imp_ref[k:k + 1, :]
        for c in range(n_chunks):
            lo = c * V7X_SUBLANES
            if lo > k:
                one = jnp.where(row >= chunks[c], 1, 0)
            elif lo + V7X_SUBLANES - 1 <= k:
                one = jnp.where(row > chunks[c], 1, 0)
            else:
                one = jnp.where(sub + lo > k, jnp.where(row >= chunks[c], 1, 0),
                                jnp.where(row > chunks[c], 1, 0))
            ranks[c] = ranks[c] + one
    rank = jnp.concatenate(ranks, axis=0)
    not_sel = jnp.where((rank < n_sel) & (blk < first_own_blk), 0.0, 1.0).astype(qs_ref.dtype)
    qs_ref[...] = qb
    for h in range(HPG):
        qs_ref[F_SEL:F_SEL + n_blk, h * TQ:(h + 1) * TQ] = not_sel

    w0 = pl.multiple_of(jnp.maximum(q0 - WINDOW, 0), TQ)
    sw = jnp.dot(kw_ref[0, 0, pl.ds(w0, WIN_KEYS), :], qb, preferred_element_type=F32)
    dw = (t_lane - w0) - lax.broadcasted_iota(jnp.int32, (WIN_KEYS, lanes), 0)
    sw = jnp.where(lax.bitcast_convert_type(dw, jnp.uint32) < WINDOW, sw, NEG_MASK)
    pw = jnp.exp2(sw - jnp.max(sw, axis=0, keepdims=True))
    acc_w = jnp.dot(vw_ref[0, :, pl.ds(w0, WIN_KEYS)], pw.astype(MXU_DTYPE), preferred_element_type=F32)
    o_win = acc_w[0:HEAD_DIM] * (1.0 / acc_w[HEAD_DIM:HEAD_DIM + 1])

    def qk_chunk(c):
        k0 = pl.multiple_of(c * TK_SEL, TK_SEL)
        return jnp.dot(ks_ref[0, 0, pl.ds(k0, TK_SEL), :], qs_ref[...], preferred_element_type=F32)

    def pv_chunk(c):
        k0 = pl.multiple_of(c * TK_SEL, TK_SEL)
        return jnp.dot(vs_ref[0, :, pl.ds(k0, TK_SEL)], p_ref[...], preferred_element_type=F32)

    def stage(k, carry, s_cur_ref, s_next_ref):
        m, acc = carry
        if s_next_ref is not None:
            s_next_ref[...] = qk_chunk(k + 1)
        pv_prev = pv_chunk(jnp.maximum(k - 1, 0))
        s = s_cur_ref[...]
        m_new = jnp.maximum(m, jnp.max(s, axis=0, keepdims=True))
        p_ref[...] = jnp.exp2(s - m_new).astype(p_ref.dtype)
        return m_new, jnp.exp2(m - m_new) * (acc + pv_prev)

    n_chunks_sel = jnp.maximum((qi * TQ + TK_SEL - 1) // TK_SEL, 1)
    s0_ref[...] = qk_chunk(0)
    p_ref[...] = jnp.zeros(p_ref.shape, p_ref.dtype)

    sd = jnp.dot(ks_ref[0, 0, pl.ds(q0, TQ), :], qb, preferred_element_type=F32)
    kd = q0 + lax.broadcasted_iota(jnp.int32, (TQ, lanes), 0)
    sd = jnp.where(kd <= t_lane, sd, NEG_MASK)
    m_d = jnp.max(sd, axis=0, keepdims=True)
    pd = jnp.exp2(sd - m_d).astype(MXU_DTYPE)
    acc_d = jnp.dot(vs_ref[0, :, pl.ds(q0, TQ)], pd, preferred_element_type=F32)

    def pair_body(j, carry):
        carry = stage(2 * j, carry, s0_ref, s1_ref)
        return stage(2 * j + 1, carry, s1_ref, s0_ref)

    carry = (jnp.full((1, lanes), NEG_MASK, F32), jnp.zeros((V_ROWS, lanes), F32))
    carry = lax.fori_loop(0, (n_chunks_sel - 1) // 2, pair_body, carry)
    last = n_chunks_sel - 1

    def finish(carry):
        m_f, acc_f = carry
        m_ref[...] = m_f
        acc_ref[...] = acc_f + pv_chunk(last)

    @pl.when((n_chunks_sel & 1) == 0)
    def _():
        finish(stage(last, stage(last - 1, carry, s0_ref, s1_ref), s1_ref, None))

    @pl.when((n_chunks_sel & 1) == 1)
    def _():
        finish(stage(last, carry, s0_ref, None))

    m_s = m_ref[...]
    m_all = jnp.maximum(m_s, m_d)
    acc_s = jnp.exp2(m_s - m_all) * acc_ref[...] + jnp.exp2(m_d - m_all) * acc_d
    o_sel = acc_s[0:HEAD_DIM] * (1.0 / acc_s[HEAD_DIM:HEAD_DIM + 1])

    gates = gate_ref[0]
    def gate_row(j):
        return jnp.concatenate([gates[j * HPG + h:j * HPG + h + 1, :] for h in range(HPG)], axis=1)
    o_t = gate_row(0) * o_cmp + gate_row(1) * o_sel + gate_row(2) * o_win
    for hp in range(HPG // 2):
        pair = jnp.concatenate([o_t[:, (2 * hp) * TQ:(2 * hp + 1) * TQ],
                                o_t[:, (2 * hp + 1) * TQ:(2 * hp + 2) * TQ]], axis=0)
        o_ref[0, :, hp * 2 * HEAD_DIM:(hp + 1) * 2 * HEAD_DIM] = pair.T.astype(o_ref.dtype)


def _attention(q_t, gates_t, k_cmp, v_cmp_t, k_sel, k_win, v_t):
    b, _, t = q_t.shape
    ncp = k_cmp.shape[2]
    n_blk = t // SEL_BLOCK
    assert t % TK_SEL == 0 and t >= WIN_KEYS and F_SEL + n_blk <= F_POS, "unsupported sequence length"
    n_sel = min(N_SEL, n_blk)
    lanes = HPG * TQ
    rows = HPG * HEAD_DIM
    alibi = _alibi_query_features()
    blk_map_t = _block_map_t(ncp, n_blk)
    kernel = functools.partial(_attn_kernel, n_blk=n_blk, n_sel=n_sel)
    return pl.pallas_call(
        kernel,
        grid=(b, N_KV, t // TQ),
        in_specs=[
            pl.BlockSpec((1, rows, TQ), lambda bi, g, i: (bi, g, i)),
            pl.BlockSpec((1, 16, TQ), lambda bi, g, i: (bi, g, i)),
            pl.BlockSpec((1, 1, ncp, FEAT), lambda bi, g, i: (bi, g, 0, 0)),
            pl.BlockSpec((1, 1, V_ROWS, ncp), lambda bi, g, i: (bi, g, 0, 0)),
            pl.BlockSpec((1, 1, t, FEAT), lambda bi, g, i: (bi, g, 0, 0)),
            pl.BlockSpec((1, V_ROWS, t), lambda bi, g, i: (bi, g, 0)),
            pl.BlockSpec((1, 1, t, FEAT), lambda bi, g, i: (bi, g, 0, 0)),
            pl.BlockSpec((1, V_ROWS, t), lambda bi, g, i: (bi, N_KV + g, 0)),
            pl.BlockSpec((1, FEAT - F_POS, lanes), lambda bi, g, i: (g, 0, 0)),
            pl.BlockSpec((n_blk, ncp), lambda bi, g, i: (0, 0)),
        ],
        out_specs=pl.BlockSpec((1, TQ, rows), lambda bi, g, i: (bi, i, g)),
        out_shape=jax.ShapeDtypeStruct((b, t, N_HEADS * HEAD_DIM), MXU_DTYPE),
        scratch_shapes=[pltpu.VMEM((FEAT, lanes), MXU_DTYPE),
                        pltpu.VMEM((FEAT, lanes), MXU_DTYPE),
                        pltpu.VMEM((n_blk, TQ), F32),
                        pltpu.VMEM((TK_SEL, lanes), F32),
                        pltpu.VMEM((TK_SEL, lanes), F32),
                        pltpu.VMEM((TK_SEL, lanes), MXU_DTYPE),
                        pltpu.VMEM((1, lanes), F32),
                        pltpu.VMEM((V_ROWS, lanes), F32)],
        compiler_params=_params("parallel", "parallel", "arbitrary"),
        name="nsa_attention",
    )(q_t, gates_t, k_cmp, v_cmp_t, k_sel, v_t, k_win, v_t, alibi, blk_map_t)


def _lru_kernel(hin_ref, wl_ref, cw_ref, cb_ref, wa_ref, ba_ref, wi_ref, bi_ref, lam_ref, o_ref,
                tail_ref, h_ref):
    tt = hin_ref.shape[1]
    w = lam_ref.shape[1]

    @pl.when(pl.program_id(1) == 0)
    def _():
        tail_ref[...] = jnp.zeros_like(tail_ref)
        h_ref[...] = jnp.zeros_like(h_ref)

    hin = hin_ref[0]
    x = jnp.dot(hin, wl_ref[:, 0:w], preferred_element_type=F32)
    gate = jnp.dot(hin, wl_ref[:, w:2 * w], preferred_element_type=F32)
    ng = tt // V7X_SUBLANES
    sub = lax.broadcasted_iota(jnp.int32, (ng, V7X_SUBLANES, w), 1)
    x3 = x.reshape(ng, V7X_SUBLANES, w)
    xprev3 = jnp.concatenate([tail_ref[...][None], x3], axis=0)
    tail_ref[...] = x[tt - V7X_SUBLANES:tt]
    xc = x * cw_ref[CONV_W - 1:CONV_W, :] + cb_ref[...]
    for s in range(1, CONV_W):
        rot = pltpu.roll(xprev3, s, 1)
        xs = jnp.where(sub >= s, rot[1:], rot[:-1])
        xc = xc + xs.reshape(tt, w) * cw_ref[CONV_W - 1 - s:CONV_W - s, :]

    xb = xc.astype(MXU_DTYPE)
    r = _sigmoid(jnp.dot(xb, wa_ref[...], preferred_element_type=F32) + ba_ref[...])
    i = _sigmoid(jnp.dot(xb, wi_ref[...], preferred_element_type=F32) + bi_ref[...])
    z = -lam_ref[...]
    softplus = jnp.maximum(z, 0.0) + jnp.log1p(jnp.exp(-jnp.abs(z)))
    log_a = -LRU_C * r * softplus
    a = jnp.exp(log_a)
    bb = jnp.sqrt(_one_minus_sq(a, log_a)) * (i * xc)

    a3 = a.reshape(ng, V7X_SUBLANES, w)
    b3 = bb.reshape(ng, V7X_SUBLANES, w)
    for d in (1, 2, 4):
        ok = sub >= d
        a_sh = pltpu.roll(a3, d, 1)
        b_sh = pltpu.roll(b3, d, 1)
        b3 = jnp.where(ok, a3 * b_sh + b3, b3)
        a3 = jnp.where(ok, a3 * a_sh, a3)
    carry = h_ref[0:1, :]
    groups = []
    for g in range(ng):
        hg = b3[g] + a3[g] * carry
        groups.append(hg)
        carry = hg[V7X_SUBLANES - 1:V7X_SUBLANES, :]
    hcur = jnp.concatenate(groups, axis=0)
    h_ref[...] = jnp.broadcast_to(carry, h_ref.shape)
    o_ref[0] = (hcur * _gelu_tanh(gate)).astype(o_ref.dtype)


def _rglru(h3d, w_lru, conv_w, conv_b, wa_bd, ba, wi_bd, bi, lam, tt=256):
    b, t, d = h3d.shape
    w = w_lru.shape[1] // 2
    vec = lambda: pl.BlockSpec((1, w), lambda bi_, i: (0, 0))
    return pl.pallas_call(
        _lru_kernel,
        grid=(b, t // tt),
        in_specs=[pl.BlockSpec((1, tt, d), lambda bi_, i: (bi_, i, 0)),
                  pl.BlockSpec((d, 2 * w), lambda bi_, i: (0, 0)),
                  pl.BlockSpec((CONV_W, w), lambda bi_, i: (0, 0)), vec(),
                  pl.BlockSpec((w, w), lambda bi_, i: (0, 0)), vec(),
                  pl.BlockSpec((w, w), lambda bi_, i: (0, 0)), vec(), vec()],
        out_specs=pl.BlockSpec((1, tt, w), lambda bi_, i: (bi_, i, 0)),
        out_shape=jax.ShapeDtypeStruct((b, t, w), MXU_DTYPE),
        scratch_shapes=[pltpu.VMEM((V7X_SUBLANES, w), F32), pltpu.VMEM((V7X_SUBLANES, w), F32)],
        compiler_params=_params("parallel", "arbitrary"),
        name="rglru",
    )(h3d, w_lru, conv_w, conv_b, wa_bd, ba, wi_bd, bi, lam)


def _merge_kernel(attn_ref, lru_ref, h_ref, x_ref, wm_ref, wa_ref, wl_ref, wo_ref, g2_ref,
                  x1_ref, h2_ref):
    d = x_ref.shape[1]
    h = h_ref[...]
    mg0 = _sigmoid(jnp.dot(h, wm_ref[:, 0:d], preferred_element_type=F32))
    mg1 = _sigmoid(jnp.dot(h, wm_ref[:, d:2 * d], preferred_element_type=F32))
    ya = jnp.dot(attn_ref[...], wa_ref[...], preferred_element_type=F32)
    yl = jnp.dot(lru_ref[...], wl_ref[...], preferred_element_type=F32)
    merged = mg0 * ya + mg1 * yl
    x1 = x_ref[...] + jnp.dot(merged.astype(MXU_DTYPE), wo_ref[...], preferred_element_type=F32)
    x1_ref[...] = x1
    ms = jnp.mean(x1 * x1, axis=-1, keepdims=True)
    h2_ref[...] = (x1 * lax.rsqrt(ms + EPS) * g2_ref[...]).astype(h2_ref.dtype)


def _merge(attn, lru, h2d, x2d, w_mg, wa, wl, wo, g2, tm=256):
    m, d = x2d.shape
    row = pl.BlockSpec((tm, d), lambda i: (i, 0))
    full = lambda a: pl.BlockSpec(a.shape, lambda i: (0, 0))
    g2 = g2.reshape(1, d)
    return pl.pallas_call(
        _merge_kernel,
        grid=(m // tm,),
        in_specs=[row, row, row, row, full(w_mg), full(wa), full(wl), full(wo), full(g2)],
        out_specs=[row, row],
        out_shape=[jax.ShapeDtypeStruct((m, d), F32), jax.ShapeDtypeStruct((m, d), MXU_DTYPE)],
        compiler_params=_params("parallel"),
        name="merge_out",
    )(attn, lru, h2d, x2d, w_mg, wa, wl, wo, g2)


def _ffn_kernel(h_ref, x1_ref, wg_ref, wu_ref, wd_ref, o_ref):
    @pl.when(pl.program_id(1) == 0)
    def _():
        o_ref[...] = x1_ref[...]

    h = h_ref[...]
    g = jnp.dot(h, wg_ref[...], preferred_element_type=F32)
    u = jnp.dot(h, wu_ref[...], preferred_element_type=F32)
    act = (g * _sigmoid(g) * u).astype(MXU_DTYPE)
    o_ref[...] += jnp.dot(act, wd_ref[...], preferred_element_type=F32)


def _ffn(h2, x1, wg, wu, wd, tm=1024, tf=256):
    m, d = x1.shape
    f = wg.shape[1]
    return pl.pallas_call(
        _ffn_kernel,
        grid=(m // tm, f // tf),
        in_specs=[pl.BlockSpec((tm, d), lambda i, j: (i, 0)),
                  pl.BlockSpec((tm, d), lambda i, j: (i, 0)),
                  pl.BlockSpec((d, tf), lambda i, j: (0, j)),
                  pl.BlockSpec((d, tf), lambda i, j: (0, j)),
                  pl.BlockSpec((tf, d), lambda i, j: (j, 0))],
        out_specs=pl.BlockSpec((tm, d), lambda i, j: (i, 0)),
        out_shape=jax.ShapeDtypeStruct((m, d), F32),
        compiler_params=_params("parallel", "arbitrary"),
        name="swiglu_ffn",
    )(h2, x1, wg, wu, wd)


def _pad_last(a, n):
    return jnp.pad(a, [(0, 0)] * (a.ndim - 1) + [(0, n - a.shape[-1])])


def _layer(x, norm1_g, w_in, q_norm_g, k_norm_g, cmp_pos_k, cmp_w1_k, cmp_b1_k, cmp_w2_k, cmp_b2_k,
           cmp_pos_v, cmp_w1_v, cmp_b1_v, cmp_w2_v, cmp_b2_v, conv_w, conv_b, lru_wa, lru_ba,
           lru_wi, lru_bi, lru_lambda, w_o_attn, w_o_lru, w_out, norm2_g, w_gate, w_up, w_down):
    b, t, d = x.shape
    m = b * t
    attn_dim = N_HEADS * HEAD_DIM
    kv_dim = N_KV * HEAD_DIM
    lru_w = lru_lambda.shape[0]
    assert t % 512 == 0
    o1 = attn_dim
    o2 = o1 + 6 * kv_dim
    o3 = o2 + 3 * N_HEADS
    o4 = o3 + lru_w
    o5 = o4 + lru_w
    cast = lambda a: a.astype(MXU_DTYPE)

    wq_t = cast(w_in[:, :o1].T)
    w_kv = w_in[:, o1:o2].reshape(d, 6, N_KV, HEAD_DIM)
    w_cmp_src = cast(w_kv[:, 0:2].reshape(d, 2 * kv_dim))
    w_k2 = cast(jnp.stack([w_kv[:, 2], w_kv[:, 4]], axis=1).reshape(d, 2 * kv_dim))
    k_gain2 = jnp.stack([jnp.tile(k_norm_g[1], N_KV), jnp.tile(k_norm_g[2], N_KV)])
    lane_grp = np.arange(kv_dim) // HEAD_DIM
    grp_avg = cast(jnp.asarray((lane_grp[:, None] == lane_grp[None, :]) / HEAD_DIM, F32))
    place_np = np.zeros((kv_dim, N_KV * FEAT), np.float32)
    place_np[np.arange(kv_dim), lane_grp * FEAT + np.arange(kv_dim) % HEAD_DIM] = 1.0
    place = cast(jnp.asarray(place_np))
    w_v = jnp.stack([w_kv[:, 3], w_kv[:, 5]], axis=1)
    w_v_t = cast(_pad_last(w_v, V_ROWS).reshape(d, 2 * N_KV * V_ROWS).T)
    v_ones_col = jnp.tile(jnp.arange(V_ROWS) == HEAD_DIM, 2 * N_KV).astype(F32).reshape(-1, 1)
    w_g = w_in[:, o2:o3].reshape(d, N_KV, HPG, 3).transpose(0, 1, 3, 2).reshape(d, N_KV, 3 * HPG)
    w_g_t = cast(_pad_last(w_g, 16).reshape(d, N_KV * 16).T)
    w_lru = cast(w_in[:, o3:o5])
    w_mg = cast(w_in[:, o5:])
    q_gain_col = jnp.tile(q_norm_g * (HEAD_DIM ** -0.5 * LOG2E), N_HEADS).reshape(attn_dim, 1)
    gain_pad = lambda g: _pad_last(g.reshape(1, HEAD_DIM), FEAT)

    h3d, q_t, v_t, gates_t, k_sel, k_win, cmp_src = _in_proj(
        x, norm1_g.reshape(1, d), wq_t, q_gain_col, w_v_t, v_ones_col, w_g_t, w_k2, k_gain2, grp_avg, place,
        w_cmp_src)
    h2d = h3d.reshape(m, d)

    k_cmp = _compress(cmp_src, 0, cmp_pos_k, cmp_w1_k, cmp_b1_k, cast(_pad_last(cmp_w2_k, FEAT)),
                      _pad_last(cmp_b2_k.reshape(1, HEAD_DIM), FEAT), gain_pad(k_norm_g[0]))
    v_cmp_t = _compress(cmp_src, N_KV // CMP_GROUPS, cmp_pos_v, cmp_w1_v, cmp_b1_v,
                        cast(_pad_last(cmp_w2_v, V_ROWS).T),
                        jnp.concatenate([cmp_b2_v, v_ones_col[HEAD_DIM:V_ROWS, 0]]).reshape(V_ROWS, 1))

    attn = _attention(q_t, gates_t, k_cmp, v_cmp_t, k_sel, k_win, v_t)

    eye = jnp.eye(LRU_BLOCKS, dtype=F32)
    bd = lambda wgt: cast(jnp.einsum('nkj,nm->nkmj', wgt, eye).reshape(lru_w, lru_w))
    vec = lambda v: v.reshape(1, lru_w)
    lru = _rglru(h3d, w_lru, conv_w.reshape(CONV_W, lru_w), vec(conv_b), bd(lru_wa), vec(lru_ba),
                 bd(lru_wi), vec(lru_bi), vec(lru_lambda))

    x1, h2 = _merge(attn.reshape(m, attn_dim), lru.reshape(m, lru_w), h2d, x.reshape(m, d), w_mg,
                    cast(w_o_attn), cast(w_o_lru), cast(w_out), norm2_g)
    out = _ffn(h2, x1, cast(w_gate), cast(w_up), cast(w_down))
    return out.reshape(b, t, d)


def kernel(x, norm1_g, w_in, q_norm_g, k_norm_g, cmp_pos_k, cmp_w1_k, cmp_b1_k, cmp_w2_k, cmp_b2_k,
           cmp_pos_v, cmp_w1_v, cmp_b1_v, cmp_w2_v, cmp_b2_v, conv_w, conv_b, lru_wa, lru_ba,
           lru_wi, lru_bi, lru_lambda, w_o_attn, w_o_lru, w_out, norm2_g, w_gate, w_up, w_down):
    for l in range(norm1_g.shape[0]):
        x = _layer(x, norm1_g[l], w_in[l], q_norm_g[l], k_norm_g[l], cmp_pos_k[l], cmp_w1_k[l],
                   cmp_b1_k[l], cmp_w2_k[l], cmp_b2_k[l], cmp_pos_v[l], cmp_w1_v[l], cmp_b1_v[l],
                   cmp_w2_v[l], cmp_b2_v[l], conv_w[l], conv_b[l], lru_wa[l], lru_ba[l], lru_wi[l],
                   lru_bi[l], lru_lambda[l], w_o_attn[l], w_o_lru[l], w_out[l], norm2_g[l],
                   w_gate[l], w_up[l], w_down[l])
    return x
```

```python
import functools

import numpy as np
import jax
import jax.numpy as jnp
from jax import lax
from jax.experimental import pallas as pl
from jax.experimental.pallas import tpu as pltpu

N_HEADS = 16
HEAD_DIM = 64
N_KV = 4
HPG = N_HEADS // N_KV
CMP_BLOCK = 32
CMP_STRIDE = 16
CMP_HIDDEN = 256
SEL_BLOCK = 64
N_SEL = 16
WINDOW = 512
FORCE_BONUS = 1e4
LRU_BLOCKS = 16
CONV_W = 4
LRU_C = 8.0
EPS = 1e-6

MXU_DTYPE = jnp.bfloat16
F32 = jnp.float32

V7X_LANES = 128
V7X_SUBLANES = 8
V7X_MXU_DIM = 256
V7X_VMEM_LIMIT_BYTES = 48 * 1024 * 1024

TQ = 256
TK_SEL = 512
WIN_KEYS = WINDOW + TQ
V_ROWS = 80
LOG2E = 1.4426950408889634
FEAT = V7X_MXU_DIM
F_SEL = HEAD_DIM
F_POS = 2 * HEAD_DIM
F_CMP = F_POS + 6
NEG_MASK = -1e30
NEG_BLOCK = -(2.0 ** 100)


def _params(*sem):
    return pltpu.CompilerParams(dimension_semantics=sem, vmem_limit_bytes=V7X_VMEM_LIMIT_BYTES)


def _gelu_tanh(x):
    return 0.5 * x * (1.0 + jnp.tanh(0.7978845608028654 * (x + 0.044715 * (x * x * x))))


def _sigmoid(x):
    return 1.0 / (1.0 + jnp.exp(-x))


def _one_minus_sq(a, log_a):
    y = 2.0 * log_a
    series = -y * (1.0 + y * (0.5 + y * (1.0 / 6 + y * (1.0 / 24))))
    return jnp.where(y > -1.0 / 64, series, 1.0 - a * a)


def _nt_dot(wt, h):
    return lax.dot_general(wt, h, (((1,), (1,)), ((), ())), preferred_element_type=F32)


def _token_features(pos, col, with_block_mask):
    blk = pos >> 6
    off = pos & (SEL_BLOCK - 1)
    feat = jnp.where((col >= F_POS) & (col < F_POS + 3), blk.astype(F32),
                     jnp.where((col >= F_POS + 3) & (col < F_POS + 6), off.astype(F32), 0.0))
    if with_block_mask:
        feat = jnp.where((col >= F_SEL) & (col - F_SEL == blk) & (col < F_POS), NEG_BLOCK, feat)
    return feat


def _in_proj_kernel(x_ref, g1_ref, wq_ref, qg_ref, wv_ref, vb_ref, wg_ref, wk_ref, kg_ref, grp_ref,
                    place_ref, wc_ref, wl_ref, cw_ref, cb_ref, wa_ref, ba_ref, wi_ref, bi_ref, lam_ref,
                    h_ref, qt_ref, vt_ref, gt_ref, ks_ref, kw_ref, cs_ref, lru_ref, tail_ref, state_ref):
    x = x_ref[0]
    tm = x.shape[0]
    ms = jnp.mean(x * x, axis=-1, keepdims=True)
    h = (x * lax.rsqrt(ms + EPS) * g1_ref[...]).astype(MXU_DTYPE)
    h_ref[0] = h

    _lru_tile(h, wl_ref, cw_ref, cb_ref, wa_ref, ba_ref, wi_ref, bi_ref, lam_ref, lru_ref, tail_ref, state_ref)

    r3 = _nt_dot(wq_ref[...], h).reshape(N_HEADS, HEAD_DIM, tm)
    qn = r3 * lax.rsqrt(jnp.mean(r3 * r3, axis=1, keepdims=True) + EPS)
    qt_ref[0] = (qn.reshape(N_HEADS * HEAD_DIM, tm) * qg_ref[...]).astype(qt_ref.dtype)

    vt_ref[0] = (_nt_dot(wv_ref[...], h) + vb_ref[...]).astype(vt_ref.dtype)
    gt_ref[0] = _sigmoid(_nt_dot(wg_ref[...], h))

    k = jnp.dot(h, wk_ref[...], preferred_element_type=F32)
    kk = k * k
    kk_hi = kk.astype(MXU_DTYPE)
    kk_lo = (kk - kk_hi.astype(F32)).astype(MXU_DTYPE)
    pos = pl.program_id(1) * tm + lax.broadcasted_iota(jnp.int32, (tm, FEAT), 0)
    col = lax.broadcasted_iota(jnp.int32, (tm, FEAT), 1)
    kv_dim = N_KV * HEAD_DIM
    for branch, (o_ref, with_block_mask) in enumerate(((ks_ref, True), (kw_ref, False))):
        sl = slice(branch * kv_dim, (branch + 1) * kv_dim)
        msq = (jnp.dot(kk_hi[:, sl], grp_ref[...], preferred_element_type=F32)
               + jnp.dot(kk_lo[:, sl], grp_ref[...], preferred_element_type=F32))
        kn = (k[:, sl] * lax.rsqrt(msq + EPS) * kg_ref[branch:branch + 1, :]).astype(MXU_DTYPE)
        placed = jnp.dot(kn, place_ref[...], preferred_element_type=F32)
        feat = _token_features(pos, col, with_block_mask)
        for g in range(N_KV):
            o_ref[0, g] = (placed[:, g * FEAT:(g + 1) * FEAT] + feat).astype(o_ref.dtype)

    cs_ref[0] = jnp.dot(h, wc_ref[...], preferred_element_type=F32)


def _in_proj(x, attn_weights, lru_weights, tm=256):
    b, t, d = x.shape
    kv_dim = N_KV * HEAD_DIM
    wq_t, w_v_t, w_g_t = attn_weights[1], attn_weights[3], attn_weights[5]
    lru_w = lru_weights[-1].shape[1]
    full = lambda a: pl.BlockSpec(a.shape, lambda bi, i: (0,) * a.ndim)
    rowblk = lambda n: pl.BlockSpec((1, tm, n), lambda bi, i: (bi, i, 0))
    colblk = lambda n: pl.BlockSpec((1, n, tm), lambda bi, i: (bi, 0, i))
    kblk = pl.BlockSpec((1, N_KV, tm, FEAT), lambda bi, i: (bi, 0, i, 0))
    weights = tuple(attn_weights) + tuple(lru_weights)
    return pl.pallas_call(
        _in_proj_kernel,
        grid=(b, t // tm),
        in_specs=[rowblk(d)] + [full(a) for a in weights],
        out_specs=[rowblk(d), colblk(wq_t.shape[0]), colblk(w_v_t.shape[0]), colblk(w_g_t.shape[0]),
                   kblk, kblk, rowblk(2 * kv_dim), rowblk(lru_w)],
        out_shape=[jax.ShapeDtypeStruct((b, t, d), MXU_DTYPE),
                   jax.ShapeDtypeStruct((b, wq_t.shape[0], t), MXU_DTYPE),
                   jax.ShapeDtypeStruct((b, w_v_t.shape[0], t), MXU_DTYPE),
                   jax.ShapeDtypeStruct((b, w_g_t.shape[0], t), F32),
                   jax.ShapeDtypeStruct((b, N_KV, t, FEAT), MXU_DTYPE),
                   jax.ShapeDtypeStruct((b, N_KV, t, FEAT), MXU_DTYPE),
                   jax.ShapeDtypeStruct((b, t, 2 * kv_dim), F32),
                   jax.ShapeDtypeStruct((b, t, lru_w), MXU_DTYPE)],
        scratch_shapes=[pltpu.VMEM((V7X_SUBLANES, lru_w), F32),
                        pltpu.VMEM((V7X_SUBLANES, lru_w), F32)],
        compiler_params=_params("parallel", "arbitrary"),
        name="in_proj_lru",
    )(x, *weights)


CMP_GROUPS = V7X_LANES // HEAD_DIM


def _cmp_hidden(src_ref, pos_ref, w1_ref, b1_ref):
    ncp = src_ref.shape[1] // CMP_STRIDE
    first = jnp.zeros((ncp, CMP_GROUPS * CMP_HIDDEN), F32)
    second = jnp.zeros((ncp, CMP_GROUPS * CMP_HIDDEN), F32)
    for l in range(CMP_STRIDE):
        x = src_ref[0, pl.ds(l, ncp, stride=CMP_STRIDE), :]
        lo = (x + pos_ref[l:l + 1, :]).astype(MXU_DTYPE)
        hi = (x + pos_ref[CMP_STRIDE + l:CMP_STRIDE + l + 1, :]).astype(MXU_DTYPE)
        first = first + jnp.dot(lo, w1_ref[l], preferred_element_type=F32)
        second = second + jnp.dot(hi, w1_ref[CMP_STRIDE + l], preferred_element_type=F32)
    hid = first + pltpu.roll(second, ncp - 1, 0) + b1_ref[...]
    return _gelu_tanh(hid).astype(MXU_DTYPE)


def _cmp_k_kernel(src_ref, pos_ref, w1_ref, b1_ref, w2_ref, b2_ref, g_ref, o_ref):
    hid = _cmp_hidden(src_ref, pos_ref, w1_ref, b1_ref)
    ncp = hid.shape[0]
    idx = lax.broadcasted_iota(jnp.int32, (ncp, FEAT), 0)
    col = lax.broadcasted_iota(jnp.int32, (ncp, FEAT), 1)
    feat = jnp.where((col >= F_CMP) & (col < F_CMP + 3), (idx >> 6).astype(F32),
                     jnp.where((col >= F_CMP + 3) & (col < F_CMP + 6), (idx & 63).astype(F32), 0.0))
    for gl in range(CMP_GROUPS):
        r = jnp.dot(hid[:, gl * CMP_HIDDEN:(gl + 1) * CMP_HIDDEN], w2_ref[...],
                    preferred_element_type=F32) + b2_ref[...]
        ms = jnp.sum(r * r, axis=-1, keepdims=True) * (1.0 / HEAD_DIM)
        o_ref[0, gl] = (r * lax.rsqrt(ms + EPS) * g_ref[...] + feat).astype(o_ref.dtype)


def _cmp_v_kernel(src_ref, pos_ref, w1_ref, b1_ref, w2t_ref, b2_ref, o_ref):
    hid = _cmp_hidden(src_ref, pos_ref, w1_ref, b1_ref)
    for gl in range(CMP_GROUPS):
        r = _nt_dot(w2t_ref[...], hid[:, gl * CMP_HIDDEN:(gl + 1) * CMP_HIDDEN]) + b2_ref[...]
        o_ref[0, gl] = r.astype(o_ref.dtype)


def _compress(cmp_src, lane_block0, pos, w1, b1, w2, b2, gain_pad=None):
    b, t, _ = cmp_src.shape
    ncp = t // CMP_STRIDE
    hid_w = CMP_GROUPS * CMP_HIDDEN
    eye = jnp.eye(CMP_GROUPS, dtype=w1.dtype)
    w1_bd = jnp.einsum('ldf,gh->lgdhf', w1, eye).reshape(CMP_BLOCK, V7X_LANES, hid_w).astype(MXU_DTYPE)
    pos_t = jnp.tile(pos, (1, CMP_GROUPS))
    b1_t = jnp.tile(b1.reshape(1, CMP_HIDDEN), (1, CMP_GROUPS))
    full = lambda a: pl.BlockSpec(a.shape, lambda bi, p: (0,) * a.ndim)
    src_spec = pl.BlockSpec((1, t, V7X_LANES), lambda bi, p: (bi, 0, lane_block0 + p))
    grid = (b, N_KV // CMP_GROUPS)
    if gain_pad is not None:
        args = (pos_t, w1_bd, b1_t, w2, b2, gain_pad)
        return pl.pallas_call(
            _cmp_k_kernel,
            grid=grid,
            in_specs=[src_spec] + [full(a) for a in args],
            out_specs=pl.BlockSpec((1, CMP_GROUPS, ncp, FEAT), lambda bi, p: (bi, p, 0, 0)),
            out_shape=jax.ShapeDtypeStruct((b, N_KV, ncp, FEAT), MXU_DTYPE),
            compiler_params=_params("parallel", "parallel"),
            name="compress_k",
        )(cmp_src, *args)
    args = (pos_t, w1_bd, b1_t, w2, b2)
    return pl.pallas_call(
        _cmp_v_kernel,
        grid=grid,
        in_specs=[src_spec] + [full(a) for a in args],
        out_specs=pl.BlockSpec((1, CMP_GROUPS, V_ROWS, ncp), lambda bi, p: (bi, p, 0, 0)),
        out_shape=jax.ShapeDtypeStruct((b, N_KV, V_ROWS, ncp), MXU_DTYPE),
        compiler_params=_params("parallel", "parallel"),
        name="compress_v",
    )(cmp_src, *args)


def _split3(v):
    parts = []
    rest = np.asarray(v, np.float64)
    for _ in range(3):
        p = rest.astype(np.float32).astype(jnp.bfloat16).astype(np.float64)
        parts.append(p)
        rest = rest - p
    return parts


def _alibi_query_features():
    tab = np.zeros((N_KV, FEAT - F_POS, HPG * TQ), np.float64)
    for g in range(N_KV):
        for h in range(HPG):
            slope = 2.0 ** (-8.0 * (g * HPG + h + 1) / N_HEADS)
            parts = _split3(slope * LOG2E)
            lanes = slice(h * TQ, (h + 1) * TQ)
            for i, p in enumerate(parts):
                tab[g, i, lanes] = SEL_BLOCK * p
                tab[g, 3 + i, lanes] = p
                tab[g, 6 + i, lanes] = CMP_STRIDE * 64 * p
                tab[g, 9 + i, lanes] = CMP_STRIDE * p
    return jnp.asarray(tab, F32).astype(MXU_DTYPE)


def _block_map_t(n_cmp_pad, n_blk):
    cs = np.arange(n_cmp_pad) * CMP_STRIDE
    ce = cs + CMP_BLOCK - 1
    bs = np.arange(n_blk) * SEL_BLOCK
    be = bs + SEL_BLOCK - 1
    return jnp.asarray(((cs[None, :] <= be[:, None]) & (ce[None, :] >= bs[:, None])).astype(np.float32))


def _attn_kernel(qt_ref, qn_ref, gate_ref, kc_ref, vc_ref, ks_ref, vs_ref, kw_ref, vw_ref, alibi_ref, map_ref,
                 o_ref, qb_ref, qn_scr, qs_ref, imp_ref, s0_ref, s1_ref, p_ref, m_ref, acc_ref, nsel_ref,
                 ocmp_ref, *, n_blk, n_sel, n_tiles):
    lanes = HPG * TQ
    qi = pl.program_id(2)
    q0 = pl.multiple_of(qi * TQ, TQ)
    ncp = kc_ref.shape[2]
    lane_tok = lax.broadcasted_iota(jnp.int32, (1, lanes), 1) & (TQ - 1)

    def build_query(dst_ref, src_ref):
        for h in range(HPG):
            dst_ref[0:HEAD_DIM, h * TQ:(h + 1) * TQ] = src_ref[0, h * HEAD_DIM:(h + 1) * HEAD_DIM, :]
        dst_ref[F_SEL:F_POS, :] = jnp.zeros((F_POS - F_SEL, lanes), dst_ref.dtype)
        dst_ref[F_POS:FEAT, :] = alibi_ref[0]
        return dst_ref[...]

    def compressed_scores(qv):
        return jnp.dot(kc_ref[0, 0], qv, preferred_element_type=F32)

    def compressed_branch(sc, tile0):
        last_cmp = (tile0 + lane_tok - (CMP_BLOCK - 1)) >> 4
        sc = jnp.where(lax.broadcasted_iota(jnp.int32, (ncp, lanes), 0) <= last_cmp, sc, NEG_MASK)
        ec = jnp.exp2(sc - jnp.max(sc, axis=0, keepdims=True))
        acc_c = jnp.dot(vc_ref[0, 0], ec.astype(MXU_DTYPE), preferred_element_type=F32)
        inv_c = jnp.where(last_cmp >= 0, 1.0 / jnp.maximum(acc_c[HEAD_DIM:HEAD_DIM + 1], 1e-30), 0.0)
        psum = ec[:, 0:TQ] * inv_c[:, 0:TQ]
        for h in range(1, HPG):
            psum = psum + ec[:, h * TQ:(h + 1) * TQ] * inv_c[:, h * TQ:(h + 1) * TQ]
        return acc_c[0:HEAD_DIM] * inv_c, psum

    def select_blocks(psum, tile):
        imp = jnp.dot(map_ref[...], psum, preferred_element_type=F32)
        t_q = tile * TQ + lax.broadcasted_iota(jnp.int32, (n_blk, TQ), 1)
        blk = lax.broadcasted_iota(jnp.int32, (n_blk, TQ), 0)
        cur = t_q >> 6
        forced = (blk == 0) | (blk == cur) | (blk == cur - 1)
        imp = jnp.where(blk <= cur, imp + jnp.where(forced, FORCE_BONUS, 0.0), NEG_MASK)
        imp_ref[...] = imp
        n_chunks = n_blk // V7X_SUBLANES
        chunks = [imp[c * V7X_SUBLANES:(c + 1) * V7X_SUBLANES] for c in range(n_chunks)]
        ranks = [jnp.zeros((V7X_SUBLANES, TQ), jnp.int32) for _ in range(n_chunks)]
        sub = lax.broadcasted_iota(jnp.int32, (V7X_SUBLANES, TQ), 0)
        for k in range(n_blk):
            row = imp_ref[k:k + 1, :]
            for c in range(n_chunks):
                lo = c * V7X_SUBLANES
                if lo > k:
                    one = jnp.where(row >= chunks[c], 1, 0)
                elif lo + V7X_SUBLANES - 1 <= k:
                    one = jnp.where(row > chunks[c], 1, 0)
                else:
                    one = jnp.where(sub + lo > k, jnp.where(row >= chunks[c], 1, 0),
                                    jnp.where(row > chunks[c], 1, 0))
                ranks[c] = ranks[c] + one
        rank = jnp.concatenate(ranks, axis=0)
        first_own_blk = tile * (TQ // SEL_BLOCK)
        return jnp.where((rank < n_sel) & (blk < first_own_blk), 0.0, 1.0).astype(nsel_ref.dtype)

    qb = build_query(qb_ref, qt_ref)
    t_lane = q0 + lane_tok

    @pl.when(qi == 0)
    def _():
        ocmp_ref[...] = compressed_branch(compressed_scores(qb), 0)[0]
        nsel_ref[...] = jnp.ones(nsel_ref.shape, nsel_ref.dtype)

    o_cmp = ocmp_ref[...]
    not_sel = nsel_ref[...]
    qs_ref[...] = qb
    for h in range(HPG):
        qs_ref[F_SEL:F_SEL + n_blk, h * TQ:(h + 1) * TQ] = not_sel

    def qk_chunk(c):
        k0 = pl.multiple_of(c * TK_SEL, TK_SEL)
        return jnp.dot(ks_ref[0, 0, pl.ds(k0, TK_SEL), :], qs_ref[...], preferred_element_type=F32)

    def pv_chunk(c):
        k0 = pl.multiple_of(c * TK_SEL, TK_SEL)
        return jnp.dot(vs_ref[0, :, pl.ds(k0, TK_SEL)], p_ref[...], preferred_element_type=F32)

    def stage(k, carry, s_cur_ref, s_next_ref):
        m, acc = carry
        if s_next_ref is not None:
            s_next_ref[...] = qk_chunk(k + 1)
        pv_prev = pv_chunk(jnp.maximum(k - 1, 0))
        s = s_cur_ref[...]
        m_new = jnp.maximum(m, jnp.max(s, axis=0, keepdims=True))
        p_ref[...] = jnp.exp2(s - m_new).astype(p_ref.dtype)
        return m_new, jnp.exp2(m - m_new) * (acc + pv_prev)

    n_chunks_sel = jnp.maximum((qi * TQ + TK_SEL - 1) // TK_SEL, 1)

    nxt = jnp.minimum(qi + 1, n_tiles - 1)
    w0 = pl.multiple_of(jnp.maximum(q0 - WINDOW, 0), TQ)
    sc_next = compressed_scores(build_query(qn_scr, qn_ref))
    s0_ref[...] = qk_chunk(0)
    p_ref[...] = jnp.zeros(p_ref.shape, p_ref.dtype)
    sw = jnp.dot(kw_ref[0, 0, pl.ds(w0, WIN_KEYS), :], qb, preferred_element_type=F32)
    sd = jnp.dot(ks_ref[0, 0, pl.ds(q0, TQ), :], qb, preferred_element_type=F32)

    o_cmp_next, psum_next = compressed_branch(sc_next, nxt * TQ)
    ocmp_ref[...] = o_cmp_next
    nsel_ref[...] = select_blocks(psum_next, nxt)

    dw = (t_lane - w0) - lax.broadcasted_iota(jnp.int32, (WIN_KEYS, lanes), 0)
    sw = jnp.where(lax.bitcast_convert_type(dw, jnp.uint32) < WINDOW, sw, NEG_MASK)
    pw = jnp.exp2(sw - jnp.max(sw, axis=0, keepdims=True))
    acc_w = jnp.dot(vw_ref[0, :, pl.ds(w0, WIN_KEYS)], pw.astype(MXU_DTYPE), preferred_element_type=F32)
    o_win = acc_w[0:HEAD_DIM] * (1.0 / acc_w[HEAD_DIM:HEAD_DIM + 1])

    kd = q0 + lax.broadcasted_iota(jnp.int32, (TQ, lanes), 0)
    sd = jnp.where(kd <= t_lane, sd, NEG_MASK)
    m_d = jnp.max(sd, axis=0, keepdims=True)
    pd = jnp.exp2(sd - m_d).astype(MXU_DTYPE)
    acc_d = jnp.dot(vs_ref[0, :, pl.ds(q0, TQ)], pd, preferred_element_type=F32)

    def pair_body(j, carry):
        carry = stage(2 * j, carry, s0_ref, s1_ref)
        return stage(2 * j + 1, carry, s1_ref, s0_ref)

    carry = (jnp.full((1, lanes), NEG_MASK, F32), jnp.zeros((V_ROWS, lanes), F32))
    carry = lax.fori_loop(0, (n_chunks_sel - 1) // 2, pair_body, carry)
    last = n_chunks_sel - 1

    def finish(carry):
        m_f, acc_f = carry
        m_ref[...] = m_f
        acc_ref[...] = acc_f + pv_chunk(last)

    @pl.when((n_chunks_sel & 1) == 0)
    def _():
        finish(stage(last, stage(last - 1, carry, s0_ref, s1_ref), s1_ref, None))

    @pl.when((n_chunks_sel & 1) == 1)
    def _():
        finish(stage(last, carry, s0_ref, None))

    m_s = m_ref[...]
    m_all = jnp.maximum(m_s, m_d)
    acc_s = jnp.exp2(m_s - m_all) * acc_ref[...] + jnp.exp2(m_d - m_all) * acc_d
    o_sel = acc_s[0:HEAD_DIM] * (1.0 / acc_s[HEAD_DIM:HEAD_DIM + 1])

    gates = gate_ref[0]
    def gate_row(j):
        return jnp.concatenate([gates[j * HPG + h:j * HPG + h + 1, :] for h in range(HPG)], axis=1)
    o_t = gate_row(0) * o_cmp + gate_row(1) * o_sel + gate_row(2) * o_win
    for hp in range(HPG // 2):
        pair = jnp.concatenate([o_t[:, (2 * hp) * TQ:(2 * hp + 1) * TQ],
                                o_t[:, (2 * hp + 1) * TQ:(2 * hp + 2) * TQ]], axis=0)
        o_ref[0, :, hp * 2 * HEAD_DIM:(hp + 1) * 2 * HEAD_DIM] = pair.T.astype(o_ref.dtype)


def _attention(q_t, gates_t, k_cmp, v_cmp_t, k_sel, k_win, v_t):
    b, _, t = q_t.shape
    ncp = k_cmp.shape[2]
    n_blk = t // SEL_BLOCK
    assert t % TK_SEL == 0 and t >= WIN_KEYS and F_SEL + n_blk <= F_POS, "unsupported sequence length"
    n_sel = min(N_SEL, n_blk)
    lanes = HPG * TQ
    rows = HPG * HEAD_DIM
    alibi = _alibi_query_features()
    blk_map_t = _block_map_t(ncp, n_blk)
    n_tiles = t // TQ
    kernel = functools.partial(_attn_kernel, n_blk=n_blk, n_sel=n_sel, n_tiles=n_tiles)
    return pl.pallas_call(
        kernel,
        grid=(b, N_KV, n_tiles),
        in_specs=[
            pl.BlockSpec((1, rows, TQ), lambda bi, g, i: (bi, g, i)),
            pl.BlockSpec((1, rows, TQ), lambda bi, g, i: (bi, g, jnp.minimum(i + 1, n_tiles - 1))),
            pl.BlockSpec((1, 16, TQ), lambda bi, g, i: (bi, g, i)),
            pl.BlockSpec((1, 1, ncp, FEAT), lambda bi, g, i: (bi, g, 0, 0)),
            pl.BlockSpec((1, 1, V_ROWS, ncp), lambda bi, g, i: (bi, g, 0, 0)),
            pl.BlockSpec((1, 1, t, FEAT), lambda bi, g, i: (bi, g, 0, 0)),
            pl.BlockSpec((1, V_ROWS, t), lambda bi, g, i: (bi, g, 0)),
            pl.BlockSpec((1, 1, t, FEAT), lambda bi, g, i: (bi, g, 0, 0)),
            pl.BlockSpec((1, V_ROWS, t), lambda bi, g, i: (bi, N_KV + g, 0)),
            pl.BlockSpec((1, FEAT - F_POS, lanes), lambda bi, g, i: (g, 0, 0)),
            pl.BlockSpec((n_blk, ncp), lambda bi, g, i: (0, 0)),
        ],
        out_specs=pl.BlockSpec((1, TQ, rows), lambda bi, g, i: (bi, i, g)),
        out_shape=jax.ShapeDtypeStruct((b, t, N_HEADS * HEAD_DIM), MXU_DTYPE),
        scratch_shapes=[pltpu.VMEM((FEAT, lanes), MXU_DTYPE),
                        pltpu.VMEM((FEAT, lanes), MXU_DTYPE),
                        pltpu.VMEM((FEAT, lanes), MXU_DTYPE),
                        pltpu.VMEM((n_blk, TQ), F32),
                        pltpu.VMEM((TK_SEL, lanes), F32),
                        pltpu.VMEM((TK_SEL, lanes), F32),
                        pltpu.VMEM((TK_SEL, lanes), MXU_DTYPE),
                        pltpu.VMEM((1, lanes), F32),
                        pltpu.VMEM((V_ROWS, lanes), F32),
                        pltpu.VMEM((n_blk, TQ), MXU_DTYPE),
                        pltpu.VMEM((HEAD_DIM, lanes), F32)],
        compiler_params=_params("parallel", "parallel", "arbitrary"),
        name="nsa_attention",
    )(q_t, q_t, gates_t, k_cmp, v_cmp_t, k_sel, v_t, k_win, v_t, alibi, blk_map_t)


def _lru_tile(hin, wl_ref, cw_ref, cb_ref, wa_ref, ba_ref, wi_ref, bi_ref, lam_ref, o_ref, tail_ref, h_ref):
    tt = hin.shape[0]
    w = lam_ref.shape[1]

    @pl.when(pl.program_id(1) == 0)
    def _():
        tail_ref[...] = jnp.zeros_like(tail_ref)
        h_ref[...] = jnp.zeros_like(h_ref)

    def gate_matmul(xb, w_ref):
        n = w_ref.shape[1]
        return jnp.concatenate([jnp.dot(xb[:, j * n:(j + 1) * n], w_ref[j], preferred_element_type=F32)
                                for j in range(w_ref.shape[0])], axis=1)

    x = jnp.dot(hin, wl_ref[:, 0:w], preferred_element_type=F32)
    gate = jnp.dot(hin, wl_ref[:, w:2 * w], preferred_element_type=F32)
    ng = tt // V7X_SUBLANES
    sub = lax.broadcasted_iota(jnp.int32, (ng, V7X_SUBLANES, w), 1)
    x3 = x.reshape(ng, V7X_SUBLANES, w)
    xprev3 = jnp.concatenate([tail_ref[...][None], x3], axis=0)
    tail_ref[...] = x[tt - V7X_SUBLANES:tt]
    xc = x * cw_ref[CONV_W - 1:CONV_W, :] + cb_ref[...]
    for s in range(1, CONV_W):
        rot = pltpu.roll(xprev3, s, 1)
        xs = jnp.where(sub >= s, rot[1:], rot[:-1])
        xc = xc + xs.reshape(tt, w) * cw_ref[CONV_W - 1 - s:CONV_W - s, :]

    xb = xc.astype(MXU_DTYPE)
    r = _sigmoid(gate_matmul(xb, wa_ref) + ba_ref[...])
    i = _sigmoid(gate_matmul(xb, wi_ref) + bi_ref[...])
    z = -lam_ref[...]
    softplus = jnp.maximum(z, 0.0) + jnp.log1p(jnp.exp(-jnp.abs(z)))
    log_a = -LRU_C * r * softplus
    a = jnp.exp(log_a)
    bb = jnp.sqrt(_one_minus_sq(a, log_a)) * (i * xc)

    a3 = a.reshape(ng, V7X_SUBLANES, w)
    b3 = bb.reshape(ng, V7X_SUBLANES, w)
    for d in (1, 2, 4):
        ok = sub >= d
        a_sh = pltpu.roll(a3, d, 1)
        b_sh = pltpu.roll(b3, d, 1)
        b3 = jnp.where(ok, a3 * b_sh + b3, b3)
        a3 = jnp.where(ok, a3 * a_sh, a3)
    carry = h_ref[0:1, :]
    groups = []
    for g in range(ng):
        hg = b3[g] + a3[g] * carry
        groups.append(hg)
        carry = hg[V7X_SUBLANES - 1:V7X_SUBLANES, :]
    hcur = jnp.concatenate(groups, axis=0)
    h_ref[...] = jnp.broadcast_to(carry, h_ref.shape)
    o_ref[0] = (hcur * _gelu_tanh(gate)).astype(o_ref.dtype)


def _merge_kernel(attn_ref, lru_ref, h_ref, x_ref, wm_ref, wa_ref, wl_ref, wo_ref, g2_ref,
                  x1_ref, h2_ref):
    d = x_ref.shape[1]
    h = h_ref[...]
    mg0 = _sigmoid(jnp.dot(h, wm_ref[:, 0:d], preferred_element_type=F32))
    mg1 = _sigmoid(jnp.dot(h, wm_ref[:, d:2 * d], preferred_element_type=F32))
    ya = jnp.dot(attn_ref[...], wa_ref[...], preferred_element_type=F32)
    yl = jnp.dot(lru_ref[...], wl_ref[...], preferred_element_type=F32)
    merged = mg0 * ya + mg1 * yl
    x1 = x_ref[...] + jnp.dot(merged.astype(MXU_DTYPE), wo_ref[...], preferred_element_type=F32)
    x1_ref[...] = x1
    ms = jnp.mean(x1 * x1, axis=-1, keepdims=True)
    h2_ref[...] = (x1 * lax.rsqrt(ms + EPS) * g2_ref[...]).astype(h2_ref.dtype)


def _merge(attn, lru, h2d, x2d, w_mg, wa, wl, wo, g2, tm=256):
    m, d = x2d.shape
    row = pl.BlockSpec((tm, d), lambda i: (i, 0))
    full = lambda a: pl.BlockSpec(a.shape, lambda i: (0, 0))
    g2 = g2.reshape(1, d)
    return pl.pallas_call(
        _merge_kernel,
        grid=(m // tm,),
        in_specs=[row, row, row, row, full(w_mg), full(wa), full(wl), full(wo), full(g2)],
        out_specs=[row, row],
        out_shape=[jax.ShapeDtypeStruct((m, d), F32), jax.ShapeDtypeStruct((m, d), MXU_DTYPE)],
        compiler_params=_params("parallel"),
        name="merge_out",
    )(attn, lru, h2d, x2d, w_mg, wa, wl, wo, g2)


def _ffn_kernel(h_ref, x1_ref, wg_ref, wu_ref, wd_ref, o_ref):
    @pl.when(pl.program_id(1) == 0)
    def _():
        o_ref[...] = x1_ref[...]

    h = h_ref[...]
    g = jnp.dot(h, wg_ref[...], preferred_element_type=F32)
    u = jnp.dot(h, wu_ref[...], preferred_element_type=F32)
    act = (g * _sigmoid(g) * u).astype(MXU_DTYPE)
    o_ref[...] += jnp.dot(act, wd_ref[...], preferred_element_type=F32)


def _ffn(h2, x1, wg, wu, wd, tm=1024, tf=256):
    m, d = x1.shape
    f = wg.shape[1]
    return pl.pallas_call(
        _ffn_kernel,
        grid=(m // tm, f // tf),
        in_specs=[pl.BlockSpec((tm, d), lambda i, j: (i, 0)),
                  pl.BlockSpec((tm, d), lambda i, j: (i, 0)),
                  pl.BlockSpec((d, tf), lambda i, j: (0, j)),
                  pl.BlockSpec((d, tf), lambda i, j: (0, j)),
                  pl.BlockSpec((tf, d), lambda i, j: (j, 0))],
        out_specs=pl.BlockSpec((tm, d), lambda i, j: (i, 0)),
        out_shape=jax.ShapeDtypeStruct((m, d), F32),
        compiler_params=_params("parallel", "arbitrary"),
        name="swiglu_ffn",
    )(h2, x1, wg, wu, wd)


def _pad_last(a, n):
    return jnp.pad(a, [(0, 0)] * (a.ndim - 1) + [(0, n - a.shape[-1])])


def _layer(x, norm1_g, w_in, q_norm_g, k_norm_g, cmp_pos_k, cmp_w1_k, cmp_b1_k, cmp_w2_k, cmp_b2_k,
           cmp_pos_v, cmp_w1_v, cmp_b1_v, cmp_w2_v, cmp_b2_v, conv_w, conv_b, lru_wa, lru_ba,
           lru_wi, lru_bi, lru_lambda, w_o_attn, w_o_lru, w_out, norm2_g, w_gate, w_up, w_down):
    b, t, d = x.shape
    m = b * t
    attn_dim = N_HEADS * HEAD_DIM
    kv_dim = N_KV * HEAD_DIM
    lru_w = lru_lambda.shape[0]
    assert t % 512 == 0
    o1 = attn_dim
    o2 = o1 + 6 * kv_dim
    o3 = o2 + 3 * N_HEADS
    o4 = o3 + lru_w
    o5 = o4 + lru_w
    cast = lambda a: a.astype(MXU_DTYPE)

    wq_t = cast(w_in[:, :o1].T)
    w_kv = w_in[:, o1:o2].reshape(d, 6, N_KV, HEAD_DIM)
    w_cmp_src = cast(w_kv[:, 0:2].reshape(d, 2 * kv_dim))
    w_k2 = cast(jnp.stack([w_kv[:, 2], w_kv[:, 4]], axis=1).reshape(d, 2 * kv_dim))
    k_gain2 = jnp.stack([jnp.tile(k_norm_g[1], N_KV), jnp.tile(k_norm_g[2], N_KV)])
    lane_grp = np.arange(kv_dim) // HEAD_DIM
    grp_avg = cast(jnp.asarray((lane_grp[:, None] == lane_grp[None, :]) / HEAD_DIM, F32))
    place_np = np.zeros((kv_dim, N_KV * FEAT), np.float32)
    place_np[np.arange(kv_dim), lane_grp * FEAT + np.arange(kv_dim) % HEAD_DIM] = 1.0
    place = cast(jnp.asarray(place_np))
    w_v = jnp.stack([w_kv[:, 3], w_kv[:, 5]], axis=1)
    w_v_t = cast(_pad_last(w_v, V_ROWS).reshape(d, 2 * N_KV * V_ROWS).T)
    v_ones_col = jnp.tile(jnp.arange(V_ROWS) == HEAD_DIM, 2 * N_KV).astype(F32).reshape(-1, 1)
    w_g = w_in[:, o2:o3].reshape(d, N_KV, HPG, 3).transpose(0, 1, 3, 2).reshape(d, N_KV, 3 * HPG)
    w_g_t = cast(_pad_last(w_g, 16).reshape(d, N_KV * 16).T)
    w_lru = cast(w_in[:, o3:o5])
    w_mg = cast(w_in[:, o5:])
    q_gain_col = jnp.tile(q_norm_g * (HEAD_DIM ** -0.5 * LOG2E), N_HEADS).reshape(attn_dim, 1)
    gain_pad = lambda g: _pad_last(g.reshape(1, HEAD_DIM), FEAT)

    per_slab = V7X_MXU_DIM // (lru_w // LRU_BLOCKS)
    eye = jnp.eye(per_slab, dtype=F32)
    slabs = lambda wgt: cast(jnp.einsum('snkj,nm->snkmj', wgt.reshape(LRU_BLOCKS // per_slab, per_slab,
                                                                       *wgt.shape[1:]), eye)
                             .reshape(LRU_BLOCKS // per_slab, V7X_MXU_DIM, V7X_MXU_DIM))
    vec = lambda v: v.reshape(1, lru_w)

    h3d, q_t, v_t, gates_t, k_sel, k_win, cmp_src, lru = _in_proj(
        x,
        (norm1_g.reshape(1, d), wq_t, q_gain_col, w_v_t, v_ones_col, w_g_t, w_k2, k_gain2, grp_avg, place,
         w_cmp_src),
        (w_lru, conv_w.reshape(CONV_W, lru_w), vec(conv_b), slabs(lru_wa), vec(lru_ba), slabs(lru_wi),
         vec(lru_bi), vec(lru_lambda)))
    h2d = h3d.reshape(m, d)

    k_cmp = _compress(cmp_src, 0, cmp_pos_k, cmp_w1_k, cmp_b1_k, cast(_pad_last(cmp_w2_k, FEAT)),
                      _pad_last(cmp_b2_k.reshape(1, HEAD_DIM), FEAT), gain_pad(k_norm_g[0]))
    v_cmp_t = _compress(cmp_src, N_KV // CMP_GROUPS, cmp_pos_v, cmp_w1_v, cmp_b1_v,
                        cast(_pad_last(cmp_w2_v, V_ROWS).T),
                        jnp.concatenate([cmp_b2_v, v_ones_col[HEAD_DIM:V_ROWS, 0]]).reshape(V_ROWS, 1))

    attn = _attention(q_t, gates_t, k_cmp, v_cmp_t, k_sel, k_win, v_t)

    x1, h2 = _merge(attn.reshape(m, attn_dim), lru.reshape(m, lru_w), h2d, x.reshape(m, d), w_mg,
                    cast(w_o_attn), cast(w_o_lru), cast(w_out), norm2_g)
    out = _ffn(h2, x1, cast(w_gate), cast(w_up), cast(w_down))
    return out.reshape(b, t, d)


def kernel(x, norm1_g, w_in, q_norm_g, k_norm_g, cmp_pos_k, cmp_w1_k, cmp_b1_k, cmp_w2_k, cmp_b2_k,
           cmp_pos_v, cmp_w1_v, cmp_b1_v, cmp_w2_v, cmp_b2_v, conv_w, conv_b, lru_wa, lru_ba,
           lru_wi, lru_bi, lru_lambda, w_o_attn, w_o_lru, w_out, norm2_g, w_gate, w_up, w_down):
    for l in range(norm1_g.shape[0]):
        x = _layer(x, norm1_g[l], w_in[l], q_norm_g[l], k_norm_g[l], cmp_pos_k[l], cmp_w1_k[l],
                   cmp_b1_k[l], cmp_w2_k[l], cmp_b2_k[l], cmp_pos_v[l], cmp_w1_v[l], cmp_b1_v[l],
                   cmp_w2_v[l], cmp_b2_v[l], conv_w[l], conv_b[l], lru_wa[l], lru_ba[l], lru_wi[l],
                   lru_bi[l], lru_lambda[l], w_o_attn[l], w_o_lru[l], w_out[l], norm2_g[l],
                   w_gate[l], w_up[l], w_down[l])
    return x
```

```python
import functools

import numpy as np
import jax
import jax.numpy as jnp
from jax import lax
from jax.experimental import pallas as pl
from jax.experimental.pallas import tpu as pltpu

N_HEADS = 16
HEAD_DIM = 64
N_KV = 4
HPG = N_HEADS // N_KV
CMP_BLOCK = 32
CMP_STRIDE = 16
CMP_HIDDEN = 256
SEL_BLOCK = 64
N_SEL = 16
WINDOW = 512
FORCE_BONUS = 1e4
LRU_BLOCKS = 16
CONV_W = 4
LRU_C = 8.0
EPS = 1e-6

MXU_DTYPE = jnp.bfloat16
F32 = jnp.float32

V7X_LANES = 128
V7X_SUBLANES = 8
V7X_MXU_DIM = 256
V7X_VMEM_LIMIT_BYTES = 48 * 1024 * 1024

TQ = 256
TK_SEL = 512
WIN_KEYS = WINDOW + TQ
V_ROWS = 80
LOG2E = 1.4426950408889634
FEAT = V7X_MXU_DIM
F_SEL = HEAD_DIM
F_POS = 2 * HEAD_DIM
F_CMP = F_POS + 6
NEG_MASK = -1e30
NEG_BLOCK = -(2.0 ** 100)


def _params(*sem):
    return pltpu.CompilerParams(dimension_semantics=sem, vmem_limit_bytes=V7X_VMEM_LIMIT_BYTES)


def _gelu_tanh(x):
    return 0.5 * x * (1.0 + jnp.tanh(0.7978845608028654 * (x + 0.044715 * (x * x * x))))


def _sigmoid(x):
    return 1.0 / (1.0 + jnp.exp(-x))


def _one_minus_sq(a, log_a):
    series = (-2.0 * log_a) * (1.0 + log_a * (1.0 + log_a * (2.0 / 3)))
    return jnp.where(log_a > -1.0 / 128, series, 1.0 - a * a)


def _nt_dot(wt, h):
    return lax.dot_general(wt, h, (((1,), (1,)), ((), ())), preferred_element_type=F32)


def _token_features(pos, col, with_block_mask):
    blk = pos >> 6
    off = pos & (SEL_BLOCK - 1)
    feat = jnp.where((col >= F_POS) & (col < F_POS + 3), blk.astype(F32),
                     jnp.where((col >= F_POS + 3) & (col < F_POS + 6), off.astype(F32), 0.0))
    if with_block_mask:
        feat = jnp.where((col >= F_SEL) & (col - F_SEL == blk) & (col < F_POS), NEG_BLOCK, feat)
    return feat


def _in_proj_kernel(x_ref, g1_ref, wq_ref, qg_ref, wv_ref, vb_ref, wg_ref, wk_ref, kg_ref, grp_ref,
                    place_ref, wc_ref, wl_ref, cw_ref, cb_ref, wa_ref, ba_ref, wi_ref, bi_ref, lam_ref,
                    h_ref, qt_ref, vt_ref, gt_ref, ks_ref, kw_ref, cs_ref, lru_ref, tail_ref, state_ref):
    x = x_ref[0]
    tm = x.shape[0]
    ms = jnp.mean(x * x, axis=-1, keepdims=True)
    inv = pltpu.repeat(jnp.broadcast_to(lax.rsqrt(ms + EPS), (tm, V7X_LANES)), x.shape[1] // V7X_LANES, axis=1)
    h = (x * inv * g1_ref[...]).astype(MXU_DTYPE)
    h_ref[0] = h

    _lru_tile(h, wl_ref, cw_ref, cb_ref, wa_ref, ba_ref, wi_ref, bi_ref, lam_ref, lru_ref, tail_ref, state_ref)

    r3 = _nt_dot(wq_ref[...], h).reshape(N_HEADS, HEAD_DIM, tm)
    qn = r3 * lax.rsqrt(jnp.mean(r3 * r3, axis=1, keepdims=True) + EPS)
    qt_ref[0] = (qn.reshape(N_HEADS * HEAD_DIM, tm) * qg_ref[...]).astype(qt_ref.dtype)

    vt_ref[0] = (_nt_dot(wv_ref[...], h) + vb_ref[...]).astype(vt_ref.dtype)
    gt_ref[0] = _sigmoid(_nt_dot(wg_ref[...], h))

    k = jnp.dot(h, wk_ref[...], preferred_element_type=F32)
    kk = k * k
    kk_hi = kk.astype(MXU_DTYPE)
    kk_lo = (kk - kk_hi.astype(F32)).astype(MXU_DTYPE)
    pos = pl.program_id(1) * tm + lax.broadcasted_iota(jnp.int32, (tm, FEAT), 0)
    col = lax.broadcasted_iota(jnp.int32, (tm, FEAT), 1)
    kv_dim = N_KV * HEAD_DIM
    for branch, (o_ref, with_block_mask) in enumerate(((ks_ref, True), (kw_ref, False))):
        sl = slice(branch * kv_dim, (branch + 1) * kv_dim)
        msq = (jnp.dot(kk_hi[:, sl], grp_ref[...], preferred_element_type=F32)
               + jnp.dot(kk_lo[:, sl], grp_ref[...], preferred_element_type=F32))
        kn = (k[:, sl] * lax.rsqrt(msq + EPS) * kg_ref[branch:branch + 1, :]).astype(MXU_DTYPE)
        placed = jnp.dot(kn, place_ref[...], preferred_element_type=F32)
        feat = _token_features(pos, col, with_block_mask)
        for g in range(N_KV):
            o_ref[0, g] = (placed[:, g * FEAT:(g + 1) * FEAT] + feat).astype(o_ref.dtype)

    cs_ref[0] = jnp.dot(h, wc_ref[...], preferred_element_type=F32)


def _in_proj(x, attn_weights, lru_weights, tm=256):
    b, t, d = x.shape
    kv_dim = N_KV * HEAD_DIM
    wq_t, w_v_t, w_g_t = attn_weights[1], attn_weights[3], attn_weights[5]
    lru_w = lru_weights[-1].shape[1]
    full = lambda a: pl.BlockSpec(a.shape, lambda bi, i: (0,) * a.ndim)
    rowblk = lambda n: pl.BlockSpec((1, tm, n), lambda bi, i: (bi, i, 0))
    colblk = lambda n: pl.BlockSpec((1, n, tm), lambda bi, i: (bi, 0, i))
    kblk = pl.BlockSpec((1, N_KV, tm, FEAT), lambda bi, i: (bi, 0, i, 0))
    weights = tuple(attn_weights) + tuple(lru_weights)
    return pl.pallas_call(
        _in_proj_kernel,
        grid=(b, t // tm),
        in_specs=[rowblk(d)] + [full(a) for a in weights],
        out_specs=[rowblk(d), colblk(wq_t.shape[0]), colblk(w_v_t.shape[0]), colblk(w_g_t.shape[0]),
                   kblk, kblk, rowblk(2 * kv_dim), rowblk(lru_w)],
        out_shape=[jax.ShapeDtypeStruct((b, t, d), MXU_DTYPE),
                   jax.ShapeDtypeStruct((b, wq_t.shape[0], t), MXU_DTYPE),
                   jax.ShapeDtypeStruct((b, w_v_t.shape[0], t), MXU_DTYPE),
                   jax.ShapeDtypeStruct((b, w_g_t.shape[0], t), F32),
                   jax.ShapeDtypeStruct((b, N_KV, t, FEAT), MXU_DTYPE),
                   jax.ShapeDtypeStruct((b, N_KV, t, FEAT), MXU_DTYPE),
                   jax.ShapeDtypeStruct((b, t, 2 * kv_dim), F32),
                   jax.ShapeDtypeStruct((b, t, lru_w), MXU_DTYPE)],
        scratch_shapes=[pltpu.VMEM((V7X_SUBLANES, lru_w), F32),
                        pltpu.VMEM((V7X_SUBLANES, lru_w), F32)],
        compiler_params=_params("parallel", "arbitrary"),
        name="in_proj_lru",
    )(x, *weights)


CMP_GROUPS = V7X_LANES // HEAD_DIM


def _cmp_hidden(src_ref, pos_ref, w1_ref, b1_ref):
    ncp = src_ref.shape[1] // CMP_STRIDE
    first = jnp.zeros((ncp, CMP_GROUPS * CMP_HIDDEN), F32)
    second = jnp.zeros((ncp, CMP_GROUPS * CMP_HIDDEN), F32)
    for l in range(CMP_STRIDE):
        x = src_ref[0, pl.ds(l, ncp, stride=CMP_STRIDE), :]
        lo = (x + pos_ref[l:l + 1, :]).astype(MXU_DTYPE)
        hi = (x + pos_ref[CMP_STRIDE + l:CMP_STRIDE + l + 1, :]).astype(MXU_DTYPE)
        first = first + jnp.dot(lo, w1_ref[l], preferred_element_type=F32)
        second = second + jnp.dot(hi, w1_ref[CMP_STRIDE + l], preferred_element_type=F32)
    hid = first + pltpu.roll(second, ncp - 1, 0) + b1_ref[...]
    return _gelu_tanh(hid).astype(MXU_DTYPE)


def _cmp_k_kernel(src_ref, pos_ref, w1_ref, b1_ref, w2_ref, b2_ref, g_ref, o_ref):
    hid = _cmp_hidden(src_ref, pos_ref, w1_ref, b1_ref)
    ncp = hid.shape[0]
    idx = lax.broadcasted_iota(jnp.int32, (ncp, FEAT), 0)
    col = lax.broadcasted_iota(jnp.int32, (ncp, FEAT), 1)
    feat = jnp.where((col >= F_CMP) & (col < F_CMP + 3), (idx >> 6).astype(F32),
                     jnp.where((col >= F_CMP + 3) & (col < F_CMP + 6), (idx & 63).astype(F32), 0.0))
    for gl in range(CMP_GROUPS):
        r = jnp.dot(hid[:, gl * CMP_HIDDEN:(gl + 1) * CMP_HIDDEN], w2_ref[...],
                    preferred_element_type=F32) + b2_ref[...]
        ms = jnp.sum(r * r, axis=-1, keepdims=True) * (1.0 / HEAD_DIM)
        o_ref[0, gl] = (r * lax.rsqrt(ms + EPS) * g_ref[...] + feat).astype(o_ref.dtype)


def _cmp_v_kernel(src_ref, pos_ref, w1_ref, b1_ref, w2t_ref, b2_ref, o_ref):
    hid = _cmp_hidden(src_ref, pos_ref, w1_ref, b1_ref)
    for gl in range(CMP_GROUPS):
        r = _nt_dot(w2t_ref[...], hid[:, gl * CMP_HIDDEN:(gl + 1) * CMP_HIDDEN]) + b2_ref[...]
        o_ref[0, gl] = r.astype(o_ref.dtype)


def _compress(cmp_src, lane_block0, pos, w1, b1, w2, b2, gain_pad=None):
    b, t, _ = cmp_src.shape
    ncp = t // CMP_STRIDE
    hid_w = CMP_GROUPS * CMP_HIDDEN
    eye = jnp.eye(CMP_GROUPS, dtype=w1.dtype)
    w1_bd = jnp.einsum('ldf,gh->lgdhf', w1, eye).reshape(CMP_BLOCK, V7X_LANES, hid_w).astype(MXU_DTYPE)
    pos_t = jnp.tile(pos, (1, CMP_GROUPS))
    b1_t = jnp.tile(b1.reshape(1, CMP_HIDDEN), (1, CMP_GROUPS))
    full = lambda a: pl.BlockSpec(a.shape, lambda bi, p: (0,) * a.ndim)
    src_spec = pl.BlockSpec((1, t, V7X_LANES), lambda bi, p: (bi, 0, lane_block0 + p))
    grid = (b, N_KV // CMP_GROUPS)
    if gain_pad is not None:
        args = (pos_t, w1_bd, b1_t, w2, b2, gain_pad)
        return pl.pallas_call(
            _cmp_k_kernel,
            grid=grid,
            in_specs=[src_spec] + [full(a) for a in args],
            out_specs=pl.BlockSpec((1, CMP_GROUPS, ncp, FEAT), lambda bi, p: (bi, p, 0, 0)),
            out_shape=jax.ShapeDtypeStruct((b, N_KV, ncp, FEAT), MXU_DTYPE),
            compiler_params=_params("parallel", "parallel"),
            name="compress_k",
        )(cmp_src, *args)
    args = (pos_t, w1_bd, b1_t, w2, b2)
    return pl.pallas_call(
        _cmp_v_kernel,
        grid=grid,
        in_specs=[src_spec] + [full(a) for a in args],
        out_specs=pl.BlockSpec((1, CMP_GROUPS, V_ROWS, ncp), lambda bi, p: (bi, p, 0, 0)),
        out_shape=jax.ShapeDtypeStruct((b, N_KV, V_ROWS, ncp), MXU_DTYPE),
        compiler_params=_params("parallel", "parallel"),
        name="compress_v",
    )(cmp_src, *args)


def _split3(v):
    parts = []
    rest = np.asarray(v, np.float64)
    for _ in range(3):
        p = rest.astype(np.float32).astype(jnp.bfloat16).astype(np.float64)
        parts.append(p)
        rest = rest - p
    return parts


def _alibi_query_features():
    tab = np.zeros((N_KV, FEAT - F_POS, HPG * TQ), np.float64)
    for g in range(N_KV):
        for h in range(HPG):
            slope = 2.0 ** (-8.0 * (g * HPG + h + 1) / N_HEADS)
            parts = _split3(slope * LOG2E)
            lanes = slice(h * TQ, (h + 1) * TQ)
            for i, p in enumerate(parts):
                tab[g, i, lanes] = SEL_BLOCK * p
                tab[g, 3 + i, lanes] = p
                tab[g, 6 + i, lanes] = CMP_STRIDE * 64 * p
                tab[g, 9 + i, lanes] = CMP_STRIDE * p
    return jnp.asarray(tab, F32).astype(MXU_DTYPE)


def _block_map_t(n_cmp_pad, n_blk):
    cs = np.arange(n_cmp_pad) * CMP_STRIDE
    ce = cs + CMP_BLOCK - 1
    bs = np.arange(n_blk) * SEL_BLOCK
    be = bs + SEL_BLOCK - 1
    return jnp.asarray(((cs[None, :] <= be[:, None]) & (ce[None, :] >= bs[:, None])).astype(np.float32))


def _attn_kernel(qt_ref, qn_ref, gate_ref, kc_ref, vc_ref, ks_ref, vs_ref, kw_ref, vw_ref, alibi_ref, map_ref,
                 o_ref, qb_ref, qn_scr, qs_ref, imp_ref, s0_ref, s1_ref, p_ref, m_ref, acc_ref, nsel_ref,
                 ocmp_ref, *, n_blk, n_sel, n_tiles):
    lanes = HPG * TQ
    qi = pl.program_id(2)
    q0 = pl.multiple_of(qi * TQ, TQ)
    ncp = kc_ref.shape[2]
    lane_tok = lax.broadcasted_iota(jnp.int32, (1, lanes), 1) & (TQ - 1)

    def build_query(dst_ref, src_ref):
        for h in range(HPG):
            dst_ref[0:HEAD_DIM, h * TQ:(h + 1) * TQ] = src_ref[0, h * HEAD_DIM:(h + 1) * HEAD_DIM, :]
        dst_ref[F_SEL:F_POS, :] = jnp.zeros((F_POS - F_SEL, lanes), dst_ref.dtype)
        dst_ref[F_POS:FEAT, :] = alibi_ref[0]
        return dst_ref[...]

    def compressed_scores(qv):
        return jnp.dot(kc_ref[0, 0], qv, preferred_element_type=F32)

    def compressed_branch(sc, tile0):
        last_cmp = (tile0 + lane_tok - (CMP_BLOCK - 1)) >> 4
        sc = jnp.where(lax.broadcasted_iota(jnp.int32, (ncp, lanes), 0) <= last_cmp, sc, NEG_MASK)
        ec = jnp.exp2(sc - jnp.max(sc, axis=0, keepdims=True))
        acc_c = jnp.dot(vc_ref[0, 0], ec.astype(MXU_DTYPE), preferred_element_type=F32)
        inv_c = jnp.where(last_cmp >= 0, 1.0 / jnp.maximum(acc_c[HEAD_DIM:HEAD_DIM + 1], 1e-30), 0.0)
        psum = ec[:, 0:TQ] * inv_c[:, 0:TQ]
        for h in range(1, HPG):
            psum = psum + ec[:, h * TQ:(h + 1) * TQ] * inv_c[:, h * TQ:(h + 1) * TQ]
        return acc_c[0:HEAD_DIM] * inv_c, psum

    def select_blocks(psum, tile):
        imp = jnp.dot(map_ref[...], psum, preferred_element_type=F32)
        t_q = tile * TQ + lax.broadcasted_iota(jnp.int32, (n_blk, TQ), 1)
        blk = lax.broadcasted_iota(jnp.int32, (n_blk, TQ), 0)
        cur = t_q >> 6
        forced = (blk == 0) | (blk == cur) | (blk == cur - 1)
        imp = jnp.where(blk <= cur, imp + jnp.where(forced, FORCE_BONUS, 0.0), NEG_MASK)
        imp_ref[...] = imp
        n_chunks = n_blk // V7X_SUBLANES
        chunks = [imp[c * V7X_SUBLANES:(c + 1) * V7X_SUBLANES] for c in range(n_chunks)]
        ranks = [jnp.zeros((V7X_SUBLANES, TQ), jnp.int32) for _ in range(n_chunks)]
        sub = lax.broadcasted_iota(jnp.int32, (V7X_SUBLANES, TQ), 0)
        for k in range(n_blk):
            row = imp_ref[k:k + 1, :]
            for c in range(n_chunks):
                lo = c * V7X_SUBLANES
                if lo > k:
                    one = jnp.where(row >= chunks[c], 1, 0)
                elif lo + V7X_SUBLANES - 1 <= k:
                    one = jnp.where(row > chunks[c], 1, 0)
                else:
                    one = jnp.where(sub + lo > k, jnp.where(row >= chunks[c], 1, 0),
                                    jnp.where(row > chunks[c], 1, 0))
                ranks[c] = ranks[c] + one
        rank = jnp.concatenate(ranks, axis=0)
        first_own_blk = tile * (TQ // SEL_BLOCK)
        return jnp.where((rank < n_sel) & (blk < first_own_blk), 0.0, 1.0).astype(nsel_ref.dtype)

    qb = build_query(qb_ref, qt_ref)
    t_lane = q0 + lane_tok

    @pl.when(qi == 0)
    def _():
        ocmp_ref[...] = compressed_branch(compressed_scores(qb), 0)[0]
        nsel_ref[...] = jnp.ones(nsel_ref.shape, nsel_ref.dtype)

    o_cmp = ocmp_ref[...]
    not_sel = nsel_ref[...]
    qs_ref[...] = qb
    for h in range(HPG):
        qs_ref[F_SEL:F_SEL + n_blk, h * TQ:(h + 1) * TQ] = not_sel

    def qk_chunk(c):
        k0 = pl.multiple_of(c * TK_SEL, TK_SEL)
        return jnp.dot(ks_ref[0, 0, pl.ds(k0, TK_SEL), :], qs_ref[...], preferred_element_type=F32)

    def pv_chunk(c):
        k0 = pl.multiple_of(c * TK_SEL, TK_SEL)
        return jnp.dot(vs_ref[0, :, pl.ds(k0, TK_SEL)], p_ref[...], preferred_element_type=F32)

    def stage(k, carry, s_cur_ref, s_next_ref):
        m, acc = carry
        if s_next_ref is not None:
            s_next_ref[...] = qk_chunk(k + 1)
        pv_prev = pv_chunk(jnp.maximum(k - 1, 0))
        s = s_cur_ref[...]
        m_new = jnp.maximum(m, jnp.max(s, axis=0, keepdims=True))
        p_ref[...] = jnp.exp2(s - m_new).astype(p_ref.dtype)
        return m_new, jnp.exp2(m - m_new) * (acc + pv_prev)

    n_chunks_sel = jnp.maximum((qi * TQ + TK_SEL - 1) // TK_SEL, 1)

    nxt = jnp.minimum(qi + 1, n_tiles - 1)
    w0 = pl.multiple_of(jnp.maximum(q0 - WINDOW, 0), TQ)
    sc_next = compressed_scores(build_query(qn_scr, qn_ref))
    s0_ref[...] = qk_chunk(0)
    p_ref[...] = jnp.zeros(p_ref.shape, p_ref.dtype)
    sw = jnp.dot(kw_ref[0, 0, pl.ds(w0, WIN_KEYS), :], qb, preferred_element_type=F32)
    sd = jnp.dot(ks_ref[0, 0, pl.ds(q0, TQ), :], qb, preferred_element_type=F32)

    o_cmp_next, psum_next = compressed_branch(sc_next, nxt * TQ)
    ocmp_ref[...] = o_cmp_next
    nsel_ref[...] = select_blocks(psum_next, nxt)

    dw = (t_lane - w0) - lax.broadcasted_iota(jnp.int32, (WIN_KEYS, lanes), 0)
    sw = jnp.where(lax.bitcast_convert_type(dw, jnp.uint32) < WINDOW, sw, NEG_MASK)
    pw = jnp.exp2(sw - jnp.max(sw, axis=0, keepdims=True))
    acc_w = jnp.dot(vw_ref[0, :, pl.ds(w0, WIN_KEYS)], pw.astype(MXU_DTYPE), preferred_element_type=F32)
    o_win = acc_w[0:HEAD_DIM] * (1.0 / acc_w[HEAD_DIM:HEAD_DIM + 1])

    kd = q0 + lax.broadcasted_iota(jnp.int32, (TQ, lanes), 0)
    sd = jnp.where(kd <= t_lane, sd, NEG_MASK)
    m_d = jnp.max(sd, axis=0, keepdims=True)
    pd = jnp.exp2(sd - m_d).astype(MXU_DTYPE)
    acc_d = jnp.dot(vs_ref[0, :, pl.ds(q0, TQ)], pd, preferred_element_type=F32)

    def pair_body(j, carry):
        carry = stage(2 * j, carry, s0_ref, s1_ref)
        return stage(2 * j + 1, carry, s1_ref, s0_ref)

    carry = (jnp.full((1, lanes), NEG_MASK, F32), jnp.zeros((V_ROWS, lanes), F32))
    carry = lax.fori_loop(0, (n_chunks_sel - 1) // 2, pair_body, carry)
    last = n_chunks_sel - 1

    def finish(carry):
        m_f, acc_f = carry
        m_ref[...] = m_f
        acc_ref[...] = acc_f + pv_chunk(last)

    @pl.when((n_chunks_sel & 1) == 0)
    def _():
        finish(stage(last, stage(last - 1, carry, s0_ref, s1_ref), s1_ref, None))

    @pl.when((n_chunks_sel & 1) == 1)
    def _():
        finish(stage(last, carry, s0_ref, None))

    m_s = m_ref[...]
    m_all = jnp.maximum(m_s, m_d)
    acc_s = jnp.exp2(m_s - m_all) * acc_ref[...] + jnp.exp2(m_d - m_all) * acc_d
    o_sel = acc_s[0:HEAD_DIM] * (1.0 / acc_s[HEAD_DIM:HEAD_DIM + 1])

    gates = gate_ref[0]
    def gate_row(j):
        return jnp.concatenate([gates[j * HPG + h:j * HPG + h + 1, :] for h in range(HPG)], axis=1)
    o_t = gate_row(0) * o_cmp + gate_row(1) * o_sel + gate_row(2) * o_win
    for hp in range(HPG // 2):
        pair = jnp.concatenate([o_t[:, (2 * hp) * TQ:(2 * hp + 1) * TQ],
                                o_t[:, (2 * hp + 1) * TQ:(2 * hp + 2) * TQ]], axis=0)
        o_ref[0, :, hp * 2 * HEAD_DIM:(hp + 1) * 2 * HEAD_DIM] = pair.T.astype(o_ref.dtype)


def _attention(q_t, gates_t, k_cmp, v_cmp_t, k_sel, k_win, v_t):
    b, _, t = q_t.shape
    ncp = k_cmp.shape[2]
    n_blk = t // SEL_BLOCK
    assert t % TK_SEL == 0 and t >= WIN_KEYS and F_SEL + n_blk <= F_POS, "unsupported sequence length"
    n_sel = min(N_SEL, n_blk)
    lanes = HPG * TQ
    rows = HPG * HEAD_DIM
    alibi = _alibi_query_features()
    blk_map_t = _block_map_t(ncp, n_blk)
    n_tiles = t // TQ
    kernel = functools.partial(_attn_kernel, n_blk=n_blk, n_sel=n_sel, n_tiles=n_tiles)
    return pl.pallas_call(
        kernel,
        grid=(b, N_KV, n_tiles),
        in_specs=[
            pl.BlockSpec((1, rows, TQ), lambda bi, g, i: (bi, g, i)),
            pl.BlockSpec((1, rows, TQ), lambda bi, g, i: (bi, g, jnp.minimum(i + 1, n_tiles - 1))),
            pl.BlockSpec((1, 16, TQ), lambda bi, g, i: (bi, g, i)),
            pl.BlockSpec((1, 1, ncp, FEAT), lambda bi, g, i: (bi, g, 0, 0)),
            pl.BlockSpec((1, 1, V_ROWS, ncp), lambda bi, g, i: (bi, g, 0, 0)),
            pl.BlockSpec((1, 1, t, FEAT), lambda bi, g, i: (bi, g, 0, 0)),
            pl.BlockSpec((1, V_ROWS, t), lambda bi, g, i: (bi, g, 0)),
            pl.BlockSpec((1, 1, t, FEAT), lambda bi, g, i: (bi, g, 0, 0)),
            pl.BlockSpec((1, V_ROWS, t), lambda bi, g, i: (bi, N_KV + g, 0)),
            pl.BlockSpec((1, FEAT - F_POS, lanes), lambda bi, g, i: (g, 0, 0)),
            pl.BlockSpec((n_blk, ncp), lambda bi, g, i: (0, 0)),
        ],
        out_specs=pl.BlockSpec((1, TQ, rows), lambda bi, g, i: (bi, i, g)),
        out_shape=jax.ShapeDtypeStruct((b, t, N_HEADS * HEAD_DIM), MXU_DTYPE),
        scratch_shapes=[pltpu.VMEM((FEAT, lanes), MXU_DTYPE),
                        pltpu.VMEM((FEAT, lanes), MXU_DTYPE),
                        pltpu.VMEM((FEAT, lanes), MXU_DTYPE),
                        pltpu.VMEM((n_blk, TQ), F32),
                        pltpu.VMEM((TK_SEL, lanes), F32),
                        pltpu.VMEM((TK_SEL, lanes), F32),
                        pltpu.VMEM((TK_SEL, lanes), MXU_DTYPE),
                        pltpu.VMEM((1, lanes), F32),
                        pltpu.VMEM((V_ROWS, lanes), F32),
                        pltpu.VMEM((n_blk, TQ), MXU_DTYPE),
                        pltpu.VMEM((HEAD_DIM, lanes), F32)],
        compiler_params=_params("parallel", "parallel", "arbitrary"),
        name="nsa_attention",
    )(q_t, q_t, gates_t, k_cmp, v_cmp_t, k_sel, v_t, k_win, v_t, alibi, blk_map_t)


def _attn_tile_kernel(*refs, qi0, n_q, n_chunks, n_blk, n_sel, has_prev):
    (qt_ref, gate_ref, kc_ref, vc_ref, ks_ref, vs_ref, kd_ref, vd_ref, kw0_ref, kw1_ref, kw2_ref,
     vw0_ref, vw1_ref, vw2_ref, alibi_ref, map_ref) = refs[:16]
    o_ref, qb_ref, qs_ref, imp_ref = refs[17:] if has_prev else refs[16:]
    lanes = HPG * TQ
    qi = qi0 + pl.program_id(2)
    q0 = pl.multiple_of(qi * TQ, TQ)
    ncp = kc_ref.shape[2]
    t_lane = q0 + (lax.broadcasted_iota(jnp.int32, (1, lanes), 1) & (TQ - 1))
    blocks_per_tile = TQ // SEL_BLOCK
    own_max = (qi0 + n_q - 1) * blocks_per_tile
    valid_max = (qi0 + n_q) * blocks_per_tile

    for h in range(HPG):
        qb_ref[0:HEAD_DIM, h * TQ:(h + 1) * TQ] = qt_ref[0, h * HEAD_DIM:(h + 1) * HEAD_DIM, :]
    qb_ref[F_SEL:F_POS, :] = jnp.zeros((F_POS - F_SEL, lanes), qb_ref.dtype)
    qb_ref[F_POS:FEAT, :] = alibi_ref[0]
    qb = qb_ref[...]

    sc = jnp.dot(kc_ref[0, 0], qb, preferred_element_type=F32)
    sw = [jnp.dot(r[0, 0], qb, preferred_element_type=F32) for r in (kw0_ref, kw1_ref, kw2_ref)]
    sd = jnp.dot(kd_ref[0, 0], qb, preferred_element_type=F32)

    last_cmp = (t_lane - (CMP_BLOCK - 1)) >> 4
    sc = jnp.where(lax.broadcasted_iota(jnp.int32, (ncp, lanes), 0) <= last_cmp, sc, NEG_MASK)
    ec = jnp.exp2(sc - jnp.max(sc, axis=0, keepdims=True))
    acc_c = jnp.dot(vc_ref[0, 0], ec.astype(MXU_DTYPE), preferred_element_type=F32)
    inv_c = jnp.where(last_cmp >= 0, 1.0 / jnp.maximum(acc_c[HEAD_DIM:HEAD_DIM + 1], 1e-30), 0.0)
    o_cmp = acc_c[0:HEAD_DIM] * inv_c

    blk = lax.broadcasted_iota(jnp.int32, (n_blk, TQ), 0)
    first_own_blk = qi * blocks_per_tile
    if n_chunks == 0:
        not_sel = jnp.ones((n_blk, TQ), qs_ref.dtype)
    else:
        rank = jnp.zeros((n_blk, TQ), jnp.int32)
        if valid_max > n_sel:
            psum = ec[:, 0:TQ] * inv_c[:, 0:TQ]
            for h in range(1, HPG):
                psum = psum + ec[:, h * TQ:(h + 1) * TQ] * inv_c[:, h * TQ:(h + 1) * TQ]
            imp = jnp.dot(map_ref[...], psum, preferred_element_type=F32)
            cur = (q0 + lax.broadcasted_iota(jnp.int32, (n_blk, TQ), 1)) >> 6
            forced = (blk == 0) | (blk == cur) | (blk == cur - 1)
            imp = jnp.where(blk <= cur, imp + jnp.where(forced, FORCE_BONUS, 0.0), NEG_MASK)
            imp_ref[...] = imp
            n_rank_chunks = -(-own_max // V7X_SUBLANES)
            chunks = [imp[c * V7X_SUBLANES:(c + 1) * V7X_SUBLANES] for c in range(n_rank_chunks)]
            ranks = [jnp.zeros((V7X_SUBLANES, TQ), jnp.int32) for _ in range(n_rank_chunks)]
            sub = lax.broadcasted_iota(jnp.int32, (V7X_SUBLANES, TQ), 0)
            for k in range(min(valid_max, n_blk)):
                row = imp_ref[k:k + 1, :]
                for c in range(n_rank_chunks):
                    lo = c * V7X_SUBLANES
                    if lo > k:
                        one = jnp.where(row >= chunks[c], 1, 0)
                    elif lo + V7X_SUBLANES - 1 <= k:
                        one = jnp.where(row > chunks[c], 1, 0)
                    else:
                        one = jnp.where(sub + lo > k, jnp.where(row >= chunks[c], 1, 0),
                                        jnp.where(row > chunks[c], 1, 0))
                    ranks[c] = ranks[c] + one
            pad = [jnp.zeros(((n_blk // V7X_SUBLANES - n_rank_chunks) * V7X_SUBLANES, TQ), jnp.int32)]
            rank = jnp.concatenate(ranks + (pad if pad[0].shape[0] else []), axis=0)
        not_sel = jnp.where((rank < n_sel) & (blk < first_own_blk), 0.0, 1.0).astype(qs_ref.dtype)

    m_s = jnp.full((1, lanes), NEG_MASK, F32)
    acc_s = jnp.zeros((V_ROWS, lanes), F32)
    if n_chunks > 0:
        qs_ref[...] = qb
        for h in range(HPG):
            qs_ref[F_SEL:F_SEL + n_blk, h * TQ:(h + 1) * TQ] = not_sel
        qs = qs_ref[...]
        qk = lambda k: jnp.dot(ks_ref[0, 0, k * TK_SEL:(k + 1) * TK_SEL, :], qs, preferred_element_type=F32)
        pv = lambda k, p: jnp.dot(vs_ref[0, :, k * TK_SEL:(k + 1) * TK_SEL], p, preferred_element_type=F32)
        s_next = qk(0)
        p_prev = None
        for k in range(n_chunks):
            s_cur = s_next
            if k + 1 < n_chunks:
                s_next = qk(k + 1)
            if p_prev is not None:
                acc_s = acc_s + pv(k - 1, p_prev)
            m_new = jnp.maximum(m_s, jnp.max(s_cur, axis=0, keepdims=True))
            p_prev = jnp.exp2(s_cur - m_new).astype(MXU_DTYPE)
            acc_s = jnp.exp2(m_s - m_new) * acc_s
            m_s = m_new
        acc_s = acc_s + pv(n_chunks - 1, p_prev)

    row_pos = lax.broadcasted_iota(jnp.int32, (TQ, lanes), 0)
    causal = q0 + row_pos <= t_lane
    sd = jnp.where(causal, sd, NEG_MASK)
    m_d = jnp.max(sd, axis=0, keepdims=True)
    acc_d = jnp.dot(vd_ref[0], jnp.exp2(sd - m_d).astype(MXU_DTYPE), preferred_element_type=F32)
    m_all = jnp.maximum(m_s, m_d)
    acc_s = jnp.exp2(m_s - m_all) * acc_s + jnp.exp2(m_d - m_all) * acc_d
    o_sel = acc_s[0:HEAD_DIM] * (1.0 / acc_s[HEAD_DIM:HEAD_DIM + 1])

    wb = jnp.maximum(qi - WINDOW // TQ, 0)
    if qi0 >= WINDOW // TQ:
        d0 = (t_lane - wb * TQ) - row_pos
        sw = [jnp.where(d0 < WINDOW, sw[0], NEG_MASK), sw[1], jnp.where(causal, sw[2], NEG_MASK)]
    else:
        for j in range(3):
            dj = (t_lane - (wb + j) * TQ) - row_pos
            sw[j] = jnp.where(lax.bitcast_convert_type(dj, jnp.uint32) < WINDOW, sw[j], NEG_MASK)
    m_w = jnp.max(jnp.maximum(jnp.maximum(sw[0], sw[1]), sw[2]), axis=0, keepdims=True)
    acc_w = jnp.zeros((V_ROWS, lanes), F32)
    for s_j, v_ref in zip(sw, (vw0_ref, vw1_ref, vw2_ref)):
        acc_w = acc_w + jnp.dot(v_ref[0], jnp.exp2(s_j - m_w).astype(MXU_DTYPE), preferred_element_type=F32)
    o_win = acc_w[0:HEAD_DIM] * (1.0 / acc_w[HEAD_DIM:HEAD_DIM + 1])

    gates = gate_ref[0]
    def gate_row(j):
        return jnp.concatenate([gates[j * HPG + h:j * HPG + h + 1, :] for h in range(HPG)], axis=1)
    o_t = gate_row(0) * o_cmp + gate_row(1) * o_sel + gate_row(2) * o_win
    for hp in range(HPG // 2):
        pair = jnp.concatenate([o_t[:, (2 * hp) * TQ:(2 * hp + 1) * TQ],
                                o_t[:, (2 * hp + 1) * TQ:(2 * hp + 2) * TQ]], axis=0)
        o_ref[0, :, hp * 2 * HEAD_DIM:(hp + 1) * 2 * HEAD_DIM] = pair.T.astype(o_ref.dtype)


def _attention_static(q_t, gates_t, k_cmp, v_cmp_t, k_sel, k_win, v_t):
    b, _, t = q_t.shape
    n_blk = t // SEL_BLOCK
    assert t % TK_SEL == 0 and t >= WIN_KEYS and F_SEL + n_blk <= F_POS, "unsupported sequence length"
    n_sel = min(N_SEL, n_blk)
    n_tiles = t // TQ
    lanes = HPG * TQ
    rows = HPG * HEAD_DIM
    alibi = _alibi_query_features()
    out_shape = jax.ShapeDtypeStruct((b, t, N_HEADS * HEAD_DIM), MXU_DTYPE)
    win_tiles = WINDOW // TQ

    groups = []
    for qi in range(n_tiles):
        c = -(-qi * TQ // TK_SEL)
        if groups and groups[-1][2] == c:
            groups[-1][1] += 1
        else:
            groups.append([qi, 1, c])

    attn = None
    for qi0, n_q, c in groups:
        ncp = min(t // CMP_STRIDE, -(-((qi0 + n_q) * TQ // CMP_STRIDE) // V7X_LANES) * V7X_LANES)
        blk_map_t = _block_map_t(ncp, n_blk)
        kc = max(c, 1) * TK_SEL
        tile = lambda bi, g, i, qi0=qi0: qi0 + i
        wtile = lambda j, qi0=qi0: (lambda bi, g, i: jnp.maximum(qi0 + i - win_tiles, 0) + j)
        in_specs = [
            pl.BlockSpec((1, rows, TQ), lambda bi, g, i: (bi, g, tile(bi, g, i))),
            pl.BlockSpec((1, 16, TQ), lambda bi, g, i: (bi, g, tile(bi, g, i))),
            pl.BlockSpec((1, 1, ncp, FEAT), lambda bi, g, i: (bi, g, 0, 0)),
            pl.BlockSpec((1, 1, V_ROWS, ncp), lambda bi, g, i: (bi, g, 0, 0)),
            pl.BlockSpec((1, 1, kc, FEAT), lambda bi, g, i: (bi, g, 0, 0)),
            pl.BlockSpec((1, V_ROWS, kc), lambda bi, g, i: (bi, g, 0)),
            pl.BlockSpec((1, 1, TQ, FEAT), lambda bi, g, i: (bi, g, tile(bi, g, i), 0)),
            pl.BlockSpec((1, V_ROWS, TQ), lambda bi, g, i: (bi, g, tile(bi, g, i))),
        ]
        in_specs += [pl.BlockSpec((1, 1, TQ, FEAT), lambda bi, g, i, f=wtile(j): (bi, g, f(bi, g, i), 0))
                     for j in range(3)]
        in_specs += [pl.BlockSpec((1, V_ROWS, TQ), lambda bi, g, i, f=wtile(j): (bi, N_KV + g, f(bi, g, i)))
                     for j in range(3)]
        in_specs += [pl.BlockSpec((1, FEAT - F_POS, lanes), lambda bi, g, i: (g, 0, 0)),
                     pl.BlockSpec((n_blk, ncp), lambda bi, g, i: (0, 0))]
        args = [q_t, gates_t, k_cmp, v_cmp_t, k_sel, v_t, k_sel, v_t, k_win, k_win, k_win, v_t, v_t, v_t,
                alibi, blk_map_t]
        aliases = {}
        if attn is not None:
            in_specs.append(pl.BlockSpec(memory_space=pl.ANY))
            args.append(attn)
            aliases = {len(args) - 1: 0}
        kernel = functools.partial(_attn_tile_kernel, qi0=qi0, n_q=n_q, n_chunks=c, n_blk=n_blk, n_sel=n_sel,
                                   has_prev=attn is not None)
        attn = pl.pallas_call(
            kernel,
            grid=(b, N_KV, n_q),
            in_specs=in_specs,
            out_specs=pl.BlockSpec((1, TQ, rows), lambda bi, g, i: (bi, tile(bi, g, i), g)),
            out_shape=out_shape,
            scratch_shapes=[pltpu.VMEM((FEAT, lanes), MXU_DTYPE),
                            pltpu.VMEM((FEAT, lanes), MXU_DTYPE),
                            pltpu.VMEM((n_blk, TQ), F32)],
            input_output_aliases=aliases,
            compiler_params=_params("parallel", "parallel", "arbitrary"),
            name=f"nsa_attention_c{c}",
        )(*args)
    return attn


def _lru_tile(hin, wl_ref, cw_ref, cb_ref, wa_ref, ba_ref, wi_ref, bi_ref, lam_ref, o_ref, tail_ref, h_ref):
    tt = hin.shape[0]
    w = lam_ref.shape[1]

    @pl.when(pl.program_id(1) == 0)
    def _():
        tail_ref[...] = jnp.zeros_like(tail_ref)
        h_ref[...] = jnp.zeros_like(h_ref)

    def gate_matmul(xb, w_ref):
        n = w_ref.shape[1]
        return jnp.concatenate([jnp.dot(xb[:, j * n:(j + 1) * n], w_ref[j], preferred_element_type=F32)
                                for j in range(w_ref.shape[0])], axis=1)

    x = jnp.dot(hin, wl_ref[:, 0:w], preferred_element_type=F32)
    gate = jnp.dot(hin, wl_ref[:, w:2 * w], preferred_element_type=F32)
    ng = tt // V7X_SUBLANES
    sub = lax.broadcasted_iota(jnp.int32, (ng, V7X_SUBLANES, w), 1)
    x3 = x.reshape(ng, V7X_SUBLANES, w)
    xprev3 = jnp.concatenate([tail_ref[...][None], x3], axis=0)
    tail_ref[...] = x[tt - V7X_SUBLANES:tt]
    xc = x * cw_ref[CONV_W - 1:CONV_W, :] + cb_ref[...]
    for s in range(1, CONV_W):
        rot = pltpu.roll(xprev3, s, 1)
        xs = jnp.where(sub >= s, rot[1:], rot[:-1])
        xc = xc + xs.reshape(tt, w) * cw_ref[CONV_W - 1 - s:CONV_W - s, :]

    xb = xc.astype(MXU_DTYPE)
    r = _sigmoid(gate_matmul(xb, wa_ref) + ba_ref[...])
    i = _sigmoid(gate_matmul(xb, wi_ref) + bi_ref[...])
    z = -lam_ref[...]
    softplus = jnp.maximum(z, 0.0) + jnp.log1p(jnp.exp(-jnp.abs(z)))
    log_a = -LRU_C * r * softplus
    a = jnp.exp(log_a)
    bb = jnp.sqrt(_one_minus_sq(a, log_a)) * (i * xc)

    a3 = a.reshape(ng, V7X_SUBLANES, w)
    b3 = bb.reshape(ng, V7X_SUBLANES, w)
    for d in (1, 2, 4):
        ok = sub >= d
        a_sh = pltpu.roll(a3, d, 1)
        b_sh = pltpu.roll(b3, d, 1)
        b3 = jnp.where(ok, a3 * b_sh + b3, b3)
        a3 = jnp.where(ok, a3 * a_sh, a3)
    carry = h_ref[0:1, :]
    groups = []
    for g in range(ng):
        hg = b3[g] + a3[g] * carry
        groups.append(hg)
        carry = hg[V7X_SUBLANES - 1:V7X_SUBLANES, :]
    hcur = jnp.concatenate(groups, axis=0)
    h_ref[...] = jnp.broadcast_to(carry, h_ref.shape)
    o_ref[0] = (hcur * _gelu_tanh(gate)).astype(o_ref.dtype)


def _merge_kernel(attn_ref, lru_ref, h_ref, x_ref, wm_ref, wa_ref, wl_ref, wo_ref, g2_ref,
                  x1_ref, h2_ref):
    d = x_ref.shape[1]
    h = h_ref[...]
    mg0 = _sigmoid(jnp.dot(h, wm_ref[:, 0:d], preferred_element_type=F32))
    mg1 = _sigmoid(jnp.dot(h, wm_ref[:, d:2 * d], preferred_element_type=F32))
    ya = jnp.dot(attn_ref[...], wa_ref[...], preferred_element_type=F32)
    yl = jnp.dot(lru_ref[...], wl_ref[...], preferred_element_type=F32)
    merged = mg0 * ya + mg1 * yl
    x1 = x_ref[...] + jnp.dot(merged.astype(MXU_DTYPE), wo_ref[...], preferred_element_type=F32)
    x1_ref[...] = x1
    ms = jnp.mean(x1 * x1, axis=-1, keepdims=True)
    h2_ref[...] = (x1 * lax.rsqrt(ms + EPS) * g2_ref[...]).astype(h2_ref.dtype)


def _merge(attn, lru, h2d, x2d, w_mg, wa, wl, wo, g2, tm=256):
    m, d = x2d.shape
    row = pl.BlockSpec((tm, d), lambda i: (i, 0))
    full = lambda a: pl.BlockSpec(a.shape, lambda i: (0, 0))
    g2 = g2.reshape(1, d)
    return pl.pallas_call(
        _merge_kernel,
        grid=(m // tm,),
        in_specs=[row, row, row, row, full(w_mg), full(wa), full(wl), full(wo), full(g2)],
        out_specs=[row, row],
        out_shape=[jax.ShapeDtypeStruct((m, d), F32), jax.ShapeDtypeStruct((m, d), MXU_DTYPE)],
        compiler_params=_params("parallel"),
        name="merge_out",
    )(attn, lru, h2d, x2d, w_mg, wa, wl, wo, g2)


def _ffn_kernel(h_ref, x1_ref, wg_ref, wu_ref, wd_ref, o_ref):
    @pl.when(pl.program_id(1) == 0)
    def _():
        o_ref[...] = x1_ref[...]

    h = h_ref[...]
    g = jnp.dot(h, wg_ref[...], preferred_element_type=F32)
    u = jnp.dot(h, wu_ref[...], preferred_element_type=F32)
    act = (g * _sigmoid(g) * u).astype(MXU_DTYPE)
    o_ref[...] += jnp.dot(act, wd_ref[...], preferred_element_type=F32)


def _ffn(h2, x1, wg, wu, wd, tm=1024, tf=256):
    m, d = x1.shape
    f = wg.shape[1]
    return pl.pallas_call(
        _ffn_kernel,
        grid=(m // tm, f // tf),
        in_specs=[pl.BlockSpec((tm, d), lambda i, j: (i, 0)),
                  pl.BlockSpec((tm, d), lambda i, j: (i, 0)),
                  pl.BlockSpec((d, tf), lambda i, j: (0, j)),
                  pl.BlockSpec((d, tf), lambda i, j: (0, j)),
                  pl.BlockSpec((tf, d), lambda i, j: (j, 0))],
        out_specs=pl.BlockSpec((tm, d), lambda i, j: (i, 0)),
        out_shape=jax.ShapeDtypeStruct((m, d), F32),
        compiler_params=_params("parallel", "arbitrary"),
        name="swiglu_ffn",
    )(h2, x1, wg, wu, wd)


def _pad_last(a, n):
    return jnp.pad(a, [(0, 0)] * (a.ndim - 1) + [(0, n - a.shape[-1])])


def _layer(x, norm1_g, w_in, q_norm_g, k_norm_g, cmp_pos_k, cmp_w1_k, cmp_b1_k, cmp_w2_k, cmp_b2_k,
           cmp_pos_v, cmp_w1_v, cmp_b1_v, cmp_w2_v, cmp_b2_v, conv_w, conv_b, lru_wa, lru_ba,
           lru_wi, lru_bi, lru_lambda, w_o_attn, w_o_lru, w_out, norm2_g, w_gate, w_up, w_down):
    b, t, d = x.shape
    m = b * t
    attn_dim = N_HEADS * HEAD_DIM
    kv_dim = N_KV * HEAD_DIM
    lru_w = lru_lambda.shape[0]
    assert t % 512 == 0
    o1 = attn_dim
    o2 = o1 + 6 * kv_dim
    o3 = o2 + 3 * N_HEADS
    o4 = o3 + lru_w
    o5 = o4 + lru_w
    cast = lambda a: a.astype(MXU_DTYPE)

    wq_t = cast(w_in[:, :o1].T)
    w_kv = w_in[:, o1:o2].reshape(d, 6, N_KV, HEAD_DIM)
    w_cmp_src = cast(w_kv[:, 0:2].reshape(d, 2 * kv_dim))
    w_k2 = cast(jnp.stack([w_kv[:, 2], w_kv[:, 4]], axis=1).reshape(d, 2 * kv_dim))
    k_gain2 = jnp.stack([jnp.tile(k_norm_g[1], N_KV), jnp.tile(k_norm_g[2], N_KV)])
    lane_grp = np.arange(kv_dim) // HEAD_DIM
    grp_avg = cast(jnp.asarray((lane_grp[:, None] == lane_grp[None, :]) / HEAD_DIM, F32))
    place_np = np.zeros((kv_dim, N_KV * FEAT), np.float32)
    place_np[np.arange(kv_dim), lane_grp * FEAT + np.arange(kv_dim) % HEAD_DIM] = 1.0
    place = cast(jnp.asarray(place_np))
    w_v = jnp.stack([w_kv[:, 3], w_kv[:, 5]], axis=1)
    w_v_t = cast(_pad_last(w_v, V_ROWS).reshape(d, 2 * N_KV * V_ROWS).T)
    v_ones_col = jnp.tile(jnp.arange(V_ROWS) == HEAD_DIM, 2 * N_KV).astype(F32).reshape(-1, 1)
    w_g = w_in[:, o2:o3].reshape(d, N_KV, HPG, 3).transpose(0, 1, 3, 2).reshape(d, N_KV, 3 * HPG)
    w_g_t = cast(_pad_last(w_g, 16).reshape(d, N_KV * 16).T)
    w_lru = cast(w_in[:, o3:o5])
    w_mg = cast(w_in[:, o5:])
    q_gain_col = jnp.tile(q_norm_g * (HEAD_DIM ** -0.5 * LOG2E), N_HEADS).reshape(attn_dim, 1)
    gain_pad = lambda g: _pad_last(g.reshape(1, HEAD_DIM), FEAT)

    per_slab = V7X_MXU_DIM // (lru_w // LRU_BLOCKS)
    eye = jnp.eye(per_slab, dtype=F32)
    slabs = lambda wgt: cast(jnp.einsum('snkj,nm->snkmj', wgt.reshape(LRU_BLOCKS // per_slab, per_slab,
                                                                       *wgt.shape[1:]), eye)
                             .reshape(LRU_BLOCKS // per_slab, V7X_MXU_DIM, V7X_MXU_DIM))
    vec = lambda v: v.reshape(1, lru_w)

    h3d, q_t, v_t, gates_t, k_sel, k_win, cmp_src, lru = _in_proj(
        x,
        (norm1_g.reshape(1, d), wq_t, q_gain_col, w_v_t, v_ones_col, w_g_t, w_k2, k_gain2, grp_avg, place,
         w_cmp_src),
        (w_lru, conv_w.reshape(CONV_W, lru_w), vec(conv_b), slabs(lru_wa), vec(lru_ba), slabs(lru_wi),
         vec(lru_bi), vec(lru_lambda)))
    h2d = h3d.reshape(m, d)

    k_cmp = _compress(cmp_src, 0, cmp_pos_k, cmp_w1_k, cmp_b1_k, cast(_pad_last(cmp_w2_k, FEAT)),
                      _pad_last(cmp_b2_k.reshape(1, HEAD_DIM), FEAT), gain_pad(k_norm_g[0]))
    v_cmp_t = _compress(cmp_src, N_KV // CMP_GROUPS, cmp_pos_v, cmp_w1_v, cmp_b1_v,
                        cast(_pad_last(cmp_w2_v, V_ROWS).T),
                        jnp.concatenate([cmp_b2_v, v_ones_col[HEAD_DIM:V_ROWS, 0]]).reshape(V_ROWS, 1))

    attn = _attention_static(q_t, gates_t, k_cmp, v_cmp_t, k_sel, k_win, v_t)

    x1, h2 = _merge(attn.reshape(m, attn_dim), lru.reshape(m, lru_w), h2d, x.reshape(m, d), w_mg,
                    cast(w_o_attn), cast(w_o_lru), cast(w_out), norm2_g)
    out = _ffn(h2, x1, cast(w_gate), cast(w_up), cast(w_down))
    return out.reshape(b, t, d)


def kernel(x, norm1_g, w_in, q_norm_g, k_norm_g, cmp_pos_k, cmp_w1_k, cmp_b1_k, cmp_w2_k, cmp_b2_k,
           cmp_pos_v, cmp_w1_v, cmp_b1_v, cmp_w2_v, cmp_b2_v, conv_w, conv_b, lru_wa, lru_ba,
           lru_wi, lru_bi, lru_lambda, w_o_attn, w_o_lru, w_out, norm2_g, w_gate, w_up, w_down):
    for l in range(norm1_g.shape[0]):
        x = _layer(x, norm1_g[l], w_in[l], q_norm_g[l], k_norm_g[l], cmp_pos_k[l], cmp_w1_k[l],
                   cmp_b1_k[l], cmp_w2_k[l], cmp_b2_k[l], cmp_pos_v[l], cmp_w1_v[l], cmp_b1_v[l],
                   cmp_w2_v[l], cmp_b2_v[l], conv_w[l], conv_b[l], lru_wa[l], lru_ba[l], lru_wi[l],
                   lru_bi[l], lru_lambda[l], w_o_attn[l], w_o_lru[l], w_out[l], norm2_g[l],
                   w_gate[l], w_up[l], w_down[l])
    return x
```

```python
import functools

import numpy as np
import jax
import jax.numpy as jnp
from jax import lax
from jax.experimental import pallas as pl
from jax.experimental.pallas import tpu as pltpu

N_HEADS = 16
HEAD_DIM = 64
N_KV = 4
HPG = N_HEADS // N_KV
CMP_BLOCK = 32
CMP_STRIDE = 16
CMP_HIDDEN = 256
SEL_BLOCK = 64
N_SEL = 16
WINDOW = 512
FORCE_BONUS = 1e4
LRU_BLOCKS = 16
CONV_W = 4
LRU_C = 8.0
EPS = 1e-6

MXU_DTYPE = jnp.bfloat16
F32 = jnp.float32

V7X_LANES = 128
V7X_SUBLANES = 8
V7X_MXU_DIM = 256
V7X_VMEM_LIMIT_BYTES = 48 * 1024 * 1024

TQ = 256
TK_SEL = 512
WIN_KEYS = WINDOW + TQ
V_ROWS = 80
LOG2E = 1.4426950408889634
FEAT = V7X_MXU_DIM
F_SEL = HEAD_DIM
F_POS = 2 * HEAD_DIM
F_CMP = F_POS + 6
NEG_MASK = -1e30
NEG_BLOCK = -(2.0 ** 100)


def _params(*sem):
    return pltpu.CompilerParams(dimension_semantics=sem, vmem_limit_bytes=V7X_VMEM_LIMIT_BYTES)


def _gelu_tanh(x):
    return 0.5 * x * (1.0 + jnp.tanh(0.7978845608028654 * (x + 0.044715 * (x * x * x))))


def _sigmoid(x):
    return 1.0 / (1.0 + jnp.exp(-x))


def _one_minus_sq(a, log_a):
    series = (-2.0 * log_a) * (1.0 + log_a * (1.0 + log_a * (2.0 / 3)))
    return jnp.where(log_a > -1.0 / 128, series, 1.0 - a * a)


def _nt_dot(wt, h):
    return lax.dot_general(wt, h, (((1,), (1,)), ((), ())), preferred_element_type=F32)


def _token_features(pos, col, with_block_mask):
    blk = pos >> 6
    off = pos & (SEL_BLOCK - 1)
    feat = jnp.where((col >= F_POS) & (col < F_POS + 3), blk.astype(F32),
                     jnp.where((col >= F_POS + 3) & (col < F_POS + 6), off.astype(F32), 0.0))
    if with_block_mask:
        feat = jnp.where((col >= F_SEL) & (col - F_SEL == blk) & (col < F_POS), NEG_BLOCK, feat)
    return feat


def _in_proj_kernel(x_ref, g1_ref, wq_ref, qg_ref, wv_ref, vb_ref, wg_ref, wk_ref, kg_ref, grp_ref,
                    place_ref, wc_ref, wl_ref, cw_ref, cb_ref, wa_ref, ba_ref, wi_ref, bi_ref, lam_ref,
                    h_ref, qt_ref, vt_ref, gt_ref, ks_ref, kw_ref, cs_ref, lru_ref, tail_ref, state_ref):
    x = x_ref[0]
    tm = x.shape[0]
    ms = jnp.mean(x * x, axis=-1, keepdims=True)
    h = (x * lax.rsqrt(ms + EPS) * g1_ref[...]).astype(MXU_DTYPE)
    h_ref[0] = h

    _lru_tile(h, wl_ref, cw_ref, cb_ref, wa_ref, ba_ref, wi_ref, bi_ref, lam_ref, lru_ref, tail_ref, state_ref)

    r3 = _nt_dot(wq_ref[...], h).reshape(N_HEADS, HEAD_DIM, tm)
    qn = r3 * lax.rsqrt(jnp.mean(r3 * r3, axis=1, keepdims=True) + EPS)
    qt_ref[0] = (qn.reshape(N_HEADS * HEAD_DIM, tm) * qg_ref[...]).astype(qt_ref.dtype)

    vt_ref[0] = (_nt_dot(wv_ref[...], h) + vb_ref[...]).astype(vt_ref.dtype)
    gt_ref[0] = _sigmoid(_nt_dot(wg_ref[...], h))

    k = jnp.dot(h, wk_ref[...], preferred_element_type=F32)
    kk = k * k
    kk_hi = kk.astype(MXU_DTYPE)
    kk_lo = (kk - kk_hi.astype(F32)).astype(MXU_DTYPE)
    pos = pl.program_id(1) * tm + lax.broadcasted_iota(jnp.int32, (tm, FEAT), 0)
    col = lax.broadcasted_iota(jnp.int32, (tm, FEAT), 1)
    kv_dim = N_KV * HEAD_DIM
    for branch, (o_ref, with_block_mask) in enumerate(((ks_ref, True), (kw_ref, False))):
        sl = slice(branch * kv_dim, (branch + 1) * kv_dim)
        msq = (jnp.dot(kk_hi[:, sl], grp_ref[...], preferred_element_type=F32)
               + jnp.dot(kk_lo[:, sl], grp_ref[...], preferred_element_type=F32))
        kn = (k[:, sl] * lax.rsqrt(msq + EPS) * kg_ref[branch:branch + 1, :]).astype(MXU_DTYPE)
        placed = jnp.dot(kn, place_ref[...], preferred_element_type=F32)
        feat = _token_features(pos, col, with_block_mask)
        for g in range(N_KV):
            o_ref[0, g] = (placed[:, g * FEAT:(g + 1) * FEAT] + feat).astype(o_ref.dtype)

    cs_ref[0] = jnp.dot(h, wc_ref[...], preferred_element_type=F32)


def _in_proj(x, attn_weights, lru_weights, tm=256):
    b, t, d = x.shape
    kv_dim = N_KV * HEAD_DIM
    wq_t, w_v_t, w_g_t = attn_weights[1], attn_weights[3], attn_weights[5]
    lru_w = lru_weights[-1].shape[1]
    full = lambda a: pl.BlockSpec(a.shape, lambda bi, i: (0,) * a.ndim)
    rowblk = lambda n: pl.BlockSpec((1, tm, n), lambda bi, i: (bi, i, 0))
    colblk = lambda n: pl.BlockSpec((1, n, tm), lambda bi, i: (bi, 0, i))
    kblk = pl.BlockSpec((1, N_KV, tm, FEAT), lambda bi, i: (bi, 0, i, 0))
    weights = tuple(attn_weights) + tuple(lru_weights)
    return pl.pallas_call(
        _in_proj_kernel,
        grid=(b, t // tm),
        in_specs=[rowblk(d)] + [full(a) for a in weights],
        out_specs=[rowblk(d), colblk(wq_t.shape[0]), colblk(w_v_t.shape[0]), colblk(w_g_t.shape[0]),
                   kblk, kblk, rowblk(2 * kv_dim), rowblk(lru_w)],
        out_shape=[jax.ShapeDtypeStruct((b, t, d), MXU_DTYPE),
                   jax.ShapeDtypeStruct((b, wq_t.shape[0], t), MXU_DTYPE),
                   jax.ShapeDtypeStruct((b, w_v_t.shape[0], t), MXU_DTYPE),
                   jax.ShapeDtypeStruct((b, w_g_t.shape[0], t), F32),
                   jax.ShapeDtypeStruct((b, N_KV, t, FEAT), MXU_DTYPE),
                   jax.ShapeDtypeStruct((b, N_KV, t, FEAT), MXU_DTYPE),
                   jax.ShapeDtypeStruct((b, t, 2 * kv_dim), F32),
                   jax.ShapeDtypeStruct((b, t, lru_w), MXU_DTYPE)],
        scratch_shapes=[pltpu.VMEM((V7X_SUBLANES, lru_w), F32),
                        pltpu.VMEM((V7X_SUBLANES, lru_w), F32)],
        compiler_params=_params("parallel", "arbitrary"),
        name="in_proj_lru",
    )(x, *weights)


CMP_GROUPS = V7X_LANES // HEAD_DIM


def _cmp_hidden(src_ref, pos_ref, w1_ref, b1_ref):
    ncp = src_ref.shape[1] // CMP_STRIDE
    first = jnp.zeros((ncp, CMP_GROUPS * CMP_HIDDEN), F32)
    second = jnp.zeros((ncp, CMP_GROUPS * CMP_HIDDEN), F32)
    for l in range(CMP_STRIDE):
        x = src_ref[0, pl.ds(l, ncp, stride=CMP_STRIDE), :]
        lo = (x + pos_ref[l:l + 1, :]).astype(MXU_DTYPE)
        hi = (x + pos_ref[CMP_STRIDE + l:CMP_STRIDE + l + 1, :]).astype(MXU_DTYPE)
        first = first + jnp.dot(lo, w1_ref[l], preferred_element_type=F32)
        second = second + jnp.dot(hi, w1_ref[CMP_STRIDE + l], preferred_element_type=F32)
    hid = first + pltpu.roll(second, ncp - 1, 0) + b1_ref[...]
    return _gelu_tanh(hid).astype(MXU_DTYPE)


def _cmp_k_kernel(src_ref, pos_ref, w1_ref, b1_ref, w2_ref, b2_ref, g_ref, o_ref):
    hid = _cmp_hidden(src_ref, pos_ref, w1_ref, b1_ref)
    ncp = hid.shape[0]
    idx = lax.broadcasted_iota(jnp.int32, (ncp, FEAT), 0)
    col = lax.broadcasted_iota(jnp.int32, (ncp, FEAT), 1)
    feat = jnp.where((col >= F_CMP) & (col < F_CMP + 3), (idx >> 6).astype(F32),
                     jnp.where((col >= F_CMP + 3) & (col < F_CMP + 6), (idx & 63).astype(F32), 0.0))
    for gl in range(CMP_GROUPS):
        r = jnp.dot(hid[:, gl * CMP_HIDDEN:(gl + 1) * CMP_HIDDEN], w2_ref[...],
                    preferred_element_type=F32) + b2_ref[...]
        ms = jnp.sum(r * r, axis=-1, keepdims=True) * (1.0 / HEAD_DIM)
        o_ref[0, gl] = (r * lax.rsqrt(ms + EPS) * g_ref[...] + feat).astype(o_ref.dtype)


def _cmp_v_kernel(src_ref, pos_ref, w1_ref, b1_ref, w2t_ref, b2_ref, o_ref):
    hid = _cmp_hidden(src_ref, pos_ref, w1_ref, b1_ref)
    for gl in range(CMP_GROUPS):
        r = _nt_dot(w2t_ref[...], hid[:, gl * CMP_HIDDEN:(gl + 1) * CMP_HIDDEN]) + b2_ref[...]
        o_ref[0, gl] = r.astype(o_ref.dtype)


def _compress(cmp_src, lane_block0, pos, w1, b1, w2, b2, gain_pad=None):
    b, t, _ = cmp_src.shape
    ncp = t // CMP_STRIDE
    hid_w = CMP_GROUPS * CMP_HIDDEN
    eye = jnp.eye(CMP_GROUPS, dtype=w1.dtype)
    w1_bd = jnp.einsum('ldf,gh->lgdhf', w1, eye).reshape(CMP_BLOCK, V7X_LANES, hid_w).astype(MXU_DTYPE)
    pos_t = jnp.tile(pos, (1, CMP_GROUPS))
    b1_t = jnp.tile(b1.reshape(1, CMP_HIDDEN), (1, CMP_GROUPS))
    full = lambda a: pl.BlockSpec(a.shape, lambda bi, p: (0,) * a.ndim)
    src_spec = pl.BlockSpec((1, t, V7X_LANES), lambda bi, p: (bi, 0, lane_block0 + p))
    grid = (b, N_KV // CMP_GROUPS)
    if gain_pad is not None:
        args = (pos_t, w1_bd, b1_t, w2, b2, gain_pad)
        return pl.pallas_call(
            _cmp_k_kernel,
            grid=grid,
            in_specs=[src_spec] + [full(a) for a in args],
            out_specs=pl.BlockSpec((1, CMP_GROUPS, ncp, FEAT), lambda bi, p: (bi, p, 0, 0)),
            out_shape=jax.ShapeDtypeStruct((b, N_KV, ncp, FEAT), MXU_DTYPE),
            compiler_params=_params("parallel", "parallel"),
            name="compress_k",
        )(cmp_src, *args)
    args = (pos_t, w1_bd, b1_t, w2, b2)
    return pl.pallas_call(
        _cmp_v_kernel,
        grid=grid,
        in_specs=[src_spec] + [full(a) for a in args],
        out_specs=pl.BlockSpec((1, CMP_GROUPS, V_ROWS, ncp), lambda bi, p: (bi, p, 0, 0)),
        out_shape=jax.ShapeDtypeStruct((b, N_KV, V_ROWS, ncp), MXU_DTYPE),
        compiler_params=_params("parallel", "parallel"),
        name="compress_v",
    )(cmp_src, *args)


def _split3(v):
    parts = []
    rest = np.asarray(v, np.float64)
    for _ in range(3):
        p = rest.astype(np.float32).astype(jnp.bfloat16).astype(np.float64)
        parts.append(p)
        rest = rest - p
    return parts


def _alibi_query_features():
    tab = np.zeros((N_KV, FEAT - F_POS, HPG * TQ), np.float64)
    for g in range(N_KV):
        for h in range(HPG):
            slope = 2.0 ** (-8.0 * (g * HPG + h + 1) / N_HEADS)
            parts = _split3(slope * LOG2E)
            lanes = slice(h * TQ, (h + 1) * TQ)
            for i, p in enumerate(parts):
                tab[g, i, lanes] = SEL_BLOCK * p
                tab[g, 3 + i, lanes] = p
                tab[g, 6 + i, lanes] = CMP_STRIDE * 64 * p
                tab[g, 9 + i, lanes] = CMP_STRIDE * p
    return jnp.asarray(tab, F32).astype(MXU_DTYPE)


def _block_map_t(n_cmp_pad, n_blk):
    cs = np.arange(n_cmp_pad) * CMP_STRIDE
    ce = cs + CMP_BLOCK - 1
    bs = np.arange(n_blk) * SEL_BLOCK
    be = bs + SEL_BLOCK - 1
    return jnp.asarray(((cs[None, :] <= be[:, None]) & (ce[None, :] >= bs[:, None])).astype(np.float32))


def _attn_tiles_kernel(*refs, tiles, win_blk0, n_blk, n_sel, has_prev):
    (qt_ref, gate_ref, kc_ref, vc_ref, ks_ref, vs_ref, kd_ref, vd_ref, kwa_ref, kwb_ref, vwa_ref, vwb_ref,
     alibi_ref, map_ref) = refs[:14]
    o_ref, qb_ref, qs_ref, imp_ref = refs[15:] if has_prev else refs[14:]
    lanes = HPG * TQ
    ncp = kc_ref.shape[2]
    blocks_per_tile = TQ // SEL_BLOCK
    win_tiles = WINDOW // TQ
    lane_tok = lax.broadcasted_iota(jnp.int32, (1, lanes), 1) & (TQ - 1)
    row_pos = lax.broadcasted_iota(jnp.int32, (TQ, lanes), 0)
    blk = lax.broadcasted_iota(jnp.int32, (n_blk, TQ), 0)

    def window_tile(qi, j):
        lt = max(qi - win_tiles, 0) + j - 2 * win_blk0
        kref, vref = (kwa_ref, vwa_ref) if lt < 2 else (kwb_ref, vwb_ref)
        sl = slice((lt % 2) * TQ, (lt % 2 + 1) * TQ)
        return kref[0, 0, sl, :], vref[0, :, sl]

    def head(ti, qi):
        cols = slice(ti * TQ, (ti + 1) * TQ)
        t_lane = qi * TQ + lane_tok
        for h in range(HPG):
            qb_ref[ti, 0:HEAD_DIM, h * TQ:(h + 1) * TQ] = qt_ref[0, h * HEAD_DIM:(h + 1) * HEAD_DIM, cols]
        qb_ref[ti, F_SEL:F_POS, :] = jnp.zeros((F_POS - F_SEL, lanes), qb_ref.dtype)
        qb_ref[ti, F_POS:FEAT, :] = alibi_ref[0]
        qb = qb_ref[ti]
        sc = jnp.dot(kc_ref[0, 0], qb, preferred_element_type=F32)
        sw = [jnp.dot(window_tile(qi, j)[0], qb, preferred_element_type=F32) for j in range(3)]
        sd = jnp.dot(kd_ref[0, 0, cols, :], qb, preferred_element_type=F32)
        last_cmp = (t_lane - (CMP_BLOCK - 1)) >> 4
        sc = jnp.where(lax.broadcasted_iota(jnp.int32, (ncp, lanes), 0) <= last_cmp, sc, NEG_MASK)
        ec = jnp.exp2(sc - jnp.max(sc, axis=0, keepdims=True))
        acc_c = jnp.dot(vc_ref[0, 0], ec.astype(MXU_DTYPE), preferred_element_type=F32)
        inv_c = jnp.where(last_cmp >= 0, 1.0 / jnp.maximum(acc_c[HEAD_DIM:HEAD_DIM + 1], 1e-30), 0.0)
        imp = None
        if (qi + 1) * blocks_per_tile > n_sel and qi > 0:
            psum = ec[:, 0:TQ] * inv_c[:, 0:TQ]
            for h in range(1, HPG):
                psum = psum + ec[:, h * TQ:(h + 1) * TQ] * inv_c[:, h * TQ:(h + 1) * TQ]
            imp = jnp.dot(map_ref[...], psum, preferred_element_type=F32)
        return dict(qb=qb, t_lane=t_lane, sw=sw, sd=sd, o_cmp=acc_c[0:HEAD_DIM] * inv_c, imp=imp)

    def body(ti, qi, st):
        cols = slice(ti * TQ, (ti + 1) * TQ)
        qb, t_lane, sw, sd = st["qb"], st["t_lane"], st["sw"], st["sd"]
        q0 = qi * TQ
        n_chunks = -(-q0 // TK_SEL)
        first_own_blk = qi * blocks_per_tile

        rank = jnp.zeros((n_blk, TQ), jnp.int32)
        if st["imp"] is not None:
            cur = (q0 + lax.broadcasted_iota(jnp.int32, (n_blk, TQ), 1)) >> 6
            forced = (blk == 0) | (blk == cur) | (blk == cur - 1)
            imp = jnp.where(blk <= cur, st["imp"] + jnp.where(forced, FORCE_BONUS, 0.0), NEG_MASK)
            imp_ref[ti] = imp
            n_rank_chunks = -(-first_own_blk // V7X_SUBLANES)
            chunks = [imp[c * V7X_SUBLANES:(c + 1) * V7X_SUBLANES] for c in range(n_rank_chunks)]
            ranks = [jnp.zeros((V7X_SUBLANES, TQ), jnp.int32) for _ in range(n_rank_chunks)]
            sub = lax.broadcasted_iota(jnp.int32, (V7X_SUBLANES, TQ), 0)
            for k in range(min(first_own_blk + blocks_per_tile, n_blk)):
                row = imp_ref[ti, k:k + 1, :]
                for c in range(n_rank_chunks):
                    lo = c * V7X_SUBLANES
                    if lo > k:
                        one = jnp.where(row >= chunks[c], 1, 0)
                    elif lo + V7X_SUBLANES - 1 <= k:
                        one = jnp.where(row > chunks[c], 1, 0)
                    else:
                        one = jnp.where(sub + lo > k, jnp.where(row >= chunks[c], 1, 0),
                                        jnp.where(row > chunks[c], 1, 0))
                    ranks[c] = ranks[c] + one
            pad = [jnp.zeros(((n_blk // V7X_SUBLANES - n_rank_chunks) * V7X_SUBLANES, TQ), jnp.int32)]
            rank = jnp.concatenate(ranks + (pad if pad[0].shape[0] else []), axis=0)

        m_s = jnp.full((1, lanes), NEG_MASK, F32)
        acc_s = jnp.zeros((V_ROWS, lanes), F32)
        if n_chunks > 0:
            not_sel = jnp.where((rank < n_sel) & (blk < first_own_blk), 0.0, 1.0).astype(qs_ref.dtype)
            qs_ref[ti] = qb
            for h in range(HPG):
                qs_ref[ti, F_SEL:F_SEL + n_blk, h * TQ:(h + 1) * TQ] = not_sel
            qs = qs_ref[ti]
            qk = lambda k: jnp.dot(ks_ref[0, 0, k * TK_SEL:(k + 1) * TK_SEL, :], qs, preferred_element_type=F32)
            pv = lambda k, p: jnp.dot(vs_ref[0, :, k * TK_SEL:(k + 1) * TK_SEL], p, preferred_element_type=F32)
            s_next = qk(0)
            p_prev = None
            for k in range(n_chunks):
                s_cur = s_next
                if k + 1 < n_chunks:
                    s_next = qk(k + 1)
                if p_prev is not None:
                    acc_s = acc_s + pv(k - 1, p_prev)
                m_new = jnp.maximum(m_s, jnp.max(s_cur, axis=0, keepdims=True))
                p_prev = jnp.exp2(s_cur - m_new).astype(MXU_DTYPE)
                acc_s = jnp.exp2(m_s - m_new) * acc_s
                m_s = m_new
            acc_s = acc_s + pv(n_chunks - 1, p_prev)

        causal = q0 + row_pos <= t_lane
        sd = jnp.where(causal, sd, NEG_MASK)
        m_d = jnp.max(sd, axis=0, keepdims=True)
        acc_d = jnp.dot(vd_ref[0, :, cols], jnp.exp2(sd - m_d).astype(MXU_DTYPE), preferred_element_type=F32)
        m_all = jnp.maximum(m_s, m_d)
        acc_s = jnp.exp2(m_s - m_all) * acc_s + jnp.exp2(m_d - m_all) * acc_d
        o_sel = acc_s[0:HEAD_DIM] * (1.0 / acc_s[HEAD_DIM:HEAD_DIM + 1])

        wb = max(qi - win_tiles, 0)
        if qi >= win_tiles:
            d0 = (t_lane - wb * TQ) - row_pos
            sw = [jnp.where(d0 < WINDOW, sw[0], NEG_MASK), sw[1], jnp.where(causal, sw[2], NEG_MASK)]
        else:
            for j in range(3):
                dj = (t_lane - (wb + j) * TQ) - row_pos
                sw[j] = jnp.where(lax.bitcast_convert_type(dj, jnp.uint32) < WINDOW, sw[j], NEG_MASK)
        m_w = jnp.max(jnp.maximum(jnp.maximum(sw[0], sw[1]), sw[2]), axis=0, keepdims=True)
        acc_w = jnp.zeros((V_ROWS, lanes), F32)
        for j in range(3):
            acc_w = acc_w + jnp.dot(window_tile(qi, j)[1], jnp.exp2(sw[j] - m_w).astype(MXU_DTYPE),
                                    preferred_element_type=F32)
        o_win = acc_w[0:HEAD_DIM] * (1.0 / acc_w[HEAD_DIM:HEAD_DIM + 1])

        gates = gate_ref[0, :, cols]
        def gate_row(j):
            return jnp.concatenate([gates[j * HPG + h:j * HPG + h + 1, :] for h in range(HPG)], axis=1)
        o_t = gate_row(0) * st["o_cmp"] + gate_row(1) * o_sel + gate_row(2) * o_win
        for hp in range(HPG // 2):
            pair = jnp.concatenate([o_t[:, (2 * hp) * TQ:(2 * hp + 1) * TQ],
                                    o_t[:, (2 * hp + 1) * TQ:(2 * hp + 2) * TQ]], axis=0)
            o_ref[0, cols, hp * 2 * HEAD_DIM:(hp + 1) * 2 * HEAD_DIM] = pair.T.astype(o_ref.dtype)

    states = [head(ti, qi) for ti, qi in enumerate(tiles)]
    for ti, qi in enumerate(tiles):
        body(ti, qi, states[ti])


def _attention_static(q_t, gates_t, k_cmp, v_cmp_t, k_sel, k_win, v_t):
    b, _, t = q_t.shape
    n_blk = t // SEL_BLOCK
    tiles_per_call = 2
    tb = tiles_per_call * TQ
    assert tb == TK_SEL and t % tb == 0 and t >= 2 * tb and F_SEL + n_blk <= F_POS, "unsupported sequence length"
    n_sel = min(N_SEL, n_blk)
    lanes = HPG * TQ
    rows = HPG * HEAD_DIM
    alibi = _alibi_query_features()
    out_shape = jax.ShapeDtypeStruct((b, t, N_HEADS * HEAD_DIM), MXU_DTYPE)

    attn = None
    for m in range(t // tb):
        tiles = tuple(range(m * tiles_per_call, (m + 1) * tiles_per_call))
        ncp = min(t // CMP_STRIDE, -(-((m + 1) * tb // CMP_STRIDE) // V7X_LANES) * V7X_LANES)
        blk_map_t = _block_map_t(ncp, n_blk)
        kc = (m + 1) * TK_SEL
        wb0 = max(m - 1, 0)
        in_specs = [
            pl.BlockSpec((1, rows, tb), lambda bi, g, m=m: (bi, g, m)),
            pl.BlockSpec((1, 16, tb), lambda bi, g, m=m: (bi, g, m)),
            pl.BlockSpec((1, 1, ncp, FEAT), lambda bi, g: (bi, g, 0, 0)),
            pl.BlockSpec((1, 1, V_ROWS, ncp), lambda bi, g: (bi, g, 0, 0)),
            pl.BlockSpec((1, 1, kc, FEAT), lambda bi, g: (bi, g, 0, 0)),
            pl.BlockSpec((1, V_ROWS, kc), lambda bi, g: (bi, g, 0)),
            pl.BlockSpec((1, 1, tb, FEAT), lambda bi, g, m=m: (bi, g, m, 0)),
            pl.BlockSpec((1, V_ROWS, tb), lambda bi, g, m=m: (bi, g, m)),
            pl.BlockSpec((1, 1, tb, FEAT), lambda bi, g, w=wb0: (bi, g, w, 0)),
            pl.BlockSpec((1, 1, tb, FEAT), lambda bi, g, w=wb0: (bi, g, w + 1, 0)),
            pl.BlockSpec((1, V_ROWS, tb), lambda bi, g, w=wb0: (bi, N_KV + g, w)),
            pl.BlockSpec((1, V_ROWS, tb), lambda bi, g, w=wb0: (bi, N_KV + g, w + 1)),
            pl.BlockSpec((1, FEAT - F_POS, lanes), lambda bi, g: (g, 0, 0)),
            pl.BlockSpec((n_blk, ncp), lambda bi, g: (0, 0)),
        ]
        args = [q_t, gates_t, k_cmp, v_cmp_t, k_sel, v_t, k_sel, v_t, k_win, k_win, v_t, v_t, alibi, blk_map_t]
        aliases = {}
        if attn is not None:
            in_specs.append(pl.BlockSpec(memory_space=pl.ANY))
            args.append(attn)
            aliases = {len(args) - 1: 0}
        kernel = functools.partial(_attn_tiles_kernel, tiles=tiles, win_blk0=wb0, n_blk=n_blk, n_sel=n_sel,
                                   has_prev=attn is not None)
        attn = pl.pallas_call(
            kernel,
            grid=(b, N_KV),
            in_specs=in_specs,
            out_specs=pl.BlockSpec((1, tb, rows), lambda bi, g, m=m: (bi, m, g)),
            out_shape=out_shape,
            scratch_shapes=[pltpu.VMEM((tiles_per_call, FEAT, lanes), MXU_DTYPE),
                            pltpu.VMEM((tiles_per_call, FEAT, lanes), MXU_DTYPE),
                            pltpu.VMEM((tiles_per_call, n_blk, TQ), F32)],
            input_output_aliases=aliases,
            compiler_params=_params("parallel", "parallel"),
            name=f"nsa_attention_{m}",
        )(*args)
    return attn


def _lru_tile(hin, wl_ref, cw_ref, cb_ref, wa_ref, ba_ref, wi_ref, bi_ref, lam_ref, o_ref, tail_ref, h_ref):
    tt = hin.shape[0]
    w = lam_ref.shape[1]

    @pl.when(pl.program_id(1) == 0)
    def _():
        tail_ref[...] = jnp.zeros_like(tail_ref)
        h_ref[...] = jnp.zeros_like(h_ref)

    def gate_matmul(xb, w_ref):
        n = w_ref.shape[1]
        return jnp.concatenate([jnp.dot(xb[:, j * n:(j + 1) * n], w_ref[j], preferred_element_type=F32)
                                for j in range(w_ref.shape[0])], axis=1)

    x = jnp.dot(hin, wl_ref[:, 0:w], preferred_element_type=F32)
    gate = jnp.dot(hin, wl_ref[:, w:2 * w], preferred_element_type=F32)
    ng = tt // V7X_SUBLANES
    sub = lax.broadcasted_iota(jnp.int32, (ng, V7X_SUBLANES, w), 1)
    x3 = x.reshape(ng, V7X_SUBLANES, w)
    xprev3 = jnp.concatenate([tail_ref[...][None], x3], axis=0)
    tail_ref[...] = x[tt - V7X_SUBLANES:tt]
    xc = x * cw_ref[CONV_W - 1:CONV_W, :] + cb_ref[...]
    for s in range(1, CONV_W):
        rot = pltpu.roll(xprev3, s, 1)
        xs = jnp.where(sub >= s, rot[1:], rot[:-1])
        xc = xc + xs.reshape(tt, w) * cw_ref[CONV_W - 1 - s:CONV_W - s, :]

    xb = xc.astype(MXU_DTYPE)
    r = _sigmoid(gate_matmul(xb, wa_ref) + ba_ref[...])
    i = _sigmoid(gate_matmul(xb, wi_ref) + bi_ref[...])
    z = -lam_ref[...]
    softplus = jnp.maximum(z, 0.0) + jnp.log1p(jnp.exp(-jnp.abs(z)))
    log_a = -LRU_C * r * softplus
    a = jnp.exp(log_a)
    bb = jnp.sqrt(_one_minus_sq(a, log_a)) * (i * xc)

    a3 = a.reshape(ng, V7X_SUBLANES, w)
    b3 = bb.reshape(ng, V7X_SUBLANES, w)
    for d in (1, 2, 4):
        ok = sub >= d
        a_sh = pltpu.roll(a3, d, 1)
        b_sh = pltpu.roll(b3, d, 1)
        b3 = jnp.where(ok, a3 * b_sh + b3, b3)
        a3 = jnp.where(ok, a3 * a_sh, a3)
    carry = h_ref[0:1, :]
    groups = []
    for g in range(ng):
        hg = b3[g] + a3[g] * carry
        groups.append(hg)
        carry = hg[V7X_SUBLANES - 1:V7X_SUBLANES, :]
    hcur = jnp.concatenate(groups, axis=0)
    h_ref[...] = jnp.broadcast_to(carry, h_ref.shape)
    o_ref[0] = (hcur * _gelu_tanh(gate)).astype(o_ref.dtype)


def _merge_kernel(attn_ref, lru_ref, h_ref, x_ref, wm_ref, wa_ref, wl_ref, wo_ref, g2_ref,
                  x1_ref, h2_ref):
    d = x_ref.shape[1]
    h = h_ref[...]
    mg0 = _sigmoid(jnp.dot(h, wm_ref[:, 0:d], preferred_element_type=F32))
    mg1 = _sigmoid(jnp.dot(h, wm_ref[:, d:2 * d], preferred_element_type=F32))
    ya = jnp.dot(attn_ref[...], wa_ref[...], preferred_element_type=F32)
    yl = jnp.dot(lru_ref[...], wl_ref[...], preferred_element_type=F32)
    merged = mg0 * ya + mg1 * yl
    x1 = x_ref[...] + jnp.dot(merged.astype(MXU_DTYPE), wo_ref[...], preferred_element_type=F32)
    x1_ref[...] = x1
    ms = jnp.mean(x1 * x1, axis=-1, keepdims=True)
    h2_ref[...] = (x1 * lax.rsqrt(ms + EPS) * g2_ref[...]).astype(h2_ref.dtype)


def _merge(attn, lru, h2d, x2d, w_mg, wa, wl, wo, g2, tm=256):
    m, d = x2d.shape
    row = pl.BlockSpec((tm, d), lambda i: (i, 0))
    full = lambda a: pl.BlockSpec(a.shape, lambda i: (0, 0))
    g2 = g2.reshape(1, d)
    return pl.pallas_call(
        _merge_kernel,
        grid=(m // tm,),
        in_specs=[row, row, row, row, full(w_mg), full(wa), full(wl), full(wo), full(g2)],
        out_specs=[row, row],
        out_shape=[jax.ShapeDtypeStruct((m, d), F32), jax.ShapeDtypeStruct((m, d), MXU_DTYPE)],
        compiler_params=_params("parallel"),
        name="merge_out",
    )(attn, lru, h2d, x2d, w_mg, wa, wl, wo, g2)


def _ffn_kernel(h_ref, x1_ref, wg_ref, wu_ref, wd_ref, o_ref):
    @pl.when(pl.program_id(1) == 0)
    def _():
        o_ref[...] = x1_ref[...]

    h = h_ref[...]
    g = jnp.dot(h, wg_ref[...], preferred_element_type=F32)
    u = jnp.dot(h, wu_ref[...], preferred_element_type=F32)
    act = (g * _sigmoid(g) * u).astype(MXU_DTYPE)
    o_ref[...] += jnp.dot(act, wd_ref[...], preferred_element_type=F32)


def _ffn(h2, x1, wg, wu, wd, tm=1024, tf=256):
    m, d = x1.shape
    f = wg.shape[1]
    return pl.pallas_call(
        _ffn_kernel,
        grid=(m // tm, f // tf),
        in_specs=[pl.BlockSpec((tm, d), lambda i, j: (i, 0)),
                  pl.BlockSpec((tm, d), lambda i, j: (i, 0)),
                  pl.BlockSpec((d, tf), lambda i, j: (0, j)),
                  pl.BlockSpec((d, tf), lambda i, j: (0, j)),
                  pl.BlockSpec((tf, d), lambda i, j: (j, 0))],
        out_specs=pl.BlockSpec((tm, d), lambda i, j: (i, 0)),
        out_shape=jax.ShapeDtypeStruct((m, d), F32),
        compiler_params=_params("parallel", "arbitrary"),
        name="swiglu_ffn",
    )(h2, x1, wg, wu, wd)


def _pad_last(a, n):
    return jnp.pad(a, [(0, 0)] * (a.ndim - 1) + [(0, n - a.shape[-1])])


def _layer(x, norm1_g, w_in, q_norm_g, k_norm_g, cmp_pos_k, cmp_w1_k, cmp_b1_k, cmp_w2_k, cmp_b2_k,
           cmp_pos_v, cmp_w1_v, cmp_b1_v, cmp_w2_v, cmp_b2_v, conv_w, conv_b, lru_wa, lru_ba,
           lru_wi, lru_bi, lru_lambda, w_o_attn, w_o_lru, w_out, norm2_g, w_gate, w_up, w_down):
    b, t, d = x.shape
    m = b * t
    attn_dim = N_HEADS * HEAD_DIM
    kv_dim = N_KV * HEAD_DIM
    lru_w = lru_lambda.shape[0]
    assert t % 512 == 0
    o1 = attn_dim
    o2 = o1 + 6 * kv_dim
    o3 = o2 + 3 * N_HEADS
    o4 = o3 + lru_w
    o5 = o4 + lru_w
    cast = lambda a: a.astype(MXU_DTYPE)

    wq_t = cast(w_in[:, :o1].T)
    w_kv = w_in[:, o1:o2].reshape(d, 6, N_KV, HEAD_DIM)
    w_cmp_src = cast(w_kv[:, 0:2].reshape(d, 2 * kv_dim))
    w_k2 = cast(jnp.stack([w_kv[:, 2], w_kv[:, 4]], axis=1).reshape(d, 2 * kv_dim))
    k_gain2 = jnp.stack([jnp.tile(k_norm_g[1], N_KV), jnp.tile(k_norm_g[2], N_KV)])
    lane_grp = np.arange(kv_dim) // HEAD_DIM
    grp_avg = cast(jnp.asarray((lane_grp[:, None] == lane_grp[None, :]) / HEAD_DIM, F32))
    place_np = np.zeros((kv_dim, N_KV * FEAT), np.float32)
    place_np[np.arange(kv_dim), lane_grp * FEAT + np.arange(kv_dim) % HEAD_DIM] = 1.0
    place = cast(jnp.asarray(place_np))
    w_v = jnp.stack([w_kv[:, 3], w_kv[:, 5]], axis=1)
    w_v_t = cast(_pad_last(w_v, V_ROWS).reshape(d, 2 * N_KV * V_ROWS).T)
    v_ones_col = jnp.tile(jnp.arange(V_ROWS) == HEAD_DIM, 2 * N_KV).astype(F32).reshape(-1, 1)
    w_g = w_in[:, o2:o3].reshape(d, N_KV, HPG, 3).transpose(0, 1, 3, 2).reshape(d, N_KV, 3 * HPG)
    w_g_t = cast(_pad_last(w_g, 16).reshape(d, N_KV * 16).T)
    w_lru = cast(w_in[:, o3:o5])
    w_mg = cast(w_in[:, o5:])
    q_gain_col = jnp.tile(q_norm_g * (HEAD_DIM ** -0.5 * LOG2E), N_HEADS).reshape(attn_dim, 1)
    gain_pad = lambda g: _pad_last(g.reshape(1, HEAD_DIM), FEAT)

    per_slab = V7X_MXU_DIM // (lru_w // LRU_BLOCKS)
    eye = jnp.eye(per_slab, dtype=F32)
    slabs = lambda wgt: cast(jnp.einsum('snkj,nm->snkmj', wgt.reshape(LRU_BLOCKS // per_slab, per_slab,
                                                                       *wgt.shape[1:]), eye)
                             .reshape(LRU_BLOCKS // per_slab, V7X_MXU_DIM, V7X_MXU_DIM))
    vec = lambda v: v.reshape(1, lru_w)

    h3d, q_t, v_t, gates_t, k_sel, k_win, cmp_src, lru = _in_proj(
        x,
        (norm1_g.reshape(1, d), wq_t, q_gain_col, w_v_t, v_ones_col, w_g_t, w_k2, k_gain2, grp_avg, place,
         w_cmp_src),
        (w_lru, conv_w.reshape(CONV_W, lru_w), vec(conv_b), slabs(lru_wa), vec(lru_ba), slabs(lru_wi),
         vec(lru_bi), vec(lru_lambda)))
    h2d = h3d.reshape(m, d)

    k_cmp = _compress(cmp_src, 0, cmp_pos_k, cmp_w1_k, cmp_b1_k, cast(_pad_last(cmp_w2_k, FEAT)),
                      _pad_last(cmp_b2_k.reshape(1, HEAD_DIM), FEAT), gain_pad(k_norm_g[0]))
    v_cmp_t = _compress(cmp_src, N_KV // CMP_GROUPS, cmp_pos_v, cmp_w1_v, cmp_b1_v,
                        cast(_pad_last(cmp_w2_v, V_ROWS).T),
                        jnp.concatenate([cmp_b2_v, v_ones_col[HEAD_DIM:V_ROWS, 0]]).reshape(V_ROWS, 1))

    attn = _attention_static(q_t, gates_t, k_cmp, v_cmp_t, k_sel, k_win, v_t)

    x1, h2 = _merge(attn.reshape(m, attn_dim), lru.reshape(m, lru_w), h2d, x.reshape(m, d), w_mg,
                    cast(w_o_attn), cast(w_o_lru), cast(w_out), norm2_g)
    out = _ffn(h2, x1, cast(w_gate), cast(w_up), cast(w_down))
    return out.reshape(b, t, d)


def kernel(x, norm1_g, w_in, q_norm_g, k_norm_g, cmp_pos_k, cmp_w1_k, cmp_b1_k, cmp_w2_k, cmp_b2_k,
           cmp_pos_v, cmp_w1_v, cmp_b1_v, cmp_w2_v, cmp_b2_v, conv_w, conv_b, lru_wa, lru_ba,
           lru_wi, lru_bi, lru_lambda, w_o_attn, w_o_lru, w_out, norm2_g, w_gate, w_up, w_down):
    for l in range(norm1_g.shape[0]):
        x = _layer(x, norm1_g[l], w_in[l], q_norm_g[l], k_norm_g[l], cmp_pos_k[l], cmp_w1_k[l],
                   cmp_b1_k[l], cmp_w2_k[l], cmp_b2_k[l], cmp_pos_v[l], cmp_w1_v[l], cmp_b1_v[l],
                   cmp_w2_v[l], cmp_b2_v[l], conv_w[l], conv_b[l], lru_wa[l], lru_ba[l], lru_wi[l],
                   lru_bi[l], lru_lambda[l], w_o_attn[l], w_o_lru[l], w_out[l], norm2_g[l],
                   w_gate[l], w_up[l], w_down[l])
    return x
```

```python
import functools

import numpy as np
import jax
import jax.numpy as jnp
from jax import lax
from jax.experimental import pallas as pl
from jax.experimental.pallas import tpu as pltpu

N_HEADS = 16
HEAD_DIM = 64
N_KV = 4
HPG = N_HEADS // N_KV
CMP_BLOCK = 32
CMP_STRIDE = 16
CMP_HIDDEN = 256
SEL_BLOCK = 64
N_SEL = 16
WINDOW = 512
FORCE_BONUS = 1e4
LRU_BLOCKS = 16
CONV_W = 4
LRU_C = 8.0
EPS = 1e-6

MXU_DTYPE = jnp.bfloat16
F32 = jnp.float32

V7X_LANES = 128
V7X_SUBLANES = 8
V7X_MXU_DIM = 256
V7X_VMEM_LIMIT_BYTES = 48 * 1024 * 1024

TQ = 256
TK_SEL = 512
WIN_KEYS = WINDOW + TQ
V_ROWS = 80
LOG2E = 1.4426950408889634
FEAT = V7X_MXU_DIM
F_SEL = HEAD_DIM
F_POS = 2 * HEAD_DIM
F_CMP = F_POS + 6
NEG_MASK = -1e30
NEG_BLOCK = -(2.0 ** 100)


def _params(*sem):
    return pltpu.CompilerParams(dimension_semantics=sem, vmem_limit_bytes=V7X_VMEM_LIMIT_BYTES)


def _gelu_tanh(x):
    return 0.5 * x * (1.0 + jnp.tanh(0.7978845608028654 * (x + 0.044715 * (x * x * x))))


def _sigmoid(x):
    return 1.0 / (1.0 + jnp.exp(-x))


def _one_minus_sq(a, log_a):
    series = (-2.0 * log_a) * (1.0 + log_a * (1.0 + log_a * (2.0 / 3)))
    return jnp.where(log_a > -1.0 / 128, series, 1.0 - a * a)


def _nt_dot(wt, h):
    return lax.dot_general(wt, h, (((1,), (1,)), ((), ())), preferred_element_type=F32)


def _token_features(pos, col, with_block_mask):
    blk = pos >> 6
    off = pos & (SEL_BLOCK - 1)
    feat = jnp.where((col >= F_POS) & (col < F_POS + 3), blk.astype(F32),
                     jnp.where((col >= F_POS + 3) & (col < F_POS + 6), off.astype(F32), 0.0))
    if with_block_mask:
        feat = jnp.where((col >= F_SEL) & (col - F_SEL == blk) & (col < F_POS), NEG_BLOCK, feat)
    return feat


def _in_proj_kernel(x_ref, g1_ref, wq_ref, qg_ref, wv_ref, vb_ref, wg_ref, wk_ref, kg_ref, grp_ref,
                    place_ref, wc_ref, wl_ref, cw_ref, cb_ref, wa_ref, ba_ref, wi_ref, bi_ref, lam_ref,
                    h_ref, qt_ref, vt_ref, gt_ref, ks_ref, kw_ref, cs_ref, lru_ref, tail_ref, state_ref):
    x = x_ref[0]
    tm = x.shape[0]
    ms = jnp.mean(x * x, axis=-1, keepdims=True)
    h = (x * lax.rsqrt(ms + EPS) * g1_ref[...]).astype(MXU_DTYPE)
    h_ref[0] = h

    lru_mid = _lru_gates(h, wl_ref, cw_ref, cb_ref, wa_ref, wi_ref, tail_ref, state_ref)

    r3 = _nt_dot(wq_ref[...], h).reshape(N_HEADS, HEAD_DIM, tm)
    qn = r3 * lax.rsqrt(jnp.mean(r3 * r3, axis=1, keepdims=True) + EPS)
    qt_ref[0] = (qn.reshape(N_HEADS * HEAD_DIM, tm) * qg_ref[...]).astype(qt_ref.dtype)

    vt_ref[0] = (_nt_dot(wv_ref[...], h) + vb_ref[...]).astype(vt_ref.dtype)
    gt_ref[0] = _sigmoid(_nt_dot(wg_ref[...], h))

    k = jnp.dot(h, wk_ref[...], preferred_element_type=F32)
    kk = k * k
    kk_hi = kk.astype(MXU_DTYPE)
    kk_lo = (kk - kk_hi.astype(F32)).astype(MXU_DTYPE)
    pos = pl.program_id(1) * tm + lax.broadcasted_iota(jnp.int32, (tm, FEAT), 0)
    col = lax.broadcasted_iota(jnp.int32, (tm, FEAT), 1)
    kv_dim = N_KV * HEAD_DIM
    for branch, (o_ref, with_block_mask) in enumerate(((ks_ref, True), (kw_ref, False))):
        sl = slice(branch * kv_dim, (branch + 1) * kv_dim)
        msq = (jnp.dot(kk_hi[:, sl], grp_ref[...], preferred_element_type=F32)
               + jnp.dot(kk_lo[:, sl], grp_ref[...], preferred_element_type=F32))
        kn = (k[:, sl] * lax.rsqrt(msq + EPS) * kg_ref[branch:branch + 1, :]).astype(MXU_DTYPE)
        placed = jnp.dot(kn, place_ref[...], preferred_element_type=F32)
        feat = _token_features(pos, col, with_block_mask)
        for g in range(N_KV):
            o_ref[0, g] = (placed[:, g * FEAT:(g + 1) * FEAT] + feat).astype(o_ref.dtype)

    cs_ref[0] = jnp.dot(h, wc_ref[...], preferred_element_type=F32)

    _lru_scan(*lru_mid, ba_ref, bi_ref, lam_ref, lru_ref, state_ref)


def _in_proj(x, attn_weights, lru_weights, tm=256):
    b, t, d = x.shape
    kv_dim = N_KV * HEAD_DIM
    wq_t, w_v_t, w_g_t = attn_weights[1], attn_weights[3], attn_weights[5]
    lru_w = lru_weights[-1].shape[1]
    full = lambda a: pl.BlockSpec(a.shape, lambda bi, i: (0,) * a.ndim)
    rowblk = lambda n: pl.BlockSpec((1, tm, n), lambda bi, i: (bi, i, 0))
    colblk = lambda n: pl.BlockSpec((1, n, tm), lambda bi, i: (bi, 0, i))
    kblk = pl.BlockSpec((1, N_KV, tm, FEAT), lambda bi, i: (bi, 0, i, 0))
    weights = tuple(attn_weights) + tuple(lru_weights)
    return pl.pallas_call(
        _in_proj_kernel,
        grid=(b, t // tm),
        in_specs=[rowblk(d)] + [full(a) for a in weights],
        out_specs=[rowblk(d), colblk(wq_t.shape[0]), colblk(w_v_t.shape[0]), colblk(w_g_t.shape[0]),
                   kblk, kblk, rowblk(2 * kv_dim), rowblk(lru_w)],
        out_shape=[jax.ShapeDtypeStruct((b, t, d), MXU_DTYPE),
                   jax.ShapeDtypeStruct((b, wq_t.shape[0], t), MXU_DTYPE),
                   jax.ShapeDtypeStruct((b, w_v_t.shape[0], t), MXU_DTYPE),
                   jax.ShapeDtypeStruct((b, w_g_t.shape[0], t), F32),
                   jax.ShapeDtypeStruct((b, N_KV, t, FEAT), MXU_DTYPE),
                   jax.ShapeDtypeStruct((b, N_KV, t, FEAT), MXU_DTYPE),
                   jax.ShapeDtypeStruct((b, t, 2 * kv_dim), F32),
                   jax.ShapeDtypeStruct((b, t, lru_w), MXU_DTYPE)],
        scratch_shapes=[pltpu.VMEM((V7X_SUBLANES, lru_w), F32),
                        pltpu.VMEM((V7X_SUBLANES, lru_w), F32)],
        compiler_params=_params("parallel", "arbitrary"),
        name="in_proj_lru",
    )(x, *weights)


CMP_GROUPS = V7X_LANES // HEAD_DIM


def _cmp_hidden(src_ref, pos_ref, w1_ref, b1_ref):
    ncp = src_ref.shape[1] // CMP_STRIDE
    first = jnp.zeros((ncp, CMP_GROUPS * CMP_HIDDEN), F32)
    second = jnp.zeros((ncp, CMP_GROUPS * CMP_HIDDEN), F32)
    for l in range(CMP_STRIDE):
        x = src_ref[0, pl.ds(l, ncp, stride=CMP_STRIDE), :]
        lo = (x + pos_ref[l:l + 1, :]).astype(MXU_DTYPE)
        hi = (x + pos_ref[CMP_STRIDE + l:CMP_STRIDE + l + 1, :]).astype(MXU_DTYPE)
        first = first + jnp.dot(lo, w1_ref[l], preferred_element_type=F32)
        second = second + jnp.dot(hi, w1_ref[CMP_STRIDE + l], preferred_element_type=F32)
    hid = first + pltpu.roll(second, ncp - 1, 0) + b1_ref[...]
    return _gelu_tanh(hid).astype(MXU_DTYPE)


def _cmp_k_kernel(src_ref, pos_ref, w1_ref, b1_ref, w2_ref, b2_ref, g_ref, o_ref):
    hid = _cmp_hidden(src_ref, pos_ref, w1_ref, b1_ref)
    ncp = hid.shape[0]
    idx = lax.broadcasted_iota(jnp.int32, (ncp, FEAT), 0)
    col = lax.broadcasted_iota(jnp.int32, (ncp, FEAT), 1)
    feat = jnp.where((col >= F_CMP) & (col < F_CMP + 3), (idx >> 6).astype(F32),
                     jnp.where((col >= F_CMP + 3) & (col < F_CMP + 6), (idx & 63).astype(F32), 0.0))
    for gl in range(CMP_GROUPS):
        r = jnp.dot(hid[:, gl * CMP_HIDDEN:(gl + 1) * CMP_HIDDEN], w2_ref[...],
                    preferred_element_type=F32) + b2_ref[...]
        ms = jnp.sum(r * r, axis=-1, keepdims=True) * (1.0 / HEAD_DIM)
        o_ref[0, gl] = (r * lax.rsqrt(ms + EPS) * g_ref[...] + feat).astype(o_ref.dtype)


def _cmp_v_kernel(src_ref, pos_ref, w1_ref, b1_ref, w2t_ref, b2_ref, o_ref):
    hid = _cmp_hidden(src_ref, pos_ref, w1_ref, b1_ref)
    for gl in range(CMP_GROUPS):
        r = _nt_dot(w2t_ref[...], hid[:, gl * CMP_HIDDEN:(gl + 1) * CMP_HIDDEN]) + b2_ref[...]
        o_ref[0, gl] = r.astype(o_ref.dtype)


def _compress(cmp_src, lane_block0, pos, w1, b1, w2, b2, gain_pad=None):
    b, t, _ = cmp_src.shape
    ncp = t // CMP_STRIDE
    hid_w = CMP_GROUPS * CMP_HIDDEN
    eye = jnp.eye(CMP_GROUPS, dtype=w1.dtype)
    w1_bd = jnp.einsum('ldf,gh->lgdhf', w1, eye).reshape(CMP_BLOCK, V7X_LANES, hid_w).astype(MXU_DTYPE)
    pos_t = jnp.tile(pos, (1, CMP_GROUPS))
    b1_t = jnp.tile(b1.reshape(1, CMP_HIDDEN), (1, CMP_GROUPS))
    full = lambda a: pl.BlockSpec(a.shape, lambda bi, p: (0,) * a.ndim)
    src_spec = pl.BlockSpec((1, t, V7X_LANES), lambda bi, p: (bi, 0, lane_block0 + p))
    grid = (b, N_KV // CMP_GROUPS)
    if gain_pad is not None:
        args = (pos_t, w1_bd, b1_t, w2, b2, gain_pad)
        return pl.pallas_call(
            _cmp_k_kernel,
            grid=grid,
            in_specs=[src_spec] + [full(a) for a in args],
            out_specs=pl.BlockSpec((1, CMP_GROUPS, ncp, FEAT), lambda bi, p: (bi, p, 0, 0)),
            out_shape=jax.ShapeDtypeStruct((b, N_KV, ncp, FEAT), MXU_DTYPE),
            compiler_params=_params("parallel", "parallel"),
            name="compress_k",
        )(cmp_src, *args)
    args = (pos_t, w1_bd, b1_t, w2, b2)
    return pl.pallas_call(
        _cmp_v_kernel,
        grid=grid,
        in_specs=[src_spec] + [full(a) for a in args],
        out_specs=pl.BlockSpec((1, CMP_GROUPS, V_ROWS, ncp), lambda bi, p: (bi, p, 0, 0)),
        out_shape=jax.ShapeDtypeStruct((b, N_KV, V_ROWS, ncp), MXU_DTYPE),
        compiler_params=_params("parallel", "parallel"),
        name="compress_v",
    )(cmp_src, *args)


def _split3(v):
    parts = []
    rest = np.asarray(v, np.float64)
    for _ in range(3):
        p = rest.astype(np.float32).astype(jnp.bfloat16).astype(np.float64)
        parts.append(p)
        rest = rest - p
    return parts


def _alibi_query_features():
    tab = np.zeros((N_KV, FEAT - F_POS, HPG * TQ), np.float64)
    for g in range(N_KV):
        for h in range(HPG):
            slope = 2.0 ** (-8.0 * (g * HPG + h + 1) / N_HEADS)
            parts = _split3(slope * LOG2E)
            lanes = slice(h * TQ, (h + 1) * TQ)
            for i, p in enumerate(parts):
                tab[g, i, lanes] = SEL_BLOCK * p
                tab[g, 3 + i, lanes] = p
                tab[g, 6 + i, lanes] = CMP_STRIDE * 64 * p
                tab[g, 9 + i, lanes] = CMP_STRIDE * p
    return jnp.asarray(tab, F32).astype(MXU_DTYPE)


def _block_map_t(n_cmp_pad, n_blk):
    cs = np.arange(n_cmp_pad) * CMP_STRIDE
    ce = cs + CMP_BLOCK - 1
    bs = np.arange(n_blk) * SEL_BLOCK
    be = bs + SEL_BLOCK - 1
    return jnp.asarray(((cs[None, :] <= be[:, None]) & (ce[None, :] >= bs[:, None])).astype(np.float32))


def _prob(s, m):
    return jnp.exp2(s - m).astype(MXU_DTYPE)


def _attn_tiles_kernel(*refs, tiles, win_blk0, n_blk, n_sel, has_prev):
    (qt_ref, gate_ref, kc_ref, vc_ref, ks_ref, vs_ref, kd_ref, vd_ref, kwa_ref, kwb_ref, vwa_ref, vwb_ref,
     alibi_ref, map_ref) = refs[:14]
    o_ref, qb_ref, qs_ref, imp_ref = refs[15:] if has_prev else refs[14:]
    lanes = HPG * TQ
    ncp = kc_ref.shape[2]
    blocks_per_tile = TQ // SEL_BLOCK
    win_tiles = WINDOW // TQ
    lane_tok = lax.broadcasted_iota(jnp.int32, (1, lanes), 1) & (TQ - 1)
    row_pos = lax.broadcasted_iota(jnp.int32, (TQ, lanes), 0)
    blk = lax.broadcasted_iota(jnp.int32, (n_blk, TQ), 0)

    def window_tile(qi, j):
        lt = max(qi - win_tiles, 0) + j - 2 * win_blk0
        kref, vref = (kwa_ref, vwa_ref) if lt < 2 else (kwb_ref, vwb_ref)
        sl = slice((lt % 2) * TQ, (lt % 2 + 1) * TQ)
        return kref[0, 0, sl, :], vref[0, :, sl]

    def head(ti, qi):
        cols = slice(ti * TQ, (ti + 1) * TQ)
        t_lane = qi * TQ + lane_tok
        for h in range(HPG):
            qb_ref[ti, 0:HEAD_DIM, h * TQ:(h + 1) * TQ] = qt_ref[0, h * HEAD_DIM:(h + 1) * HEAD_DIM, cols]
        qb_ref[ti, F_SEL:F_POS, :] = jnp.zeros((F_POS - F_SEL, lanes), qb_ref.dtype)
        qb_ref[ti, F_POS:FEAT, :] = alibi_ref[0]
        qb = qb_ref[ti]
        sc = jnp.dot(kc_ref[0, 0], qb, preferred_element_type=F32)
        sw = [jnp.dot(window_tile(qi, j)[0], qb, preferred_element_type=F32) for j in range(3)]
        sd = jnp.dot(kd_ref[0, 0, cols, :], qb, preferred_element_type=F32)
        last_cmp = (t_lane - (CMP_BLOCK - 1)) >> 4
        sc = jnp.where(lax.broadcasted_iota(jnp.int32, (ncp, lanes), 0) <= last_cmp, sc, NEG_MASK)
        ec = jnp.exp2(sc - jnp.max(sc, axis=0, keepdims=True))
        acc_c = jnp.dot(vc_ref[0, 0], ec.astype(MXU_DTYPE), preferred_element_type=F32)
        inv_c = jnp.where(last_cmp >= 0, 1.0 / jnp.maximum(acc_c[HEAD_DIM:HEAD_DIM + 1], 1e-30), 0.0)
        imp = None
        if (qi + 1) * blocks_per_tile > n_sel and qi > 0:
            psum = ec[:, 0:TQ] * inv_c[:, 0:TQ]
            for h in range(1, HPG):
                psum = psum + ec[:, h * TQ:(h + 1) * TQ] * inv_c[:, h * TQ:(h + 1) * TQ]
            imp = jnp.dot(map_ref[...], psum, preferred_element_type=F32)
        return dict(qb=qb, t_lane=t_lane, sw=sw, sd=sd, o_cmp=acc_c[0:HEAD_DIM] * inv_c, imp=imp)

    def rank_init(ti, qi, st):
        first_own_blk = qi * blocks_per_tile
        st.update(ranks=[], k_done=0, k_total=0)
        if st["imp"] is None:
            return
        cur = (qi * TQ + lax.broadcasted_iota(jnp.int32, (n_blk, TQ), 1)) >> 6
        forced = (blk == 0) | (blk == cur) | (blk == cur - 1)
        imp = jnp.where(blk <= cur, st["imp"] + jnp.where(forced, FORCE_BONUS, 0.0), NEG_MASK)
        imp_ref[ti] = imp
        n_rank_chunks = -(-first_own_blk // V7X_SUBLANES)
        st["chunks"] = [imp[c * V7X_SUBLANES:(c + 1) * V7X_SUBLANES] for c in range(n_rank_chunks)]
        st["ranks"] = [jnp.zeros((V7X_SUBLANES, TQ), jnp.int32) for _ in range(n_rank_chunks)]
        st["k_total"] = min(first_own_blk + blocks_per_tile, n_blk)

    def rank_rounds(ti, st, n):
        sub = lax.broadcasted_iota(jnp.int32, (V7X_SUBLANES, TQ), 0)
        stop = min(st["k_done"] + n, st["k_total"])
        for k in range(st["k_done"], stop):
            row = imp_ref[ti, k:k + 1, :]
            for c, mine in enumerate(st["chunks"]):
                lo = c * V7X_SUBLANES
                if lo > k:
                    one = jnp.where(row >= mine, 1, 0)
                elif lo + V7X_SUBLANES - 1 <= k:
                    one = jnp.where(row > mine, 1, 0)
                else:
                    one = jnp.where(sub + lo > k, jnp.where(row >= mine, 1, 0), jnp.where(row > mine, 1, 0))
                st["ranks"][c] = st["ranks"][c] + one
        st["k_done"] = stop

    def chunk_setup(ti, qi, st):
        rank_rounds(ti, st, st["k_total"])
        first_own_blk = qi * blocks_per_tile
        n_chunks = -(-qi * TQ // TK_SEL)
        st.update(n_chunks=n_chunks, m=jnp.full((1, lanes), NEG_MASK, F32), acc=jnp.zeros((V_ROWS, lanes), F32),
                  p_prev=None)
        if n_chunks == 0:
            return
        rank = jnp.zeros((n_blk, TQ), jnp.int32)
        if st["ranks"]:
            pad = [jnp.zeros((n_blk - len(st["ranks"]) * V7X_SUBLANES, TQ), jnp.int32)]
            rank = jnp.concatenate(st["ranks"] + (pad if pad[0].shape[0] else []), axis=0)
        not_sel = jnp.where((rank < n_sel) & (blk < first_own_blk), 0.0, 1.0).astype(qs_ref.dtype)
        qs_ref[ti] = st["qb"]
        for h in range(HPG):
            qs_ref[ti, F_SEL:F_SEL + n_blk, h * TQ:(h + 1) * TQ] = not_sel
        st["qs"] = qs_ref[ti]
        st["s_next"] = chunk_qk(st, 0)

    def chunk_qk(st, k):
        return jnp.dot(ks_ref[0, 0, k * TK_SEL:(k + 1) * TK_SEL, :], st["qs"], preferred_element_type=F32)

    def chunk_pv(st, k):
        return jnp.dot(vs_ref[0, :, k * TK_SEL:(k + 1) * TK_SEL], st["p_prev"], preferred_element_type=F32)

    def chunk_stage(st, k):
        s_cur = st["s_next"]
        if k + 1 < st["n_chunks"]:
            st["s_next"] = chunk_qk(st, k + 1)
        if st["p_prev"] is not None:
            st["acc"] = st["acc"] + chunk_pv(st, k - 1)
        m_new = jnp.maximum(st["m"], jnp.max(s_cur, axis=0, keepdims=True))
        st["p_prev"] = _prob(s_cur, m_new)
        st["acc"] = jnp.exp2(st["m"] - m_new) * st["acc"]
        st["m"] = m_new

    def chunk_finish(st):
        if st["n_chunks"] > 0:
            st["acc"] = st["acc"] + chunk_pv(st, st["n_chunks"] - 1)

    def diagonal(ti, qi, st):
        cols = slice(ti * TQ, (ti + 1) * TQ)
        st["causal"] = qi * TQ + row_pos <= st["t_lane"]
        sd = jnp.where(st["causal"], st["sd"], NEG_MASK)
        m_d = jnp.max(sd, axis=0, keepdims=True)
        acc_d = jnp.dot(vd_ref[0, :, cols], _prob(sd, m_d), preferred_element_type=F32)
        m_all = jnp.maximum(st["m"], m_d)
        acc_s = jnp.exp2(st["m"] - m_all) * st["acc"] + jnp.exp2(m_d - m_all) * acc_d
        st["o_sel"] = acc_s[0:HEAD_DIM] * (1.0 / acc_s[HEAD_DIM:HEAD_DIM + 1])

    def window(ti, qi, st):
        sw, t_lane = st["sw"], st["t_lane"]
        wb = max(qi - win_tiles, 0)
        if qi >= win_tiles:
            d0 = (t_lane - wb * TQ) - row_pos
            sw = [jnp.where(d0 < WINDOW, sw[0], NEG_MASK), sw[1], jnp.where(st["causal"], sw[2], NEG_MASK)]
        else:
            for j in range(3):
                dj = (t_lane - (wb + j) * TQ) - row_pos
                sw[j] = jnp.where(lax.bitcast_convert_type(dj, jnp.uint32) < WINDOW, sw[j], NEG_MASK)
        m_w = jnp.max(jnp.maximum(jnp.maximum(sw[0], sw[1]), sw[2]), axis=0, keepdims=True)
        acc_w = jnp.zeros((V_ROWS, lanes), F32)
        for j in range(3):
            acc_w = acc_w + jnp.dot(window_tile(qi, j)[1], _prob(sw[j], m_w), preferred_element_type=F32)
        st["o_win"] = acc_w[0:HEAD_DIM] * (1.0 / acc_w[HEAD_DIM:HEAD_DIM + 1])

    def output(ti, st):
        cols = slice(ti * TQ, (ti + 1) * TQ)
        gates = gate_ref[0, :, cols]
        def gate_row(j):
            return jnp.concatenate([gates[j * HPG + h:j * HPG + h + 1, :] for h in range(HPG)], axis=1)
        o_t = gate_row(0) * st["o_cmp"] + gate_row(1) * st["o_sel"] + gate_row(2) * st["o_win"]
        for hp in range(HPG // 2):
            pair = jnp.concatenate([o_t[:, (2 * hp) * TQ:(2 * hp + 1) * TQ],
                                    o_t[:, (2 * hp + 1) * TQ:(2 * hp + 2) * TQ]], axis=0)
            o_ref[0, cols, hp * 2 * HEAD_DIM:(hp + 1) * 2 * HEAD_DIM] = pair.T.astype(o_ref.dtype)

    (qa, qb_i) = tiles
    sa, sb = head(0, qa), head(1, qb_i)
    rank_init(0, qa, sa)
    rank_init(1, qb_i, sb)
    chunk_setup(0, qa, sa)
    rounds_per_stage = -(-sb["k_total"] // max(sa["n_chunks"], 1))
    for k in range(sa["n_chunks"]):
        chunk_stage(sa, k)
        rank_rounds(1, sb, rounds_per_stage)
    chunk_finish(sa)
    chunk_setup(1, qb_i, sb)
    tail_a = [lambda: diagonal(0, qa, sa), lambda: window(0, qa, sa), lambda: output(0, sa)]
    for k in range(sb["n_chunks"]):
        chunk_stage(sb, k)
        if tail_a:
            tail_a.pop(0)()
    for phase in tail_a:
        phase()
    chunk_finish(sb)
    diagonal(1, qb_i, sb)
    window(1, qb_i, sb)
    output(1, sb)


def _attention_static(q_t, gates_t, k_cmp, v_cmp_t, k_sel, k_win, v_t):
    b, _, t = q_t.shape
    n_blk = t // SEL_BLOCK
    tiles_per_call = 2
    tb = tiles_per_call * TQ
    assert tb == TK_SEL and t % tb == 0 and t >= 2 * tb and F_SEL + n_blk <= F_POS, "unsupported sequence length"
    n_sel = min(N_SEL, n_blk)
    lanes = HPG * TQ
    rows = HPG * HEAD_DIM
    alibi = _alibi_query_features()
    out_shape = jax.ShapeDtypeStruct((b, t, N_HEADS * HEAD_DIM), MXU_DTYPE)

    attn = None
    for m in range(t // tb):
        tiles = tuple(range(m * tiles_per_call, (m + 1) * tiles_per_call))
        ncp = min(t // CMP_STRIDE, -(-((m + 1) * tb // CMP_STRIDE) // V7X_LANES) * V7X_LANES)
        blk_map_t = _block_map_t(ncp, n_blk)
        kc = (m + 1) * TK_SEL
        wb0 = max(m - 1, 0)
        in_specs = [
            pl.BlockSpec((1, rows, tb), lambda bi, g, m=m: (bi, g, m)),
            pl.BlockSpec((1, 16, tb), lambda bi, g, m=m: (bi, g, m)),
            pl.BlockSpec((1, 1, ncp, FEAT), lambda bi, g: (bi, g, 0, 0)),
            pl.BlockSpec((1, 1, V_ROWS, ncp), lambda bi, g: (bi, g, 0, 0)),
            pl.BlockSpec((1, 1, kc, FEAT), lambda bi, g: (bi, g, 0, 0)),
            pl.BlockSpec((1, V_ROWS, kc), lambda bi, g: (bi, g, 0)),
            pl.BlockSpec((1, 1, tb, FEAT), lambda bi, g, m=m: (bi, g, m, 0)),
            pl.BlockSpec((1, V_ROWS, tb), lambda bi, g, m=m: (bi, g, m)),
            pl.BlockSpec((1, 1, tb, FEAT), lambda bi, g, w=wb0: (bi, g, w, 0)),
            pl.BlockSpec((1, 1, tb, FEAT), lambda bi, g, w=wb0: (bi, g, w + 1, 0)),
            pl.BlockSpec((1, V_ROWS, tb), lambda bi, g, w=wb0: (bi, N_KV + g, w)),
            pl.BlockSpec((1, V_ROWS, tb), lambda bi, g, w=wb0: (bi, N_KV + g, w + 1)),
            pl.BlockSpec((1, FEAT - F_POS, lanes), lambda bi, g: (g, 0, 0)),
            pl.BlockSpec((n_blk, ncp), lambda bi, g: (0, 0)),
        ]
        args = [q_t, gates_t, k_cmp, v_cmp_t, k_sel, v_t, k_sel, v_t, k_win, k_win, v_t, v_t, alibi, blk_map_t]
        aliases = {}
        if attn is not None:
            in_specs.append(pl.BlockSpec(memory_space=pl.ANY))
            args.append(attn)
            aliases = {len(args) - 1: 0}
        kernel = functools.partial(_attn_tiles_kernel, tiles=tiles, win_blk0=wb0, n_blk=n_blk, n_sel=n_sel,
                                   has_prev=attn is not None)
        attn = pl.pallas_call(
            kernel,
            grid=(b, N_KV),
            in_specs=in_specs,
            out_specs=pl.BlockSpec((1, tb, rows), lambda bi, g, m=m: (bi, m, g)),
            out_shape=out_shape,
            scratch_shapes=[pltpu.VMEM((tiles_per_call, FEAT, lanes), MXU_DTYPE),
                            pltpu.VMEM((tiles_per_call, FEAT, lanes), MXU_DTYPE),
                            pltpu.VMEM((tiles_per_call, n_blk, TQ), F32)],
            input_output_aliases=aliases,
            compiler_params=_params("parallel", "parallel"),
            name=f"nsa_attention_{m}",
        )(*args)
    return attn


def _lru_gates(hin, wl_ref, cw_ref, cb_ref, wa_ref, wi_ref, tail_ref, h_ref):
    tt = hin.shape[0]
    w = cb_ref.shape[1]

    @pl.when(pl.program_id(1) == 0)
    def _():
        tail_ref[...] = jnp.zeros_like(tail_ref)
        h_ref[...] = jnp.zeros_like(h_ref)

    def gate_matmul(xb, w_ref):
        n = w_ref.shape[1]
        return jnp.concatenate([jnp.dot(xb[:, j * n:(j + 1) * n], w_ref[j], preferred_element_type=F32)
                                for j in range(w_ref.shape[0])], axis=1)

    x = jnp.dot(hin, wl_ref[:, 0:w], preferred_element_type=F32)
    gate = jnp.dot(hin, wl_ref[:, w:2 * w], preferred_element_type=F32)
    ng = tt // V7X_SUBLANES
    sub = lax.broadcasted_iota(jnp.int32, (ng, V7X_SUBLANES, w), 1)
    x3 = x.reshape(ng, V7X_SUBLANES, w)
    xprev3 = jnp.concatenate([tail_ref[...][None], x3], axis=0)
    tail_ref[...] = x[tt - V7X_SUBLANES:tt]
    xc = x * cw_ref[CONV_W - 1:CONV_W, :] + cb_ref[...]
    for s in range(1, CONV_W):
        rot = pltpu.roll(xprev3, s, 1)
        xs = jnp.where(sub >= s, rot[1:], rot[:-1])
        xc = xc + xs.reshape(tt, w) * cw_ref[CONV_W - 1 - s:CONV_W - s, :]

    xb = xc.astype(MXU_DTYPE)
    return xc, gate_matmul(xb, wa_ref), gate_matmul(xb, wi_ref), gate


def _lru_scan(xc, r_pre, i_pre, gate, ba_ref, bi_ref, lam_ref, o_ref, h_ref):
    tt, w = xc.shape
    ng = tt // V7X_SUBLANES
    sub = lax.broadcasted_iota(jnp.int32, (ng, V7X_SUBLANES, w), 1)
    r = _sigmoid(r_pre + ba_ref[...])
    i = _sigmoid(i_pre + bi_ref[...])
    z = -lam_ref[...]
    softplus = jnp.maximum(z, 0.0) + jnp.log1p(jnp.exp(-jnp.abs(z)))
    log_a = -LRU_C * r * softplus
    a = jnp.exp(log_a)
    bb = jnp.sqrt(_one_minus_sq(a, log_a)) * (i * xc)

    a3 = a.reshape(ng, V7X_SUBLANES, w)
    b3 = bb.reshape(ng, V7X_SUBLANES, w)
    for d in (1, 2, 4):
        ok = sub >= d
        a_sh = pltpu.roll(a3, d, 1)
        b_sh = pltpu.roll(b3, d, 1)
        b3 = jnp.where(ok, a3 * b_sh + b3, b3)
        a3 = jnp.where(ok, a3 * a_sh, a3)
    carry = h_ref[0:1, :]
    groups = []
    for g in range(ng):
        hg = b3[g] + a3[g] * carry
        groups.append(hg)
        carry = hg[V7X_SUBLANES - 1:V7X_SUBLANES, :]
    hcur = jnp.concatenate(groups, axis=0)
    h_ref[...] = jnp.broadcast_to(carry, h_ref.shape)
    o_ref[0] = (hcur * _gelu_tanh(gate)).astype(o_ref.dtype)


def _merge_kernel(attn_ref, lru_ref, h_ref, x_ref, wm_ref, wa_ref, wl_ref, wo_ref, g2_ref,
                  x1_ref, h2_ref):
    d = x_ref.shape[1]
    h = h_ref[...]
    mg0 = _sigmoid(jnp.dot(h, wm_ref[:, 0:d], preferred_element_type=F32))
    mg1 = _sigmoid(jnp.dot(h, wm_ref[:, d:2 * d], preferred_element_type=F32))
    ya = jnp.dot(attn_ref[...], wa_ref[...], preferred_element_type=F32)
    yl = jnp.dot(lru_ref[...], wl_ref[...], preferred_element_type=F32)
    merged = mg0 * ya + mg1 * yl
    x1 = x_ref[...] + jnp.dot(merged.astype(MXU_DTYPE), wo_ref[...], preferred_element_type=F32)
    x1_ref[...] = x1
    ms = jnp.mean(x1 * x1, axis=-1, keepdims=True)
    h2_ref[...] = (x1 * lax.rsqrt(ms + EPS) * g2_ref[...]).astype(h2_ref.dtype)


def _merge(attn, lru, h2d, x2d, w_mg, wa, wl, wo, g2, tm=256):
    m, d = x2d.shape
    row = pl.BlockSpec((tm, d), lambda i: (i, 0))
    full = lambda a: pl.BlockSpec(a.shape, lambda i: (0, 0))
    g2 = g2.reshape(1, d)
    return pl.pallas_call(
        _merge_kernel,
        grid=(m // tm,),
        in_specs=[row, row, row, row, full(w_mg), full(wa), full(wl), full(wo), full(g2)],
        out_specs=[row, row],
        out_shape=[jax.ShapeDtypeStruct((m, d), F32), jax.ShapeDtypeStruct((m, d), MXU_DTYPE)],
        compiler_params=_params("parallel"),
        name="merge_out",
    )(attn, lru, h2d, x2d, w_mg, wa, wl, wo, g2)


def _ffn_kernel(h_ref, x1_ref, wg_ref, wu_ref, wd_ref, o_ref):
    @pl.when(pl.program_id(1) == 0)
    def _():
        o_ref[...] = x1_ref[...]

    h = h_ref[...]
    g = jnp.dot(h, wg_ref[...].astype(MXU_DTYPE), preferred_element_type=F32)
    u = jnp.dot(h, wu_ref[...].astype(MXU_DTYPE), preferred_element_type=F32)
    act = (g * _sigmoid(g) * u).astype(MXU_DTYPE)
    o_ref[...] += jnp.dot(act, wd_ref[...].astype(MXU_DTYPE), preferred_element_type=F32)


def _ffn(h2, x1, wg, wu, wd, tm=1024, tf=256):
    m, d = x1.shape
    f = wg.shape[1]
    return pl.pallas_call(
        _ffn_kernel,
        grid=(m // tm, f // tf),
        in_specs=[pl.BlockSpec((tm, d), lambda i, j: (i, 0)),
                  pl.BlockSpec((tm, d), lambda i, j: (i, 0)),
                  pl.BlockSpec((d, tf), lambda i, j: (0, j)),
                  pl.BlockSpec((d, tf), lambda i, j: (0, j)),
                  pl.BlockSpec((tf, d), lambda i, j: (j, 0))],
        out_specs=pl.BlockSpec((tm, d), lambda i, j: (i, 0)),
        out_shape=jax.ShapeDtypeStruct((m, d), F32),
        compiler_params=_params("parallel", "arbitrary"),
        name="swiglu_ffn",
    )(h2, x1, wg, wu, wd)


def _pad_last(a, n):
    return jnp.pad(a, [(0, 0)] * (a.ndim - 1) + [(0, n - a.shape[-1])])


def _layer(x, norm1_g, w_in, q_norm_g, k_norm_g, cmp_pos_k, cmp_w1_k, cmp_b1_k, cmp_w2_k, cmp_b2_k,
           cmp_pos_v, cmp_w1_v, cmp_b1_v, cmp_w2_v, cmp_b2_v, conv_w, conv_b, lru_wa, lru_ba,
           lru_wi, lru_bi, lru_lambda, w_o_attn, w_o_lru, w_out, norm2_g, w_gate, w_up, w_down):
    b, t, d = x.shape
    m = b * t
    attn_dim = N_HEADS * HEAD_DIM
    kv_dim = N_KV * HEAD_DIM
    lru_w = lru_lambda.shape[0]
    assert t % 512 == 0
    o1 = attn_dim
    o2 = o1 + 6 * kv_dim
    o3 = o2 + 3 * N_HEADS
    o4 = o3 + lru_w
    o5 = o4 + lru_w
    cast = lambda a: a.astype(MXU_DTYPE)

    wq_t = cast(w_in[:, :o1].T)
    w_kv = w_in[:, o1:o2].reshape(d, 6, N_KV, HEAD_DIM)
    w_cmp_src = cast(w_kv[:, 0:2].reshape(d, 2 * kv_dim))
    w_k2 = cast(jnp.stack([w_kv[:, 2], w_kv[:, 4]], axis=1).reshape(d, 2 * kv_dim))
    k_gain2 = jnp.stack([jnp.tile(k_norm_g[1], N_KV), jnp.tile(k_norm_g[2], N_KV)])
    lane_grp = np.arange(kv_dim) // HEAD_DIM
    grp_avg = cast(jnp.asarray((lane_grp[:, None] == lane_grp[None, :]) / HEAD_DIM, F32))
    place_np = np.zeros((kv_dim, N_KV * FEAT), np.float32)
    place_np[np.arange(kv_dim), lane_grp * FEAT + np.arange(kv_dim) % HEAD_DIM] = 1.0
    place = cast(jnp.asarray(place_np))
    w_v = jnp.stack([w_kv[:, 3], w_kv[:, 5]], axis=1)
    w_v_t = cast(_pad_last(w_v, V_ROWS).reshape(d, 2 * N_KV * V_ROWS).T)
    v_ones_col = jnp.tile(jnp.arange(V_ROWS) == HEAD_DIM, 2 * N_KV).astype(F32).reshape(-1, 1)
    w_g = w_in[:, o2:o3].reshape(d, N_KV, HPG, 3).transpose(0, 1, 3, 2).reshape(d, N_KV, 3 * HPG)
    w_g_t = cast(_pad_last(w_g, 16).reshape(d, N_KV * 16).T)
    w_lru = cast(w_in[:, o3:o5])
    w_mg = cast(w_in[:, o5:])
    q_gain_col = jnp.tile(q_norm_g * (HEAD_DIM ** -0.5 * LOG2E), N_HEADS).reshape(attn_dim, 1)
    gain_pad = lambda g: _pad_last(g.reshape(1, HEAD_DIM), FEAT)

    per_slab = V7X_MXU_DIM // (lru_w // LRU_BLOCKS)
    eye = jnp.eye(per_slab, dtype=F32)
    slabs = lambda wgt: cast(jnp.einsum('snkj,nm->snkmj', wgt.reshape(LRU_BLOCKS // per_slab, per_slab,
                                                                       *wgt.shape[1:]), eye)
                             .reshape(LRU_BLOCKS // per_slab, V7X_MXU_DIM, V7X_MXU_DIM))
    vec = lambda v: v.reshape(1, lru_w)

    h3d, q_t, v_t, gates_t, k_sel, k_win, cmp_src, lru = _in_proj(
        x,
        (norm1_g.reshape(1, d), wq_t, q_gain_col, w_v_t, v_ones_col, w_g_t, w_k2, k_gain2, grp_avg, place,
         w_cmp_src),
        (w_lru, conv_w.reshape(CONV_W, lru_w), vec(conv_b), slabs(lru_wa), vec(lru_ba), slabs(lru_wi),
         vec(lru_bi), vec(lru_lambda)))
    h2d = h3d.reshape(m, d)

    k_cmp = _compress(cmp_src, 0, cmp_pos_k, cmp_w1_k, cmp_b1_k, cast(_pad_last(cmp_w2_k, FEAT)),
                      _pad_last(cmp_b2_k.reshape(1, HEAD_DIM), FEAT), gain_pad(k_norm_g[0]))
    v_cmp_t = _compress(cmp_src, N_KV // CMP_GROUPS, cmp_pos_v, cmp_w1_v, cmp_b1_v,
                        cast(_pad_last(cmp_w2_v, V_ROWS).T),
                        jnp.concatenate([cmp_b2_v, v_ones_col[HEAD_DIM:V_ROWS, 0]]).reshape(V_ROWS, 1))

    attn = _attention_static(q_t, gates_t, k_cmp, v_cmp_t, k_sel, k_win, v_t)

    x1, h2 = _merge(attn.reshape(m, attn_dim), lru.reshape(m, lru_w), h2d, x.reshape(m, d), w_mg,
                    cast(w_o_attn), cast(w_o_lru), cast(w_out), norm2_g)
    out = _ffn(h2, x1, w_gate, w_up, w_down)
    return out.reshape(b, t, d)


def kernel(x, norm1_g, w_in, q_norm_g, k_norm_g, cmp_pos_k, cmp_w1_k, cmp_b1_k, cmp_w2_k, cmp_b2_k,
           cmp_pos_v, cmp_w1_v, cmp_b1_v, cmp_w2_v, cmp_b2_v, conv_w, conv_b, lru_wa, lru_ba,
           lru_wi, lru_bi, lru_lambda, w_o_attn, w_o_lru, w_out, norm2_g, w_gate, w_up, w_down):
    for l in range(norm1_g.shape[0]):
        x = _layer(x, norm1_g[l], w_in[l], q_norm_g[l], k_norm_g[l], cmp_pos_k[l], cmp_w1_k[l],
                   cmp_b1_k[l], cmp_w2_k[l], cmp_b2_k[l], cmp_pos_v[l], cmp_w1_v[l], cmp_b1_v[l],
                   cmp_w2_v[l], cmp_b2_v[l], conv_w[l], conv_b[l], lru_wa[l], lru_ba[l], lru_wi[l],
                   lru_bi[l], lru_lambda[l], w_o_attn[l], w_o_lru[l], w_out[l], norm2_g[l],
                   w_gate[l], w_up[l], w_down[l])
    return x
```

```python
import functools

import numpy as np
import jax
import jax.numpy as jnp
from jax import lax
from jax.experimental import pallas as pl
from jax.experimental.pallas import tpu as pltpu

N_HEADS = 16
HEAD_DIM = 64
N_KV = 4
HPG = N_HEADS // N_KV
CMP_BLOCK = 32
CMP_STRIDE = 16
CMP_HIDDEN = 256
SEL_BLOCK = 64
N_SEL = 16
WINDOW = 512
FORCE_BONUS = 1e4
LRU_BLOCKS = 16
CONV_W = 4
LRU_C = 8.0
EPS = 1e-6

MXU_DTYPE = jnp.bfloat16
F32 = jnp.float32

V7X_LANES = 128
V7X_SUBLANES = 8
V7X_MXU_DIM = 256
V7X_VMEM_LIMIT_BYTES = 48 * 1024 * 1024

TQ = 256
TK_SEL = 256
WIN_KEYS = WINDOW + TQ
V_ROWS = 80
LOG2E = 1.4426950408889634
FEAT = V7X_MXU_DIM
F_SEL = HEAD_DIM
F_POS = 2 * HEAD_DIM
F_CMP = F_POS + 6
NEG_MASK = -1e30
NEG_BLOCK = -(2.0 ** 100)


def _params(*sem):
    return pltpu.CompilerParams(dimension_semantics=sem, vmem_limit_bytes=V7X_VMEM_LIMIT_BYTES)


def _gelu_tanh(x):
    return 0.5 * x * (1.0 + jnp.tanh(0.7978845608028654 * (x + 0.044715 * (x * x * x))))


def _sigmoid(x):
    return 1.0 / (1.0 + jnp.exp(-x))


def _one_minus_sq(a, log_a):
    series = (-2.0 * log_a) * (1.0 + log_a * (1.0 + log_a * (2.0 / 3)))
    return jnp.where(log_a > -1.0 / 128, series, 1.0 - a * a)


def _nt_dot(wt, h):
    return lax.dot_general(wt, h, (((1,), (1,)), ((), ())), preferred_element_type=F32)


def _token_features(pos, col, with_block_mask):
    blk = pos >> 6
    off = pos & (SEL_BLOCK - 1)
    feat = jnp.where((col >= F_POS) & (col < F_POS + 3), blk.astype(F32),
                     jnp.where((col >= F_POS + 3) & (col < F_POS + 6), off.astype(F32), 0.0))
    if with_block_mask:
        feat = jnp.where((col >= F_SEL) & (col - F_SEL == blk) & (col < F_POS), NEG_BLOCK, feat)
    return feat


def _in_proj_kernel(x_ref, g1_ref, wq_ref, qg_ref, wv_ref, vb_ref, wg_ref, wk_ref, kg_ref, grp_ref,
                    place_ref, wc_ref, wl_ref, cw_ref, cb_ref, wa_ref, ba_ref, wi_ref, bi_ref, lam_ref,
                    h_ref, qt_ref, vt_ref, gt_ref, ks_ref, kw_ref, cs_ref, lru_ref, tail_ref, state_ref):
    x = x_ref[0]
    tm = x.shape[0]
    ms = jnp.mean(x * x, axis=-1, keepdims=True)
    h = (x * lax.rsqrt(ms + EPS) * g1_ref[...]).astype(MXU_DTYPE)
    h_ref[0] = h

    lru_mid = _lru_gates(h, wl_ref, cw_ref, cb_ref, wa_ref, wi_ref, tail_ref, state_ref)

    r3 = _nt_dot(wq_ref[...], h).reshape(N_HEADS, HEAD_DIM, tm)
    qn = r3 * lax.rsqrt(jnp.mean(r3 * r3, axis=1, keepdims=True) + EPS)
    qt_ref[0] = (qn.reshape(N_HEADS * HEAD_DIM, tm) * qg_ref[...]).astype(qt_ref.dtype)

    vt_ref[0] = (_nt_dot(wv_ref[...], h) + vb_ref[...]).astype(vt_ref.dtype)
    gt_ref[0] = _sigmoid(_nt_dot(wg_ref[...], h))

    k = jnp.dot(h, wk_ref[...], preferred_element_type=F32)
    kk = k * k
    kk_hi = kk.astype(MXU_DTYPE)
    kk_lo = (kk - kk_hi.astype(F32)).astype(MXU_DTYPE)
    pos = pl.program_id(1) * tm + lax.broadcasted_iota(jnp.int32, (tm, FEAT), 0)
    col = lax.broadcasted_iota(jnp.int32, (tm, FEAT), 1)
    kv_dim = N_KV * HEAD_DIM
    for branch, (o_ref, with_block_mask) in enumerate(((ks_ref, True), (kw_ref, False))):
        sl = slice(branch * kv_dim, (branch + 1) * kv_dim)
        msq = (jnp.dot(kk_hi[:, sl], grp_ref[...], preferred_element_type=F32)
               + jnp.dot(kk_lo[:, sl], grp_ref[...], preferred_element_type=F32))
        kn = (k[:, sl] * lax.rsqrt(msq + EPS) * kg_ref[branch:branch + 1, :]).astype(MXU_DTYPE)
        placed = jnp.dot(kn, place_ref[...], preferred_element_type=F32)
        feat = _token_features(pos, col, with_block_mask)
        for g in range(N_KV):
            o_ref[0, g] = (placed[:, g * FEAT:(g + 1) * FEAT] + feat).astype(o_ref.dtype)

    cs_ref[0] = jnp.dot(h, wc_ref[...], preferred_element_type=F32)

    _lru_scan(*lru_mid, ba_ref, bi_ref, lam_ref, lru_ref, state_ref)


def _in_proj(x, attn_weights, lru_weights, tm=256):
    b, t, d = x.shape
    kv_dim = N_KV * HEAD_DIM
    wq_t, w_v_t, w_g_t = attn_weights[1], attn_weights[3], attn_weights[5]
    lru_w = lru_weights[-1].shape[1]
    full = lambda a: pl.BlockSpec(a.shape, lambda bi, i: (0,) * a.ndim)
    rowblk = lambda n: pl.BlockSpec((1, tm, n), lambda bi, i: (bi, i, 0))
    colblk = lambda n: pl.BlockSpec((1, n, tm), lambda bi, i: (bi, 0, i))
    kblk = pl.BlockSpec((1, N_KV, tm, FEAT), lambda bi, i: (bi, 0, i, 0))
    weights = tuple(attn_weights) + tuple(lru_weights)
    return pl.pallas_call(
        _in_proj_kernel,
        grid=(b, t // tm),
        in_specs=[rowblk(d)] + [full(a) for a in weights],
        out_specs=[rowblk(d), colblk(wq_t.shape[0]), colblk(w_v_t.shape[0]), colblk(w_g_t.shape[0]),
                   kblk, kblk, rowblk(2 * kv_dim), rowblk(lru_w)],
        out_shape=[jax.ShapeDtypeStruct((b, t, d), MXU_DTYPE),
                   jax.ShapeDtypeStruct((b, wq_t.shape[0], t), MXU_DTYPE),
                   jax.ShapeDtypeStruct((b, w_v_t.shape[0], t), MXU_DTYPE),
                   jax.ShapeDtypeStruct((b, w_g_t.shape[0], t), F32),
                   jax.ShapeDtypeStruct((b, N_KV, t, FEAT), MXU_DTYPE),
                   jax.ShapeDtypeStruct((b, N_KV, t, FEAT), MXU_DTYPE),
                   jax.ShapeDtypeStruct((b, t, 2 * kv_dim), F32),
                   jax.ShapeDtypeStruct((b, t, lru_w), MXU_DTYPE)],
        scratch_shapes=[pltpu.VMEM((V7X_SUBLANES, lru_w), F32),
                        pltpu.VMEM((V7X_SUBLANES, lru_w), F32)],
        compiler_params=_params("parallel", "arbitrary"),
        name="in_proj_lru",
    )(x, *weights)


CMP_GROUPS = V7X_LANES // HEAD_DIM


def _cmp_hidden(src_ref, pos_ref, w1_ref, b1_ref):
    ncp = src_ref.shape[1] // CMP_STRIDE
    first = jnp.zeros((ncp, CMP_GROUPS * CMP_HIDDEN), F32)
    second = jnp.zeros((ncp, CMP_GROUPS * CMP_HIDDEN), F32)
    for l in range(CMP_STRIDE):
        x = src_ref[0, pl.ds(l, ncp, stride=CMP_STRIDE), :]
        lo = (x + pos_ref[l:l + 1, :]).astype(MXU_DTYPE)
        hi = (x + pos_ref[CMP_STRIDE + l:CMP_STRIDE + l + 1, :]).astype(MXU_DTYPE)
        first = first + jnp.dot(lo, w1_ref[l], preferred_element_type=F32)
        second = second + jnp.dot(hi, w1_ref[CMP_STRIDE + l], preferred_element_type=F32)
    hid = first + pltpu.roll(second, ncp - 1, 0) + b1_ref[...]
    return _gelu_tanh(hid).astype(MXU_DTYPE)


def _cmp_k_kernel(src_ref, pos_ref, w1_ref, b1_ref, w2_ref, b2_ref, g_ref, o_ref):
    hid = _cmp_hidden(src_ref, pos_ref, w1_ref, b1_ref)
    ncp = hid.shape[0]
    idx = lax.broadcasted_iota(jnp.int32, (ncp, FEAT), 0)
    col = lax.broadcasted_iota(jnp.int32, (ncp, FEAT), 1)
    feat = jnp.where((col >= F_CMP) & (col < F_CMP + 3), (idx >> 6).astype(F32),
                     jnp.where((col >= F_CMP + 3) & (col < F_CMP + 6), (idx & 63).astype(F32), 0.0))
    for gl in range(CMP_GROUPS):
        r = jnp.dot(hid[:, gl * CMP_HIDDEN:(gl + 1) * CMP_HIDDEN], w2_ref[...],
                    preferred_element_type=F32) + b2_ref[...]
        ms = jnp.sum(r * r, axis=-1, keepdims=True) * (1.0 / HEAD_DIM)
        o_ref[0, gl] = (r * lax.rsqrt(ms + EPS) * g_ref[...] + feat).astype(o_ref.dtype)


def _cmp_v_kernel(src_ref, pos_ref, w1_ref, b1_ref, w2t_ref, b2_ref, o_ref):
    hid = _cmp_hidden(src_ref, pos_ref, w1_ref, b1_ref)
    for gl in range(CMP_GROUPS):
        r = _nt_dot(w2t_ref[...], hid[:, gl * CMP_HIDDEN:(gl + 1) * CMP_HIDDEN]) + b2_ref[...]
        o_ref[0, gl] = r.astype(o_ref.dtype)


def _compress(cmp_src, lane_block0, pos, w1, b1, w2, b2, gain_pad=None):
    b, t, _ = cmp_src.shape
    ncp = t // CMP_STRIDE
    hid_w = CMP_GROUPS * CMP_HIDDEN
    eye = jnp.eye(CMP_GROUPS, dtype=w1.dtype)
    w1_bd = jnp.einsum('ldf,gh->lgdhf', w1, eye).reshape(CMP_BLOCK, V7X_LANES, hid_w).astype(MXU_DTYPE)
    pos_t = jnp.tile(pos, (1, CMP_GROUPS))
    b1_t = jnp.tile(b1.reshape(1, CMP_HIDDEN), (1, CMP_GROUPS))
    full = lambda a: pl.BlockSpec(a.shape, lambda bi, p: (0,) * a.ndim)
    src_spec = pl.BlockSpec((1, t, V7X_LANES), lambda bi, p: (bi, 0, lane_block0 + p))
    grid = (b, N_KV // CMP_GROUPS)
    if gain_pad is not None:
        args = (pos_t, w1_bd, b1_t, w2, b2, gain_pad)
        return pl.pallas_call(
            _cmp_k_kernel,
            grid=grid,
            in_specs=[src_spec] + [full(a) for a in args],
            out_specs=pl.BlockSpec((1, CMP_GROUPS, ncp, FEAT), lambda bi, p: (bi, p, 0, 0)),
            out_shape=jax.ShapeDtypeStruct((b, N_KV, ncp, FEAT), MXU_DTYPE),
            compiler_params=_params("parallel", "parallel"),
            name="compress_k",
        )(cmp_src, *args)
    args = (pos_t, w1_bd, b1_t, w2, b2)
    return pl.pallas_call(
        _cmp_v_kernel,
        grid=grid,
        in_specs=[src_spec] + [full(a) for a in args],
        out_specs=pl.BlockSpec((1, CMP_GROUPS, V_ROWS, ncp), lambda bi, p: (bi, p, 0, 0)),
        out_shape=jax.ShapeDtypeStruct((b, N_KV, V_ROWS, ncp), MXU_DTYPE),
        compiler_params=_params("parallel", "parallel"),
        name="compress_v",
    )(cmp_src, *args)


def _split3(v):
    parts = []
    rest = np.asarray(v, np.float64)
    for _ in range(3):
        p = rest.astype(np.float32).astype(jnp.bfloat16).astype(np.float64)
        parts.append(p)
        rest = rest - p
    return parts


def _alibi_query_features():
    tab = np.zeros((N_KV, FEAT - F_POS, HPG * TQ), np.float64)
    for g in range(N_KV):
        for h in range(HPG):
            slope = 2.0 ** (-8.0 * (g * HPG + h + 1) / N_HEADS)
            parts = _split3(slope * LOG2E)
            lanes = slice(h * TQ, (h + 1) * TQ)
            for i, p in enumerate(parts):
                tab[g, i, lanes] = SEL_BLOCK * p
                tab[g, 3 + i, lanes] = p
                tab[g, 6 + i, lanes] = CMP_STRIDE * 64 * p
                tab[g, 9 + i, lanes] = CMP_STRIDE * p
    return jnp.asarray(tab, F32).astype(MXU_DTYPE)


def _block_map_t(n_cmp_pad, n_blk):
    cs = np.arange(n_cmp_pad) * CMP_STRIDE
    ce = cs + CMP_BLOCK - 1
    bs = np.arange(n_blk) * SEL_BLOCK
    be = bs + SEL_BLOCK - 1
    return jnp.asarray(((cs[None, :] <= be[:, None]) & (ce[None, :] >= bs[:, None])).astype(np.float32))


def _prob(s, m):
    return jnp.exp2(s - m).astype(MXU_DTYPE)


def _attn_tiles_kernel(*refs, tiles, win_blk0, n_blk, n_sel, has_prev):
    (qt_ref, gate_ref, kc_ref, vc_ref, ks_ref, vs_ref, kd_ref, vd_ref, kwa_ref, kwb_ref, vwa_ref, vwb_ref,
     alibi_ref, map_ref) = refs[:14]
    o_ref, qb_ref, qs_ref, imp_ref = refs[15:] if has_prev else refs[14:]
    lanes = HPG * TQ
    ncp = kc_ref.shape[2]
    blocks_per_tile = TQ // SEL_BLOCK
    win_tiles = WINDOW // TQ
    lane_tok = lax.broadcasted_iota(jnp.int32, (1, lanes), 1) & (TQ - 1)
    row_pos = lax.broadcasted_iota(jnp.int32, (TQ, lanes), 0)
    blk = lax.broadcasted_iota(jnp.int32, (n_blk, TQ), 0)

    def window_tile(qi, j):
        lt = max(qi - win_tiles, 0) + j - 2 * win_blk0
        kref, vref = (kwa_ref, vwa_ref) if lt < 2 else (kwb_ref, vwb_ref)
        sl = slice((lt % 2) * TQ, (lt % 2 + 1) * TQ)
        return kref[0, 0, sl, :], vref[0, :, sl]

    def head(ti, qi):
        cols = slice(ti * TQ, (ti + 1) * TQ)
        t_lane = qi * TQ + lane_tok
        for h in range(HPG):
            qb_ref[ti, 0:HEAD_DIM, h * TQ:(h + 1) * TQ] = qt_ref[0, h * HEAD_DIM:(h + 1) * HEAD_DIM, cols]
        qb_ref[ti, F_SEL:F_POS, :] = jnp.zeros((F_POS - F_SEL, lanes), qb_ref.dtype)
        qb_ref[ti, F_POS:FEAT, :] = alibi_ref[0]
        qb = qb_ref[ti]
        sc = jnp.dot(kc_ref[0, 0], qb, preferred_element_type=F32)
        sw = [jnp.dot(window_tile(qi, j)[0], qb, preferred_element_type=F32) for j in range(3)]
        sd = jnp.dot(kd_ref[0, 0, cols, :], qb, preferred_element_type=F32)
        last_cmp = (t_lane - (CMP_BLOCK - 1)) >> 4
        sc = jnp.where(lax.broadcasted_iota(jnp.int32, (ncp, lanes), 0) <= last_cmp, sc, NEG_MASK)
        ec = jnp.exp2(sc - jnp.max(sc, axis=0, keepdims=True))
        acc_c = jnp.dot(vc_ref[0, 0], ec.astype(MXU_DTYPE), preferred_element_type=F32)
        inv_c = jnp.where(last_cmp >= 0, 1.0 / jnp.maximum(acc_c[HEAD_DIM:HEAD_DIM + 1], 1e-30), 0.0)
        imp = None
        if (qi + 1) * blocks_per_tile > n_sel and qi > 0:
            psum = ec[:, 0:TQ] * inv_c[:, 0:TQ]
            for h in range(1, HPG):
                psum = psum + ec[:, h * TQ:(h + 1) * TQ] * inv_c[:, h * TQ:(h + 1) * TQ]
            imp = jnp.dot(map_ref[...], psum, preferred_element_type=F32)
        return dict(qb=qb, t_lane=t_lane, sw=sw, sd=sd, o_cmp=acc_c[0:HEAD_DIM] * inv_c, imp=imp)

    def rank_init(ti, qi, st):
        first_own_blk = qi * blocks_per_tile
        st.update(ranks=[], k_done=0, k_total=0)
        if st["imp"] is None:
            return
        cur = (qi * TQ + lax.broadcasted_iota(jnp.int32, (n_blk, TQ), 1)) >> 6
        forced = (blk == 0) | (blk == cur) | (blk == cur - 1)
        imp = jnp.where(blk <= cur, st["imp"] + jnp.where(forced, FORCE_BONUS, 0.0), NEG_MASK)
        imp_ref[ti] = imp
        n_rank_chunks = -(-first_own_blk // V7X_SUBLANES)
        st["chunks"] = [imp[c * V7X_SUBLANES:(c + 1) * V7X_SUBLANES] for c in range(n_rank_chunks)]
        st["ranks"] = [jnp.zeros((V7X_SUBLANES, TQ), jnp.int32) for _ in range(n_rank_chunks)]
        st["k_total"] = min(first_own_blk + blocks_per_tile, n_blk)

    def rank_rounds(ti, st, n):
        sub = lax.broadcasted_iota(jnp.int32, (V7X_SUBLANES, TQ), 0)
        stop = min(st["k_done"] + n, st["k_total"])
        for k in range(st["k_done"], stop):
            row = imp_ref[ti, k:k + 1, :]
            for c, mine in enumerate(st["chunks"]):
                lo = c * V7X_SUBLANES
                if lo > k:
                    one = jnp.where(row >= mine, 1, 0)
                elif lo + V7X_SUBLANES - 1 <= k:
                    one = jnp.where(row > mine, 1, 0)
                else:
                    one = jnp.where(sub + lo > k, jnp.where(row >= mine, 1, 0), jnp.where(row > mine, 1, 0))
                st["ranks"][c] = st["ranks"][c] + one
        st["k_done"] = stop

    def chunk_setup(ti, qi, st):
        rank_rounds(ti, st, st["k_total"])
        first_own_blk = qi * blocks_per_tile
        n_chunks = -(-qi * TQ // TK_SEL)
        st.update(n_chunks=n_chunks, m=jnp.full((1, lanes), NEG_MASK, F32), acc=jnp.zeros((V_ROWS, lanes), F32),
                  p_prev=None)
        if n_chunks == 0:
            return
        rank = jnp.zeros((n_blk, TQ), jnp.int32)
        if st["ranks"]:
            pad = [jnp.zeros((n_blk - len(st["ranks"]) * V7X_SUBLANES, TQ), jnp.int32)]
            rank = jnp.concatenate(st["ranks"] + (pad if pad[0].shape[0] else []), axis=0)
        not_sel = jnp.where((rank < n_sel) & (blk < first_own_blk), 0.0, 1.0).astype(qs_ref.dtype)
        qs_ref[ti] = st["qb"]
        for h in range(HPG):
            qs_ref[ti, F_SEL:F_SEL + n_blk, h * TQ:(h + 1) * TQ] = not_sel
        st["qs"] = qs_ref[ti]
        st["s_next"] = chunk_qk(st, 0)

    def chunk_qk(st, k):
        return jnp.dot(ks_ref[0, 0, k * TK_SEL:(k + 1) * TK_SEL, :], st["qs"], preferred_element_type=F32)

    def chunk_pv(st, k):
        return jnp.dot(vs_ref[0, :, k * TK_SEL:(k + 1) * TK_SEL], st["p_prev"], preferred_element_type=F32)

    def chunk_stage(st, k):
        s_cur = st["s_next"]
        if k + 1 < st["n_chunks"]:
            st["s_next"] = chunk_qk(st, k + 1)
        if st["p_prev"] is not None:
            st["acc"] = st["acc"] + chunk_pv(st, k - 1)
        m_new = jnp.maximum(st["m"], jnp.max(s_cur, axis=0, keepdims=True))
        st["p_prev"] = _prob(s_cur, m_new)
        st["acc"] = jnp.exp2(st["m"] - m_new) * st["acc"]
        st["m"] = m_new

    def chunk_finish(st):
        if st["n_chunks"] > 0:
            st["acc"] = st["acc"] + chunk_pv(st, st["n_chunks"] - 1)

    def diagonal(ti, qi, st):
        cols = slice(ti * TQ, (ti + 1) * TQ)
        st["causal"] = qi * TQ + row_pos <= st["t_lane"]
        sd = jnp.where(st["causal"], st["sd"], NEG_MASK)
        m_d = jnp.max(sd, axis=0, keepdims=True)
        acc_d = jnp.dot(vd_ref[0, :, cols], _prob(sd, m_d), preferred_element_type=F32)
        m_all = jnp.maximum(st["m"], m_d)
        acc_s = jnp.exp2(st["m"] - m_all) * st["acc"] + jnp.exp2(m_d - m_all) * acc_d
        st["o_sel"] = acc_s[0:HEAD_DIM] * (1.0 / acc_s[HEAD_DIM:HEAD_DIM + 1])

    def window(ti, qi, st):
        sw, t_lane = st["sw"], st["t_lane"]
        wb = max(qi - win_tiles, 0)
        if qi >= win_tiles:
            d0 = (t_lane - wb * TQ) - row_pos
            sw = [jnp.where(d0 < WINDOW, sw[0], NEG_MASK), sw[1], jnp.where(st["causal"], sw[2], NEG_MASK)]
        else:
            for j in range(3):
                dj = (t_lane - (wb + j) * TQ) - row_pos
                sw[j] = jnp.where(lax.bitcast_convert_type(dj, jnp.uint32) < WINDOW, sw[j], NEG_MASK)
        m_w = jnp.max(jnp.maximum(jnp.maximum(sw[0], sw[1]), sw[2]), axis=0, keepdims=True)
        acc_w = jnp.zeros((V_ROWS, lanes), F32)
        for j in range(3):
            acc_w = acc_w + jnp.dot(window_tile(qi, j)[1], _prob(sw[j], m_w), preferred_element_type=F32)
        st["o_win"] = acc_w[0:HEAD_DIM] * (1.0 / acc_w[HEAD_DIM:HEAD_DIM + 1])

    def output(ti, st):
        cols = slice(ti * TQ, (ti + 1) * TQ)
        gates = gate_ref[0, :, cols]
        def gate_row(j):
            return jnp.concatenate([gates[j * HPG + h:j * HPG + h + 1, :] for h in range(HPG)], axis=1)
        o_t = gate_row(0) * st["o_cmp"] + gate_row(1) * st["o_sel"] + gate_row(2) * st["o_win"]
        for hp in range(HPG // 2):
            pair = jnp.concatenate([o_t[:, (2 * hp) * TQ:(2 * hp + 1) * TQ],
                                    o_t[:, (2 * hp + 1) * TQ:(2 * hp + 2) * TQ]], axis=0)
            o_ref[0, cols, hp * 2 * HEAD_DIM:(hp + 1) * 2 * HEAD_DIM] = pair.T.astype(o_ref.dtype)

    (qa, qb_i) = tiles
    sa, sb = head(0, qa), head(1, qb_i)
    rank_init(0, qa, sa)
    rank_init(1, qb_i, sb)
    chunk_setup(0, qa, sa)
    rounds_per_stage = -(-sb["k_total"] // max(sa["n_chunks"], 1))
    for k in range(sa["n_chunks"]):
        chunk_stage(sa, k)
        rank_rounds(1, sb, rounds_per_stage)
    chunk_finish(sa)
    chunk_setup(1, qb_i, sb)
    tail_a = [lambda: diagonal(0, qa, sa), lambda: window(0, qa, sa), lambda: output(0, sa)]
    for k in range(sb["n_chunks"]):
        chunk_stage(sb, k)
        if tail_a:
            tail_a.pop(0)()
    for phase in tail_a:
        phase()
    chunk_finish(sb)
    diagonal(1, qb_i, sb)
    window(1, qb_i, sb)
    output(1, sb)


def _attention_static(q_t, gates_t, k_cmp, v_cmp_t, k_sel, k_win, v_t):
    b, _, t = q_t.shape
    n_blk = t // SEL_BLOCK
    tiles_per_call = 2
    tb = tiles_per_call * TQ
    assert t % max(tb, TK_SEL) == 0 and t >= 2 * tb and F_SEL + n_blk <= F_POS, "unsupported sequence length"
    n_sel = min(N_SEL, n_blk)
    lanes = HPG * TQ
    rows = HPG * HEAD_DIM
    alibi = _alibi_query_features()
    out_shape = jax.ShapeDtypeStruct((b, t, N_HEADS * HEAD_DIM), MXU_DTYPE)

    attn = None
    for m in range(t // tb):
        tiles = tuple(range(m * tiles_per_call, (m + 1) * tiles_per_call))
        ncp = min(t // CMP_STRIDE, -(-((m + 1) * tb // CMP_STRIDE) // V7X_LANES) * V7X_LANES)
        blk_map_t = _block_map_t(ncp, n_blk)
        kc = -(-tiles[-1] * TQ // TK_SEL) * TK_SEL
        wb0 = max(m - 1, 0)
        in_specs = [
            pl.BlockSpec((1, rows, tb), lambda bi, g, m=m: (bi, g, m)),
            pl.BlockSpec((1, 16, tb), lambda bi, g, m=m: (bi, g, m)),
            pl.BlockSpec((1, 1, ncp, FEAT), lambda bi, g: (bi, g, 0, 0)),
            pl.BlockSpec((1, 1, V_ROWS, ncp), lambda bi, g: (bi, g, 0, 0)),
            pl.BlockSpec((1, 1, kc, FEAT), lambda bi, g: (bi, g, 0, 0)),
            pl.BlockSpec((1, V_ROWS, kc), lambda bi, g: (bi, g, 0)),
            pl.BlockSpec((1, 1, tb, FEAT), lambda bi, g, m=m: (bi, g, m, 0)),
            pl.BlockSpec((1, V_ROWS, tb), lambda bi, g, m=m: (bi, g, m)),
            pl.BlockSpec((1, 1, tb, FEAT), lambda bi, g, w=wb0: (bi, g, w, 0)),
            pl.BlockSpec((1, 1, tb, FEAT), lambda bi, g, w=wb0: (bi, g, w + 1, 0)),
            pl.BlockSpec((1, V_ROWS, tb), lambda bi, g, w=wb0: (bi, N_KV + g, w)),
            pl.BlockSpec((1, V_ROWS, tb), lambda bi, g, w=wb0: (bi, N_KV + g, w + 1)),
            pl.BlockSpec((1, FEAT - F_POS, lanes), lambda bi, g: (g, 0, 0)),
            pl.BlockSpec((n_blk, ncp), lambda bi, g: (0, 0)),
        ]
        args = [q_t, gates_t, k_cmp, v_cmp_t, k_sel, v_t, k_sel, v_t, k_win, k_win, v_t, v_t, alibi, blk_map_t]
        aliases = {}
        if attn is not None:
            in_specs.append(pl.BlockSpec(memory_space=pl.ANY))
            args.append(attn)
            aliases = {len(args) - 1: 0}
        kernel = functools.partial(_attn_tiles_kernel, tiles=tiles, win_blk0=wb0, n_blk=n_blk, n_sel=n_sel,
                                   has_prev=attn is not None)
        attn = pl.pallas_call(
            kernel,
            grid=(b, N_KV),
            in_specs=in_specs,
            out_specs=pl.BlockSpec((1, tb, rows), lambda bi, g, m=m: (bi, m, g)),
            out_shape=out_shape,
            scratch_shapes=[pltpu.VMEM((tiles_per_call, FEAT, lanes), MXU_DTYPE),
                            pltpu.VMEM((tiles_per_call, FEAT, lanes), MXU_DTYPE),
                            pltpu.VMEM((tiles_per_call, n_blk, TQ), F32)],
            input_output_aliases=aliases,
            compiler_params=_params("parallel", "parallel"),
            name=f"nsa_attention_{m}",
        )(*args)
    return attn


def _lru_gates(hin, wl_ref, cw_ref, cb_ref, wa_ref, wi_ref, tail_ref, h_ref):
    tt = hin.shape[0]
    w = cb_ref.shape[1]

    @pl.when(pl.program_id(1) == 0)
    def _():
        tail_ref[...] = jnp.zeros_like(tail_ref)
        h_ref[...] = jnp.zeros_like(h_ref)

    def gate_matmul(xb, w_ref):
        n = w_ref.shape[1]
        return jnp.concatenate([jnp.dot(xb[:, j * n:(j + 1) * n], w_ref[j], preferred_element_type=F32)
                                for j in range(w_ref.shape[0])], axis=1)

    x = jnp.dot(hin, wl_ref[:, 0:w], preferred_element_type=F32)
    gate = jnp.dot(hin, wl_ref[:, w:2 * w], preferred_element_type=F32)
    ng = tt // V7X_SUBLANES
    sub = lax.broadcasted_iota(jnp.int32, (ng, V7X_SUBLANES, w), 1)
    x3 = x.reshape(ng, V7X_SUBLANES, w)
    xprev3 = jnp.concatenate([tail_ref[...][None], x3], axis=0)
    tail_ref[...] = x[tt - V7X_SUBLANES:tt]
    xc = x * cw_ref[CONV_W - 1:CONV_W, :] + cb_ref[...]
    for s in range(1, CONV_W):
        rot = pltpu.roll(xprev3, s, 1)
        xs = jnp.where(sub >= s, rot[1:], rot[:-1])
        xc = xc + xs.reshape(tt, w) * cw_ref[CONV_W - 1 - s:CONV_W - s, :]

    xb = xc.astype(MXU_DTYPE)
    return xc, gate_matmul(xb, wa_ref), gate_matmul(xb, wi_ref), gate


def _lru_scan(xc, r_pre, i_pre, gate, ba_ref, bi_ref, lam_ref, o_ref, h_ref):
    tt, w = xc.shape
    ng = tt // V7X_SUBLANES
    sub = lax.broadcasted_iota(jnp.int32, (ng, V7X_SUBLANES, w), 1)
    r = _sigmoid(r_pre + ba_ref[...])
    i = _sigmoid(i_pre + bi_ref[...])
    z = -lam_ref[...]
    softplus = jnp.maximum(z, 0.0) + jnp.log1p(jnp.exp(-jnp.abs(z)))
    log_a = -LRU_C * r * softplus
    a = jnp.exp(log_a)
    bb = jnp.sqrt(_one_minus_sq(a, log_a)) * (i * xc)

    a3 = a.reshape(ng, V7X_SUBLANES, w)
    b3 = bb.reshape(ng, V7X_SUBLANES, w)
    for d in (1, 2, 4):
        ok = sub >= d
        a_sh = pltpu.roll(a3, d, 1)
        b_sh = pltpu.roll(b3, d, 1)
        b3 = jnp.where(ok, a3 * b_sh + b3, b3)
        a3 = jnp.where(ok, a3 * a_sh, a3)
    carry = h_ref[0:1, :]
    groups = []
    for g in range(ng):
        hg = b3[g] + a3[g] * carry
        groups.append(hg)
        carry = hg[V7X_SUBLANES - 1:V7X_SUBLANES, :]
    hcur = jnp.concatenate(groups, axis=0)
    h_ref[...] = jnp.broadcast_to(carry, h_ref.shape)
    o_ref[0] = (hcur * _gelu_tanh(gate)).astype(o_ref.dtype)


def _merge_kernel(attn_ref, lru_ref, h_ref, x_ref, wm_ref, wa_ref, wl_ref, wo_ref, g2_ref,
                  x1_ref, h2_ref):
    d = x_ref.shape[1]
    h = h_ref[...]
    mg0 = _sigmoid(jnp.dot(h, wm_ref[:, 0:d], preferred_element_type=F32))
    mg1 = _sigmoid(jnp.dot(h, wm_ref[:, d:2 * d], preferred_element_type=F32))
    ya = jnp.dot(attn_ref[...], wa_ref[...], preferred_element_type=F32)
    yl = jnp.dot(lru_ref[...], wl_ref[...], preferred_element_type=F32)
    merged = mg0 * ya + mg1 * yl
    x1 = x_ref[...] + jnp.dot(merged.astype(MXU_DTYPE), wo_ref[...], preferred_element_type=F32)
    x1_ref[...] = x1
    ms = jnp.mean(x1 * x1, axis=-1, keepdims=True)
    h2_ref[...] = (x1 * lax.rsqrt(ms + EPS) * g2_ref[...]).astype(h2_ref.dtype)


def _merge(attn, lru, h2d, x2d, w_mg, wa, wl, wo, g2, tm=256):
    m, d = x2d.shape
    row = pl.BlockSpec((tm, d), lambda i: (i, 0))
    full = lambda a: pl.BlockSpec(a.shape, lambda i: (0, 0))
    g2 = g2.reshape(1, d)
    return pl.pallas_call(
        _merge_kernel,
        grid=(m // tm,),
        in_specs=[row, row, row, row, full(w_mg), full(wa), full(wl), full(wo), full(g2)],
        out_specs=[row, row],
        out_shape=[jax.ShapeDtypeStruct((m, d), F32), jax.ShapeDtypeStruct((m, d), MXU_DTYPE)],
        compiler_params=_params("parallel"),
        name="merge_out",
    )(attn, lru, h2d, x2d, w_mg, wa, wl, wo, g2)


def _ffn_kernel(h_ref, x1_ref, wg_ref, wu_ref, wd_ref, o_ref):
    @pl.when(pl.program_id(1) == 0)
    def _():
        o_ref[...] = x1_ref[...]

    h = h_ref[...]
    g = jnp.dot(h, wg_ref[...].astype(MXU_DTYPE), preferred_element_type=F32)
    u = jnp.dot(h, wu_ref[...].astype(MXU_DTYPE), preferred_element_type=F32)
    act = (g * _sigmoid(g) * u).astype(MXU_DTYPE)
    o_ref[...] += jnp.dot(act, wd_ref[...].astype(MXU_DTYPE), preferred_element_type=F32)


def _ffn(h2, x1, wg, wu, wd, tm=1024, tf=256):
    m, d = x1.shape
    f = wg.shape[1]
    return pl.pallas_call(
        _ffn_kernel,
        grid=(m // tm, f // tf),
        in_specs=[pl.BlockSpec((tm, d), lambda i, j: (i, 0)),
                  pl.BlockSpec((tm, d), lambda i, j: (i, 0)),
                  pl.BlockSpec((d, tf), lambda i, j: (0, j)),
                  pl.BlockSpec((d, tf), lambda i, j: (0, j)),
                  pl.BlockSpec((tf, d), lambda i, j: (j, 0))],
        out_specs=pl.BlockSpec((tm, d), lambda i, j: (i, 0)),
        out_shape=jax.ShapeDtypeStruct((m, d), F32),
        compiler_params=_params("parallel", "arbitrary"),
        name="swiglu_ffn",
    )(h2, x1, wg, wu, wd)


def _pad_last(a, n):
    return jnp.pad(a, [(0, 0)] * (a.ndim - 1) + [(0, n - a.shape[-1])])


def _layer(x, norm1_g, w_in, q_norm_g, k_norm_g, cmp_pos_k, cmp_w1_k, cmp_b1_k, cmp_w2_k, cmp_b2_k,
           cmp_pos_v, cmp_w1_v, cmp_b1_v, cmp_w2_v, cmp_b2_v, conv_w, conv_b, lru_wa, lru_ba,
           lru_wi, lru_bi, lru_lambda, w_o_attn, w_o_lru, w_out, norm2_g, w_gate, w_up, w_down):
    b, t, d = x.shape
    m = b * t
    attn_dim = N_HEADS * HEAD_DIM
    kv_dim = N_KV * HEAD_DIM
    lru_w = lru_lambda.shape[0]
    assert t % 512 == 0
    o1 = attn_dim
    o2 = o1 + 6 * kv_dim
    o3 = o2 + 3 * N_HEADS
    o4 = o3 + lru_w
    o5 = o4 + lru_w
    cast = lambda a: a.astype(MXU_DTYPE)

    wq_t = cast(w_in[:, :o1].T)
    w_kv = w_in[:, o1:o2].reshape(d, 6, N_KV, HEAD_DIM)
    w_cmp_src = cast(w_kv[:, 0:2].reshape(d, 2 * kv_dim))
    w_k2 = cast(jnp.stack([w_kv[:, 2], w_kv[:, 4]], axis=1).reshape(d, 2 * kv_dim))
    k_gain2 = jnp.stack([jnp.tile(k_norm_g[1], N_KV), jnp.tile(k_norm_g[2], N_KV)])
    lane_grp = np.arange(kv_dim) // HEAD_DIM
    grp_avg = cast(jnp.asarray((lane_grp[:, None] == lane_grp[None, :]) / HEAD_DIM, F32))
    place_np = np.zeros((kv_dim, N_KV * FEAT), np.float32)
    place_np[np.arange(kv_dim), lane_grp * FEAT + np.arange(kv_dim) % HEAD_DIM] = 1.0
    place = cast(jnp.asarray(place_np))
    w_v = jnp.stack([w_kv[:, 3], w_kv[:, 5]], axis=1)
    w_v_t = cast(_pad_last(w_v, V_ROWS).reshape(d, 2 * N_KV * V_ROWS).T)
    v_ones_col = jnp.tile(jnp.arange(V_ROWS) == HEAD_DIM, 2 * N_KV).astype(F32).reshape(-1, 1)
    w_g = w_in[:, o2:o3].reshape(d, N_KV, HPG, 3).transpose(0, 1, 3, 2).reshape(d, N_KV, 3 * HPG)
    w_g_t = cast(_pad_last(w_g, 16).reshape(d, N_KV * 16).T)
    w_lru = cast(w_in[:, o3:o5])
    w_mg = cast(w_in[:, o5:])
    q_gain_col = jnp.tile(q_norm_g * (HEAD_DIM ** -0.5 * LOG2E), N_HEADS).reshape(attn_dim, 1)
    gain_pad = lambda g: _pad_last(g.reshape(1, HEAD_DIM), FEAT)

    per_slab = V7X_MXU_DIM // (lru_w // LRU_BLOCKS)
    eye = jnp.eye(per_slab, dtype=F32)
    slabs = lambda wgt: cast(jnp.einsum('snkj,nm->snkmj', wgt.reshape(LRU_BLOCKS // per_slab, per_slab,
                                                                       *wgt.shape[1:]), eye)
                             .reshape(LRU_BLOCKS // per_slab, V7X_MXU_DIM, V7X_MXU_DIM))
    vec = lambda v: v.reshape(1, lru_w)

    h3d, q_t, v_t, gates_t, k_sel, k_win, cmp_src, lru = _in_proj(
        x,
        (norm1_g.reshape(1, d), wq_t, q_gain_col, w_v_t, v_ones_col, w_g_t, w_k2, k_gain2, grp_avg, place,
         w_cmp_src),
        (w_lru, conv_w.reshape(CONV_W, lru_w), vec(conv_b), slabs(lru_wa), vec(lru_ba), slabs(lru_wi),
         vec(lru_bi), vec(lru_lambda)))
    h2d = h3d.reshape(m, d)

    k_cmp = _compress(cmp_src, 0, cmp_pos_k, cmp_w1_k, cmp_b1_k, cast(_pad_last(cmp_w2_k, FEAT)),
                      _pad_last(cmp_b2_k.reshape(1, HEAD_DIM), FEAT), gain_pad(k_norm_g[0]))
    v_cmp_t = _compress(cmp_src, N_KV // CMP_GROUPS, cmp_pos_v, cmp_w1_v, cmp_b1_v,
                        cast(_pad_last(cmp_w2_v, V_ROWS).T),
                        jnp.concatenate([cmp_b2_v, v_ones_col[HEAD_DIM:V_ROWS, 0]]).reshape(V_ROWS, 1))

    attn = _attention_static(q_t, gates_t, k_cmp, v_cmp_t, k_sel, k_win, v_t)

    x1, h2 = _merge(attn.reshape(m, attn_dim), lru.reshape(m, lru_w), h2d, x.reshape(m, d), w_mg,
                    cast(w_o_attn), cast(w_o_lru), cast(w_out), norm2_g)
    out = _ffn(h2, x1, w_gate, w_up, w_down)
    return out.reshape(b, t, d)


def kernel(x, norm1_g, w_in, q_norm_g, k_norm_g, cmp_pos_k, cmp_w1_k, cmp_b1_k, cmp_w2_k, cmp_b2_k,
           cmp_pos_v, cmp_w1_v, cmp_b1_v, cmp_w2_v, cmp_b2_v, conv_w, conv_b, lru_wa, lru_ba,
           lru_wi, lru_bi, lru_lambda, w_o_attn, w_o_lru, w_out, norm2_g, w_gate, w_up, w_down):
    for l in range(norm1_g.shape[0]):
        x = _layer(x, norm1_g[l], w_in[l], q_norm_g[l], k_norm_g[l], cmp_pos_k[l], cmp_w1_k[l],
                   cmp_b1_k[l], cmp_w2_k[l], cmp_b2_k[l], cmp_pos_v[l], cmp_w1_v[l], cmp_b1_v[l],
                   cmp_w2_v[l], cmp_b2_v[l], conv_w[l], conv_b[l], lru_wa[l], lru_ba[l], lru_wi[l],
                   lru_bi[l], lru_lambda[l], w_o_attn[l], w_o_lru[l], w_out[l], norm2_g[l],
                   w_gate[l], w_up[l], w_down[l])
    return x
```

```python
import functools

import numpy as np
import jax
import jax.numpy as jnp
from jax import lax
from jax.experimental import pallas as pl
from jax.experimental.pallas import tpu as pltpu

N_HEADS = 16
HEAD_DIM = 64
N_KV = 4
HPG = N_HEADS // N_KV
CMP_BLOCK = 32
CMP_STRIDE = 16
CMP_HIDDEN = 256
SEL_BLOCK = 64
N_SEL = 16
WINDOW = 512
FORCE_BONUS = 1e4
LRU_BLOCKS = 16
CONV_W = 4
LRU_C = 8.0
EPS = 1e-6

MXU_DTYPE = jnp.bfloat16
F32 = jnp.float32

V7X_LANES = 128
V7X_SUBLANES = 8
V7X_MXU_DIM = 256
V7X_VMEM_LIMIT_BYTES = 48 * 1024 * 1024

TQ = 256
TK_SEL = 256
TILES_PER_CALL = 2
V_ROWS = 80
LOG2E = 1.4426950408889634
FEAT = V7X_MXU_DIM
F_SEL = HEAD_DIM
F_POS = 2 * HEAD_DIM
F_CMP = F_POS + 6
NEG_MASK = -1e30
NEG_BLOCK = -(2.0 ** 100)


def _params(*sem):
    return pltpu.CompilerParams(dimension_semantics=sem, vmem_limit_bytes=V7X_VMEM_LIMIT_BYTES)


def _gelu_tanh(x):
    return 0.5 * x * (1.0 + jnp.tanh(0.7978845608028654 * (x + 0.044715 * (x * x * x))))


def _sigmoid(x):
    return 1.0 / (1.0 + jnp.exp(-x))


def _one_minus_sq(a, log_a):
    series = (-2.0 * log_a) * (1.0 + log_a * (1.0 + log_a * (2.0 / 3)))
    return jnp.where(log_a > -1.0 / 128, series, 1.0 - a * a)


def _nt_dot(wt, h):
    return lax.dot_general(wt, h, (((1,), (1,)), ((), ())), preferred_element_type=F32)


def _token_features(pos, col, with_block_mask):
    blk = pos >> 6
    off = pos & (SEL_BLOCK - 1)
    feat = jnp.where((col >= F_POS) & (col < F_POS + 3), blk.astype(F32),
                     jnp.where((col >= F_POS + 3) & (col < F_POS + 6), off.astype(F32), 0.0))
    if with_block_mask:
        feat = jnp.where((col >= F_SEL) & (col - F_SEL == blk) & (col < F_POS), NEG_BLOCK, feat)
    return feat


def _in_proj_kernel(x_ref, g1_ref, wq_ref, qg_ref, wv_ref, vb_ref, wg_ref, wk_ref, kg_ref, grp_ref,
                    place_ref, wc_ref, wl_ref, cw_ref, cb_ref, wa_ref, ba_ref, wi_ref, bi_ref, lam_ref,
                    h_ref, qt_ref, vt_ref, gt_ref, ks_ref, kw_ref, cs_ref, lru_ref, tail_ref, state_ref):
    x = x_ref[0]
    tm = x.shape[0]
    ms = jnp.mean(x * x, axis=-1, keepdims=True)
    h = (x * lax.rsqrt(ms + EPS) * g1_ref[...]).astype(MXU_DTYPE)
    h_ref[0] = h

    lru_mid = _lru_gates(h, wl_ref, cw_ref, cb_ref, wa_ref, wi_ref, tail_ref, state_ref)

    r3 = _nt_dot(wq_ref[...], h).reshape(N_HEADS, HEAD_DIM, tm)
    qn = r3 * lax.rsqrt(jnp.mean(r3 * r3, axis=1, keepdims=True) + EPS)
    qt_ref[0] = (qn.reshape(N_HEADS * HEAD_DIM, tm) * qg_ref[...]).astype(qt_ref.dtype)

    vt_ref[0] = (_nt_dot(wv_ref[...], h) + vb_ref[...]).astype(vt_ref.dtype)
    gt_ref[0] = _sigmoid(_nt_dot(wg_ref[...], h))

    k = jnp.dot(h, wk_ref[...], preferred_element_type=F32)
    kk = k * k
    kk_hi = kk.astype(MXU_DTYPE)
    kk_lo = (kk - kk_hi.astype(F32)).astype(MXU_DTYPE)
    pos = pl.program_id(1) * tm + lax.broadcasted_iota(jnp.int32, (tm, FEAT), 0)
    col = lax.broadcasted_iota(jnp.int32, (tm, FEAT), 1)
    kv_dim = N_KV * HEAD_DIM
    for branch, (o_ref, with_block_mask) in enumerate(((ks_ref, True), (kw_ref, False))):
        sl = slice(branch * kv_dim, (branch + 1) * kv_dim)
        msq = (jnp.dot(kk_hi[:, sl], grp_ref[...], preferred_element_type=F32)
               + jnp.dot(kk_lo[:, sl], grp_ref[...], preferred_element_type=F32))
        kn = (k[:, sl] * lax.rsqrt(msq + EPS) * kg_ref[branch:branch + 1, :]).astype(MXU_DTYPE)
        placed = jnp.dot(kn, place_ref[...], preferred_element_type=F32)
        feat = _token_features(pos, col, with_block_mask)
        for g in range(N_KV):
            o_ref[0, g] = (placed[:, g * FEAT:(g + 1) * FEAT] + feat).astype(o_ref.dtype)

    cs_ref[0] = jnp.dot(h, wc_ref[...], preferred_element_type=F32)

    _lru_scan(*lru_mid, ba_ref, bi_ref, lam_ref, lru_ref, state_ref)


def _in_proj(x, attn_weights, lru_weights, tm=512):
    b, t, d = x.shape
    kv_dim = N_KV * HEAD_DIM
    wq_t, w_v_t, w_g_t = attn_weights[1], attn_weights[3], attn_weights[5]
    lru_w = lru_weights[-1].shape[1]
    full = lambda a: pl.BlockSpec(a.shape, lambda bi, i: (0,) * a.ndim)
    rowblk = lambda n: pl.BlockSpec((1, tm, n), lambda bi, i: (bi, i, 0))
    colblk = lambda n: pl.BlockSpec((1, n, tm), lambda bi, i: (bi, 0, i))
    kblk = pl.BlockSpec((1, N_KV, tm, FEAT), lambda bi, i: (bi, 0, i, 0))
    weights = tuple(attn_weights) + tuple(lru_weights)
    return pl.pallas_call(
        _in_proj_kernel,
        grid=(b, t // tm),
        in_specs=[rowblk(d)] + [full(a) for a in weights],
        out_specs=[rowblk(d), colblk(wq_t.shape[0]), colblk(w_v_t.shape[0]), colblk(w_g_t.shape[0]),
                   kblk, kblk, rowblk(2 * kv_dim), rowblk(lru_w)],
        out_shape=[jax.ShapeDtypeStruct((b, t, d), MXU_DTYPE),
                   jax.ShapeDtypeStruct((b, wq_t.shape[0], t), MXU_DTYPE),
                   jax.ShapeDtypeStruct((b, w_v_t.shape[0], t), MXU_DTYPE),
                   jax.ShapeDtypeStruct((b, w_g_t.shape[0], t), F32),
                   jax.ShapeDtypeStruct((b, N_KV, t, FEAT), MXU_DTYPE),
                   jax.ShapeDtypeStruct((b, N_KV, t, FEAT), MXU_DTYPE),
                   jax.ShapeDtypeStruct((b, t, 2 * kv_dim), F32),
                   jax.ShapeDtypeStruct((b, t, lru_w), MXU_DTYPE)],
        scratch_shapes=[pltpu.VMEM((V7X_SUBLANES, lru_w), F32),
                        pltpu.VMEM((V7X_SUBLANES, lru_w), F32)],
        compiler_params=_params("parallel", "arbitrary"),
        name="in_proj_lru",
    )(x, *weights)


CMP_GROUPS = V7X_LANES // HEAD_DIM


def _cmp_hidden(src_ref, pos_ref, w1_ref, b1_ref):
    ncp = src_ref.shape[1] // CMP_STRIDE
    first = jnp.zeros((ncp, CMP_GROUPS * CMP_HIDDEN), F32)
    second = jnp.zeros((ncp, CMP_GROUPS * CMP_HIDDEN), F32)
    for l in range(CMP_STRIDE):
        x = src_ref[0, pl.ds(l, ncp, stride=CMP_STRIDE), :]
        lo = (x + pos_ref[l:l + 1, :]).astype(MXU_DTYPE)
        hi = (x + pos_ref[CMP_STRIDE + l:CMP_STRIDE + l + 1, :]).astype(MXU_DTYPE)
        first = first + jnp.dot(lo, w1_ref[l], preferred_element_type=F32)
        second = second + jnp.dot(hi, w1_ref[CMP_STRIDE + l], preferred_element_type=F32)
    hid = first + pltpu.roll(second, ncp - 1, 0) + b1_ref[...]
    return _gelu_tanh(hid).astype(MXU_DTYPE)


def _cmp_k_kernel(src_ref, pos_ref, w1_ref, b1_ref, w2_ref, b2_ref, g_ref, o_ref):
    hid = _cmp_hidden(src_ref, pos_ref, w1_ref, b1_ref)
    ncp = hid.shape[0]
    idx = lax.broadcasted_iota(jnp.int32, (ncp, FEAT), 0)
    col = lax.broadcasted_iota(jnp.int32, (ncp, FEAT), 1)
    feat = jnp.where((col >= F_CMP) & (col < F_CMP + 3), (idx >> 6).astype(F32),
                     jnp.where((col >= F_CMP + 3) & (col < F_CMP + 6), (idx & 63).astype(F32), 0.0))
    for gl in range(CMP_GROUPS):
        r = jnp.dot(hid[:, gl * CMP_HIDDEN:(gl + 1) * CMP_HIDDEN], w2_ref[...],
                    preferred_element_type=F32) + b2_ref[...]
        ms = jnp.sum(r * r, axis=-1, keepdims=True) * (1.0 / HEAD_DIM)
        o_ref[0, gl] = (r * lax.rsqrt(ms + EPS) * g_ref[...] + feat).astype(o_ref.dtype)


def _cmp_v_kernel(src_ref, pos_ref, w1_ref, b1_ref, w2t_ref, b2_ref, o_ref):
    hid = _cmp_hidden(src_ref, pos_ref, w1_ref, b1_ref)
    for gl in range(CMP_GROUPS):
        r = _nt_dot(w2t_ref[...], hid[:, gl * CMP_HIDDEN:(gl + 1) * CMP_HIDDEN]) + b2_ref[...]
        o_ref[0, gl] = r.astype(o_ref.dtype)


def _compress(cmp_src, lane_block0, pos, w1, b1, w2, b2, gain_pad=None):
    b, t, _ = cmp_src.shape
    ncp = t // CMP_STRIDE
    hid_w = CMP_GROUPS * CMP_HIDDEN
    eye = jnp.eye(CMP_GROUPS, dtype=w1.dtype)
    w1_bd = jnp.einsum('ldf,gh->lgdhf', w1, eye).reshape(CMP_BLOCK, V7X_LANES, hid_w).astype(MXU_DTYPE)
    pos_t = jnp.tile(pos, (1, CMP_GROUPS))
    b1_t = jnp.tile(b1.reshape(1, CMP_HIDDEN), (1, CMP_GROUPS))
    full = lambda a: pl.BlockSpec(a.shape, lambda bi, p: (0,) * a.ndim)
    src_spec = pl.BlockSpec((1, t, V7X_LANES), lambda bi, p: (bi, 0, lane_block0 + p))
    grid = (b, N_KV // CMP_GROUPS)
    if gain_pad is not None:
        args = (pos_t, w1_bd, b1_t, w2, b2, gain_pad)
        return pl.pallas_call(
            _cmp_k_kernel,
            grid=grid,
            in_specs=[src_spec] + [full(a) for a in args],
            out_specs=pl.BlockSpec((1, CMP_GROUPS, ncp, FEAT), lambda bi, p: (bi, p, 0, 0)),
            out_shape=jax.ShapeDtypeStruct((b, N_KV, ncp, FEAT), MXU_DTYPE),
            compiler_params=_params("parallel", "parallel"),
            name="compress_k",
        )(cmp_src, *args)
    args = (pos_t, w1_bd, b1_t, w2, b2)
    return pl.pallas_call(
        _cmp_v_kernel,
        grid=grid,
        in_specs=[src_spec] + [full(a) for a in args],
        out_specs=pl.BlockSpec((1, CMP_GROUPS, V_ROWS, ncp), lambda bi, p: (bi, p, 0, 0)),
        out_shape=jax.ShapeDtypeStruct((b, N_KV, V_ROWS, ncp), MXU_DTYPE),
        compiler_params=_params("parallel", "parallel"),
        name="compress_v",
    )(cmp_src, *args)


def _split3(v):
    parts = []
    rest = np.asarray(v, np.float64)
    for _ in range(3):
        p = rest.astype(np.float32).astype(jnp.bfloat16).astype(np.float64)
        parts.append(p)
        rest = rest - p
    return parts


def _alibi_query_features():
    tab = np.zeros((N_KV, FEAT - F_POS, HPG * TQ), np.float64)
    for g in range(N_KV):
        for h in range(HPG):
            slope = 2.0 ** (-8.0 * (g * HPG + h + 1) / N_HEADS)
            parts = _split3(slope * LOG2E)
            lanes = slice(h * TQ, (h + 1) * TQ)
            for i, p in enumerate(parts):
                tab[g, i, lanes] = SEL_BLOCK * p
                tab[g, 3 + i, lanes] = p
                tab[g, 6 + i, lanes] = CMP_STRIDE * 64 * p
                tab[g, 9 + i, lanes] = CMP_STRIDE * p
    return jnp.asarray(tab, F32).astype(MXU_DTYPE)


def _block_map_t(n_cmp_pad, n_blk):
    cs = np.arange(n_cmp_pad) * CMP_STRIDE
    ce = cs + CMP_BLOCK - 1
    bs = np.arange(n_blk) * SEL_BLOCK
    be = bs + SEL_BLOCK - 1
    return jnp.asarray(((cs[None, :] <= be[:, None]) & (ce[None, :] >= bs[:, None])).astype(np.float32))


def _prob(s, m):
    return jnp.exp2(s - m).astype(MXU_DTYPE)


def _attn_tiles_kernel(*refs, tiles, win_blk0, n_blk, n_sel):
    (qt_ref, gate_ref, kc_ref, vc_ref, ks_ref, vs_ref, kd_ref, vd_ref, kwa_ref, kwb_ref, vwa_ref, vwb_ref,
     alibi_ref, map_ref) = refs[:14]
    o_ref, qb_ref, qs_ref, imp_ref = refs[15:]
    lanes = HPG * TQ
    ncp = kc_ref.shape[2]
    blocks_per_tile = TQ // SEL_BLOCK
    win_tiles = WINDOW // TQ
    lane_tok = lax.broadcasted_iota(jnp.int32, (1, lanes), 1) & (TQ - 1)
    row_pos = lax.broadcasted_iota(jnp.int32, (TQ, lanes), 0)
    blk = lax.broadcasted_iota(jnp.int32, (n_blk, TQ), 0)

    def window_tile(qi, j):
        n = len(tiles)
        lt = max(qi - win_tiles, 0) + j - n * win_blk0
        kref, vref = (kwa_ref, vwa_ref) if lt < n else (kwb_ref, vwb_ref)
        sl = slice((lt % n) * TQ, (lt % n + 1) * TQ)
        return kref[0, 0, sl, :], vref[0, :, sl]

    def head(ti, qi):
        cols = slice(ti * TQ, (ti + 1) * TQ)
        t_lane = qi * TQ + lane_tok
        for h in range(HPG):
            qb_ref[ti, 0:HEAD_DIM, h * TQ:(h + 1) * TQ] = qt_ref[0, h * HEAD_DIM:(h + 1) * HEAD_DIM, cols]
        qb_ref[ti, F_SEL:F_POS, :] = jnp.zeros((F_POS - F_SEL, lanes), qb_ref.dtype)
        qb_ref[ti, F_POS:FEAT, :] = alibi_ref[0]
        qb = qb_ref[ti]
        sc = jnp.dot(kc_ref[0, 0], qb, preferred_element_type=F32)
        sw = [jnp.dot(window_tile(qi, j)[0], qb, preferred_element_type=F32) for j in range(3)]
        sd = jnp.dot(kd_ref[0, 0, cols, :], qb, preferred_element_type=F32)
        last_cmp = (t_lane - (CMP_BLOCK - 1)) >> 4
        sc = jnp.where(lax.broadcasted_iota(jnp.int32, (ncp, lanes), 0) <= last_cmp, sc, NEG_MASK)
        ec = jnp.exp2(sc - jnp.max(sc, axis=0, keepdims=True))
        acc_c = jnp.dot(vc_ref[0, 0], ec.astype(MXU_DTYPE), preferred_element_type=F32)
        inv_c = jnp.where(last_cmp >= 0, 1.0 / jnp.maximum(acc_c[HEAD_DIM:HEAD_DIM + 1], 1e-30), 0.0)
        imp = None
        if (qi + 1) * blocks_per_tile > n_sel and qi > 0:
            psum = ec[:, 0:TQ] * inv_c[:, 0:TQ]
            for h in range(1, HPG):
                psum = psum + ec[:, h * TQ:(h + 1) * TQ] * inv_c[:, h * TQ:(h + 1) * TQ]
            imp = jnp.dot(map_ref[...], psum, preferred_element_type=F32)
        return dict(qb=qb, t_lane=t_lane, sw=sw, sd=sd, o_cmp=acc_c[0:HEAD_DIM] * inv_c, imp=imp)

    def rank_init(ti, qi, st):
        first_own_blk = qi * blocks_per_tile
        st.update(ranks=[], k_done=0, k_total=0)
        if st["imp"] is None:
            return
        cur = (qi * TQ + lax.broadcasted_iota(jnp.int32, (n_blk, TQ), 1)) >> 6
        forced = (blk == 0) | (blk == cur) | (blk == cur - 1)
        imp = jnp.where(blk <= cur, st["imp"] + jnp.where(forced, FORCE_BONUS, 0.0), NEG_MASK)
        imp_ref[ti] = imp
        n_rank_chunks = -(-first_own_blk // V7X_SUBLANES)
        st["chunks"] = [imp[c * V7X_SUBLANES:(c + 1) * V7X_SUBLANES] for c in range(n_rank_chunks)]
        st["ranks"] = [jnp.zeros((V7X_SUBLANES, TQ), jnp.int32) for _ in range(n_rank_chunks)]
        st["k_total"] = min(first_own_blk + blocks_per_tile, n_blk)

    def rank_rounds(ti, st, n):
        sub = lax.broadcasted_iota(jnp.int32, (V7X_SUBLANES, TQ), 0)
        stop = min(st["k_done"] + n, st["k_total"])
        for k in range(st["k_done"], stop):
            row = imp_ref[ti, k:k + 1, :]
            for c, mine in enumerate(st["chunks"]):
                lo = c * V7X_SUBLANES
                if lo > k:
                    one = jnp.where(row >= mine, 1, 0)
                elif lo + V7X_SUBLANES - 1 <= k:
                    one = jnp.where(row > mine, 1, 0)
                else:
                    one = jnp.where(sub + lo > k, jnp.where(row >= mine, 1, 0), jnp.where(row > mine, 1, 0))
                st["ranks"][c] = st["ranks"][c] + one
        st["k_done"] = stop

    def chunk_setup(ti, qi, st):
        rank_rounds(ti, st, st["k_total"])
        first_own_blk = qi * blocks_per_tile
        n_chunks = -(-qi * TQ // TK_SEL)
        st.update(n_chunks=n_chunks, m=jnp.full((1, lanes), NEG_MASK, F32), acc=jnp.zeros((V_ROWS, lanes), F32),
                  p_prev=None)
        if n_chunks == 0:
            return
        rank = jnp.zeros((n_blk, TQ), jnp.int32)
        if st["ranks"]:
            pad = [jnp.zeros((n_blk - len(st["ranks"]) * V7X_SUBLANES, TQ), jnp.int32)]
            rank = jnp.concatenate(st["ranks"] + (pad if pad[0].shape[0] else []), axis=0)
        not_sel = jnp.where((rank < n_sel) & (blk < first_own_blk), 0.0, 1.0).astype(qs_ref.dtype)
        qs_ref[ti] = st["qb"]
        for h in range(HPG):
            qs_ref[ti, F_SEL:F_SEL + n_blk, h * TQ:(h + 1) * TQ] = not_sel
        st["qs"] = qs_ref[ti]
        st["s_next"] = chunk_qk(st, 0)

    def chunk_qk(st, k):
        return jnp.dot(ks_ref[0, 0, k * TK_SEL:(k + 1) * TK_SEL, :], st["qs"], preferred_element_type=F32)

    def chunk_pv(st, k):
        return jnp.dot(vs_ref[0, :, k * TK_SEL:(k + 1) * TK_SEL], st["p_prev"], preferred_element_type=F32)

    def chunk_stage(st, k):
        s_cur = st["s_next"]
        if k + 1 < st["n_chunks"]:
            st["s_next"] = chunk_qk(st, k + 1)
        if st["p_prev"] is not None:
            st["acc"] = st["acc"] + chunk_pv(st, k - 1)
        m_new = jnp.maximum(st["m"], jnp.max(s_cur, axis=0, keepdims=True))
        st["p_prev"] = _prob(s_cur, m_new)
        st["acc"] = jnp.exp2(st["m"] - m_new) * st["acc"]
        st["m"] = m_new

    def chunk_finish(st):
        if st["n_chunks"] > 0:
            st["acc"] = st["acc"] + chunk_pv(st, st["n_chunks"] - 1)

    def diagonal(ti, qi, st):
        cols = slice(ti * TQ, (ti + 1) * TQ)
        st["causal"] = qi * TQ + row_pos <= st["t_lane"]
        sd = jnp.where(st["causal"], st["sd"], NEG_MASK)
        m_d = jnp.max(sd, axis=0, keepdims=True)
        acc_d = jnp.dot(vd_ref[0, :, cols], _prob(sd, m_d), preferred_element_type=F32)
        m_all = jnp.maximum(st["m"], m_d)
        acc_s = jnp.exp2(st["m"] - m_all) * st["acc"] + jnp.exp2(m_d - m_all) * acc_d
        st["o_sel"] = acc_s[0:HEAD_DIM] * (1.0 / acc_s[HEAD_DIM:HEAD_DIM + 1])

    def window(ti, qi, st):
        sw, t_lane = st["sw"], st["t_lane"]
        wb = max(qi - win_tiles, 0)
        if qi >= win_tiles:
            d0 = (t_lane - wb * TQ) - row_pos
            sw = [jnp.where(d0 < WINDOW, sw[0], NEG_MASK), sw[1], jnp.where(st["causal"], sw[2], NEG_MASK)]
        else:
            for j in range(3):
                dj = (t_lane - (wb + j) * TQ) - row_pos
                sw[j] = jnp.where(lax.bitcast_convert_type(dj, jnp.uint32) < WINDOW, sw[j], NEG_MASK)
        m_w = jnp.max(jnp.maximum(jnp.maximum(sw[0], sw[1]), sw[2]), axis=0, keepdims=True)
        acc_w = jnp.zeros((V_ROWS, lanes), F32)
        for j in range(3):
            acc_w = acc_w + jnp.dot(window_tile(qi, j)[1], _prob(sw[j], m_w), preferred_element_type=F32)
        st["o_win"] = acc_w[0:HEAD_DIM] * (1.0 / acc_w[HEAD_DIM:HEAD_DIM + 1])

    def output(ti, st):
        cols = slice(ti * TQ, (ti + 1) * TQ)
        gates = gate_ref[0, :, cols]
        def gate_row(j):
            return jnp.concatenate([gates[j * HPG + h:j * HPG + h + 1, :] for h in range(HPG)], axis=1)
        o_t = gate_row(0) * st["o_cmp"] + gate_row(1) * st["o_sel"] + gate_row(2) * st["o_win"]
        for hp in range(HPG // 2):
            pair = jnp.concatenate([o_t[:, (2 * hp) * TQ:(2 * hp + 1) * TQ],
                                    o_t[:, (2 * hp + 1) * TQ:(2 * hp + 2) * TQ]], axis=0)
            o_ref[0, cols, hp * 2 * HEAD_DIM:(hp + 1) * 2 * HEAD_DIM] = pair.T.astype(o_ref.dtype)

    def tail_phases(ti, qi, st):
        return [lambda: diagonal(ti, qi, st), lambda: window(ti, qi, st), lambda: output(ti, st)]

    cur = head(0, tiles[0])
    rank_init(0, tiles[0], cur)
    chunk_setup(0, tiles[0], cur)
    pending = []
    for ti, qi in enumerate(tiles):
        nxt = None
        if ti + 1 < len(tiles):
            nxt = head(ti + 1, tiles[ti + 1])
            rank_init(ti + 1, tiles[ti + 1], nxt)
            rounds_per_stage = -(-nxt["k_total"] // max(cur["n_chunks"], 1))
        for k in range(cur["n_chunks"]):
            chunk_stage(cur, k)
            if nxt is not None:
                rank_rounds(ti + 1, nxt, rounds_per_stage)
            if pending:
                pending.pop(0)()
        for phase in pending:
            phase()
        chunk_finish(cur)
        pending = tail_phases(ti, qi, cur)
        if nxt is not None:
            chunk_setup(ti + 1, tiles[ti + 1], nxt)
            cur = nxt
    for phase in pending:
        phase()


def _attention_static(q_t, gates_t, k_cmp, v_cmp_t, k_sel, k_win, v_t):
    b, _, t = q_t.shape
    n_blk = t // SEL_BLOCK
    tiles_per_call = TILES_PER_CALL
    tb = tiles_per_call * TQ
    assert tb >= WINDOW and t % max(tb, TK_SEL) == 0 and t >= 2 * tb and F_SEL + n_blk <= F_POS, \
        "unsupported sequence length"
    n_sel = min(N_SEL, n_blk)
    lanes = HPG * TQ
    rows = HPG * HEAD_DIM
    alibi = _alibi_query_features()
    out_shape = jax.ShapeDtypeStruct((b, t, N_HEADS * HEAD_DIM), MXU_DTYPE)

    attn = jnp.zeros(out_shape.shape, out_shape.dtype)
    for m in range(t // tb):
        tiles = tuple(range(m * tiles_per_call, (m + 1) * tiles_per_call))
        ncp = min(t // CMP_STRIDE, -(-((m + 1) * tb // CMP_STRIDE) // V7X_LANES) * V7X_LANES)
        blk_map_t = _block_map_t(ncp, n_blk)
        kc = -(-tiles[-1] * TQ // TK_SEL) * TK_SEL
        wb0 = max(m - 1, 0)
        in_specs = [
            pl.BlockSpec((1, rows, tb), lambda bi, g, m=m: (bi, g, m)),
            pl.BlockSpec((1, 16, tb), lambda bi, g, m=m: (bi, g, m)),
            pl.BlockSpec((1, 1, ncp, FEAT), lambda bi, g: (bi, g, 0, 0)),
            pl.BlockSpec((1, 1, V_ROWS, ncp), lambda bi, g: (bi, g, 0, 0)),
            pl.BlockSpec((1, 1, kc, FEAT), lambda bi, g: (bi, g, 0, 0)),
            pl.BlockSpec((1, V_ROWS, kc), lambda bi, g: (bi, g, 0)),
            pl.BlockSpec((1, 1, tb, FEAT), lambda bi, g, m=m: (bi, g, m, 0)),
            pl.BlockSpec((1, V_ROWS, tb), lambda bi, g, m=m: (bi, g, m)),
            pl.BlockSpec((1, 1, tb, FEAT), lambda bi, g, w=wb0: (bi, g, w, 0)),
            pl.BlockSpec((1, 1, tb, FEAT), lambda bi, g, w=wb0: (bi, g, w + 1, 0)),
            pl.BlockSpec((1, V_ROWS, tb), lambda bi, g, w=wb0: (bi, N_KV + g, w)),
            pl.BlockSpec((1, V_ROWS, tb), lambda bi, g, w=wb0: (bi, N_KV + g, w + 1)),
            pl.BlockSpec((1, FEAT - F_POS, lanes), lambda bi, g: (g, 0, 0)),
            pl.BlockSpec((n_blk, ncp), lambda bi, g: (0, 0)),
        ]
        in_specs.append(pl.BlockSpec(memory_space=pl.ANY))
        args = [q_t, gates_t, k_cmp, v_cmp_t, k_sel, v_t, k_sel, v_t, k_win, k_win, v_t, v_t, alibi, blk_map_t,
                attn]
        kernel = functools.partial(_attn_tiles_kernel, tiles=tiles, win_blk0=wb0, n_blk=n_blk, n_sel=n_sel)
        attn = pl.pallas_call(
            kernel,
            grid=(b, N_KV),
            in_specs=in_specs,
            out_specs=pl.BlockSpec((1, tb, rows), lambda bi, g, m=m: (bi, m, g)),
            out_shape=out_shape,
            scratch_shapes=[pltpu.VMEM((tiles_per_call, FEAT, lanes), MXU_DTYPE),
                            pltpu.VMEM((tiles_per_call, FEAT, lanes), MXU_DTYPE),
                            pltpu.VMEM((tiles_per_call, n_blk, TQ), F32)],
            input_output_aliases={len(args) - 1: 0},
            compiler_params=_params("parallel", "parallel"),
            name=f"nsa_attention_{m}",
        )(*args)
    return attn


def _lru_gates(hin, wl_ref, cw_ref, cb_ref, wa_ref, wi_ref, tail_ref, h_ref):
    tt = hin.shape[0]
    w = cb_ref.shape[1]

    @pl.when(pl.program_id(1) == 0)
    def _():
        tail_ref[...] = jnp.zeros_like(tail_ref)
        h_ref[...] = jnp.zeros_like(h_ref)

    def gate_matmul(xb, w_ref):
        n = w_ref.shape[1]
        return jnp.concatenate([jnp.dot(xb[:, j * n:(j + 1) * n], w_ref[j], preferred_element_type=F32)
                                for j in range(w_ref.shape[0])], axis=1)

    x = jnp.dot(hin, wl_ref[:, 0:w], preferred_element_type=F32)
    gate = jnp.dot(hin, wl_ref[:, w:2 * w], preferred_element_type=F32)
    ng = tt // V7X_SUBLANES
    sub = lax.broadcasted_iota(jnp.int32, (ng, V7X_SUBLANES, w), 1)
    x3 = x.reshape(ng, V7X_SUBLANES, w)
    xprev3 = jnp.concatenate([tail_ref[...][None], x3], axis=0)
    tail_ref[...] = x[tt - V7X_SUBLANES:tt]
    xc = x * cw_ref[CONV_W - 1:CONV_W, :] + cb_ref[...]
    for s in range(1, CONV_W):
        rot = pltpu.roll(xprev3, s, 1)
        xs = jnp.where(sub >= s, rot[1:], rot[:-1])
        xc = xc + xs.reshape(tt, w) * cw_ref[CONV_W - 1 - s:CONV_W - s, :]

    xb = xc.astype(MXU_DTYPE)
    return xc, gate_matmul(xb, wa_ref), gate_matmul(xb, wi_ref), gate


def _lru_scan(xc, r_pre, i_pre, gate, ba_ref, bi_ref, lam_ref, o_ref, h_ref):
    tt, w = xc.shape
    ng = tt // V7X_SUBLANES
    sub = lax.broadcasted_iota(jnp.int32, (ng, V7X_SUBLANES, w), 1)
    r = _sigmoid(r_pre + ba_ref[...])
    i = _sigmoid(i_pre + bi_ref[...])
    z = -lam_ref[...]
    softplus = jnp.maximum(z, 0.0) + jnp.log1p(jnp.exp(-jnp.abs(z)))
    log_a = -LRU_C * r * softplus
    a = jnp.exp(log_a)
    bb = jnp.sqrt(_one_minus_sq(a, log_a)) * (i * xc)

    a3 = a.reshape(ng, V7X_SUBLANES, w)
    b3 = bb.reshape(ng, V7X_SUBLANES, w)
    for d in (1, 2, 4):
        ok = sub >= d
        a_sh = pltpu.roll(a3, d, 1)
        b_sh = pltpu.roll(b3, d, 1)
        b3 = jnp.where(ok, a3 * b_sh + b3, b3)
        a3 = jnp.where(ok, a3 * a_sh, a3)
    carry = h_ref[0:1, :]
    groups = []
    for g in range(ng):
        hg = b3[g] + a3[g] * carry
        groups.append(hg)
        carry = hg[V7X_SUBLANES - 1:V7X_SUBLANES, :]
    hcur = jnp.concatenate(groups, axis=0)
    h_ref[...] = jnp.broadcast_to(carry, h_ref.shape)
    o_ref[0] = (hcur * _gelu_tanh(gate)).astype(o_ref.dtype)


def _merge_kernel(attn_ref, lru_ref, h_ref, x_ref, wm_ref, wa_ref, wl_ref, wo_ref, g2_ref,
                  x1_ref, h2_ref):
    d = x_ref.shape[1]
    h = h_ref[...]
    mg0 = _sigmoid(jnp.dot(h, wm_ref[:, 0:d], preferred_element_type=F32))
    mg1 = _sigmoid(jnp.dot(h, wm_ref[:, d:2 * d], preferred_element_type=F32))
    ya = jnp.dot(attn_ref[...], wa_ref[...], preferred_element_type=F32)
    yl = jnp.dot(lru_ref[...], wl_ref[...], preferred_element_type=F32)
    merged = mg0 * ya + mg1 * yl
    x1 = x_ref[...] + jnp.dot(merged.astype(MXU_DTYPE), wo_ref[...], preferred_element_type=F32)
    x1_ref[...] = x1
    ms = jnp.mean(x1 * x1, axis=-1, keepdims=True)
    h2_ref[...] = (x1 * lax.rsqrt(ms + EPS) * g2_ref[...]).astype(h2_ref.dtype)


def _merge(attn, lru, h2d, x2d, w_mg, wa, wl, wo, g2, tm=256):
    m, d = x2d.shape
    row = pl.BlockSpec((tm, d), lambda i: (i, 0))
    full = lambda a: pl.BlockSpec(a.shape, lambda i: (0, 0))
    g2 = g2.reshape(1, d)
    return pl.pallas_call(
        _merge_kernel,
        grid=(m // tm,),
        in_specs=[row, row, row, row, full(w_mg), full(wa), full(wl), full(wo), full(g2)],
        out_specs=[row, row],
        out_shape=[jax.ShapeDtypeStruct((m, d), F32), jax.ShapeDtypeStruct((m, d), MXU_DTYPE)],
        compiler_params=_params("parallel"),
        name="merge_out",
    )(attn, lru, h2d, x2d, w_mg, wa, wl, wo, g2)


def _ffn_kernel(h_ref, x1_ref, wg_ref, wu_ref, wd_ref, o_ref):
    @pl.when(pl.program_id(1) == 0)
    def _():
        o_ref[...] = x1_ref[...]

    h = h_ref[...]
    g = jnp.dot(h, wg_ref[...].astype(MXU_DTYPE), preferred_element_type=F32)
    u = jnp.dot(h, wu_ref[...].astype(MXU_DTYPE), preferred_element_type=F32)
    act = (g * _sigmoid(g) * u).astype(MXU_DTYPE)
    o_ref[...] += jnp.dot(act, wd_ref[...].astype(MXU_DTYPE), preferred_element_type=F32)


def _ffn(h2, x1, wg, wu, wd, tm=1024, tf=256):
    m, d = x1.shape
    f = wg.shape[1]
    return pl.pallas_call(
        _ffn_kernel,
        grid=(m // tm, f // tf),
        in_specs=[pl.BlockSpec((tm, d), lambda i, j: (i, 0)),
                  pl.BlockSpec((tm, d), lambda i, j: (i, 0)),
                  pl.BlockSpec((d, tf), lambda i, j: (0, j)),
                  pl.BlockSpec((d, tf), lambda i, j: (0, j)),
                  pl.BlockSpec((tf, d), lambda i, j: (j, 0))],
        out_specs=pl.BlockSpec((tm, d), lambda i, j: (i, 0)),
        out_shape=jax.ShapeDtypeStruct((m, d), F32),
        compiler_params=_params("parallel", "arbitrary"),
        name="swiglu_ffn",
    )(h2, x1, wg, wu, wd)


def _pad_last(a, n):
    return jnp.pad(a, [(0, 0)] * (a.ndim - 1) + [(0, n - a.shape[-1])])


def _layer(x, norm1_g, w_in, q_norm_g, k_norm_g, cmp_pos_k, cmp_w1_k, cmp_b1_k, cmp_w2_k, cmp_b2_k,
           cmp_pos_v, cmp_w1_v, cmp_b1_v, cmp_w2_v, cmp_b2_v, conv_w, conv_b, lru_wa, lru_ba,
           lru_wi, lru_bi, lru_lambda, w_o_attn, w_o_lru, w_out, norm2_g, w_gate, w_up, w_down):
    b, t, d = x.shape
    m = b * t
    attn_dim = N_HEADS * HEAD_DIM
    kv_dim = N_KV * HEAD_DIM
    lru_w = lru_lambda.shape[0]
    assert t % 512 == 0
    o1 = attn_dim
    o2 = o1 + 6 * kv_dim
    o3 = o2 + 3 * N_HEADS
    o4 = o3 + lru_w
    o5 = o4 + lru_w
    cast = lambda a: a.astype(MXU_DTYPE)

    wq_t = cast(w_in[:, :o1].T)
    w_kv = w_in[:, o1:o2].reshape(d, 6, N_KV, HEAD_DIM)
    w_cmp_src = cast(w_kv[:, 0:2].reshape(d, 2 * kv_dim))
    w_k2 = cast(jnp.stack([w_kv[:, 2], w_kv[:, 4]], axis=1).reshape(d, 2 * kv_dim))
    k_gain2 = jnp.stack([jnp.tile(k_norm_g[1], N_KV), jnp.tile(k_norm_g[2], N_KV)])
    lane_grp = np.arange(kv_dim) // HEAD_DIM
    grp_avg = cast(jnp.asarray((lane_grp[:, None] == lane_grp[None, :]) / HEAD_DIM, F32))
    place_np = np.zeros((kv_dim, N_KV * FEAT), np.float32)
    place_np[np.arange(kv_dim), lane_grp * FEAT + np.arange(kv_dim) % HEAD_DIM] = 1.0
    place = cast(jnp.asarray(place_np))
    w_v = jnp.stack([w_kv[:, 3], w_kv[:, 5]], axis=1)
    w_v_t = cast(_pad_last(w_v, V_ROWS).reshape(d, 2 * N_KV * V_ROWS).T)
    v_ones_col = jnp.tile(jnp.arange(V_ROWS) == HEAD_DIM, 2 * N_KV).astype(F32).reshape(-1, 1)
    w_g = w_in[:, o2:o3].reshape(d, N_KV, HPG, 3).transpose(0, 1, 3, 2).reshape(d, N_KV, 3 * HPG)
    w_g_t = cast(_pad_last(w_g, 16).reshape(d, N_KV * 16).T)
    w_lru = cast(w_in[:, o3:o5])
    w_mg = cast(w_in[:, o5:])
    q_gain_col = jnp.tile(q_norm_g * (HEAD_DIM ** -0.5 * LOG2E), N_HEADS).reshape(attn_dim, 1)
    gain_pad = lambda g: _pad_last(g.reshape(1, HEAD_DIM), FEAT)

    per_slab = V7X_MXU_DIM // (lru_w // LRU_BLOCKS)
    eye = jnp.eye(per_slab, dtype=F32)
    slabs = lambda wgt: cast(jnp.einsum('snkj,nm->snkmj', wgt.reshape(LRU_BLOCKS // per_slab, per_slab,
                                                                       *wgt.shape[1:]), eye)
                             .reshape(LRU_BLOCKS // per_slab, V7X_MXU_DIM, V7X_MXU_DIM))
    vec = lambda v: v.reshape(1, lru_w)

    h3d, q_t, v_t, gates_t, k_sel, k_win, cmp_src, lru = _in_proj(
        x,
        (norm1_g.reshape(1, d), wq_t, q_gain_col, w_v_t, v_ones_col, w_g_t, w_k2, k_gain2, grp_avg, place,
         w_cmp_src),
        (w_lru, conv_w.reshape(CONV_W, lru_w), vec(conv_b), slabs(lru_wa), vec(lru_ba), slabs(lru_wi),
         vec(lru_bi), vec(lru_lambda)))
    h2d = h3d.reshape(m, d)

    k_cmp = _compress(cmp_src, 0, cmp_pos_k, cmp_w1_k, cmp_b1_k, cast(_pad_last(cmp_w2_k, FEAT)),
                      _pad_last(cmp_b2_k.reshape(1, HEAD_DIM), FEAT), gain_pad(k_norm_g[0]))
    v_cmp_t = _compress(cmp_src, N_KV // CMP_GROUPS, cmp_pos_v, cmp_w1_v, cmp_b1_v,
                        cast(_pad_last(cmp_w2_v, V_ROWS).T),
                        jnp.concatenate([cmp_b2_v, v_ones_col[HEAD_DIM:V_ROWS, 0]]).reshape(V_ROWS, 1))

    attn = _attention_static(q_t, gates_t, k_cmp, v_cmp_t, k_sel, k_win, v_t)

    x1, h2 = _merge(attn.reshape(m, attn_dim), lru.reshape(m, lru_w), h2d, x.reshape(m, d), w_mg,
                    cast(w_o_attn), cast(w_o_lru), cast(w_out), norm2_g)
    out = _ffn(h2, x1, w_gate, w_up, w_down)
    return out.reshape(b, t, d)


def kernel(x, norm1_g, w_in, q_norm_g, k_norm_g, cmp_pos_k, cmp_w1_k, cmp_b1_k, cmp_w2_k, cmp_b2_k,
           cmp_pos_v, cmp_w1_v, cmp_b1_v, cmp_w2_v, cmp_b2_v, conv_w, conv_b, lru_wa, lru_ba,
           lru_wi, lru_bi, lru_lambda, w_o_attn, w_o_lru, w_out, norm2_g, w_gate, w_up, w_down):
    for l in range(norm1_g.shape[0]):
        x = _layer(x, norm1_g[l], w_in[l], q_norm_g[l], k_norm_g[l], cmp_pos_k[l], cmp_w1_k[l],
                   cmp_b1_k[l], cmp_w2_k[l], cmp_b2_k[l], cmp_pos_v[l], cmp_w1_v[l], cmp_b1_v[l],
                   cmp_w2_v[l], cmp_b2_v[l], conv_w[l], conv_b[l], lru_wa[l], lru_ba[l], lru_wi[l],
                   lru_bi[l], lru_lambda[l], w_o_attn[l], w_o_lru[l], w_out[l], norm2_g[l],
                   w_gate[l], w_up[l], w_down[l])
    return x
```

```python
import functools

import numpy as np
import jax
import jax.numpy as jnp
from jax import lax
from jax.experimental import pallas as pl
from jax.experimental.pallas import tpu as pltpu

N_HEADS = 16
HEAD_DIM = 64
N_KV = 4
HPG = N_HEADS // N_KV
CMP_BLOCK = 32
CMP_STRIDE = 16
CMP_HIDDEN = 256
SEL_BLOCK = 64
N_SEL = 16
WINDOW = 512
FORCE_BONUS = 1e4
LRU_BLOCKS = 16
CONV_W = 4
LRU_C = 8.0
EPS = 1e-6

MXU_DTYPE = jnp.bfloat16
F32 = jnp.float32

V7X_LANES = 128
V7X_SUBLANES = 8
V7X_MXU_DIM = 256
V7X_VMEM_LIMIT_BYTES = 48 * 1024 * 1024

TQ = 256
TK_SEL = 256
TILES_PER_CALL = 2
V_ROWS = 80
LOG2E = 1.4426950408889634
FEAT = V7X_MXU_DIM
F_SEL = HEAD_DIM
F_POS = 2 * HEAD_DIM
F_CMP = F_POS + 6
NEG_MASK = -1e30
NEG_BLOCK = -(2.0 ** 100)


def _params(*sem):
    return pltpu.CompilerParams(dimension_semantics=sem, vmem_limit_bytes=V7X_VMEM_LIMIT_BYTES)


def _gelu_tanh(x):
    return 0.5 * x * (1.0 + jnp.tanh(0.7978845608028654 * (x + 0.044715 * (x * x * x))))


def _sigmoid(x):
    return 1.0 / (1.0 + jnp.exp(-x))


def _one_minus_sq(a, log_a):
    series = (-2.0 * log_a) * (1.0 + log_a * (1.0 + log_a * (2.0 / 3)))
    return jnp.where(log_a > -1.0 / 128, series, 1.0 - a * a)


def _nt_dot(wt, h):
    return lax.dot_general(wt, h, (((1,), (1,)), ((), ())), preferred_element_type=F32)


def _token_features(pos, col, with_block_mask):
    blk = pos >> 6
    off = pos & (SEL_BLOCK - 1)
    feat = jnp.where((col >= F_POS) & (col < F_POS + 3), blk.astype(F32),
                     jnp.where((col >= F_POS + 3) & (col < F_POS + 6), off.astype(F32), 0.0))
    if with_block_mask:
        feat = jnp.where((col >= F_SEL) & (col - F_SEL == blk) & (col < F_POS), NEG_BLOCK, feat)
    return feat


def _in_proj_kernel(x_ref, g1_ref, wq_ref, qg_ref, wv_ref, vb_ref, wg_ref, wk_ref, kg_ref, grp_ref,
                    place_ref, wc_ref, wl_ref, cw_ref, cb_ref, wa_ref, ba_ref, wi_ref, bi_ref, lam_ref,
                    h_ref, qt_ref, vt_ref, gt_ref, ks_ref, kw_ref, cs_ref, lru_ref, tail_ref, state_ref):
    x = x_ref[0]
    tm = x.shape[0]
    ms = jnp.mean(x * x, axis=-1, keepdims=True)
    h = (x * lax.rsqrt(ms + EPS) * g1_ref[...]).astype(MXU_DTYPE)
    h_ref[0] = h

    lru_mid = _lru_gates(h, wl_ref, cw_ref, cb_ref, wa_ref, wi_ref, tail_ref, state_ref)

    r3 = _nt_dot(wq_ref[...], h).reshape(N_HEADS, HEAD_DIM, tm)
    qn = r3 * lax.rsqrt(jnp.mean(r3 * r3, axis=1, keepdims=True) + EPS)
    qt_ref[0] = (qn.reshape(N_HEADS * HEAD_DIM, tm) * qg_ref[...]).astype(qt_ref.dtype)

    vt_ref[0] = (_nt_dot(wv_ref[...], h) + vb_ref[...]).astype(vt_ref.dtype)
    gt_ref[0] = _sigmoid(_nt_dot(wg_ref[...], h))

    k = jnp.dot(h, wk_ref[...], preferred_element_type=F32)
    kk = k * k
    kk_hi = kk.astype(MXU_DTYPE)
    kk_lo = (kk - kk_hi.astype(F32)).astype(MXU_DTYPE)
    pos = pl.program_id(1) * tm + lax.broadcasted_iota(jnp.int32, (tm, FEAT), 0)
    col = lax.broadcasted_iota(jnp.int32, (tm, FEAT), 1)
    kv_dim = N_KV * HEAD_DIM
    for branch, (o_ref, with_block_mask) in enumerate(((ks_ref, True), (kw_ref, False))):
        sl = slice(branch * kv_dim, (branch + 1) * kv_dim)
        msq = (jnp.dot(kk_hi[:, sl], grp_ref[...], preferred_element_type=F32)
               + jnp.dot(kk_lo[:, sl], grp_ref[...], preferred_element_type=F32))
        kn = (k[:, sl] * lax.rsqrt(msq + EPS) * kg_ref[branch:branch + 1, :]).astype(MXU_DTYPE)
        placed = jnp.dot(kn, place_ref[...], preferred_element_type=F32)
        feat = _token_features(pos, col, with_block_mask)
        for g in range(N_KV):
            o_ref[0, g] = (placed[:, g * FEAT:(g + 1) * FEAT] + feat).astype(o_ref.dtype)

    cs_ref[0] = jnp.dot(h, wc_ref[...], preferred_element_type=F32)

    _lru_scan(*lru_mid, ba_ref, bi_ref, lam_ref, lru_ref, state_ref)


def _in_proj(x, attn_weights, lru_weights, tm=512):
    b, t, d = x.shape
    kv_dim = N_KV * HEAD_DIM
    wq_t, w_v_t, w_g_t = attn_weights[1], attn_weights[3], attn_weights[5]
    lru_w = lru_weights[-1].shape[1]
    full = lambda a: pl.BlockSpec(a.shape, lambda bi, i: (0,) * a.ndim)
    rowblk = lambda n: pl.BlockSpec((1, tm, n), lambda bi, i: (bi, i, 0))
    colblk = lambda n: pl.BlockSpec((1, n, tm), lambda bi, i: (bi, 0, i))
    kblk = pl.BlockSpec((1, N_KV, tm, FEAT), lambda bi, i: (bi, 0, i, 0))
    weights = tuple(attn_weights) + tuple(lru_weights)
    return pl.pallas_call(
        _in_proj_kernel,
        grid=(b, t // tm),
        in_specs=[rowblk(d)] + [full(a) for a in weights],
        out_specs=[rowblk(d), colblk(wq_t.shape[0]), colblk(w_v_t.shape[0]), colblk(w_g_t.shape[0]),
                   kblk, kblk, rowblk(2 * kv_dim), rowblk(lru_w)],
        out_shape=[jax.ShapeDtypeStruct((b, t, d), MXU_DTYPE),
                   jax.ShapeDtypeStruct((b, wq_t.shape[0], t), MXU_DTYPE),
                   jax.ShapeDtypeStruct((b, w_v_t.shape[0], t), MXU_DTYPE),
                   jax.ShapeDtypeStruct((b, w_g_t.shape[0], t), F32),
                   jax.ShapeDtypeStruct((b, N_KV, t, FEAT), MXU_DTYPE),
                   jax.ShapeDtypeStruct((b, N_KV, t, FEAT), MXU_DTYPE),
                   jax.ShapeDtypeStruct((b, t, 2 * kv_dim), F32),
                   jax.ShapeDtypeStruct((b, t, lru_w), MXU_DTYPE)],
        scratch_shapes=[pltpu.VMEM((V7X_SUBLANES, lru_w), F32),
                        pltpu.VMEM((V7X_SUBLANES, lru_w), F32)],
        compiler_params=_params("parallel", "arbitrary"),
        name="in_proj_lru",
    )(x, *weights)


CMP_GROUPS = V7X_LANES // HEAD_DIM


def _cmp_hidden(src_ref, pos_ref, w1_ref, b1_ref):
    ncp = src_ref.shape[1] // CMP_STRIDE
    first = jnp.zeros((ncp, CMP_GROUPS * CMP_HIDDEN), F32)
    second = jnp.zeros((ncp, CMP_GROUPS * CMP_HIDDEN), F32)
    for l in range(CMP_STRIDE):
        x = src_ref[0, pl.ds(l, ncp, stride=CMP_STRIDE), :]
        lo = (x + pos_ref[l:l + 1, :]).astype(MXU_DTYPE)
        hi = (x + pos_ref[CMP_STRIDE + l:CMP_STRIDE + l + 1, :]).astype(MXU_DTYPE)
        first = first + jnp.dot(lo, w1_ref[l], preferred_element_type=F32)
        second = second + jnp.dot(hi, w1_ref[CMP_STRIDE + l], preferred_element_type=F32)
    hid = first + pltpu.roll(second, ncp - 1, 0) + b1_ref[...]
    return _gelu_tanh(hid).astype(MXU_DTYPE)


def _cmp_k_kernel(src_ref, pos_ref, w1_ref, b1_ref, w2_ref, b2_ref, g_ref, o_ref):
    hid = _cmp_hidden(src_ref, pos_ref, w1_ref, b1_ref)
    ncp = hid.shape[0]
    idx = lax.broadcasted_iota(jnp.int32, (ncp, FEAT), 0)
    col = lax.broadcasted_iota(jnp.int32, (ncp, FEAT), 1)
    feat = jnp.where((col >= F_CMP) & (col < F_CMP + 3), (idx >> 6).astype(F32),
                     jnp.where((col >= F_CMP + 3) & (col < F_CMP + 6), (idx & 63).astype(F32), 0.0))
    for gl in range(CMP_GROUPS):
        r = jnp.dot(hid[:, gl * CMP_HIDDEN:(gl + 1) * CMP_HIDDEN], w2_ref[...],
                    preferred_element_type=F32) + b2_ref[...]
        ms = jnp.sum(r * r, axis=-1, keepdims=True) * (1.0 / HEAD_DIM)
        o_ref[0, gl] = (r * lax.rsqrt(ms + EPS) * g_ref[...] + feat).astype(o_ref.dtype)


def _cmp_v_kernel(src_ref, pos_ref, w1_ref, b1_ref, w2t_ref, b2_ref, o_ref):
    hid = _cmp_hidden(src_ref, pos_ref, w1_ref, b1_ref)
    for gl in range(CMP_GROUPS):
        r = _nt_dot(w2t_ref[...], hid[:, gl * CMP_HIDDEN:(gl + 1) * CMP_HIDDEN]) + b2_ref[...]
        o_ref[0, gl] = r.astype(o_ref.dtype)


def _compress(cmp_src, lane_block0, pos, w1, b1, w2, b2, gain_pad=None):
    b, t, _ = cmp_src.shape
    ncp = t // CMP_STRIDE
    hid_w = CMP_GROUPS * CMP_HIDDEN
    eye = jnp.eye(CMP_GROUPS, dtype=w1.dtype)
    w1_bd = jnp.einsum('ldf,gh->lgdhf', w1, eye).reshape(CMP_BLOCK, V7X_LANES, hid_w).astype(MXU_DTYPE)
    pos_t = jnp.tile(pos, (1, CMP_GROUPS))
    b1_t = jnp.tile(b1.reshape(1, CMP_HIDDEN), (1, CMP_GROUPS))
    full = lambda a: pl.BlockSpec(a.shape, lambda bi, p: (0,) * a.ndim)
    src_spec = pl.BlockSpec((1, t, V7X_LANES), lambda bi, p: (bi, 0, lane_block0 + p))
    grid = (b, N_KV // CMP_GROUPS)
    if gain_pad is not None:
        args = (pos_t, w1_bd, b1_t, w2, b2, gain_pad)
        return pl.pallas_call(
            _cmp_k_kernel,
            grid=grid,
            in_specs=[src_spec] + [full(a) for a in args],
            out_specs=pl.BlockSpec((1, CMP_GROUPS, ncp, FEAT), lambda bi, p: (bi, p, 0, 0)),
            out_shape=jax.ShapeDtypeStruct((b, N_KV, ncp, FEAT), MXU_DTYPE),
            compiler_params=_params("parallel", "parallel"),
            name="compress_k",
        )(cmp_src, *args)
    args = (pos_t, w1_bd, b1_t, w2, b2)
    return pl.pallas_call(
        _cmp_v_kernel,
        grid=grid,
        in_specs=[src_spec] + [full(a) for a in args],
        out_specs=pl.BlockSpec((1, CMP_GROUPS, V_ROWS, ncp), lambda bi, p: (bi, p, 0, 0)),
        out_shape=jax.ShapeDtypeStruct((b, N_KV, V_ROWS, ncp), MXU_DTYPE),
        compiler_params=_params("parallel", "parallel"),
        name="compress_v",
    )(cmp_src, *args)


def _split3(v):
    parts = []
    rest = np.asarray(v, np.float64)
    for _ in range(3):
        p = rest.astype(np.float32).astype(jnp.bfloat16).astype(np.float64)
        parts.append(p)
        rest = rest - p
    return parts


def _alibi_query_features():
    tab = np.zeros((N_KV, FEAT - F_POS, HPG * TQ), np.float64)
    for g in range(N_KV):
        for h in range(HPG):
            slope = 2.0 ** (-8.0 * (g * HPG + h + 1) / N_HEADS)
            parts = _split3(slope * LOG2E)
            lanes = slice(h * TQ, (h + 1) * TQ)
            for i, p in enumerate(parts):
                tab[g, i, lanes] = SEL_BLOCK * p
                tab[g, 3 + i, lanes] = p
                tab[g, 6 + i, lanes] = CMP_STRIDE * 64 * p
                tab[g, 9 + i, lanes] = CMP_STRIDE * p
    return jnp.asarray(tab, F32).astype(MXU_DTYPE)


def _block_map_t(n_cmp_pad, n_blk):
    cs = np.arange(n_cmp_pad) * CMP_STRIDE
    ce = cs + CMP_BLOCK - 1
    bs = np.arange(n_blk) * SEL_BLOCK
    be = bs + SEL_BLOCK - 1
    return jnp.asarray(((cs[None, :] <= be[:, None]) & (ce[None, :] >= bs[:, None])).astype(np.float32))


def _prob(s, m):
    return jnp.exp2(s - m).astype(MXU_DTYPE)


def _attn_tiles_kernel(*refs, tiles, win_blk0, n_blk, n_sel):
    (qt_ref, gate_ref, kc_ref, vc_ref, ks_ref, vs_ref, kd_ref, vd_ref, kwa_ref, kwb_ref, vwa_ref, vwb_ref,
     alibi_ref, map_ref) = refs[:14]
    o_ref, qb_ref, qs_ref, imp_ref = refs[15:]
    lanes = HPG * TQ
    ncp = kc_ref.shape[2]
    blocks_per_tile = TQ // SEL_BLOCK
    win_tiles = WINDOW // TQ
    lane_tok = lax.broadcasted_iota(jnp.int32, (1, lanes), 1) & (TQ - 1)
    row_pos = lax.broadcasted_iota(jnp.int32, (TQ, lanes), 0)
    blk = lax.broadcasted_iota(jnp.int32, (n_blk, TQ), 0)

    def window_tile(qi, j):
        n = len(tiles)
        lt = max(qi - win_tiles, 0) + j - n * win_blk0
        kref, vref = (kwa_ref, vwa_ref) if lt < n else (kwb_ref, vwb_ref)
        sl = slice((lt % n) * TQ, (lt % n + 1) * TQ)
        return kref[0, 0, sl, :], vref[0, :, sl]

    def head(ti, qi):
        cols = slice(ti * TQ, (ti + 1) * TQ)
        t_lane = qi * TQ + lane_tok
        for h in range(HPG):
            qb_ref[ti, 0:HEAD_DIM, h * TQ:(h + 1) * TQ] = qt_ref[0, h * HEAD_DIM:(h + 1) * HEAD_DIM, cols]
        qb_ref[ti, F_SEL:F_POS, :] = jnp.zeros((F_POS - F_SEL, lanes), qb_ref.dtype)
        qb_ref[ti, F_POS:FEAT, :] = alibi_ref[0]
        qb = qb_ref[ti]
        sc = jnp.dot(kc_ref[0, 0], qb, preferred_element_type=F32)
        sw = [jnp.dot(window_tile(qi, j)[0], qb, preferred_element_type=F32) for j in range(3)]
        sd = jnp.dot(kd_ref[0, 0, cols, :], qb, preferred_element_type=F32)
        last_cmp = (t_lane - (CMP_BLOCK - 1)) >> 4
        sc = jnp.where(lax.broadcasted_iota(jnp.int32, (ncp, lanes), 0) <= last_cmp, sc, NEG_MASK)
        ec = jnp.exp2(sc - jnp.max(sc, axis=0, keepdims=True))
        acc_c = jnp.dot(vc_ref[0, 0], ec.astype(MXU_DTYPE), preferred_element_type=F32)
        inv_c = jnp.where(last_cmp >= 0, 1.0 / jnp.maximum(acc_c[HEAD_DIM:HEAD_DIM + 1], 1e-30), 0.0)
        imp = None
        if (qi + 1) * blocks_per_tile > n_sel and qi > 0:
            psum = ec[:, 0:TQ] * inv_c[:, 0:TQ]
            for h in range(1, HPG):
                psum = psum + ec[:, h * TQ:(h + 1) * TQ] * inv_c[:, h * TQ:(h + 1) * TQ]
            imp = jnp.dot(map_ref[...], psum, preferred_element_type=F32)
        return dict(qb=qb, t_lane=t_lane, sw=sw, sd=sd, o_cmp=acc_c[0:HEAD_DIM] * inv_c, imp=imp,
                    causal=qi * TQ + row_pos <= t_lane)

    def rank_init(ti, qi, st):
        first_own_blk = qi * blocks_per_tile
        st.update(ranks=[], k_done=0, k_total=0)
        if st["imp"] is None:
            return
        cur = (qi * TQ + lax.broadcasted_iota(jnp.int32, (n_blk, TQ), 1)) >> 6
        forced = (blk == 0) | (blk == cur) | (blk == cur - 1)
        imp = jnp.where(blk <= cur, st["imp"] + jnp.where(forced, FORCE_BONUS, 0.0), NEG_MASK)
        imp_ref[ti] = imp
        n_rank_chunks = -(-first_own_blk // V7X_SUBLANES)
        st["chunks"] = [imp[c * V7X_SUBLANES:(c + 1) * V7X_SUBLANES] for c in range(n_rank_chunks)]
        st["ranks"] = [jnp.zeros((V7X_SUBLANES, TQ), jnp.int32) for _ in range(n_rank_chunks)]
        st["k_total"] = min(first_own_blk + blocks_per_tile, n_blk)

    def rank_rounds(ti, st, n):
        sub = lax.broadcasted_iota(jnp.int32, (V7X_SUBLANES, TQ), 0)
        stop = min(st["k_done"] + n, st["k_total"])
        for k in range(st["k_done"], stop):
            row = imp_ref[ti, k:k + 1, :]
            for c, mine in enumerate(st["chunks"]):
                lo = c * V7X_SUBLANES
                if lo > k:
                    one = jnp.where(row >= mine, 1, 0)
                elif lo + V7X_SUBLANES - 1 <= k:
                    one = jnp.where(row > mine, 1, 0)
                else:
                    one = jnp.where(sub + lo > k, jnp.where(row >= mine, 1, 0), jnp.where(row > mine, 1, 0))
                st["ranks"][c] = st["ranks"][c] + one
        st["k_done"] = stop

    def chunk_setup(ti, qi, st):
        rank_rounds(ti, st, st["k_total"])
        first_own_blk = qi * blocks_per_tile
        n_chunks = -(-qi * TQ // TK_SEL)
        st.update(n_chunks=n_chunks, m=jnp.full((1, lanes), NEG_MASK, F32), acc=jnp.zeros((V_ROWS, lanes), F32),
                  p_prev=None)
        if n_chunks == 0:
            return
        rank = jnp.zeros((n_blk, TQ), jnp.int32)
        if st["ranks"]:
            pad = [jnp.zeros((n_blk - len(st["ranks"]) * V7X_SUBLANES, TQ), jnp.int32)]
            rank = jnp.concatenate(st["ranks"] + (pad if pad[0].shape[0] else []), axis=0)
        not_sel = jnp.where((rank < n_sel) & (blk < first_own_blk), 0.0, 1.0).astype(qs_ref.dtype)
        qs_ref[ti] = st["qb"]
        for h in range(HPG):
            qs_ref[ti, F_SEL:F_SEL + n_blk, h * TQ:(h + 1) * TQ] = not_sel
        st["qs"] = qs_ref[ti]
        st["s_next"] = chunk_qk(st, 0)

    def chunk_qk(st, k):
        return jnp.dot(ks_ref[0, 0, k * TK_SEL:(k + 1) * TK_SEL, :], st["qs"], preferred_element_type=F32)

    def chunk_pv(st, k):
        return jnp.dot(vs_ref[0, :, k * TK_SEL:(k + 1) * TK_SEL], st["p_prev"], preferred_element_type=F32)

    def chunk_stage(st, k):
        s_cur = st["s_next"]
        if k + 1 < st["n_chunks"]:
            st["s_next"] = chunk_qk(st, k + 1)
        if st["p_prev"] is not None:
            st["acc"] = st["acc"] + chunk_pv(st, k - 1)
        m_new = jnp.maximum(st["m"], jnp.max(s_cur, axis=0, keepdims=True))
        st["p_prev"] = _prob(s_cur, m_new)
        st["acc"] = jnp.exp2(st["m"] - m_new) * st["acc"]
        st["m"] = m_new

    def chunk_finish(st):
        if st["n_chunks"] > 0:
            st["acc"] = st["acc"] + chunk_pv(st, st["n_chunks"] - 1)

    def diagonal(ti, qi, st):
        cols = slice(ti * TQ, (ti + 1) * TQ)
        sd = jnp.where(st["causal"], st["sd"], NEG_MASK)
        m_d = jnp.max(sd, axis=0, keepdims=True)
        acc_d = jnp.dot(vd_ref[0, :, cols], _prob(sd, m_d), preferred_element_type=F32)
        m_all = jnp.maximum(st["m"], m_d)
        acc_s = jnp.exp2(st["m"] - m_all) * st["acc"] + jnp.exp2(m_d - m_all) * acc_d
        st["o_sel"] = acc_s[0:HEAD_DIM] * (1.0 / acc_s[HEAD_DIM:HEAD_DIM + 1])

    def window(ti, qi, st):
        sw, t_lane = st["sw"], st["t_lane"]
        wb = max(qi - win_tiles, 0)
        if qi >= win_tiles:
            d0 = (t_lane - wb * TQ) - row_pos
            sw = [jnp.where(d0 < WINDOW, sw[0], NEG_MASK), sw[1], jnp.where(st["causal"], sw[2], NEG_MASK)]
        else:
            for j in range(3):
                dj = (t_lane - (wb + j) * TQ) - row_pos
                sw[j] = jnp.where(lax.bitcast_convert_type(dj, jnp.uint32) < WINDOW, sw[j], NEG_MASK)
        m_w = jnp.max(jnp.maximum(jnp.maximum(sw[0], sw[1]), sw[2]), axis=0, keepdims=True)
        acc_w = jnp.zeros((V_ROWS, lanes), F32)
        for j in range(3):
            acc_w = acc_w + jnp.dot(window_tile(qi, j)[1], _prob(sw[j], m_w), preferred_element_type=F32)
        st["o_win"] = acc_w[0:HEAD_DIM] * (1.0 / acc_w[HEAD_DIM:HEAD_DIM + 1])

    def output(ti, st):
        cols = slice(ti * TQ, (ti + 1) * TQ)
        gates = gate_ref[0, :, cols]
        def gate_row(j):
            return jnp.concatenate([gates[j * HPG + h:j * HPG + h + 1, :] for h in range(HPG)], axis=1)
        o_t = gate_row(0) * st["o_cmp"] + gate_row(1) * st["o_sel"] + gate_row(2) * st["o_win"]
        for hp in range(HPG // 2):
            pair = jnp.concatenate([o_t[:, (2 * hp) * TQ:(2 * hp + 1) * TQ],
                                    o_t[:, (2 * hp + 1) * TQ:(2 * hp + 2) * TQ]], axis=0)
            o_ref[0, cols, hp * 2 * HEAD_DIM:(hp + 1) * 2 * HEAD_DIM] = pair.T.astype(o_ref.dtype)

    def tail_phases(ti, qi, st):
        return [lambda: diagonal(ti, qi, st), lambda: output(ti, st)]

    cur = head(0, tiles[0])
    rank_init(0, tiles[0], cur)
    chunk_setup(0, tiles[0], cur)
    pending = []
    for ti, qi in enumerate(tiles):
        nxt = None
        if ti + 1 < len(tiles):
            nxt = head(ti + 1, tiles[ti + 1])
            rank_init(ti + 1, tiles[ti + 1], nxt)
            rounds_per_stage = -(-nxt["k_total"] // max(cur["n_chunks"], 1))
        pending.append(lambda ti=ti, qi=qi, st=cur: window(ti, qi, st))
        for k in range(cur["n_chunks"]):
            chunk_stage(cur, k)
            if nxt is not None:
                rank_rounds(ti + 1, nxt, rounds_per_stage)
            if pending:
                pending.pop(0)()
        for phase in pending:
            phase()
        chunk_finish(cur)
        pending = tail_phases(ti, qi, cur)
        if nxt is not None:
            chunk_setup(ti + 1, tiles[ti + 1], nxt)
            cur = nxt
    for phase in pending:
        phase()


def _attention_static(q_t, gates_t, k_cmp, v_cmp_t, k_sel, k_win, v_t):
    b, _, t = q_t.shape
    n_blk = t // SEL_BLOCK
    tiles_per_call = TILES_PER_CALL
    tb = tiles_per_call * TQ
    assert tb >= WINDOW and t % max(tb, TK_SEL) == 0 and t >= 2 * tb and F_SEL + n_blk <= F_POS, \
        "unsupported sequence length"
    n_sel = min(N_SEL, n_blk)
    lanes = HPG * TQ
    rows = HPG * HEAD_DIM
    alibi = _alibi_query_features()
    out_shape = jax.ShapeDtypeStruct((b, t, N_HEADS * HEAD_DIM), MXU_DTYPE)

    attn = jnp.zeros(out_shape.shape, out_shape.dtype)
    for m in range(t // tb):
        tiles = tuple(range(m * tiles_per_call, (m + 1) * tiles_per_call))
        ncp = min(t // CMP_STRIDE, -(-((m + 1) * tb // CMP_STRIDE) // V7X_LANES) * V7X_LANES)
        blk_map_t = _block_map_t(ncp, n_blk)
        kc = -(-tiles[-1] * TQ // TK_SEL) * TK_SEL
        wb0 = max(m - 1, 0)
        in_specs = [
            pl.BlockSpec((1, rows, tb), lambda bi, g, m=m: (bi, g, m)),
            pl.BlockSpec((1, 16, tb), lambda bi, g, m=m: (bi, g, m)),
            pl.BlockSpec((1, 1, ncp, FEAT), lambda bi, g: (bi, g, 0, 0)),
            pl.BlockSpec((1, 1, V_ROWS, ncp), lambda bi, g: (bi, g, 0, 0)),
            pl.BlockSpec((1, 1, kc, FEAT), lambda bi, g: (bi, g, 0, 0)),
            pl.BlockSpec((1, V_ROWS, kc), lambda bi, g: (bi, g, 0)),
            pl.BlockSpec((1, 1, tb, FEAT), lambda bi, g, m=m: (bi, g, m, 0)),
            pl.BlockSpec((1, V_ROWS, tb), lambda bi, g, m=m: (bi, g, m)),
            pl.BlockSpec((1, 1, tb, FEAT), lambda bi, g, w=wb0: (bi, g, w, 0)),
            pl.BlockSpec((1, 1, tb, FEAT), lambda bi, g, w=wb0: (bi, g, w + 1, 0)),
            pl.BlockSpec((1, V_ROWS, tb), lambda bi, g, w=wb0: (bi, N_KV + g, w)),
            pl.BlockSpec((1, V_ROWS, tb), lambda bi, g, w=wb0: (bi, N_KV + g, w + 1)),
            pl.BlockSpec((1, FEAT - F_POS, lanes), lambda bi, g: (g, 0, 0)),
            pl.BlockSpec((n_blk, ncp), lambda bi, g: (0, 0)),
        ]
        in_specs.append(pl.BlockSpec(memory_space=pl.ANY))
        args = [q_t, gates_t, k_cmp, v_cmp_t, k_sel, v_t, k_sel, v_t, k_win, k_win, v_t, v_t, alibi, blk_map_t,
                attn]
        kernel = functools.partial(_attn_tiles_kernel, tiles=tiles, win_blk0=wb0, n_blk=n_blk, n_sel=n_sel)
        attn = pl.pallas_call(
            kernel,
            grid=(b, N_KV),
            in_specs=in_specs,
            out_specs=pl.BlockSpec((1, tb, rows), lambda bi, g, m=m: (bi, m, g)),
            out_shape=out_shape,
            scratch_shapes=[pltpu.VMEM((tiles_per_call, FEAT, lanes), MXU_DTYPE),
                            pltpu.VMEM((tiles_per_call, FEAT, lanes), MXU_DTYPE),
                            pltpu.VMEM((tiles_per_call, n_blk, TQ), F32)],
            input_output_aliases={len(args) - 1: 0},
            compiler_params=_params("parallel", "parallel"),
            name=f"nsa_attention_{m}",
        )(*args)
    return attn


def _lru_gates(hin, wl_ref, cw_ref, cb_ref, wa_ref, wi_ref, tail_ref, h_ref):
    tt = hin.shape[0]
    w = cb_ref.shape[1]

    @pl.when(pl.program_id(1) == 0)
    def _():
        tail_ref[...] = jnp.zeros_like(tail_ref)
        h_ref[...] = jnp.zeros_like(h_ref)

    def gate_matmul(xb, w_ref):
        n = w_ref.shape[1]
        return jnp.concatenate([jnp.dot(xb[:, j * n:(j + 1) * n], w_ref[j], preferred_element_type=F32)
                                for j in range(w_ref.shape[0])], axis=1)

    x = jnp.dot(hin, wl_ref[:, 0:w], preferred_element_type=F32)
    gate = jnp.dot(hin, wl_ref[:, w:2 * w], preferred_element_type=F32)
    ng = tt // V7X_SUBLANES
    sub = lax.broadcasted_iota(jnp.int32, (ng, V7X_SUBLANES, w), 1)
    x3 = x.reshape(ng, V7X_SUBLANES, w)
    xprev3 = jnp.concatenate([tail_ref[...][None], x3], axis=0)
    tail_ref[...] = x[tt - V7X_SUBLANES:tt]
    xc = x * cw_ref[CONV_W - 1:CONV_W, :] + cb_ref[...]
    for s in range(1, CONV_W):
        rot = pltpu.roll(xprev3, s, 1)
        xs = jnp.where(sub >= s, rot[1:], rot[:-1])
        xc = xc + xs.reshape(tt, w) * cw_ref[CONV_W - 1 - s:CONV_W - s, :]

    xb = xc.astype(MXU_DTYPE)
    return xc, gate_matmul(xb, wa_ref), gate_matmul(xb, wi_ref), gate


def _lru_scan(xc, r_pre, i_pre, gate, ba_ref, bi_ref, lam_ref, o_ref, h_ref):
    tt, w = xc.shape
    ng = tt // V7X_SUBLANES
    sub = lax.broadcasted_iota(jnp.int32, (ng, V7X_SUBLANES, w), 1)
    r = _sigmoid(r_pre + ba_ref[...])
    i = _sigmoid(i_pre + bi_ref[...])
    z = -lam_ref[...]
    softplus = jnp.maximum(z, 0.0) + jnp.log1p(jnp.exp(-jnp.abs(z)))
    log_a = -LRU_C * r * softplus
    a = jnp.exp(log_a)
    bb = jnp.sqrt(_one_minus_sq(a, log_a)) * (i * xc)

    a3 = a.reshape(ng, V7X_SUBLANES, w)
    b3 = bb.reshape(ng, V7X_SUBLANES, w)
    for d in (1, 2, 4):
        ok = sub >= d
        a_sh = pltpu.roll(a3, d, 1)
        b_sh = pltpu.roll(b3, d, 1)
        b3 = jnp.where(ok, a3 * b_sh + b3, b3)
        a3 = jnp.where(ok, a3 * a_sh, a3)
    carry = h_ref[0:1, :]
    groups = []
    for g in range(ng):
        hg = b3[g] + a3[g] * carry
        groups.append(hg)
        carry = hg[V7X_SUBLANES - 1:V7X_SUBLANES, :]
    hcur = jnp.concatenate(groups, axis=0)
    h_ref[...] = jnp.broadcast_to(carry, h_ref.shape)
    o_ref[0] = (hcur * _gelu_tanh(gate)).astype(o_ref.dtype)


def _merge_kernel(attn_ref, lru_ref, h_ref, x_ref, wm_ref, wa_ref, wl_ref, wo_ref, g2_ref,
                  x1_ref, h2_ref):
    d = x_ref.shape[1]
    h = h_ref[...]
    mg0 = _sigmoid(jnp.dot(h, wm_ref[:, 0:d], preferred_element_type=F32))
    mg1 = _sigmoid(jnp.dot(h, wm_ref[:, d:2 * d], preferred_element_type=F32))
    ya = jnp.dot(attn_ref[...], wa_ref[...], preferred_element_type=F32)
    yl = jnp.dot(lru_ref[...], wl_ref[...], preferred_element_type=F32)
    merged = mg0 * ya + mg1 * yl
    x1 = x_ref[...] + jnp.dot(merged.astype(MXU_DTYPE), wo_ref[...], preferred_element_type=F32)
    x1_ref[...] = x1
    ms = jnp.mean(x1 * x1, axis=-1, keepdims=True)
    h2_ref[...] = (x1 * lax.rsqrt(ms + EPS) * g2_ref[...]).astype(h2_ref.dtype)


def _merge(attn, lru, h2d, x2d, w_mg, wa, wl, wo, g2, tm=256):
    m, d = x2d.shape
    row = pl.BlockSpec((tm, d), lambda i: (i, 0))
    full = lambda a: pl.BlockSpec(a.shape, lambda i: (0, 0))
    g2 = g2.reshape(1, d)
    return pl.pallas_call(
        _merge_kernel,
        grid=(m // tm,),
        in_specs=[row, row, row, row, full(w_mg), full(wa), full(wl), full(wo), full(g2)],
        out_specs=[row, row],
        out_shape=[jax.ShapeDtypeStruct((m, d), F32), jax.ShapeDtypeStruct((m, d), MXU_DTYPE)],
        compiler_params=_params("parallel"),
        name="merge_out",
    )(attn, lru, h2d, x2d, w_mg, wa, wl, wo, g2)


def _ffn_kernel(h_ref, x1_ref, wg_ref, wu_ref, wd_ref, o_ref):
    @pl.when(pl.program_id(1) == 0)
    def _():
        o_ref[...] = x1_ref[...]

    h = h_ref[...]
    g = jnp.dot(h, wg_ref[...].astype(MXU_DTYPE), preferred_element_type=F32)
    u = jnp.dot(h, wu_ref[...].astype(MXU_DTYPE), preferred_element_type=F32)
    act = (g * _sigmoid(g) * u).astype(MXU_DTYPE)
    o_ref[...] += jnp.dot(act, wd_ref[...].astype(MXU_DTYPE), preferred_element_type=F32)


def _ffn(h2, x1, wg, wu, wd, tm=1024, tf=256):
    m, d = x1.shape
    f = wg.shape[1]
    return pl.pallas_call(
        _ffn_kernel,
        grid=(m // tm, f // tf),
        in_specs=[pl.BlockSpec((tm, d), lambda i, j: (i, 0)),
                  pl.BlockSpec((tm, d), lambda i, j: (i, 0)),
                  pl.BlockSpec((d, tf), lambda i, j: (0, j)),
                  pl.BlockSpec((d, tf), lambda i, j: (0, j)),
                  pl.BlockSpec((tf, d), lambda i, j: (j, 0))],
        out_specs=pl.BlockSpec((tm, d), lambda i, j: (i, 0)),
        out_shape=jax.ShapeDtypeStruct((m, d), F32),
        compiler_params=_params("parallel", "arbitrary"),
        name="swiglu_ffn",
    )(h2, x1, wg, wu, wd)


def _pad_last(a, n):
    return jnp.pad(a, [(0, 0)] * (a.ndim - 1) + [(0, n - a.shape[-1])])


def _layer(x, norm1_g, w_in, q_norm_g, k_norm_g, cmp_pos_k, cmp_w1_k, cmp_b1_k, cmp_w2_k, cmp_b2_k,
           cmp_pos_v, cmp_w1_v, cmp_b1_v, cmp_w2_v, cmp_b2_v, conv_w, conv_b, lru_wa, lru_ba,
           lru_wi, lru_bi, lru_lambda, w_o_attn, w_o_lru, w_out, norm2_g, w_gate, w_up, w_down):
    b, t, d = x.shape
    m = b * t
    attn_dim = N_HEADS * HEAD_DIM
    kv_dim = N_KV * HEAD_DIM
    lru_w = lru_lambda.shape[0]
    assert t % 512 == 0
    o1 = attn_dim
    o2 = o1 + 6 * kv_dim
    o3 = o2 + 3 * N_HEADS
    o4 = o3 + lru_w
    o5 = o4 + lru_w
    cast = lambda a: a.astype(MXU_DTYPE)

    wq_t = cast(w_in[:, :o1].T)
    w_kv = w_in[:, o1:o2].reshape(d, 6, N_KV, HEAD_DIM)
    w_cmp_src = cast(w_kv[:, 0:2].reshape(d, 2 * kv_dim))
    w_k2 = cast(jnp.stack([w_kv[:, 2], w_kv[:, 4]], axis=1).reshape(d, 2 * kv_dim))
    k_gain2 = jnp.stack([jnp.tile(k_norm_g[1], N_KV), jnp.tile(k_norm_g[2], N_KV)])
    lane_grp = np.arange(kv_dim) // HEAD_DIM
    grp_avg = cast(jnp.asarray((lane_grp[:, None] == lane_grp[None, :]) / HEAD_DIM, F32))
    place_np = np.zeros((kv_dim, N_KV * FEAT), np.float32)
    place_np[np.arange(kv_dim), lane_grp * FEAT + np.arange(kv_dim) % HEAD_DIM] = 1.0
    place = cast(jnp.asarray(place_np))
    w_v = jnp.stack([w_kv[:, 3], w_kv[:, 5]], axis=1)
    w_v_t = cast(_pad_last(w_v, V_ROWS).reshape(d, 2 * N_KV * V_ROWS).T)
    v_ones_col = jnp.tile(jnp.arange(V_ROWS) == HEAD_DIM, 2 * N_KV).astype(F32).reshape(-1, 1)
    w_g = w_in[:, o2:o3].reshape(d, N_KV, HPG, 3).transpose(0, 1, 3, 2).reshape(d, N_KV, 3 * HPG)
    w_g_t = cast(_pad_last(w_g, 16).reshape(d, N_KV * 16).T)
    w_lru = cast(w_in[:, o3:o5])
    w_mg = cast(w_in[:, o5:])
    q_gain_col = jnp.tile(q_norm_g * (HEAD_DIM ** -0.5 * LOG2E), N_HEADS).reshape(attn_dim, 1)
    gain_pad = lambda g: _pad_last(g.reshape(1, HEAD_DIM), FEAT)

    per_slab = V7X_MXU_DIM // (lru_w // LRU_BLOCKS)
    eye = jnp.eye(per_slab, dtype=F32)
    slabs = lambda wgt: cast(jnp.einsum('snkj,nm->snkmj', wgt.reshape(LRU_BLOCKS // per_slab, per_slab,
                                                                       *wgt.shape[1:]), eye)
                             .reshape(LRU_BLOCKS // per_slab, V7X_MXU_DIM, V7X_MXU_DIM))
    vec = lambda v: v.reshape(1, lru_w)

    h3d, q_t, v_t, gates_t, k_sel, k_win, cmp_src, lru = _in_proj(
        x,
        (norm1_g.reshape(1, d), wq_t, q_gain_col, w_v_t, v_ones_col, w_g_t, w_k2, k_gain2, grp_avg, place,
         w_cmp_src),
        (w_lru, conv_w.reshape(CONV_W, lru_w), vec(conv_b), slabs(lru_wa), vec(lru_ba), slabs(lru_wi),
         vec(lru_bi), vec(lru_lambda)))
    h2d = h3d.reshape(m, d)

    k_cmp = _compress(cmp_src, 0, cmp_pos_k, cmp_w1_k, cmp_b1_k, cast(_pad_last(cmp_w2_k, FEAT)),
                      _pad_last(cmp_b2_k.reshape(1, HEAD_DIM), FEAT), gain_pad(k_norm_g[0]))
    v_cmp_t = _compress(cmp_src, N_KV // CMP_GROUPS, cmp_pos_v, cmp_w1_v, cmp_b1_v,
                        cast(_pad_last(cmp_w2_v, V_ROWS).T),
                        jnp.concatenate([cmp_b2_v, v_ones_col[HEAD_DIM:V_ROWS, 0]]).reshape(V_ROWS, 1))

    attn = _attention_static(q_t, gates_t, k_cmp, v_cmp_t, k_sel, k_win, v_t)

    x1, h2 = _merge(attn.reshape(m, attn_dim), lru.reshape(m, lru_w), h2d, x.reshape(m, d), w_mg,
                    cast(w_o_attn), cast(w_o_lru), cast(w_out), norm2_g)
    out = _ffn(h2, x1, w_gate, w_up, w_down)
    return out.reshape(b, t, d)


def kernel(x, norm1_g, w_in, q_norm_g, k_norm_g, cmp_pos_k, cmp_w1_k, cmp_b1_k, cmp_w2_k, cmp_b2_k,
           cmp_pos_v, cmp_w1_v, cmp_b1_v, cmp_w2_v, cmp_b2_v, conv_w, conv_b, lru_wa, lru_ba,
           lru_wi, lru_bi, lru_lambda, w_o_attn, w_o_lru, w_out, norm2_g, w_gate, w_up, w_down):
    for l in range(norm1_g.shape[0]):
        x = _layer(x, norm1_g[l], w_in[l], q_norm_g[l], k_norm_g[l], cmp_pos_k[l], cmp_w1_k[l],
                   cmp_b1_k[l], cmp_w2_k[l], cmp_b2_k[l], cmp_pos_v[l], cmp_w1_v[l], cmp_b1_v[l],
                   cmp_w2_v[l], cmp_b2_v[l], conv_w[l], conv_b[l], lru_wa[l], lru_ba[l], lru_wi[l],
                   lru_bi[l], lru_lambda[l], w_o_attn[l], w_o_lru[l], w_out[l], norm2_g[l],
                   w_gate[l], w_up[l], w_down[l])
    return x
```

```python
import functools

import numpy as np
import jax
import jax.numpy as jnp
from jax import lax
from jax.experimental import pallas as pl
from jax.experimental.pallas import tpu as pltpu

N_HEADS = 16
HEAD_DIM = 64
N_KV = 4
HPG = N_HEADS // N_KV
CMP_BLOCK = 32
CMP_STRIDE = 16
CMP_HIDDEN = 256
SEL_BLOCK = 64
N_SEL = 16
WINDOW = 512
FORCE_BONUS = 1e4
LRU_BLOCKS = 16
CONV_W = 4
LRU_C = 8.0
EPS = 1e-6

MXU_DTYPE = jnp.bfloat16
F32 = jnp.float32

V7X_LANES = 128
V7X_SUBLANES = 8
V7X_MXU_DIM = 256
V7X_VMEM_LIMIT_BYTES = 48 * 1024 * 1024

TQ = 256
TK_SEL = 256
TILES_PER_CALL = 2
V_ROWS = 80
LOG2E = 1.4426950408889634
FEAT = V7X_MXU_DIM
F_SEL = HEAD_DIM
F_POS = 2 * HEAD_DIM
F_CMP = F_POS + 6
NEG_MASK = -1e30
NEG_BLOCK = -(2.0 ** 100)


def _params(*sem):
    return pltpu.CompilerParams(dimension_semantics=sem, vmem_limit_bytes=V7X_VMEM_LIMIT_BYTES)


def _gelu_tanh(x):
    return 0.5 * x * (1.0 + jnp.tanh(0.7978845608028654 * (x + 0.044715 * (x * x * x))))


def _sigmoid(x):
    return 1.0 / (1.0 + jnp.exp(-x))


def _one_minus_sq(a, log_a):
    series = (-2.0 * log_a) * (1.0 + log_a * (1.0 + log_a * (2.0 / 3)))
    return jnp.where(log_a > -1.0 / 128, series, 1.0 - a * a)


def _nt_dot(wt, h):
    return lax.dot_general(wt, h, (((1,), (1,)), ((), ())), preferred_element_type=F32)


def _token_features(pos, col, with_block_mask):
    blk = pos >> 6
    off = pos & (SEL_BLOCK - 1)
    feat = jnp.where((col >= F_POS) & (col < F_POS + 3), blk.astype(F32),
                     jnp.where((col >= F_POS + 3) & (col < F_POS + 6), off.astype(F32), 0.0))
    if with_block_mask:
        feat = jnp.where((col >= F_SEL) & (col - F_SEL == blk) & (col < F_POS), NEG_BLOCK, feat)
    return feat


def _in_proj_kernel(x_ref, g1_ref, wq_ref, qg_ref, wv_ref, vb_ref, wg_ref, wk_ref, kg_ref, grp_ref,
                    place_ref, wc_ref, wl_ref, cw_ref, cb_ref, wa_ref, ba_ref, wi_ref, bi_ref, lam_ref,
                    h_ref, qt_ref, vt_ref, gt_ref, ks_ref, kw_ref, cs_ref, lru_ref, tail_ref, state_ref):
    x = x_ref[0]
    tm = x.shape[0]
    ms = jnp.mean(x * x, axis=-1, keepdims=True)
    h = (x * lax.rsqrt(ms + EPS) * g1_ref[...]).astype(MXU_DTYPE)
    h_ref[0] = h

    lru_mid = _lru_gates(h, wl_ref, cw_ref, cb_ref, wa_ref, wi_ref, tail_ref, state_ref)

    r3 = _nt_dot(wq_ref[...], h).reshape(N_HEADS, HEAD_DIM, tm)
    qn = r3 * lax.rsqrt(jnp.mean(r3 * r3, axis=1, keepdims=True) + EPS)
    qt_ref[0] = (qn.reshape(N_HEADS * HEAD_DIM, tm) * qg_ref[...]).astype(qt_ref.dtype)

    vt_ref[0] = (_nt_dot(wv_ref[...], h) + vb_ref[...]).astype(vt_ref.dtype)
    gt_ref[0] = _sigmoid(_nt_dot(wg_ref[...], h))

    k = jnp.dot(h, wk_ref[...], preferred_element_type=F32)
    kk = k * k
    kk_hi = kk.astype(MXU_DTYPE)
    kk_lo = (kk - kk_hi.astype(F32)).astype(MXU_DTYPE)
    pos = pl.program_id(1) * tm + lax.broadcasted_iota(jnp.int32, (tm, FEAT), 0)
    col = lax.broadcasted_iota(jnp.int32, (tm, FEAT), 1)
    kv_dim = N_KV * HEAD_DIM
    for branch, (o_ref, with_block_mask) in enumerate(((ks_ref, True), (kw_ref, False))):
        sl = slice(branch * kv_dim, (branch + 1) * kv_dim)
        msq = (jnp.dot(kk_hi[:, sl], grp_ref[...], preferred_element_type=F32)
               + jnp.dot(kk_lo[:, sl], grp_ref[...], preferred_element_type=F32))
        kn = (k[:, sl] * lax.rsqrt(msq + EPS) * kg_ref[branch:branch + 1, :]).astype(MXU_DTYPE)
        placed = jnp.dot(kn, place_ref[...], preferred_element_type=F32)
        feat = _token_features(pos, col, with_block_mask)
        for g in range(N_KV):
            o_ref[0, g] = (placed[:, g * FEAT:(g + 1) * FEAT] + feat).astype(o_ref.dtype)

    cs_ref[0] = jnp.dot(h, wc_ref[...], preferred_element_type=F32)

    _lru_scan(*lru_mid, ba_ref, bi_ref, lam_ref, lru_ref, state_ref)


def _in_proj(x, attn_weights, lru_weights, tm=512):
    b, t, d = x.shape
    kv_dim = N_KV * HEAD_DIM
    wq_t, w_v_t, w_g_t = attn_weights[1], attn_weights[3], attn_weights[5]
    lru_w = lru_weights[-1].shape[1]
    full = lambda a: pl.BlockSpec(a.shape, lambda bi, i: (0,) * a.ndim)
    rowblk = lambda n: pl.BlockSpec((1, tm, n), lambda bi, i: (bi, i, 0))
    colblk = lambda n: pl.BlockSpec((1, n, tm), lambda bi, i: (bi, 0, i))
    kblk = pl.BlockSpec((1, N_KV, tm, FEAT), lambda bi, i: (bi, 0, i, 0))
    weights = tuple(attn_weights) + tuple(lru_weights)
    return pl.pallas_call(
        _in_proj_kernel,
        grid=(b, t // tm),
        in_specs=[rowblk(d)] + [full(a) for a in weights],
        out_specs=[rowblk(d), colblk(wq_t.shape[0]), colblk(w_v_t.shape[0]), colblk(w_g_t.shape[0]),
                   kblk, kblk, rowblk(2 * kv_dim), rowblk(lru_w)],
        out_shape=[jax.ShapeDtypeStruct((b, t, d), MXU_DTYPE),
                   jax.ShapeDtypeStruct((b, wq_t.shape[0], t), MXU_DTYPE),
                   jax.ShapeDtypeStruct((b, w_v_t.shape[0], t), MXU_DTYPE),
                   jax.ShapeDtypeStruct((b, w_g_t.shape[0], t), F32),
                   jax.ShapeDtypeStruct((b, N_KV, t, FEAT), MXU_DTYPE),
                   jax.ShapeDtypeStruct((b, N_KV, t, FEAT), MXU_DTYPE),
                   jax.ShapeDtypeStruct((b, t, 2 * kv_dim), F32),
                   jax.ShapeDtypeStruct((b, t, lru_w), MXU_DTYPE)],
        scratch_shapes=[pltpu.VMEM((V7X_SUBLANES, lru_w), F32),
                        pltpu.VMEM((V7X_SUBLANES, lru_w), F32)],
        compiler_params=_params("parallel", "arbitrary"),
        name="in_proj_lru",
    )(x, *weights)


CMP_GROUPS = V7X_LANES // HEAD_DIM


def _cmp_hidden(src_ref, pos_ref, w1_ref, b1_ref):
    ncp = src_ref.shape[1] // CMP_STRIDE
    first = jnp.zeros((ncp, CMP_GROUPS * CMP_HIDDEN), F32)
    second = jnp.zeros((ncp, CMP_GROUPS * CMP_HIDDEN), F32)
    for l in range(CMP_STRIDE):
        x = src_ref[0, pl.ds(l, ncp, stride=CMP_STRIDE), :]
        lo = (x + pos_ref[l:l + 1, :]).astype(MXU_DTYPE)
        hi = (x + pos_ref[CMP_STRIDE + l:CMP_STRIDE + l + 1, :]).astype(MXU_DTYPE)
        first = first + jnp.dot(lo, w1_ref[l], preferred_element_type=F32)
        second = second + jnp.dot(hi, w1_ref[CMP_STRIDE + l], preferred_element_type=F32)
    hid = first + pltpu.roll(second, ncp - 1, 0) + b1_ref[...]
    return _gelu_tanh(hid).astype(MXU_DTYPE)


def _cmp_k_kernel(src_ref, pos_ref, w1_ref, b1_ref, w2_ref, b2_ref, g_ref, o_ref):
    hid = _cmp_hidden(src_ref, pos_ref, w1_ref, b1_ref)
    ncp = hid.shape[0]
    idx = lax.broadcasted_iota(jnp.int32, (ncp, FEAT), 0)
    col = lax.broadcasted_iota(jnp.int32, (ncp, FEAT), 1)
    feat = jnp.where((col >= F_CMP) & (col < F_CMP + 3), (idx >> 6).astype(F32),
                     jnp.where((col >= F_CMP + 3) & (col < F_CMP + 6), (idx & 63).astype(F32), 0.0))
    for gl in range(CMP_GROUPS):
        r = jnp.dot(hid[:, gl * CMP_HIDDEN:(gl + 1) * CMP_HIDDEN], w2_ref[...],
                    preferred_element_type=F32) + b2_ref[...]
        ms = jnp.sum(r * r, axis=-1, keepdims=True) * (1.0 / HEAD_DIM)
        o_ref[0, gl] = (r * lax.rsqrt(ms + EPS) * g_ref[...] + feat).astype(o_ref.dtype)


def _cmp_v_kernel(src_ref, pos_ref, w1_ref, b1_ref, w2t_ref, b2_ref, o_ref):
    hid = _cmp_hidden(src_ref, pos_ref, w1_ref, b1_ref)
    for gl in range(CMP_GROUPS):
        r = _nt_dot(w2t_ref[...], hid[:, gl * CMP_HIDDEN:(gl + 1) * CMP_HIDDEN]) + b2_ref[...]
        o_ref[0, gl] = r.astype(o_ref.dtype)


def _compress(cmp_src, lane_block0, pos, w1, b1, w2, b2, gain_pad=None):
    b, t, _ = cmp_src.shape
    ncp = t // CMP_STRIDE
    hid_w = CMP_GROUPS * CMP_HIDDEN
    w1c = w1.astype(MXU_DTYPE)
    zero = jnp.zeros_like(w1c)
    w1_bd = jnp.concatenate([jnp.concatenate([w1c if h == g else zero for h in range(CMP_GROUPS)], axis=2)
                             for g in range(CMP_GROUPS)], axis=1)
    pos_t = jnp.tile(pos, (1, CMP_GROUPS))
    b1_t = jnp.tile(b1.reshape(1, CMP_HIDDEN), (1, CMP_GROUPS))
    full = lambda a: pl.BlockSpec(a.shape, lambda bi, p: (0,) * a.ndim)
    src_spec = pl.BlockSpec((1, t, V7X_LANES), lambda bi, p: (bi, 0, lane_block0 + p))
    grid = (b, N_KV // CMP_GROUPS)
    if gain_pad is not None:
        args = (pos_t, w1_bd, b1_t, w2, b2, gain_pad)
        return pl.pallas_call(
            _cmp_k_kernel,
            grid=grid,
            in_specs=[src_spec] + [full(a) for a in args],
            out_specs=pl.BlockSpec((1, CMP_GROUPS, ncp, FEAT), lambda bi, p: (bi, p, 0, 0)),
            out_shape=jax.ShapeDtypeStruct((b, N_KV, ncp, FEAT), MXU_DTYPE),
            compiler_params=_params("parallel", "parallel"),
            name="compress_k",
        )(cmp_src, *args)
    args = (pos_t, w1_bd, b1_t, w2, b2)
    return pl.pallas_call(
        _cmp_v_kernel,
        grid=grid,
        in_specs=[src_spec] + [full(a) for a in args],
        out_specs=pl.BlockSpec((1, CMP_GROUPS, V_ROWS, ncp), lambda bi, p: (bi, p, 0, 0)),
        out_shape=jax.ShapeDtypeStruct((b, N_KV, V_ROWS, ncp), MXU_DTYPE),
        compiler_params=_params("parallel", "parallel"),
        name="compress_v",
    )(cmp_src, *args)


def _split3(v):
    parts = []
    rest = np.asarray(v, np.float64)
    for _ in range(3):
        p = rest.astype(np.float32).astype(jnp.bfloat16).astype(np.float64)
        parts.append(p)
        rest = rest - p
    return parts


def _alibi_query_features():
    tab = np.zeros((N_KV, FEAT - F_POS, HPG * TQ), np.float64)
    for g in range(N_KV):
        for h in range(HPG):
            slope = 2.0 ** (-8.0 * (g * HPG + h + 1) / N_HEADS)
            parts = _split3(slope * LOG2E)
            lanes = slice(h * TQ, (h + 1) * TQ)
            for i, p in enumerate(parts):
                tab[g, i, lanes] = SEL_BLOCK * p
                tab[g, 3 + i, lanes] = p
                tab[g, 6 + i, lanes] = CMP_STRIDE * 64 * p
                tab[g, 9 + i, lanes] = CMP_STRIDE * p
    return jnp.asarray(tab, F32).astype(MXU_DTYPE)


def _block_map_t(n_cmp_pad, n_blk):
    cs = np.arange(n_cmp_pad) * CMP_STRIDE
    ce = cs + CMP_BLOCK - 1
    bs = np.arange(n_blk) * SEL_BLOCK
    be = bs + SEL_BLOCK - 1
    return jnp.asarray(((cs[None, :] <= be[:, None]) & (ce[None, :] >= bs[:, None])).astype(np.float32))


def _prob(s, m):
    return jnp.exp2(s - m).astype(MXU_DTYPE)


def _attn_tiles_kernel(*refs, tiles, win_blk0, n_blk, n_sel):
    (qt_ref, gate_ref, kc_ref, vc_ref, ks_ref, vs_ref, kd_ref, vd_ref, kwa_ref, kwb_ref, vwa_ref, vwb_ref,
     alibi_ref, map_ref) = refs[:14]
    o_ref, qb_ref, qs_ref, imp_ref = refs[15:]
    lanes = HPG * TQ
    ncp = kc_ref.shape[2]
    blocks_per_tile = TQ // SEL_BLOCK
    win_tiles = WINDOW // TQ
    lane_tok = lax.broadcasted_iota(jnp.int32, (1, lanes), 1) & (TQ - 1)
    row_pos = lax.broadcasted_iota(jnp.int32, (TQ, lanes), 0)
    blk = lax.broadcasted_iota(jnp.int32, (n_blk, TQ), 0)

    def window_tile(qi, j):
        n = len(tiles)
        lt = max(qi - win_tiles, 0) + j - n * win_blk0
        kref, vref = (kwa_ref, vwa_ref) if lt < n else (kwb_ref, vwb_ref)
        sl = slice((lt % n) * TQ, (lt % n + 1) * TQ)
        return kref[0, 0, sl, :], vref[0, :, sl]

    def head(ti, qi):
        cols = slice(ti * TQ, (ti + 1) * TQ)
        t_lane = qi * TQ + lane_tok
        for h in range(HPG):
            qb_ref[ti, 0:HEAD_DIM, h * TQ:(h + 1) * TQ] = qt_ref[0, h * HEAD_DIM:(h + 1) * HEAD_DIM, cols]
        qb_ref[ti, F_SEL:F_POS, :] = jnp.zeros((F_POS - F_SEL, lanes), qb_ref.dtype)
        qb_ref[ti, F_POS:FEAT, :] = alibi_ref[0]
        qb = qb_ref[ti]
        sc = jnp.dot(kc_ref[0, 0], qb, preferred_element_type=F32)
        sw = [jnp.dot(window_tile(qi, j)[0], qb, preferred_element_type=F32) for j in range(3)]
        sd = jnp.dot(kd_ref[0, 0, cols, :], qb, preferred_element_type=F32)
        last_cmp = (t_lane - (CMP_BLOCK - 1)) >> 4
        sc = jnp.where(lax.broadcasted_iota(jnp.int32, (ncp, lanes), 0) <= last_cmp, sc, NEG_MASK)
        ec = jnp.exp2(sc - jnp.max(sc, axis=0, keepdims=True))
        acc_c = jnp.dot(vc_ref[0, 0], ec.astype(MXU_DTYPE), preferred_element_type=F32)
        inv_c = jnp.where(last_cmp >= 0, 1.0 / jnp.maximum(acc_c[HEAD_DIM:HEAD_DIM + 1], 1e-30), 0.0)
        imp = None
        if (qi + 1) * blocks_per_tile > n_sel and qi > 0:
            psum = ec[:, 0:TQ] * inv_c[:, 0:TQ]
            for h in range(1, HPG):
                psum = psum + ec[:, h * TQ:(h + 1) * TQ] * inv_c[:, h * TQ:(h + 1) * TQ]
            imp = jnp.dot(map_ref[...], psum, preferred_element_type=F32)
        return dict(qb=qb, t_lane=t_lane, sw=sw, sd=sd, o_cmp=acc_c[0:HEAD_DIM] * inv_c, imp=imp,
                    causal=qi * TQ + row_pos <= t_lane)

    def rank_init(ti, qi, st):
        first_own_blk = qi * blocks_per_tile
        st.update(ranks=[], k_done=0, k_total=0)
        if st["imp"] is None:
            return
        cur = (qi * TQ + lax.broadcasted_iota(jnp.int32, (n_blk, TQ), 1)) >> 6
        forced = (blk == 0) | (blk == cur) | (blk == cur - 1)
        imp = jnp.where(blk <= cur, st["imp"] + jnp.where(forced, FORCE_BONUS, 0.0), NEG_MASK)
        imp_ref[ti] = imp
        n_rank_chunks = -(-first_own_blk // V7X_SUBLANES)
        st["chunks"] = [imp[c * V7X_SUBLANES:(c + 1) * V7X_SUBLANES] for c in range(n_rank_chunks)]
        st["ranks"] = [jnp.zeros((V7X_SUBLANES, TQ), jnp.int32) for _ in range(n_rank_chunks)]
        st["k_total"] = min(first_own_blk + blocks_per_tile, n_blk)

    def rank_rounds(ti, st, n):
        sub = lax.broadcasted_iota(jnp.int32, (V7X_SUBLANES, TQ), 0)
        stop = min(st["k_done"] + n, st["k_total"])
        for k in range(st["k_done"], stop):
            row = imp_ref[ti, k:k + 1, :]
            for c, mine in enumerate(st["chunks"]):
                lo = c * V7X_SUBLANES
                if lo > k:
                    one = jnp.where(row >= mine, 1, 0)
                elif lo + V7X_SUBLANES - 1 <= k:
                    one = jnp.where(row > mine, 1, 0)
                else:
                    one = jnp.where(sub + lo > k, jnp.where(row >= mine, 1, 0), jnp.where(row > mine, 1, 0))
                st["ranks"][c] = st["ranks"][c] + one
        st["k_done"] = stop

    def chunk_setup(ti, qi, st):
        rank_rounds(ti, st, st["k_total"])
        first_own_blk = qi * blocks_per_tile
        n_chunks = -(-qi * TQ // TK_SEL)
        st.update(n_chunks=n_chunks, m=jnp.full((1, lanes), NEG_MASK, F32), acc=jnp.zeros((V_ROWS, lanes), F32),
                  p_prev=None)
        if n_chunks == 0:
            return
        rank = jnp.zeros((n_blk, TQ), jnp.int32)
        if st["ranks"]:
            pad = [jnp.zeros((n_blk - len(st["ranks"]) * V7X_SUBLANES, TQ), jnp.int32)]
            rank = jnp.concatenate(st["ranks"] + (pad if pad[0].shape[0] else []), axis=0)
        not_sel = jnp.where((rank < n_sel) & (blk < first_own_blk), 0.0, 1.0).astype(qs_ref.dtype)
        qs_ref[ti] = st["qb"]
        for h in range(HPG):
            qs_ref[ti, F_SEL:F_SEL + n_blk, h * TQ:(h + 1) * TQ] = not_sel
        st["qs"] = qs_ref[ti]
        st["s_next"] = chunk_qk(st, 0)

    def chunk_qk(st, k):
        return jnp.dot(ks_ref[0, 0, k * TK_SEL:(k + 1) * TK_SEL, :], st["qs"], preferred_element_type=F32)

    def chunk_pv(st, k):
        return jnp.dot(vs_ref[0, :, k * TK_SEL:(k + 1) * TK_SEL], st["p_prev"], preferred_element_type=F32)

    def chunk_stage(st, k):
        s_cur = st["s_next"]
        if k + 1 < st["n_chunks"]:
            st["s_next"] = chunk_qk(st, k + 1)
        if st["p_prev"] is not None:
            st["acc"] = st["acc"] + chunk_pv(st, k - 1)
        m_new = jnp.maximum(st["m"], jnp.max(s_cur, axis=0, keepdims=True))
        st["p_prev"] = _prob(s_cur, m_new)
        st["acc"] = jnp.exp2(st["m"] - m_new) * st["acc"]
        st["m"] = m_new

    def chunk_finish(st):
        if st["n_chunks"] > 0:
            st["acc"] = st["acc"] + chunk_pv(st, st["n_chunks"] - 1)

    def diagonal(ti, qi, st):
        cols = slice(ti * TQ, (ti + 1) * TQ)
        sd = jnp.where(st["causal"], st["sd"], NEG_MASK)
        m_d = jnp.max(sd, axis=0, keepdims=True)
        acc_d = jnp.dot(vd_ref[0, :, cols], _prob(sd, m_d), preferred_element_type=F32)
        m_all = jnp.maximum(st["m"], m_d)
        acc_s = jnp.exp2(st["m"] - m_all) * st["acc"] + jnp.exp2(m_d - m_all) * acc_d
        st["o_sel"] = acc_s[0:HEAD_DIM] * (1.0 / acc_s[HEAD_DIM:HEAD_DIM + 1])

    def window(ti, qi, st):
        sw, t_lane = st["sw"], st["t_lane"]
        wb = max(qi - win_tiles, 0)
        if qi >= win_tiles:
            d0 = (t_lane - wb * TQ) - row_pos
            sw = [jnp.where(d0 < WINDOW, sw[0], NEG_MASK), sw[1], jnp.where(st["causal"], sw[2], NEG_MASK)]
        else:
            for j in range(3):
                dj = (t_lane - (wb + j) * TQ) - row_pos
                sw[j] = jnp.where(lax.bitcast_convert_type(dj, jnp.uint32) < WINDOW, sw[j], NEG_MASK)
        m_w = jnp.max(jnp.maximum(jnp.maximum(sw[0], sw[1]), sw[2]), axis=0, keepdims=True)
        acc_w = jnp.zeros((V_ROWS, lanes), F32)
        for j in range(3):
            acc_w = acc_w + jnp.dot(window_tile(qi, j)[1], _prob(sw[j], m_w), preferred_element_type=F32)
        st["o_win"] = acc_w[0:HEAD_DIM] * (1.0 / acc_w[HEAD_DIM:HEAD_DIM + 1])

    def output(ti, st):
        cols = slice(ti * TQ, (ti + 1) * TQ)
        gates = gate_ref[0, :, cols]
        def gate_row(j):
            return jnp.concatenate([gates[j * HPG + h:j * HPG + h + 1, :] for h in range(HPG)], axis=1)
        o_t = gate_row(0) * st["o_cmp"] + gate_row(1) * st["o_sel"] + gate_row(2) * st["o_win"]
        for hp in range(HPG // 2):
            pair = jnp.concatenate([o_t[:, (2 * hp) * TQ:(2 * hp + 1) * TQ],
                                    o_t[:, (2 * hp + 1) * TQ:(2 * hp + 2) * TQ]], axis=0)
            o_ref[0, cols, hp * 2 * HEAD_DIM:(hp + 1) * 2 * HEAD_DIM] = pair.T.astype(o_ref.dtype)

    def tail_phases(ti, qi, st):
        return [lambda: diagonal(ti, qi, st), lambda: output(ti, st)]

    cur = head(0, tiles[0])
    rank_init(0, tiles[0], cur)
    chunk_setup(0, tiles[0], cur)
    pending = []
    for ti, qi in enumerate(tiles):
        nxt = None
        if ti + 1 < len(tiles):
            nxt = head(ti + 1, tiles[ti + 1])
            rank_init(ti + 1, tiles[ti + 1], nxt)
            rounds_per_stage = -(-nxt["k_total"] // max(cur["n_chunks"], 1))
        pending.append(lambda ti=ti, qi=qi, st=cur: window(ti, qi, st))
        for k in range(cur["n_chunks"]):
            chunk_stage(cur, k)
            if nxt is not None:
                rank_rounds(ti + 1, nxt, rounds_per_stage)
            if pending:
                pending.pop(0)()
        for phase in pending:
            phase()
        chunk_finish(cur)
        pending = tail_phases(ti, qi, cur)
        if nxt is not None:
            chunk_setup(ti + 1, tiles[ti + 1], nxt)
            cur = nxt
    for phase in pending:
        phase()


def _attention_static(q_t, gates_t, k_cmp, v_cmp_t, k_sel, k_win, v_t):
    b, _, t = q_t.shape
    n_blk = t // SEL_BLOCK
    tiles_per_call = TILES_PER_CALL
    tb = tiles_per_call * TQ
    assert tb >= WINDOW and t % max(tb, TK_SEL) == 0 and t >= 2 * tb and F_SEL + n_blk <= F_POS, \
        "unsupported sequence length"
    n_sel = min(N_SEL, n_blk)
    lanes = HPG * TQ
    rows = HPG * HEAD_DIM
    alibi = _alibi_query_features()
    out_shape = jax.ShapeDtypeStruct((b, t, N_HEADS * HEAD_DIM), MXU_DTYPE)

    attn = jnp.zeros(out_shape.shape, out_shape.dtype)
    for m in range(t // tb):
        tiles = tuple(range(m * tiles_per_call, (m + 1) * tiles_per_call))
        ncp = min(t // CMP_STRIDE, -(-((m + 1) * tb // CMP_STRIDE) // V7X_LANES) * V7X_LANES)
        blk_map_t = _block_map_t(ncp, n_blk)
        kc = -(-tiles[-1] * TQ // TK_SEL) * TK_SEL
        wb0 = max(m - 1, 0)
        in_specs = [
            pl.BlockSpec((1, rows, tb), lambda bi, g, m=m: (bi, g, m)),
            pl.BlockSpec((1, 16, tb), lambda bi, g, m=m: (bi, g, m)),
            pl.BlockSpec((1, 1, ncp, FEAT), lambda bi, g: (bi, g, 0, 0)),
            pl.BlockSpec((1, 1, V_ROWS, ncp), lambda bi, g: (bi, g, 0, 0)),
            pl.BlockSpec((1, 1, kc, FEAT), lambda bi, g: (bi, g, 0, 0)),
            pl.BlockSpec((1, V_ROWS, kc), lambda bi, g: (bi, g, 0)),
            pl.BlockSpec((1, 1, tb, FEAT), lambda bi, g, m=m: (bi, g, m, 0)),
            pl.BlockSpec((1, V_ROWS, tb), lambda bi, g, m=m: (bi, g, m)),
            pl.BlockSpec((1, 1, tb, FEAT), lambda bi, g, w=wb0: (bi, g, w, 0)),
            pl.BlockSpec((1, 1, tb, FEAT), lambda bi, g, w=wb0: (bi, g, w + 1, 0)),
            pl.BlockSpec((1, V_ROWS, tb), lambda bi, g, w=wb0: (bi, N_KV + g, w)),
            pl.BlockSpec((1, V_ROWS, tb), lambda bi, g, w=wb0: (bi, N_KV + g, w + 1)),
            pl.BlockSpec((1, FEAT - F_POS, lanes), lambda bi, g: (g, 0, 0)),
            pl.BlockSpec((n_blk, ncp), lambda bi, g: (0, 0)),
        ]
        in_specs.append(pl.BlockSpec(memory_space=pl.ANY))
        args = [q_t, gates_t, k_cmp, v_cmp_t, k_sel, v_t, k_sel, v_t, k_win, k_win, v_t, v_t, alibi, blk_map_t,
                attn]
        kernel = functools.partial(_attn_tiles_kernel, tiles=tiles, win_blk0=wb0, n_blk=n_blk, n_sel=n_sel)
        attn = pl.pallas_call(
            kernel,
            grid=(b, N_KV),
            in_specs=in_specs,
            out_specs=pl.BlockSpec((1, tb, rows), lambda bi, g, m=m: (bi, m, g)),
            out_shape=out_shape,
            scratch_shapes=[pltpu.VMEM((tiles_per_call, FEAT, lanes), MXU_DTYPE),
                            pltpu.VMEM((tiles_per_call, FEAT, lanes), MXU_DTYPE),
                            pltpu.VMEM((tiles_per_call, n_blk, TQ), F32)],
            input_output_aliases={len(args) - 1: 0},
            compiler_params=_params("parallel", "parallel"),
            name=f"nsa_attention_{m}",
        )(*args)
    return attn


def _lru_gates(hin, wl_ref, cw_ref, cb_ref, wa_ref, wi_ref, tail_ref, h_ref):
    tt = hin.shape[0]
    w = cb_ref.shape[1]

    @pl.when(pl.program_id(1) == 0)
    def _():
        tail_ref[...] = jnp.zeros_like(tail_ref)
        h_ref[...] = jnp.zeros_like(h_ref)

    def gate_matmul(xb, w_ref):
        n = w_ref.shape[1]
        return jnp.concatenate([jnp.dot(xb[:, j * n:(j + 1) * n], w_ref[j], preferred_element_type=F32)
                                for j in range(w_ref.shape[0])], axis=1)

    x = jnp.dot(hin, wl_ref[:, 0:w], preferred_element_type=F32)
    gate = jnp.dot(hin, wl_ref[:, w:2 * w], preferred_element_type=F32)
    ng = tt // V7X_SUBLANES
    sub = lax.broadcasted_iota(jnp.int32, (ng, V7X_SUBLANES, w), 1)
    x3 = x.reshape(ng, V7X_SUBLANES, w)
    xprev3 = jnp.concatenate([tail_ref[...][None], x3], axis=0)
    tail_ref[...] = x[tt - V7X_SUBLANES:tt]
    xc = x * cw_ref[CONV_W - 1:CONV_W, :] + cb_ref[...]
    for s in range(1, CONV_W):
        rot = pltpu.roll(xprev3, s, 1)
        xs = jnp.where(sub >= s, rot[1:], rot[:-1])
        xc = xc + xs.reshape(tt, w) * cw_ref[CONV_W - 1 - s:CONV_W - s, :]

    xb = xc.astype(MXU_DTYPE)
    return xc, gate_matmul(xb, wa_ref), gate_matmul(xb, wi_ref), gate


def _lru_scan(xc, r_pre, i_pre, gate, ba_ref, bi_ref, lam_ref, o_ref, h_ref):
    tt, w = xc.shape
    ng = tt // V7X_SUBLANES
    sub = lax.broadcasted_iota(jnp.int32, (ng, V7X_SUBLANES, w), 1)
    r = _sigmoid(r_pre + ba_ref[...])
    i = _sigmoid(i_pre + bi_ref[...])
    z = -lam_ref[...]
    softplus = jnp.maximum(z, 0.0) + jnp.log1p(jnp.exp(-jnp.abs(z)))
    log_a = -LRU_C * r * softplus
    a = jnp.exp(log_a)
    bb = jnp.sqrt(_one_minus_sq(a, log_a)) * (i * xc)

    a3 = a.reshape(ng, V7X_SUBLANES, w)
    b3 = bb.reshape(ng, V7X_SUBLANES, w)
    for d in (1, 2, 4):
        ok = sub >= d
        a_sh = pltpu.roll(a3, d, 1)
        b_sh = pltpu.roll(b3, d, 1)
        b3 = jnp.where(ok, a3 * b_sh + b3, b3)
        a3 = jnp.where(ok, a3 * a_sh, a3)
    carry = h_ref[0:1, :]
    groups = []
    for g in range(ng):
        hg = b3[g] + a3[g] * carry
        groups.append(hg)
        carry = hg[V7X_SUBLANES - 1:V7X_SUBLANES, :]
    hcur = jnp.concatenate(groups, axis=0)
    h_ref[...] = jnp.broadcast_to(carry, h_ref.shape)
    o_ref[0] = (hcur * _gelu_tanh(gate)).astype(o_ref.dtype)


def _merge_kernel(attn_ref, lru_ref, h_ref, x_ref, wm_ref, wa_ref, wl_ref, wo_ref, g2_ref,
                  x1_ref, h2_ref):
    d = x_ref.shape[1]
    h = h_ref[...]
    mg0 = _sigmoid(jnp.dot(h, wm_ref[:, 0:d], preferred_element_type=F32))
    mg1 = _sigmoid(jnp.dot(h, wm_ref[:, d:2 * d], preferred_element_type=F32))
    ya = jnp.dot(attn_ref[...], wa_ref[...], preferred_element_type=F32)
    yl = jnp.dot(lru_ref[...], wl_ref[...], preferred_element_type=F32)
    merged = mg0 * ya + mg1 * yl
    x1 = x_ref[...] + jnp.dot(merged.astype(MXU_DTYPE), wo_ref[...], preferred_element_type=F32)
    x1_ref[...] = x1
    ms = jnp.mean(x1 * x1, axis=-1, keepdims=True)
    h2_ref[...] = (x1 * lax.rsqrt(ms + EPS) * g2_ref[...]).astype(h2_ref.dtype)


def _merge(attn, lru, h2d, x2d, w_mg, wa, wl, wo, g2, tm=256):
    m, d = x2d.shape
    row = pl.BlockSpec((tm, d), lambda i: (i, 0))
    full = lambda a: pl.BlockSpec(a.shape, lambda i: (0, 0))
    g2 = g2.reshape(1, d)
    return pl.pallas_call(
        _merge_kernel,
        grid=(m // tm,),
        in_specs=[row, row, row, row, full(w_mg), full(wa), full(wl), full(wo), full(g2)],
        out_specs=[row, row],
        out_shape=[jax.ShapeDtypeStruct((m, d), F32), jax.ShapeDtypeStruct((m, d), MXU_DTYPE)],
        compiler_params=_params("parallel"),
        name="merge_out",
    )(attn, lru, h2d, x2d, w_mg, wa, wl, wo, g2)


def _ffn_kernel(h_ref, x1_ref, wg_ref, wu_ref, wd_ref, o_ref):
    @pl.when(pl.program_id(1) == 0)
    def _():
        o_ref[...] = x1_ref[...]

    h = h_ref[...]
    g = jnp.dot(h, wg_ref[...].astype(MXU_DTYPE), preferred_element_type=F32)
    u = jnp.dot(h, wu_ref[...].astype(MXU_DTYPE), preferred_element_type=F32)
    act = (g * _sigmoid(g) * u).astype(MXU_DTYPE)
    o_ref[...] += jnp.dot(act, wd_ref[...].astype(MXU_DTYPE), preferred_element_type=F32)


def _ffn(h2, x1, wg, wu, wd, tm=1024, tf=256):
    m, d = x1.shape
    f = wg.shape[1]
    return pl.pallas_call(
        _ffn_kernel,
        grid=(m // tm, f // tf),
        in_specs=[pl.BlockSpec((tm, d), lambda i, j: (i, 0)),
                  pl.BlockSpec((tm, d), lambda i, j: (i, 0)),
                  pl.BlockSpec((d, tf), lambda i, j: (0, j)),
                  pl.BlockSpec((d, tf), lambda i, j: (0, j)),
                  pl.BlockSpec((tf, d), lambda i, j: (j, 0))],
        out_specs=pl.BlockSpec((tm, d), lambda i, j: (i, 0)),
        out_shape=jax.ShapeDtypeStruct((m, d), F32),
        compiler_params=_params("parallel", "arbitrary"),
        name="swiglu_ffn",
    )(h2, x1, wg, wu, wd)


def _pad_last(a, n):
    return jnp.pad(a, [(0, 0)] * (a.ndim - 1) + [(0, n - a.shape[-1])])


def _layer(x, norm1_g, w_in, q_norm_g, k_norm_g, cmp_pos_k, cmp_w1_k, cmp_b1_k, cmp_w2_k, cmp_b2_k,
           cmp_pos_v, cmp_w1_v, cmp_b1_v, cmp_w2_v, cmp_b2_v, conv_w, conv_b, lru_wa, lru_ba,
           lru_wi, lru_bi, lru_lambda, w_o_attn, w_o_lru, w_out, norm2_g, w_gate, w_up, w_down):
    b, t, d = x.shape
    m = b * t
    attn_dim = N_HEADS * HEAD_DIM
    kv_dim = N_KV * HEAD_DIM
    lru_w = lru_lambda.shape[0]
    assert t % 512 == 0
    o1 = attn_dim
    o2 = o1 + 6 * kv_dim
    o3 = o2 + 3 * N_HEADS
    o4 = o3 + lru_w
    o5 = o4 + lru_w
    cast = lambda a: a.astype(MXU_DTYPE)

    wq_t = cast(w_in[:, :o1].T)
    w_kv = w_in[:, o1:o2].reshape(d, 6, N_KV, HEAD_DIM)
    w_cmp_src = cast(w_kv[:, 0:2].reshape(d, 2 * kv_dim))
    w_k2 = cast(jnp.stack([w_kv[:, 2], w_kv[:, 4]], axis=1).reshape(d, 2 * kv_dim))
    k_gain2 = jnp.stack([jnp.tile(k_norm_g[1], N_KV), jnp.tile(k_norm_g[2], N_KV)])
    lane_grp = np.arange(kv_dim) // HEAD_DIM
    grp_avg = cast(jnp.asarray((lane_grp[:, None] == lane_grp[None, :]) / HEAD_DIM, F32))
    place_np = np.zeros((kv_dim, N_KV * FEAT), np.float32)
    place_np[np.arange(kv_dim), lane_grp * FEAT + np.arange(kv_dim) % HEAD_DIM] = 1.0
    place = cast(jnp.asarray(place_np))
    w_v = jnp.stack([w_kv[:, 3], w_kv[:, 5]], axis=1)
    w_v_t = cast(_pad_last(w_v, V_ROWS).reshape(d, 2 * N_KV * V_ROWS).T)
    v_ones_col = jnp.tile(jnp.arange(V_ROWS) == HEAD_DIM, 2 * N_KV).astype(F32).reshape(-1, 1)
    w_g = w_in[:, o2:o3].reshape(d, N_KV, HPG, 3).transpose(0, 1, 3, 2).reshape(d, N_KV, 3 * HPG)
    w_g_t = cast(_pad_last(w_g, 16).reshape(d, N_KV * 16).T)
    w_lru = cast(w_in[:, o3:o5])
    w_mg = cast(w_in[:, o5:])
    q_gain_col = jnp.tile(q_norm_g * (HEAD_DIM ** -0.5 * LOG2E), N_HEADS).reshape(attn_dim, 1)
    gain_pad = lambda g: _pad_last(g.reshape(1, HEAD_DIM), FEAT)

    per_slab = V7X_MXU_DIM // (lru_w // LRU_BLOCKS)
    eye = jnp.eye(per_slab, dtype=F32)
    slabs = lambda wgt: cast(jnp.einsum('snkj,nm->snkmj', wgt.reshape(LRU_BLOCKS // per_slab, per_slab,
                                                                       *wgt.shape[1:]), eye)
                             .reshape(LRU_BLOCKS // per_slab, V7X_MXU_DIM, V7X_MXU_DIM))
    vec = lambda v: v.reshape(1, lru_w)

    h3d, q_t, v_t, gates_t, k_sel, k_win, cmp_src, lru = _in_proj(
        x,
        (norm1_g.reshape(1, d), wq_t, q_gain_col, w_v_t, v_ones_col, w_g_t, w_k2, k_gain2, grp_avg, place,
         w_cmp_src),
        (w_lru, conv_w.reshape(CONV_W, lru_w), vec(conv_b), slabs(lru_wa), vec(lru_ba), slabs(lru_wi),
         vec(lru_bi), vec(lru_lambda)))
    h2d = h3d.reshape(m, d)

    k_cmp = _compress(cmp_src, 0, cmp_pos_k, cmp_w1_k, cmp_b1_k, cast(_pad_last(cmp_w2_k, FEAT)),
                      _pad_last(cmp_b2_k.reshape(1, HEAD_DIM), FEAT), gain_pad(k_norm_g[0]))
    v_cmp_t = _compress(cmp_src, N_KV // CMP_GROUPS, cmp_pos_v, cmp_w1_v, cmp_b1_v,
                        cast(_pad_last(cmp_w2_v, V_ROWS).T),
                        jnp.concatenate([cmp_b2_v, v_ones_col[HEAD_DIM:V_ROWS, 0]]).reshape(V_ROWS, 1))

    attn = _attention_static(q_t, gates_t, k_cmp, v_cmp_t, k_sel, k_win, v_t)

    x1, h2 = _merge(attn.reshape(m, attn_dim), lru.reshape(m, lru_w), h2d, x.reshape(m, d), w_mg,
                    cast(w_o_attn), cast(w_o_lru), cast(w_out), norm2_g)
    out = _ffn(h2, x1, w_gate, w_up, w_down)
    return out.reshape(b, t, d)


def kernel(x, norm1_g, w_in, q_norm_g, k_norm_g, cmp_pos_k, cmp_w1_k, cmp_b1_k, cmp_w2_k, cmp_b2_k,
           cmp_pos_v, cmp_w1_v, cmp_b1_v, cmp_w2_v, cmp_b2_v, conv_w, conv_b, lru_wa, lru_ba,
           lru_wi, lru_bi, lru_lambda, w_o_attn, w_o_lru, w_out, norm2_g, w_gate, w_up, w_down):
    for l in range(norm1_g.shape[0]):
        x = _layer(x, norm1_g[l], w_in[l], q_norm_g[l], k_norm_g[l], cmp_pos_k[l], cmp_w1_k[l],
                   cmp_b1_k[l], cmp_w2_k[l], cmp_b2_k[l], cmp_pos_v[l], cmp_w1_v[l], cmp_b1_v[l],
                   cmp_w2_v[l], cmp_b2_v[l], conv_w[l], conv_b[l], lru_wa[l], lru_ba[l], lru_wi[l],
                   lru_bi[l], lru_lambda[l], w_o_attn[l], w_o_lru[l], w_out[l], norm2_g[l],
                   w_gate[l], w_up[l], w_down[l])
    return x
```

```python
import functools

import numpy as np
import jax
import jax.numpy as jnp
from jax import lax
from jax.experimental import pallas as pl
from jax.experimental.pallas import tpu as pltpu

N_HEADS = 16
HEAD_DIM = 64
N_KV = 4
HPG = N_HEADS // N_KV
CMP_BLOCK = 32
CMP_STRIDE = 16
CMP_HIDDEN = 256
SEL_BLOCK = 64
N_SEL = 16
WINDOW = 512
FORCE_BONUS = 1e4
LRU_BLOCKS = 16
CONV_W = 4
LRU_C = 8.0
EPS = 1e-6

MXU_DTYPE = jnp.bfloat16
F32 = jnp.float32

V7X_LANES = 128
V7X_SUBLANES = 8
V7X_MXU_DIM = 256
V7X_VMEM_LIMIT_BYTES = 48 * 1024 * 1024

TQ = 256
TK_SEL = 256
TILES_PER_CALL = 2
V_ROWS = 80
LOG2E = 1.4426950408889634
FEAT = V7X_MXU_DIM
F_SEL = HEAD_DIM
F_POS = 2 * HEAD_DIM
F_CMP = F_POS + 6
NEG_MASK = -1e30
NEG_BLOCK = -(2.0 ** 100)


def _params(*sem):
    return pltpu.CompilerParams(dimension_semantics=sem, vmem_limit_bytes=V7X_VMEM_LIMIT_BYTES)


def _gelu_tanh(x):
    return 0.5 * x * (1.0 + jnp.tanh(0.7978845608028654 * (x + 0.044715 * (x * x * x))))


def _sigmoid(x):
    return 1.0 / (1.0 + jnp.exp(-x))


def _one_minus_sq(a, log_a):
    series = (-2.0 * log_a) * (1.0 + log_a * (1.0 + log_a * (2.0 / 3)))
    return jnp.where(log_a > -1.0 / 128, series, 1.0 - a * a)


def _nt_dot(wt, h):
    return lax.dot_general(wt, h, (((1,), (1,)), ((), ())), preferred_element_type=F32)


def _token_features(pos, col, with_block_mask):
    blk = pos >> 6
    off = pos & (SEL_BLOCK - 1)
    feat = jnp.where((col >= F_POS) & (col < F_POS + 3), blk.astype(F32),
                     jnp.where((col >= F_POS + 3) & (col < F_POS + 6), off.astype(F32), 0.0))
    if with_block_mask:
        feat = jnp.where((col >= F_SEL) & (col - F_SEL == blk) & (col < F_POS), NEG_BLOCK, feat)
    return feat


def _in_proj_kernel(x_ref, g1_ref, wq_ref, qg_ref, wv_ref, vb_ref, wg_ref, wk_ref, kg_ref, grp_ref,
                    place_ref, wc_ref, wl_ref, cw_ref, cb_ref, wa_ref, ba_ref, wi_ref, bi_ref, lam_ref,
                    h_ref, qt_ref, vt_ref, gt_ref, ks_ref, kw_ref, cs_ref, lru_ref, tail_ref, state_ref):
    x = x_ref[0]
    tm = x.shape[0]
    ms = jnp.mean(x * x, axis=-1, keepdims=True)
    h = (x * lax.rsqrt(ms + EPS) * g1_ref[...]).astype(MXU_DTYPE)
    h_ref[0] = h

    lru_mid = _lru_gates(h, wl_ref, cw_ref, cb_ref, wa_ref, wi_ref, tail_ref, state_ref)

    r3 = _nt_dot(wq_ref[...], h).reshape(N_HEADS, HEAD_DIM, tm)
    qn = r3 * lax.rsqrt(jnp.mean(r3 * r3, axis=1, keepdims=True) + EPS)
    qt_ref[0] = (qn.reshape(N_HEADS * HEAD_DIM, tm) * qg_ref[...]).astype(qt_ref.dtype)

    vt_ref[0] = (_nt_dot(wv_ref[...], h) + vb_ref[...]).astype(vt_ref.dtype)
    gt_ref[0] = _sigmoid(_nt_dot(wg_ref[...], h))

    k = jnp.dot(h, wk_ref[...], preferred_element_type=F32)
    kk = k * k
    kk_hi = kk.astype(MXU_DTYPE)
    kk_lo = (kk - kk_hi.astype(F32)).astype(MXU_DTYPE)
    pos = pl.program_id(1) * tm + lax.broadcasted_iota(jnp.int32, (tm, FEAT), 0)
    col = lax.broadcasted_iota(jnp.int32, (tm, FEAT), 1)
    kv_dim = N_KV * HEAD_DIM
    for branch, (o_ref, with_block_mask) in enumerate(((ks_ref, True), (kw_ref, False))):
        sl = slice(branch * kv_dim, (branch + 1) * kv_dim)
        msq = (jnp.dot(kk_hi[:, sl], grp_ref[...], preferred_element_type=F32)
               + jnp.dot(kk_lo[:, sl], grp_ref[...], preferred_element_type=F32))
        kn = (k[:, sl] * lax.rsqrt(msq + EPS) * kg_ref[branch:branch + 1, :]).astype(MXU_DTYPE)
        placed = jnp.dot(kn, place_ref[...], preferred_element_type=F32)
        feat = _token_features(pos, col, with_block_mask)
        for g in range(N_KV):
            o_ref[0, g] = (placed[:, g * FEAT:(g + 1) * FEAT] + feat).astype(o_ref.dtype)

    cs_ref[0] = jnp.dot(h, wc_ref[...], preferred_element_type=F32)

    _lru_scan(*lru_mid, ba_ref, bi_ref, lam_ref, lru_ref, state_ref)


def _in_proj(x, attn_weights, lru_weights, tm=512):
    b, t, d = x.shape
    kv_dim = N_KV * HEAD_DIM
    wq_t, w_v_t, w_g_t = attn_weights[1], attn_weights[3], attn_weights[5]
    lru_w = lru_weights[-1].shape[1]
    full = lambda a: pl.BlockSpec(a.shape, lambda bi, i: (0,) * a.ndim)
    rowblk = lambda n: pl.BlockSpec((1, tm, n), lambda bi, i: (bi, i, 0))
    colblk = lambda n: pl.BlockSpec((1, n, tm), lambda bi, i: (bi, 0, i))
    kblk = pl.BlockSpec((1, N_KV, tm, FEAT), lambda bi, i: (bi, 0, i, 0))
    weights = tuple(attn_weights) + tuple(lru_weights)
    return pl.pallas_call(
        _in_proj_kernel,
        grid=(b, t // tm),
        in_specs=[rowblk(d)] + [full(a) for a in weights],
        out_specs=[rowblk(d), colblk(wq_t.shape[0]), colblk(w_v_t.shape[0]), colblk(w_g_t.shape[0]),
                   kblk, kblk, rowblk(2 * kv_dim), rowblk(lru_w)],
        out_shape=[jax.ShapeDtypeStruct((b, t, d), MXU_DTYPE),
                   jax.ShapeDtypeStruct((b, wq_t.shape[0], t), MXU_DTYPE),
                   jax.ShapeDtypeStruct((b, w_v_t.shape[0], t), MXU_DTYPE),
                   jax.ShapeDtypeStruct((b, w_g_t.shape[0], t), F32),
                   jax.ShapeDtypeStruct((b, N_KV, t, FEAT), MXU_DTYPE),
                   jax.ShapeDtypeStruct((b, N_KV, t, FEAT), MXU_DTYPE),
                   jax.ShapeDtypeStruct((b, t, 2 * kv_dim), F32),
                   jax.ShapeDtypeStruct((b, t, lru_w), MXU_DTYPE)],
        scratch_shapes=[pltpu.VMEM((V7X_SUBLANES, lru_w), F32),
                        pltpu.VMEM((V7X_SUBLANES, lru_w), F32)],
        compiler_params=_params("parallel", "arbitrary"),
        name="in_proj_lru",
    )(x, *weights)


CMP_GROUPS = V7X_LANES // HEAD_DIM


def _cmp_hidden(src_ref, pos_ref, w1_ref, b1_ref):
    ncp = src_ref.shape[1] // CMP_STRIDE
    lo, hi = [], []
    for l in range(CMP_STRIDE):
        x = src_ref[0, pl.ds(l, ncp, stride=CMP_STRIDE), :]
        lo.append((x + pos_ref[l:l + 1, :]).astype(MXU_DTYPE))
        hi.append((x + pos_ref[CMP_STRIDE + l:CMP_STRIDE + l + 1, :]).astype(MXU_DTYPE))
    kdim = CMP_STRIDE * V7X_LANES
    first = jnp.dot(jnp.concatenate(lo, axis=1), w1_ref[0:CMP_STRIDE].reshape(kdim, -1),
                    preferred_element_type=F32)
    second = jnp.dot(jnp.concatenate(hi, axis=1), w1_ref[CMP_STRIDE:CMP_BLOCK].reshape(kdim, -1),
                     preferred_element_type=F32)
    hid = first + pltpu.roll(second, ncp - 1, 0) + b1_ref[...]
    return _gelu_tanh(hid).astype(MXU_DTYPE)


def _cmp_k_kernel(src_ref, pos_ref, w1_ref, b1_ref, w2_ref, b2_ref, g_ref, o_ref):
    hid = _cmp_hidden(src_ref, pos_ref, w1_ref, b1_ref)
    ncp = hid.shape[0]
    idx = lax.broadcasted_iota(jnp.int32, (ncp, FEAT), 0)
    col = lax.broadcasted_iota(jnp.int32, (ncp, FEAT), 1)
    feat = jnp.where((col >= F_CMP) & (col < F_CMP + 3), (idx >> 6).astype(F32),
                     jnp.where((col >= F_CMP + 3) & (col < F_CMP + 6), (idx & 63).astype(F32), 0.0))
    for gl in range(CMP_GROUPS):
        r = jnp.dot(hid[:, gl * CMP_HIDDEN:(gl + 1) * CMP_HIDDEN], w2_ref[...],
                    preferred_element_type=F32) + b2_ref[...]
        ms = jnp.sum(r * r, axis=-1, keepdims=True) * (1.0 / HEAD_DIM)
        o_ref[0, gl] = (r * lax.rsqrt(ms + EPS) * g_ref[...] + feat).astype(o_ref.dtype)


def _cmp_v_kernel(src_ref, pos_ref, w1_ref, b1_ref, w2t_ref, b2_ref, o_ref):
    hid = _cmp_hidden(src_ref, pos_ref, w1_ref, b1_ref)
    for gl in range(CMP_GROUPS):
        r = _nt_dot(w2t_ref[...], hid[:, gl * CMP_HIDDEN:(gl + 1) * CMP_HIDDEN]) + b2_ref[...]
        o_ref[0, gl] = r.astype(o_ref.dtype)


def _compress(cmp_src, lane_block0, pos, w1, b1, w2, b2, gain_pad=None):
    b, t, _ = cmp_src.shape
    ncp = t // CMP_STRIDE
    hid_w = CMP_GROUPS * CMP_HIDDEN
    w1c = w1.astype(MXU_DTYPE)
    zero = jnp.zeros_like(w1c)
    w1_bd = jnp.concatenate([jnp.concatenate([w1c if h == g else zero for h in range(CMP_GROUPS)], axis=2)
                             for g in range(CMP_GROUPS)], axis=1)
    pos_t = jnp.tile(pos, (1, CMP_GROUPS))
    b1_t = jnp.tile(b1.reshape(1, CMP_HIDDEN), (1, CMP_GROUPS))
    full = lambda a: pl.BlockSpec(a.shape, lambda bi, p: (0,) * a.ndim)
    src_spec = pl.BlockSpec((1, t, V7X_LANES), lambda bi, p: (bi, 0, lane_block0 + p))
    grid = (b, N_KV // CMP_GROUPS)
    if gain_pad is not None:
        args = (pos_t, w1_bd, b1_t, w2, b2, gain_pad)
        return pl.pallas_call(
            _cmp_k_kernel,
            grid=grid,
            in_specs=[src_spec] + [full(a) for a in args],
            out_specs=pl.BlockSpec((1, CMP_GROUPS, ncp, FEAT), lambda bi, p: (bi, p, 0, 0)),
            out_shape=jax.ShapeDtypeStruct((b, N_KV, ncp, FEAT), MXU_DTYPE),
            compiler_params=_params("parallel", "parallel"),
            name="compress_k",
        )(cmp_src, *args)
    args = (pos_t, w1_bd, b1_t, w2, b2)
    return pl.pallas_call(
        _cmp_v_kernel,
        grid=grid,
        in_specs=[src_spec] + [full(a) for a in args],
        out_specs=pl.BlockSpec((1, CMP_GROUPS, V_ROWS, ncp), lambda bi, p: (bi, p, 0, 0)),
        out_shape=jax.ShapeDtypeStruct((b, N_KV, V_ROWS, ncp), MXU_DTYPE),
        compiler_params=_params("parallel", "parallel"),
        name="compress_v",
    )(cmp_src, *args)


def _split3(v):
    parts = []
    rest = np.asarray(v, np.float64)
    for _ in range(3):
        p = rest.astype(np.float32).astype(jnp.bfloat16).astype(np.float64)
        parts.append(p)
        rest = rest - p
    return parts


def _alibi_query_features():
    tab = np.zeros((N_KV, FEAT - F_POS, HPG * TQ), np.float64)
    for g in range(N_KV):
        for h in range(HPG):
            slope = 2.0 ** (-8.0 * (g * HPG + h + 1) / N_HEADS)
            parts = _split3(slope * LOG2E)
            lanes = slice(h * TQ, (h + 1) * TQ)
            for i, p in enumerate(parts):
                tab[g, i, lanes] = SEL_BLOCK * p
                tab[g, 3 + i, lanes] = p
                tab[g, 6 + i, lanes] = CMP_STRIDE * 64 * p
                tab[g, 9 + i, lanes] = CMP_STRIDE * p
    return jnp.asarray(tab, F32).astype(MXU_DTYPE)


def _block_map_t(n_cmp_pad, n_blk):
    cs = np.arange(n_cmp_pad) * CMP_STRIDE
    ce = cs + CMP_BLOCK - 1
    bs = np.arange(n_blk) * SEL_BLOCK
    be = bs + SEL_BLOCK - 1
    return jnp.asarray(((cs[None, :] <= be[:, None]) & (ce[None, :] >= bs[:, None])).astype(np.float32))


def _prob(s, m):
    return jnp.exp2(s - m).astype(MXU_DTYPE)


def _attn_tiles_kernel(*refs, tiles, win_blk0, n_blk, n_sel):
    (qt_ref, gate_ref, kc_ref, vc_ref, ks_ref, vs_ref, kd_ref, vd_ref, kwa_ref, kwb_ref, vwa_ref, vwb_ref,
     alibi_ref, map_ref) = refs[:14]
    o_ref, qb_ref, qs_ref, imp_ref = refs[15:]
    lanes = HPG * TQ
    ncp = kc_ref.shape[2]
    blocks_per_tile = TQ // SEL_BLOCK
    win_tiles = WINDOW // TQ
    lane_tok = lax.broadcasted_iota(jnp.int32, (1, lanes), 1) & (TQ - 1)
    row_pos = lax.broadcasted_iota(jnp.int32, (TQ, lanes), 0)
    blk = lax.broadcasted_iota(jnp.int32, (n_blk, TQ), 0)

    def window_tile(qi, j):
        n = len(tiles)
        lt = max(qi - win_tiles, 0) + j - n * win_blk0
        kref, vref = (kwa_ref, vwa_ref) if lt < n else (kwb_ref, vwb_ref)
        sl = slice((lt % n) * TQ, (lt % n + 1) * TQ)
        return kref[0, 0, sl, :], vref[0, :, sl]

    def head(ti, qi):
        cols = slice(ti * TQ, (ti + 1) * TQ)
        t_lane = qi * TQ + lane_tok
        for h in range(HPG):
            qb_ref[ti, 0:HEAD_DIM, h * TQ:(h + 1) * TQ] = qt_ref[0, h * HEAD_DIM:(h + 1) * HEAD_DIM, cols]
        qb_ref[ti, F_SEL:F_POS, :] = jnp.zeros((F_POS - F_SEL, lanes), qb_ref.dtype)
        qb_ref[ti, F_POS:FEAT, :] = alibi_ref[0]
        qb = qb_ref[ti]
        sc = jnp.dot(kc_ref[0, 0], qb, preferred_element_type=F32)
        sw = [jnp.dot(window_tile(qi, j)[0], qb, preferred_element_type=F32) for j in range(3)]
        sd = jnp.dot(kd_ref[0, 0, cols, :], qb, preferred_element_type=F32)
        last_cmp = (t_lane - (CMP_BLOCK - 1)) >> 4
        sc = jnp.where(lax.broadcasted_iota(jnp.int32, (ncp, lanes), 0) <= last_cmp, sc, NEG_MASK)
        ec = jnp.exp2(sc - jnp.max(sc, axis=0, keepdims=True))
        acc_c = jnp.dot(vc_ref[0, 0], ec.astype(MXU_DTYPE), preferred_element_type=F32)
        inv_c = jnp.where(last_cmp >= 0, 1.0 / jnp.maximum(acc_c[HEAD_DIM:HEAD_DIM + 1], 1e-30), 0.0)
        imp = None
        if (qi + 1) * blocks_per_tile > n_sel and qi > 0:
            psum = ec[:, 0:TQ] * inv_c[:, 0:TQ]
            for h in range(1, HPG):
                psum = psum + ec[:, h * TQ:(h + 1) * TQ] * inv_c[:, h * TQ:(h + 1) * TQ]
            imp = jnp.dot(map_ref[...], psum, preferred_element_type=F32)
        return dict(qb=qb, t_lane=t_lane, sw=sw, sd=sd, o_cmp=acc_c[0:HEAD_DIM] * inv_c, imp=imp,
                    causal=qi * TQ + row_pos <= t_lane)

    def rank_init(ti, qi, st):
        first_own_blk = qi * blocks_per_tile
        st.update(ranks=[], k_done=0, k_total=0)
        if st["imp"] is None:
            return
        cur = (qi * TQ + lax.broadcasted_iota(jnp.int32, (n_blk, TQ), 1)) >> 6
        forced = (blk == 0) | (blk == cur) | (blk == cur - 1)
        imp = jnp.where(blk <= cur, st["imp"] + jnp.where(forced, FORCE_BONUS, 0.0), NEG_MASK)
        imp_ref[ti] = imp
        n_rank_chunks = -(-first_own_blk // V7X_SUBLANES)
        st["chunks"] = [imp[c * V7X_SUBLANES:(c + 1) * V7X_SUBLANES] for c in range(n_rank_chunks)]
        st["ranks"] = [jnp.zeros((V7X_SUBLANES, TQ), jnp.int32) for _ in range(n_rank_chunks)]
        st["k_total"] = min(first_own_blk + blocks_per_tile, n_blk)

    def rank_rounds(ti, st, n):
        sub = lax.broadcasted_iota(jnp.int32, (V7X_SUBLANES, TQ), 0)
        stop = min(st["k_done"] + n, st["k_total"])
        for k in range(st["k_done"], stop):
            row = imp_ref[ti, k:k + 1, :]
            for c, mine in enumerate(st["chunks"]):
                lo = c * V7X_SUBLANES
                if lo > k:
                    one = jnp.where(row >= mine, 1, 0)
                elif lo + V7X_SUBLANES - 1 <= k:
                    one = jnp.where(row > mine, 1, 0)
                else:
                    one = jnp.where(sub + lo > k, jnp.where(row >= mine, 1, 0), jnp.where(row > mine, 1, 0))
                st["ranks"][c] = st["ranks"][c] + one
        st["k_done"] = stop

    def chunk_setup(ti, qi, st):
        rank_rounds(ti, st, st["k_total"])
        first_own_blk = qi * blocks_per_tile
        n_chunks = -(-qi * TQ // TK_SEL)
        st.update(n_chunks=n_chunks, m=jnp.full((1, lanes), NEG_MASK, F32), acc=jnp.zeros((V_ROWS, lanes), F32),
                  p_prev=None)
        if n_chunks == 0:
            return
        rank = jnp.zeros((n_blk, TQ), jnp.int32)
        if st["ranks"]:
            pad = [jnp.zeros((n_blk - len(st["ranks"]) * V7X_SUBLANES, TQ), jnp.int32)]
            rank = jnp.concatenate(st["ranks"] + (pad if pad[0].shape[0] else []), axis=0)
        not_sel = jnp.where((rank < n_sel) & (blk < first_own_blk), 0.0, 1.0).astype(qs_ref.dtype)
        qs_ref[ti] = st["qb"]
        for h in range(HPG):
            qs_ref[ti, F_SEL:F_SEL + n_blk, h * TQ:(h + 1) * TQ] = not_sel
        st["qs"] = qs_ref[ti]
        st["s_next"] = chunk_qk(st, 0)

    def chunk_qk(st, k):
        return jnp.dot(ks_ref[0, 0, k * TK_SEL:(k + 1) * TK_SEL, :], st["qs"], preferred_element_type=F32)

    def chunk_pv(st, k):
        return jnp.dot(vs_ref[0, :, k * TK_SEL:(k + 1) * TK_SEL], st["p_prev"], preferred_element_type=F32)

    def chunk_stage(st, k):
        s_cur = st["s_next"]
        if k + 1 < st["n_chunks"]:
            st["s_next"] = chunk_qk(st, k + 1)
        if st["p_prev"] is not None:
            st["acc"] = st["acc"] + chunk_pv(st, k - 1)
        m_new = jnp.maximum(st["m"], jnp.max(s_cur, axis=0, keepdims=True))
        st["p_prev"] = _prob(s_cur, m_new)
        st["acc"] = jnp.exp2(st["m"] - m_new) * st["acc"]
        st["m"] = m_new

    def chunk_finish(st):
        if st["n_chunks"] > 0:
            st["acc"] = st["acc"] + chunk_pv(st, st["n_chunks"] - 1)

    def diagonal(ti, qi, st):
        cols = slice(ti * TQ, (ti + 1) * TQ)
        sd = jnp.where(st["causal"], st["sd"], NEG_MASK)
        m_d = jnp.max(sd, axis=0, keepdims=True)
        acc_d = jnp.dot(vd_ref[0, :, cols], _prob(sd, m_d), preferred_element_type=F32)
        m_all = jnp.maximum(st["m"], m_d)
        acc_s = jnp.exp2(st["m"] - m_all) * st["acc"] + jnp.exp2(m_d - m_all) * acc_d
        st["o_sel"] = acc_s[0:HEAD_DIM] * (1.0 / acc_s[HEAD_DIM:HEAD_DIM + 1])

    def window(ti, qi, st):
        sw, t_lane = st["sw"], st["t_lane"]
        wb = max(qi - win_tiles, 0)
        if qi >= win_tiles:
            d0 = (t_lane - wb * TQ) - row_pos
            sw = [jnp.where(d0 < WINDOW, sw[0], NEG_MASK), sw[1], jnp.where(st["causal"], sw[2], NEG_MASK)]
        else:
            for j in range(3):
                dj = (t_lane - (wb + j) * TQ) - row_pos
                sw[j] = jnp.where(lax.bitcast_convert_type(dj, jnp.uint32) < WINDOW, sw[j], NEG_MASK)
        m_w = jnp.max(jnp.maximum(jnp.maximum(sw[0], sw[1]), sw[2]), axis=0, keepdims=True)
        acc_w = jnp.zeros((V_ROWS, lanes), F32)
        for j in range(3):
            acc_w = acc_w + jnp.dot(window_tile(qi, j)[1], _prob(sw[j], m_w), preferred_element_type=F32)
        st["o_win"] = acc_w[0:HEAD_DIM] * (1.0 / acc_w[HEAD_DIM:HEAD_DIM + 1])

    def output(ti, st):
        cols = slice(ti * TQ, (ti + 1) * TQ)
        gates = gate_ref[0, :, cols]
        def gate_row(j):
            return jnp.concatenate([gates[j * HPG + h:j * HPG + h + 1, :] for h in range(HPG)], axis=1)
        o_t = gate_row(0) * st["o_cmp"] + gate_row(1) * st["o_sel"] + gate_row(2) * st["o_win"]
        for hp in range(HPG // 2):
            pair = jnp.concatenate([o_t[:, (2 * hp) * TQ:(2 * hp + 1) * TQ],
                                    o_t[:, (2 * hp + 1) * TQ:(2 * hp + 2) * TQ]], axis=0)
            o_ref[0, cols, hp * 2 * HEAD_DIM:(hp + 1) * 2 * HEAD_DIM] = pair.T.astype(o_ref.dtype)

    def tail_phases(ti, qi, st):
        return [lambda: diagonal(ti, qi, st), lambda: output(ti, st)]

    cur = head(0, tiles[0])
    rank_init(0, tiles[0], cur)
    chunk_setup(0, tiles[0], cur)
    pending = []
    for ti, qi in enumerate(tiles):
        nxt = None
        if ti + 1 < len(tiles):
            nxt = head(ti + 1, tiles[ti + 1])
            rank_init(ti + 1, tiles[ti + 1], nxt)
            rounds_per_stage = -(-nxt["k_total"] // max(cur["n_chunks"], 1))
        pending.append(lambda ti=ti, qi=qi, st=cur: window(ti, qi, st))
        for k in range(cur["n_chunks"]):
            chunk_stage(cur, k)
            if nxt is not None:
                rank_rounds(ti + 1, nxt, rounds_per_stage)
            if pending:
                pending.pop(0)()
        for phase in pending:
            phase()
        chunk_finish(cur)
        pending = tail_phases(ti, qi, cur)
        if nxt is not None:
            chunk_setup(ti + 1, tiles[ti + 1], nxt)
            cur = nxt
    for phase in pending:
        phase()


def _attention_static(q_t, gates_t, k_cmp, v_cmp_t, k_sel, k_win, v_t):
    b, _, t = q_t.shape
    n_blk = t // SEL_BLOCK
    tiles_per_call = TILES_PER_CALL
    tb = tiles_per_call * TQ
    assert tb >= WINDOW and t % max(tb, TK_SEL) == 0 and t >= 2 * tb and F_SEL + n_blk <= F_POS, \
        "unsupported sequence length"
    n_sel = min(N_SEL, n_blk)
    lanes = HPG * TQ
    rows = HPG * HEAD_DIM
    alibi = _alibi_query_features()
    out_shape = jax.ShapeDtypeStruct((b, t, N_HEADS * HEAD_DIM), MXU_DTYPE)

    attn = jnp.zeros(out_shape.shape, out_shape.dtype)
    for m in range(t // tb):
        tiles = tuple(range(m * tiles_per_call, (m + 1) * tiles_per_call))
        ncp = min(t // CMP_STRIDE, -(-((m + 1) * tb // CMP_STRIDE) // V7X_LANES) * V7X_LANES)
        blk_map_t = _block_map_t(ncp, n_blk)
        kc = -(-tiles[-1] * TQ // TK_SEL) * TK_SEL
        wb0 = max(m - 1, 0)
        in_specs = [
            pl.BlockSpec((1, rows, tb), lambda bi, g, m=m: (bi, g, m)),
            pl.BlockSpec((1, 16, tb), lambda bi, g, m=m: (bi, g, m)),
            pl.BlockSpec((1, 1, ncp, FEAT), lambda bi, g: (bi, g, 0, 0)),
            pl.BlockSpec((1, 1, V_ROWS, ncp), lambda bi, g: (bi, g, 0, 0)),
            pl.BlockSpec((1, 1, kc, FEAT), lambda bi, g: (bi, g, 0, 0)),
            pl.BlockSpec((1, V_ROWS, kc), lambda bi, g: (bi, g, 0)),
            pl.BlockSpec((1, 1, tb, FEAT), lambda bi, g, m=m: (bi, g, m, 0)),
            pl.BlockSpec((1, V_ROWS, tb), lambda bi, g, m=m: (bi, g, m)),
            pl.BlockSpec((1, 1, tb, FEAT), lambda bi, g, w=wb0: (bi, g, w, 0)),
            pl.BlockSpec((1, 1, tb, FEAT), lambda bi, g, w=wb0: (bi, g, w + 1, 0)),
            pl.BlockSpec((1, V_ROWS, tb), lambda bi, g, w=wb0: (bi, N_KV + g, w)),
            pl.BlockSpec((1, V_ROWS, tb), lambda bi, g, w=wb0: (bi, N_KV + g, w + 1)),
            pl.BlockSpec((1, FEAT - F_POS, lanes), lambda bi, g: (g, 0, 0)),
            pl.BlockSpec((n_blk, ncp), lambda bi, g: (0, 0)),
        ]
        in_specs.append(pl.BlockSpec(memory_space=pl.ANY))
        args = [q_t, gates_t, k_cmp, v_cmp_t, k_sel, v_t, k_sel, v_t, k_win, k_win, v_t, v_t, alibi, blk_map_t,
                attn]
        kernel = functools.partial(_attn_tiles_kernel, tiles=tiles, win_blk0=wb0, n_blk=n_blk, n_sel=n_sel)
        attn = pl.pallas_call(
            kernel,
            grid=(b, N_KV),
            in_specs=in_specs,
            out_specs=pl.BlockSpec((1, tb, rows), lambda bi, g, m=m: (bi, m, g)),
            out_shape=out_shape,
            scratch_shapes=[pltpu.VMEM((tiles_per_call, FEAT, lanes), MXU_DTYPE),
                            pltpu.VMEM((tiles_per_call, FEAT, lanes), MXU_DTYPE),
                            pltpu.VMEM((tiles_per_call, n_blk, TQ), F32)],
            input_output_aliases={len(args) - 1: 0},
            compiler_params=_params("parallel", "parallel"),
            name=f"nsa_attention_{m}",
        )(*args)
    return attn


def _lru_gates(hin, wl_ref, cw_ref, cb_ref, wa_ref, wi_ref, tail_ref, h_ref):
    tt = hin.shape[0]
    w = cb_ref.shape[1]

    @pl.when(pl.program_id(1) == 0)
    def _():
        tail_ref[...] = jnp.zeros_like(tail_ref)
        h_ref[...] = jnp.zeros_like(h_ref)

    def gate_matmul(xb, w_ref):
        n = w_ref.shape[1]
        return jnp.concatenate([jnp.dot(xb[:, j * n:(j + 1) * n], w_ref[j], preferred_element_type=F32)
                                for j in range(w_ref.shape[0])], axis=1)

    x = jnp.dot(hin, wl_ref[:, 0:w], preferred_element_type=F32)
    gate = jnp.dot(hin, wl_ref[:, w:2 * w], preferred_element_type=F32)
    ng = tt // V7X_SUBLANES
    sub = lax.broadcasted_iota(jnp.int32, (ng, V7X_SUBLANES, w), 1)
    x3 = x.reshape(ng, V7X_SUBLANES, w)
    xprev3 = jnp.concatenate([tail_ref[...][None], x3], axis=0)
    tail_ref[...] = x[tt - V7X_SUBLANES:tt]
    xc = x * cw_ref[CONV_W - 1:CONV_W, :] + cb_ref[...]
    for s in range(1, CONV_W):
        rot = pltpu.roll(xprev3, s, 1)
        xs = jnp.where(sub >= s, rot[1:], rot[:-1])
        xc = xc + xs.reshape(tt, w) * cw_ref[CONV_W - 1 - s:CONV_W - s, :]

    xb = xc.astype(MXU_DTYPE)
    return xc, gate_matmul(xb, wa_ref), gate_matmul(xb, wi_ref), gate


def _lru_scan(xc, r_pre, i_pre, gate, ba_ref, bi_ref, lam_ref, o_ref, h_ref):
    tt, w = xc.shape
    ng = tt // V7X_SUBLANES
    sub = lax.broadcasted_iota(jnp.int32, (ng, V7X_SUBLANES, w), 1)
    r = _sigmoid(r_pre + ba_ref[...])
    i = _sigmoid(i_pre + bi_ref[...])
    z = -lam_ref[...]
    softplus = jnp.maximum(z, 0.0) + jnp.log1p(jnp.exp(-jnp.abs(z)))
    log_a = -LRU_C * r * softplus
    a = jnp.exp(log_a)
    bb = jnp.sqrt(_one_minus_sq(a, log_a)) * (i * xc)

    a3 = a.reshape(ng, V7X_SUBLANES, w)
    b3 = bb.reshape(ng, V7X_SUBLANES, w)
    for d in (1, 2, 4):
        ok = sub >= d
        a_sh = pltpu.roll(a3, d, 1)
        b_sh = pltpu.roll(b3, d, 1)
        b3 = jnp.where(ok, a3 * b_sh + b3, b3)
        a3 = jnp.where(ok, a3 * a_sh, a3)
    carry = h_ref[0:1, :]
    groups = []
    for g in range(ng):
        hg = b3[g] + a3[g] * carry
        groups.append(hg)
        carry = hg[V7X_SUBLANES - 1:V7X_SUBLANES, :]
    hcur = jnp.concatenate(groups, axis=0)
    h_ref[...] = jnp.broadcast_to(carry, h_ref.shape)
    o_ref[0] = (hcur * _gelu_tanh(gate)).astype(o_ref.dtype)


def _merge_kernel(attn_ref, lru_ref, h_ref, x_ref, wm_ref, wa_ref, wl_ref, wo_ref, g2_ref,
                  x1_ref, h2_ref):
    d = x_ref.shape[1]
    h = h_ref[...]
    mg0 = _sigmoid(jnp.dot(h, wm_ref[:, 0:d], preferred_element_type=F32))
    mg1 = _sigmoid(jnp.dot(h, wm_ref[:, d:2 * d], preferred_element_type=F32))
    ya = jnp.dot(attn_ref[...], wa_ref[...], preferred_element_type=F32)
    yl = jnp.dot(lru_ref[...], wl_ref[...], preferred_element_type=F32)
    merged = mg0 * ya + mg1 * yl
    x1 = x_ref[...] + jnp.dot(merged.astype(MXU_DTYPE), wo_ref[...], preferred_element_type=F32)
    x1_ref[...] = x1
    ms = jnp.mean(x1 * x1, axis=-1, keepdims=True)
    h2_ref[...] = (x1 * lax.rsqrt(ms + EPS) * g2_ref[...]).astype(h2_ref.dtype)


def _merge(attn, lru, h2d, x2d, w_mg, wa, wl, wo, g2, tm=512):
    m, d = x2d.shape
    row = pl.BlockSpec((tm, d), lambda i: (i, 0))
    full = lambda a: pl.BlockSpec(a.shape, lambda i: (0, 0))
    g2 = g2.reshape(1, d)
    return pl.pallas_call(
        _merge_kernel,
        grid=(m // tm,),
        in_specs=[row, row, row, row, full(w_mg), full(wa), full(wl), full(wo), full(g2)],
        out_specs=[row, row],
        out_shape=[jax.ShapeDtypeStruct((m, d), F32), jax.ShapeDtypeStruct((m, d), MXU_DTYPE)],
        compiler_params=_params("parallel"),
        name="merge_out",
    )(attn, lru, h2d, x2d, w_mg, wa, wl, wo, g2)


def _ffn_kernel(h_ref, x1_ref, wg_ref, wu_ref, wd_ref, o_ref):
    @pl.when(pl.program_id(1) == 0)
    def _():
        o_ref[...] = x1_ref[...]

    h = h_ref[...]
    g = jnp.dot(h, wg_ref[...].astype(MXU_DTYPE), preferred_element_type=F32)
    u = jnp.dot(h, wu_ref[...].astype(MXU_DTYPE), preferred_element_type=F32)
    act = (g * _sigmoid(g) * u).astype(MXU_DTYPE)
    o_ref[...] += jnp.dot(act, wd_ref[...].astype(MXU_DTYPE), preferred_element_type=F32)


def _ffn(h2, x1, wg, wu, wd, tm=1024, tf=256):
    m, d = x1.shape
    f = wg.shape[1]
    return pl.pallas_call(
        _ffn_kernel,
        grid=(m // tm, f // tf),
        in_specs=[pl.BlockSpec((tm, d), lambda i, j: (i, 0)),
                  pl.BlockSpec((tm, d), lambda i, j: (i, 0)),
                  pl.BlockSpec((d, tf), lambda i, j: (0, j)),
                  pl.BlockSpec((d, tf), lambda i, j: (0, j)),
                  pl.BlockSpec((tf, d), lambda i, j: (j, 0))],
        out_specs=pl.BlockSpec((tm, d), lambda i, j: (i, 0)),
        out_shape=jax.ShapeDtypeStruct((m, d), F32),
        compiler_params=_params("parallel", "arbitrary"),
        name="swiglu_ffn",
    )(h2, x1, wg, wu, wd)


def _pad_last(a, n):
    return jnp.pad(a, [(0, 0)] * (a.ndim - 1) + [(0, n - a.shape[-1])])


def _layer(x, norm1_g, w_in, q_norm_g, k_norm_g, cmp_pos_k, cmp_w1_k, cmp_b1_k, cmp_w2_k, cmp_b2_k,
           cmp_pos_v, cmp_w1_v, cmp_b1_v, cmp_w2_v, cmp_b2_v, conv_w, conv_b, lru_wa, lru_ba,
           lru_wi, lru_bi, lru_lambda, w_o_attn, w_o_lru, w_out, norm2_g, w_gate, w_up, w_down):
    b, t, d = x.shape
    m = b * t
    attn_dim = N_HEADS * HEAD_DIM
    kv_dim = N_KV * HEAD_DIM
    lru_w = lru_lambda.shape[0]
    assert t % 512 == 0
    o1 = attn_dim
    o2 = o1 + 6 * kv_dim
    o3 = o2 + 3 * N_HEADS
    o4 = o3 + lru_w
    o5 = o4 + lru_w
    cast = lambda a: a.astype(MXU_DTYPE)

    wq_t = cast(w_in[:, :o1].T)
    w_kv = w_in[:, o1:o2].reshape(d, 6, N_KV, HEAD_DIM)
    w_cmp_src = cast(w_kv[:, 0:2].reshape(d, 2 * kv_dim))
    w_k2 = cast(jnp.stack([w_kv[:, 2], w_kv[:, 4]], axis=1).reshape(d, 2 * kv_dim))
    k_gain2 = jnp.stack([jnp.tile(k_norm_g[1], N_KV), jnp.tile(k_norm_g[2], N_KV)])
    lane_grp = np.arange(kv_dim) // HEAD_DIM
    grp_avg = cast(jnp.asarray((lane_grp[:, None] == lane_grp[None, :]) / HEAD_DIM, F32))
    place_np = np.zeros((kv_dim, N_KV * FEAT), np.float32)
    place_np[np.arange(kv_dim), lane_grp * FEAT + np.arange(kv_dim) % HEAD_DIM] = 1.0
    place = cast(jnp.asarray(place_np))
    w_v = jnp.stack([w_kv[:, 3], w_kv[:, 5]], axis=1)
    w_v_t = cast(_pad_last(w_v, V_ROWS).reshape(d, 2 * N_KV * V_ROWS).T)
    v_ones_col = jnp.tile(jnp.arange(V_ROWS) == HEAD_DIM, 2 * N_KV).astype(F32).reshape(-1, 1)
    w_g = w_in[:, o2:o3].reshape(d, N_KV, HPG, 3).transpose(0, 1, 3, 2).reshape(d, N_KV, 3 * HPG)
    w_g_t = cast(_pad_last(w_g, 16).reshape(d, N_KV * 16).T)
    w_lru = cast(w_in[:, o3:o5])
    w_mg = cast(w_in[:, o5:])
    q_gain_col = jnp.tile(q_norm_g * (HEAD_DIM ** -0.5 * LOG2E), N_HEADS).reshape(attn_dim, 1)
    gain_pad = lambda g: _pad_last(g.reshape(1, HEAD_DIM), FEAT)

    per_slab = V7X_MXU_DIM // (lru_w // LRU_BLOCKS)
    eye = jnp.eye(per_slab, dtype=F32)
    slabs = lambda wgt: cast(jnp.einsum('snkj,nm->snkmj', wgt.reshape(LRU_BLOCKS // per_slab, per_slab,
                                                                       *wgt.shape[1:]), eye)
                             .reshape(LRU_BLOCKS // per_slab, V7X_MXU_DIM, V7X_MXU_DIM))
    vec = lambda v: v.reshape(1, lru_w)

    h3d, q_t, v_t, gates_t, k_sel, k_win, cmp_src, lru = _in_proj(
        x,
        (norm1_g.reshape(1, d), wq_t, q_gain_col, w_v_t, v_ones_col, w_g_t, w_k2, k_gain2, grp_avg, place,
         w_cmp_src),
        (w_lru, conv_w.reshape(CONV_W, lru_w), vec(conv_b), slabs(lru_wa), vec(lru_ba), slabs(lru_wi),
         vec(lru_bi), vec(lru_lambda)))
    h2d = h3d.reshape(m, d)

    k_cmp = _compress(cmp_src, 0, cmp_pos_k, cmp_w1_k, cmp_b1_k, cast(_pad_last(cmp_w2_k, FEAT)),
                      _pad_last(cmp_b2_k.reshape(1, HEAD_DIM), FEAT), gain_pad(k_norm_g[0]))
    v_cmp_t = _compress(cmp_src, N_KV // CMP_GROUPS, cmp_pos_v, cmp_w1_v, cmp_b1_v,
                        cast(_pad_last(cmp_w2_v, V_ROWS).T),
                        jnp.concatenate([cmp_b2_v, v_ones_col[HEAD_DIM:V_ROWS, 0]]).reshape(V_ROWS, 1))

    attn = _attention_static(q_t, gates_t, k_cmp, v_cmp_t, k_sel, k_win, v_t)

    x1, h2 = _merge(attn.reshape(m, attn_dim), lru.reshape(m, lru_w), h2d, x.reshape(m, d), w_mg,
                    cast(w_o_attn), cast(w_o_lru), cast(w_out), norm2_g)
    out = _ffn(h2, x1, w_gate, w_up, w_down)
    return out.reshape(b, t, d)


def kernel(x, norm1_g, w_in, q_norm_g, k_norm_g, cmp_pos_k, cmp_w1_k, cmp_b1_k, cmp_w2_k, cmp_b2_k,
           cmp_pos_v, cmp_w1_v, cmp_b1_v, cmp_w2_v, cmp_b2_v, conv_w, conv_b, lru_wa, lru_ba,
           lru_wi, lru_bi, lru_lambda, w_o_attn, w_o_lru, w_out, norm2_g, w_gate, w_up, w_down):
    for l in range(norm1_g.shape[0]):
        x = _layer(x, norm1_g[l], w_in[l], q_norm_g[l], k_norm_g[l], cmp_pos_k[l], cmp_w1_k[l],
                   cmp_b1_k[l], cmp_w2_k[l], cmp_b2_k[l], cmp_pos_v[l], cmp_w1_v[l], cmp_b1_v[l],
                   cmp_w2_v[l], cmp_b2_v[l], conv_w[l], conv_b[l], lru_wa[l], lru_ba[l], lru_wi[l],
                   lru_bi[l], lru_lambda[l], w_o_attn[l], w_o_lru[l], w_out[l], norm2_g[l],
                   w_gate[l], w_up[l], w_down[l])
    return x
```

```python
import functools

import numpy as np
import jax
import jax.numpy as jnp
from jax import lax
from jax.experimental import pallas as pl
from jax.experimental.pallas import tpu as pltpu

N_HEADS = 16
HEAD_DIM = 64
N_KV = 4
HPG = N_HEADS // N_KV
CMP_BLOCK = 32
CMP_STRIDE = 16
CMP_HIDDEN = 256
SEL_BLOCK = 64
N_SEL = 16
WINDOW = 512
FORCE_BONUS = 1e4
LRU_BLOCKS = 16
CONV_W = 4
LRU_C = 8.0
EPS = 1e-6

MXU_DTYPE = jnp.bfloat16
F32 = jnp.float32

V7X_LANES = 128
V7X_SUBLANES = 8
V7X_MXU_DIM = 256
V7X_VMEM_LIMIT_BYTES = 48 * 1024 * 1024

TQ = 256
TK_SEL = 256
TILES_PER_CALL = 2
V_ROWS = 80
LOG2E = 1.4426950408889634
FEAT = V7X_MXU_DIM
F_SEL = HEAD_DIM
F_POS = 2 * HEAD_DIM
F_CMP = F_POS + 6
NEG_MASK = -1e30
NEG_BLOCK = -(2.0 ** 100)


def _params(*sem):
    return pltpu.CompilerParams(dimension_semantics=sem, vmem_limit_bytes=V7X_VMEM_LIMIT_BYTES)


def _gelu_tanh(x):
    return 0.5 * x * (1.0 + jnp.tanh(0.7978845608028654 * (x + 0.044715 * (x * x * x))))


def _sigmoid(x):
    return 1.0 / (1.0 + jnp.exp(-x))


def _one_minus_sq(a, log_a):
    series = (-2.0 * log_a) * (1.0 + log_a * (1.0 + log_a * (2.0 / 3)))
    return jnp.where(log_a > -1.0 / 128, series, 1.0 - a * a)


def _nt_dot(wt, h):
    return lax.dot_general(wt, h, (((1,), (1,)), ((), ())), preferred_element_type=F32)


def _token_features(pos, col, with_block_mask):
    blk = pos >> 6
    off = pos & (SEL_BLOCK - 1)
    feat = jnp.where((col >= F_POS) & (col < F_POS + 3), blk.astype(F32),
                     jnp.where((col >= F_POS + 3) & (col < F_POS + 6), off.astype(F32), 0.0))
    if with_block_mask:
        feat = jnp.where((col >= F_SEL) & (col - F_SEL == blk) & (col < F_POS), NEG_BLOCK, feat)
    return feat


def _in_proj_kernel(x_ref, g1_ref, wq_ref, qg_ref, wv_ref, vb_ref, wg_ref, wk_ref, kg_ref, grp_ref,
                    place_ref, wc_ref, wl_ref, cw_ref, cb_ref, wa_ref, ba_ref, wi_ref, bi_ref, lam_ref,
                    h_ref, qt_ref, vt_ref, gt_ref, ks_ref, kw_ref, cs_ref, lru_ref, tail_ref, state_ref):
    x = x_ref[0]
    tm = x.shape[0]
    ms = jnp.mean(x * x, axis=-1, keepdims=True)
    h = (x * lax.rsqrt(ms + EPS) * g1_ref[...]).astype(MXU_DTYPE)
    h_ref[0] = h

    lru_mid = _lru_gates(h, wl_ref, cw_ref, cb_ref, wa_ref, wi_ref, tail_ref, state_ref)

    r3 = _nt_dot(wq_ref[...], h).reshape(N_HEADS, HEAD_DIM, tm)
    qn = r3 * lax.rsqrt(jnp.mean(r3 * r3, axis=1, keepdims=True) + EPS)
    qt_ref[0] = (qn.reshape(N_HEADS * HEAD_DIM, tm) * qg_ref[...]).astype(qt_ref.dtype)

    vt_ref[0] = (_nt_dot(wv_ref[...], h) + vb_ref[...]).astype(vt_ref.dtype)
    gt_ref[0] = _sigmoid(_nt_dot(wg_ref[...], h))

    k = jnp.dot(h, wk_ref[...], preferred_element_type=F32)
    kk = k * k
    kk_hi = kk.astype(MXU_DTYPE)
    kk_lo = (kk - kk_hi.astype(F32)).astype(MXU_DTYPE)
    pos = pl.program_id(1) * tm + lax.broadcasted_iota(jnp.int32, (tm, FEAT), 0)
    col = lax.broadcasted_iota(jnp.int32, (tm, FEAT), 1)
    kv_dim = N_KV * HEAD_DIM
    for branch, (o_ref, with_block_mask) in enumerate(((ks_ref, True), (kw_ref, False))):
        sl = slice(branch * kv_dim, (branch + 1) * kv_dim)
        msq = (jnp.dot(kk_hi[:, sl], grp_ref[...], preferred_element_type=F32)
               + jnp.dot(kk_lo[:, sl], grp_ref[...], preferred_element_type=F32))
        kn = (k[:, sl] * lax.rsqrt(msq + EPS) * kg_ref[branch:branch + 1, :]).astype(MXU_DTYPE)
        placed = jnp.dot(kn, place_ref[...], preferred_element_type=F32)
        feat = _token_features(pos, col, with_block_mask)
        for g in range(N_KV):
            o_ref[0, g] = (placed[:, g * FEAT:(g + 1) * FEAT] + feat).astype(o_ref.dtype)

    cs_ref[0] = jnp.dot(h, wc_ref[...], preferred_element_type=F32)

    _lru_scan(*lru_mid, ba_ref, bi_ref, lam_ref, lru_ref, state_ref)


def _in_proj(x, attn_weights, lru_weights, tm=512):
    b, t, d = x.shape
    kv_dim = N_KV * HEAD_DIM
    wq_t, w_v_t, w_g_t = attn_weights[1], attn_weights[3], attn_weights[5]
    lru_w = lru_weights[-1].shape[1]
    full = lambda a: pl.BlockSpec(a.shape, lambda bi, i: (0,) * a.ndim)
    rowblk = lambda n: pl.BlockSpec((1, tm, n), lambda bi, i: (bi, i, 0))
    colblk = lambda n: pl.BlockSpec((1, n, tm), lambda bi, i: (bi, 0, i))
    kblk = pl.BlockSpec((1, N_KV, tm, FEAT), lambda bi, i: (bi, 0, i, 0))
    weights = tuple(attn_weights) + tuple(lru_weights)
    return pl.pallas_call(
        _in_proj_kernel,
        grid=(b, t // tm),
        in_specs=[rowblk(d)] + [full(a) for a in weights],
        out_specs=[rowblk(d), colblk(wq_t.shape[0]), colblk(w_v_t.shape[0]), colblk(w_g_t.shape[0]),
                   kblk, kblk, rowblk(2 * kv_dim), rowblk(lru_w)],
        out_shape=[jax.ShapeDtypeStruct((b, t, d), MXU_DTYPE),
                   jax.ShapeDtypeStruct((b, wq_t.shape[0], t), MXU_DTYPE),
                   jax.ShapeDtypeStruct((b, w_v_t.shape[0], t), MXU_DTYPE),
                   jax.ShapeDtypeStruct((b, w_g_t.shape[0], t), F32),
                   jax.ShapeDtypeStruct((b, N_KV, t, FEAT), MXU_DTYPE),
                   jax.ShapeDtypeStruct((b, N_KV, t, FEAT), MXU_DTYPE),
                   jax.ShapeDtypeStruct((b, t, 2 * kv_dim), F32),
                   jax.ShapeDtypeStruct((b, t, lru_w), MXU_DTYPE)],
        scratch_shapes=[pltpu.VMEM((V7X_SUBLANES, lru_w), F32),
                        pltpu.VMEM((V7X_SUBLANES, lru_w), F32)],
        compiler_params=_params("parallel", "arbitrary"),
        name="in_proj_lru",
    )(x, *weights)


CMP_GROUPS = V7X_LANES // HEAD_DIM


def _cmp_hidden(src_ref, pos_ref, w1_ref, b1_ref):
    ncp = src_ref.shape[1] // CMP_STRIDE
    lo, hi = [], []
    for l in range(CMP_STRIDE):
        x = src_ref[0, pl.ds(l, ncp, stride=CMP_STRIDE), :]
        lo.append((x + pos_ref[l:l + 1, :]).astype(MXU_DTYPE))
        hi.append((x + pos_ref[CMP_STRIDE + l:CMP_STRIDE + l + 1, :]).astype(MXU_DTYPE))
    kdim = CMP_STRIDE * V7X_LANES
    first = jnp.dot(jnp.concatenate(lo, axis=1), w1_ref[0:CMP_STRIDE].reshape(kdim, -1),
                    preferred_element_type=F32)
    second = jnp.dot(jnp.concatenate(hi, axis=1), w1_ref[CMP_STRIDE:CMP_BLOCK].reshape(kdim, -1),
                     preferred_element_type=F32)
    hid = first + pltpu.roll(second, ncp - 1, 0) + b1_ref[...]
    return _gelu_tanh(hid).astype(MXU_DTYPE)


def _cmp_k_kernel(src_ref, pos_ref, w1_ref, b1_ref, w2_ref, b2_ref, g_ref, o_ref):
    hid = _cmp_hidden(src_ref, pos_ref, w1_ref, b1_ref)
    ncp = hid.shape[0]
    idx = lax.broadcasted_iota(jnp.int32, (ncp, FEAT), 0)
    col = lax.broadcasted_iota(jnp.int32, (ncp, FEAT), 1)
    feat = jnp.where((col >= F_CMP) & (col < F_CMP + 3), (idx >> 6).astype(F32),
                     jnp.where((col >= F_CMP + 3) & (col < F_CMP + 6), (idx & 63).astype(F32), 0.0))
    for gl in range(CMP_GROUPS):
        r = jnp.dot(hid[:, gl * CMP_HIDDEN:(gl + 1) * CMP_HIDDEN], w2_ref[...],
                    preferred_element_type=F32) + b2_ref[...]
        ms = jnp.sum(r * r, axis=-1, keepdims=True) * (1.0 / HEAD_DIM)
        o_ref[0, gl] = (r * lax.rsqrt(ms + EPS) * g_ref[...] + feat).astype(o_ref.dtype)


def _cmp_v_kernel(src_ref, pos_ref, w1_ref, b1_ref, w2t_ref, b2_ref, o_ref):
    hid = _cmp_hidden(src_ref, pos_ref, w1_ref, b1_ref)
    for gl in range(CMP_GROUPS):
        r = _nt_dot(w2t_ref[...], hid[:, gl * CMP_HIDDEN:(gl + 1) * CMP_HIDDEN]) + b2_ref[...]
        o_ref[0, gl] = r.astype(o_ref.dtype)


def _compress(cmp_src, lane_block0, pos, w1, b1, w2, b2, gain_pad=None):
    b, t, _ = cmp_src.shape
    ncp = t // CMP_STRIDE
    hid_w = CMP_GROUPS * CMP_HIDDEN
    w1c = w1.astype(MXU_DTYPE)
    zero = jnp.zeros_like(w1c)
    w1_bd = jnp.concatenate([jnp.concatenate([w1c if h == g else zero for h in range(CMP_GROUPS)], axis=2)
                             for g in range(CMP_GROUPS)], axis=1)
    pos_t = jnp.tile(pos, (1, CMP_GROUPS))
    b1_t = jnp.tile(b1.reshape(1, CMP_HIDDEN), (1, CMP_GROUPS))
    full = lambda a: pl.BlockSpec(a.shape, lambda bi, p: (0,) * a.ndim)
    src_spec = pl.BlockSpec((1, t, V7X_LANES), lambda bi, p: (bi, 0, lane_block0 + p))
    grid = (b, N_KV // CMP_GROUPS)
    if gain_pad is not None:
        args = (pos_t, w1_bd, b1_t, w2, b2, gain_pad)
        return pl.pallas_call(
            _cmp_k_kernel,
            grid=grid,
            in_specs=[src_spec] + [full(a) for a in args],
            out_specs=pl.BlockSpec((1, CMP_GROUPS, ncp, FEAT), lambda bi, p: (bi, p, 0, 0)),
            out_shape=jax.ShapeDtypeStruct((b, N_KV, ncp, FEAT), MXU_DTYPE),
            compiler_params=_params("parallel", "parallel"),
            name="compress_k",
        )(cmp_src, *args)
    args = (pos_t, w1_bd, b1_t, w2, b2)
    return pl.pallas_call(
        _cmp_v_kernel,
        grid=grid,
        in_specs=[src_spec] + [full(a) for a in args],
        out_specs=pl.BlockSpec((1, CMP_GROUPS, V_ROWS, ncp), lambda bi, p: (bi, p, 0, 0)),
        out_shape=jax.ShapeDtypeStruct((b, N_KV, V_ROWS, ncp), MXU_DTYPE),
        compiler_params=_params("parallel", "parallel"),
        name="compress_v",
    )(cmp_src, *args)


def _split3(v):
    parts = []
    rest = np.asarray(v, np.float64)
    for _ in range(3):
        p = rest.astype(np.float32).astype(jnp.bfloat16).astype(np.float64)
        parts.append(p)
        rest = rest - p
    return parts


def _alibi_query_features():
    tab = np.zeros((N_KV, FEAT - F_POS, HPG * TQ), np.float64)
    for g in range(N_KV):
        for h in range(HPG):
            slope = 2.0 ** (-8.0 * (g * HPG + h + 1) / N_HEADS)
            parts = _split3(slope * LOG2E)
            lanes = slice(h * TQ, (h + 1) * TQ)
            for i, p in enumerate(parts):
                tab[g, i, lanes] = SEL_BLOCK * p
                tab[g, 3 + i, lanes] = p
                tab[g, 6 + i, lanes] = CMP_STRIDE * 64 * p
                tab[g, 9 + i, lanes] = CMP_STRIDE * p
    return jnp.asarray(tab, F32).astype(MXU_DTYPE)


def _block_map_t(n_cmp_pad, n_blk):
    cs = np.arange(n_cmp_pad) * CMP_STRIDE
    ce = cs + CMP_BLOCK - 1
    bs = np.arange(n_blk) * SEL_BLOCK
    be = bs + SEL_BLOCK - 1
    return jnp.asarray(((cs[None, :] <= be[:, None]) & (ce[None, :] >= bs[:, None])).astype(np.float32))


def _prob(s, m):
    return jnp.exp2(s - m).astype(MXU_DTYPE)


def _attn_tiles_kernel(*refs, tiles, win_blk0, n_blk, n_sel):
    (qt_ref, gate_ref, kc_ref, vc_ref, ks_ref, vs_ref, kd_ref, vd_ref, kwa_ref, kwb_ref, vwa_ref, vwb_ref,
     alibi_ref, map_ref) = refs[:14]
    o_ref, qb_ref, qs_ref, imp_ref = refs[15:]
    lanes = HPG * TQ
    ncp = kc_ref.shape[2]
    blocks_per_tile = TQ // SEL_BLOCK
    win_tiles = WINDOW // TQ
    lane_tok = lax.broadcasted_iota(jnp.int32, (1, lanes), 1) & (TQ - 1)
    row_pos = lax.broadcasted_iota(jnp.int32, (TQ, lanes), 0)
    blk = lax.broadcasted_iota(jnp.int32, (n_blk, TQ), 0)

    def window_tile(qi, j):
        n = len(tiles)
        lt = max(qi - win_tiles, 0) + j - n * win_blk0
        kref, vref = (kwa_ref, vwa_ref) if lt < n else (kwb_ref, vwb_ref)
        sl = slice((lt % n) * TQ, (lt % n + 1) * TQ)
        return kref[0, 0, sl, :], vref[0, :, sl]

    def head(ti, qi):
        cols = slice(ti * TQ, (ti + 1) * TQ)
        t_lane = qi * TQ + lane_tok
        for h in range(HPG):
            qb_ref[ti, 0:HEAD_DIM, h * TQ:(h + 1) * TQ] = qt_ref[0, h * HEAD_DIM:(h + 1) * HEAD_DIM, cols]
        qb_ref[ti, F_SEL:F_POS, :] = jnp.zeros((F_POS - F_SEL, lanes), qb_ref.dtype)
        qb_ref[ti, F_POS:FEAT, :] = alibi_ref[0]
        qb = qb_ref[ti]
        sc = jnp.dot(kc_ref[0, 0], qb, preferred_element_type=F32)
        sw = [jnp.dot(window_tile(qi, j)[0], qb, preferred_element_type=F32) for j in range(3)]
        sd = jnp.dot(kd_ref[0, 0, cols, :], qb, preferred_element_type=F32)
        last_cmp = (t_lane - (CMP_BLOCK - 1)) >> 4
        sc = jnp.where(lax.broadcasted_iota(jnp.int32, (ncp, lanes), 0) <= last_cmp, sc, NEG_MASK)
        ec = jnp.exp2(sc - jnp.max(sc, axis=0, keepdims=True))
        acc_c = jnp.dot(vc_ref[0, 0], ec.astype(MXU_DTYPE), preferred_element_type=F32)
        inv_c = jnp.where(last_cmp >= 0, 1.0 / jnp.maximum(acc_c[HEAD_DIM:HEAD_DIM + 1], 1e-30), 0.0)
        imp = None
        if (qi + 1) * blocks_per_tile > n_sel and qi > 0:
            psum = ec[:, 0:TQ] * inv_c[:, 0:TQ]
            for h in range(1, HPG):
                psum = psum + ec[:, h * TQ:(h + 1) * TQ] * inv_c[:, h * TQ:(h + 1) * TQ]
            imp = jnp.dot(map_ref[...], psum, preferred_element_type=F32)
        return dict(qb=qb, t_lane=t_lane, sw=sw, sd=sd, o_cmp=acc_c[0:HEAD_DIM] * inv_c, imp=imp,
                    causal=qi * TQ + row_pos <= t_lane)

    def rank_init(ti, qi, st):
        first_own_blk = qi * blocks_per_tile
        st.update(ranks=[], k_done=0, k_total=0)
        if st["imp"] is None:
            return
        cur = (qi * TQ + lax.broadcasted_iota(jnp.int32, (n_blk, TQ), 1)) >> 6
        forced = (blk == 0) | (blk == cur) | (blk == cur - 1)
        imp = jnp.where(blk <= cur, st["imp"] + jnp.where(forced, FORCE_BONUS, 0.0), NEG_MASK)
        imp_ref[ti] = imp
        n_rank_chunks = -(-first_own_blk // V7X_SUBLANES)
        st["chunks"] = [imp[c * V7X_SUBLANES:(c + 1) * V7X_SUBLANES] for c in range(n_rank_chunks)]
        st["ranks"] = [jnp.zeros((V7X_SUBLANES, TQ), jnp.int32) for _ in range(n_rank_chunks)]
        st["k_total"] = min(first_own_blk + blocks_per_tile, n_blk)

    def rank_rounds(ti, st, n):
        sub = lax.broadcasted_iota(jnp.int32, (V7X_SUBLANES, TQ), 0)
        stop = min(st["k_done"] + n, st["k_total"])
        for k in range(st["k_done"], stop):
            row = imp_ref[ti, k:k + 1, :]
            for c, mine in enumerate(st["chunks"]):
                lo = c * V7X_SUBLANES
                if lo > k:
                    one = jnp.where(row >= mine, 1, 0)
                elif lo + V7X_SUBLANES - 1 <= k:
                    one = jnp.where(row > mine, 1, 0)
                else:
                    one = jnp.where(sub + lo > k, jnp.where(row >= mine, 1, 0), jnp.where(row > mine, 1, 0))
                st["ranks"][c] = st["ranks"][c] + one
        st["k_done"] = stop

    def chunk_setup(ti, qi, st):
        rank_rounds(ti, st, st["k_total"])
        first_own_blk = qi * blocks_per_tile
        n_chunks = -(-qi * TQ // TK_SEL)
        st.update(n_chunks=n_chunks, m=jnp.full((1, lanes), NEG_MASK, F32), acc=jnp.zeros((V_ROWS, lanes), F32),
                  p_prev=None)
        if n_chunks == 0:
            return
        rank = jnp.zeros((n_blk, TQ), jnp.int32)
        if st["ranks"]:
            pad = [jnp.zeros((n_blk - len(st["ranks"]) * V7X_SUBLANES, TQ), jnp.int32)]
            rank = jnp.concatenate(st["ranks"] + (pad if pad[0].shape[0] else []), axis=0)
        not_sel = jnp.where((rank < n_sel) & (blk < first_own_blk), 0.0, 1.0).astype(qs_ref.dtype)
        qs_ref[ti] = st["qb"]
        for h in range(HPG):
            qs_ref[ti, F_SEL:F_SEL + n_blk, h * TQ:(h + 1) * TQ] = not_sel
        st["qs"] = qs_ref[ti]
        st["s_next"] = chunk_qk(st, 0)

    def chunk_qk(st, k):
        return jnp.dot(ks_ref[0, 0, k * TK_SEL:(k + 1) * TK_SEL, :], st["qs"], preferred_element_type=F32)

    def chunk_pv(st, k):
        return jnp.dot(vs_ref[0, :, k * TK_SEL:(k + 1) * TK_SEL], st["p_prev"], preferred_element_type=F32)

    def chunk_stage(st, k):
        s_cur = st["s_next"]
        if k + 1 < st["n_chunks"]:
            st["s_next"] = chunk_qk(st, k + 1)
        if st["p_prev"] is not None:
            st["acc"] = st["acc"] + chunk_pv(st, k - 1)
        m_new = jnp.maximum(st["m"], jnp.max(s_cur, axis=0, keepdims=True))
        st["p_prev"] = _prob(s_cur, m_new)
        st["acc"] = jnp.exp2(st["m"] - m_new) * st["acc"]
        st["m"] = m_new

    def chunk_finish(st):
        if st["n_chunks"] > 0:
            st["acc"] = st["acc"] + chunk_pv(st, st["n_chunks"] - 1)

    def diagonal(ti, qi, st):
        cols = slice(ti * TQ, (ti + 1) * TQ)
        sd = jnp.where(st["causal"], st["sd"], NEG_MASK)
        m_d = jnp.max(sd, axis=0, keepdims=True)
        acc_d = jnp.dot(vd_ref[0, :, cols], _prob(sd, m_d), preferred_element_type=F32)
        m_all = jnp.maximum(st["m"], m_d)
        acc_s = jnp.exp2(st["m"] - m_all) * st["acc"] + jnp.exp2(m_d - m_all) * acc_d
        st["o_sel"] = acc_s[0:HEAD_DIM] * (1.0 / acc_s[HEAD_DIM:HEAD_DIM + 1])

    def window(ti, qi, st):
        sw, t_lane = st["sw"], st["t_lane"]
        wb = max(qi - win_tiles, 0)
        if qi >= win_tiles:
            d0 = (t_lane - wb * TQ) - row_pos
            sw = [jnp.where(d0 < WINDOW, sw[0], NEG_MASK), sw[1], jnp.where(st["causal"], sw[2], NEG_MASK)]
        else:
            for j in range(3):
                dj = (t_lane - (wb + j) * TQ) - row_pos
                sw[j] = jnp.where(lax.bitcast_convert_type(dj, jnp.uint32) < WINDOW, sw[j], NEG_MASK)
        m_w = jnp.max(jnp.maximum(jnp.maximum(sw[0], sw[1]), sw[2]), axis=0, keepdims=True)
        acc_w = jnp.zeros((V_ROWS, lanes), F32)
        for j in range(3):
            acc_w = acc_w + jnp.dot(window_tile(qi, j)[1], _prob(sw[j], m_w), preferred_element_type=F32)
        st["o_win"] = acc_w[0:HEAD_DIM] * (1.0 / acc_w[HEAD_DIM:HEAD_DIM + 1])

    def output(ti, st):
        cols = slice(ti * TQ, (ti + 1) * TQ)
        gates = gate_ref[0, :, cols]
        def gate_row(j):
            return jnp.concatenate([gates[j * HPG + h:j * HPG + h + 1, :] for h in range(HPG)], axis=1)
        o_t = gate_row(0) * st["o_cmp"] + gate_row(1) * st["o_sel"] + gate_row(2) * st["o_win"]
        for hp in range(HPG // 2):
            pair = jnp.concatenate([o_t[:, (2 * hp) * TQ:(2 * hp + 1) * TQ],
                                    o_t[:, (2 * hp + 1) * TQ:(2 * hp + 2) * TQ]], axis=0)
            o_ref[0, cols, hp * 2 * HEAD_DIM:(hp + 1) * 2 * HEAD_DIM] = pair.T.astype(o_ref.dtype)

    def tail_phases(ti, qi, st):
        return [lambda: diagonal(ti, qi, st), lambda: output(ti, st)]

    cur = head(0, tiles[0])
    rank_init(0, tiles[0], cur)
    chunk_setup(0, tiles[0], cur)
    pending = []
    for ti, qi in enumerate(tiles):
        nxt = None
        if ti + 1 < len(tiles):
            nxt = head(ti + 1, tiles[ti + 1])
            rank_init(ti + 1, tiles[ti + 1], nxt)
            rounds_per_stage = -(-nxt["k_total"] // max(cur["n_chunks"], 1))
        pending.append(lambda ti=ti, qi=qi, st=cur: window(ti, qi, st))
        for k in range(cur["n_chunks"]):
            chunk_stage(cur, k)
            if nxt is not None:
                rank_rounds(ti + 1, nxt, rounds_per_stage)
            if pending:
                pending.pop(0)()
        for phase in pending:
            phase()
        chunk_finish(cur)
        pending = tail_phases(ti, qi, cur)
        if nxt is not None:
            chunk_setup(ti + 1, tiles[ti + 1], nxt)
            cur = nxt
    for phase in pending:
        phase()


def _attention_static(q_t, gates_t, k_cmp, v_cmp_t, k_sel, k_win, v_t):
    b, _, t = q_t.shape
    n_blk = t // SEL_BLOCK
    tiles_per_call = TILES_PER_CALL
    tb = tiles_per_call * TQ
    assert tb >= WINDOW and t % max(tb, TK_SEL) == 0 and t >= 2 * tb and F_SEL + n_blk <= F_POS, \
        "unsupported sequence length"
    n_sel = min(N_SEL, n_blk)
    lanes = HPG * TQ
    rows = HPG * HEAD_DIM
    alibi = _alibi_query_features()
    out_shape = jax.ShapeDtypeStruct((b, t, N_HEADS * HEAD_DIM), MXU_DTYPE)

    attn = jnp.zeros(out_shape.shape, out_shape.dtype)
    for m in range(t // tb):
        tiles = tuple(range(m * tiles_per_call, (m + 1) * tiles_per_call))
        ncp = min(t // CMP_STRIDE, -(-((m + 1) * tb // CMP_STRIDE) // V7X_LANES) * V7X_LANES)
        blk_map_t = _block_map_t(ncp, n_blk)
        kc = -(-tiles[-1] * TQ // TK_SEL) * TK_SEL
        wb0 = max(m - 1, 0)
        in_specs = [
            pl.BlockSpec((1, rows, tb), lambda bi, g, m=m: (bi, g, m)),
            pl.BlockSpec((1, 16, tb), lambda bi, g, m=m: (bi, g, m)),
            pl.BlockSpec((1, 1, ncp, FEAT), lambda bi, g: (bi, g, 0, 0)),
            pl.BlockSpec((1, 1, V_ROWS, ncp), lambda bi, g: (bi, g, 0, 0)),
            pl.BlockSpec((1, 1, kc, FEAT), lambda bi, g: (bi, g, 0, 0)),
            pl.BlockSpec((1, V_ROWS, kc), lambda bi, g: (bi, g, 0)),
            pl.BlockSpec((1, 1, tb, FEAT), lambda bi, g, m=m: (bi, g, m, 0)),
            pl.BlockSpec((1, V_ROWS, tb), lambda bi, g, m=m: (bi, g, m)),
            pl.BlockSpec((1, 1, tb, FEAT), lambda bi, g, w=wb0: (bi, g, w, 0)),
            pl.BlockSpec((1, 1, tb, FEAT), lambda bi, g, w=wb0: (bi, g, w + 1, 0)),
            pl.BlockSpec((1, V_ROWS, tb), lambda bi, g, w=wb0: (bi, N_KV + g, w)),
            pl.BlockSpec((1, V_ROWS, tb), lambda bi, g, w=wb0: (bi, N_KV + g, w + 1)),
            pl.BlockSpec((1, FEAT - F_POS, lanes), lambda bi, g: (g, 0, 0)),
            pl.BlockSpec((n_blk, ncp), lambda bi, g: (0, 0)),
        ]
        in_specs.append(pl.BlockSpec(memory_space=pl.ANY))
        args = [q_t, gates_t, k_cmp, v_cmp_t, k_sel, v_t, k_sel, v_t, k_win, k_win, v_t, v_t, alibi, blk_map_t,
                attn]
        kernel = functools.partial(_attn_tiles_kernel, tiles=tiles, win_blk0=wb0, n_blk=n_blk, n_sel=n_sel)
        attn = pl.pallas_call(
            kernel,
            grid=(b, N_KV),
            in_specs=in_specs,
            out_specs=pl.BlockSpec((1, tb, rows), lambda bi, g, m=m: (bi, m, g)),
            out_shape=out_shape,
            scratch_shapes=[pltpu.VMEM((tiles_per_call, FEAT, lanes), MXU_DTYPE),
                            pltpu.VMEM((tiles_per_call, FEAT, lanes), MXU_DTYPE),
                            pltpu.VMEM((tiles_per_call, n_blk, TQ), F32)],
            input_output_aliases={len(args) - 1: 0},
            compiler_params=_params("parallel", "parallel"),
            name=f"nsa_attention_{m}",
        )(*args)
    return attn


def _lru_gates(hin, wl_ref, cw_ref, cb_ref, wa_ref, wi_ref, tail_ref, h_ref):
    tt = hin.shape[0]
    w = cb_ref.shape[1]

    @pl.when(pl.program_id(1) == 0)
    def _():
        tail_ref[...] = jnp.zeros_like(tail_ref)
        h_ref[...] = jnp.zeros_like(h_ref)

    def gate_matmul(xb, w_ref):
        n = w_ref.shape[1]
        return jnp.concatenate([jnp.dot(xb[:, j * n:(j + 1) * n], w_ref[j], preferred_element_type=F32)
                                for j in range(w_ref.shape[0])], axis=1)

    x = jnp.dot(hin, wl_ref[:, 0:w], preferred_element_type=F32)
    gate = jnp.dot(hin, wl_ref[:, w:2 * w], preferred_element_type=F32)
    ng = tt // V7X_SUBLANES
    sub = lax.broadcasted_iota(jnp.int32, (ng, V7X_SUBLANES, w), 1)
    x3 = x.reshape(ng, V7X_SUBLANES, w)
    xprev3 = jnp.concatenate([tail_ref[...][None], x3], axis=0)
    tail_ref[...] = x[tt - V7X_SUBLANES:tt]
    xc = x * cw_ref[CONV_W - 1:CONV_W, :] + cb_ref[...]
    for s in range(1, CONV_W):
        rot = pltpu.roll(xprev3, s, 1)
        xs = jnp.where(sub >= s, rot[1:], rot[:-1])
        xc = xc + xs.reshape(tt, w) * cw_ref[CONV_W - 1 - s:CONV_W - s, :]

    xb = xc.astype(MXU_DTYPE)
    return xc, gate_matmul(xb, wa_ref), gate_matmul(xb, wi_ref), gate


def _lru_scan(xc, r_pre, i_pre, gate, ba_ref, bi_ref, lam_ref, o_ref, h_ref):
    tt, w = xc.shape
    ng = tt // V7X_SUBLANES
    sub = lax.broadcasted_iota(jnp.int32, (ng, V7X_SUBLANES, w), 1)
    r = _sigmoid(r_pre + ba_ref[...])
    i = _sigmoid(i_pre + bi_ref[...])
    z = -lam_ref[...]
    softplus = jnp.maximum(z, 0.0) + jnp.log1p(jnp.exp(-jnp.abs(z)))
    log_a = -LRU_C * r * softplus
    a = jnp.exp(log_a)
    bb = jnp.sqrt(_one_minus_sq(a, log_a)) * (i * xc)

    a3 = a.reshape(ng, V7X_SUBLANES, w)
    b3 = bb.reshape(ng, V7X_SUBLANES, w)
    for d in (1, 2, 4):
        ok = sub >= d
        a_sh = pltpu.roll(a3, d, 1)
        b_sh = pltpu.roll(b3, d, 1)
        b3 = jnp.where(ok, a3 * b_sh + b3, b3)
        a3 = jnp.where(ok, a3 * a_sh, a3)
    carry = h_ref[0:1, :]
    groups = []
    for g in range(ng):
        hg = b3[g] + a3[g] * carry
        groups.append(hg)
        carry = hg[V7X_SUBLANES - 1:V7X_SUBLANES, :]
    hcur = jnp.concatenate(groups, axis=0)
    h_ref[...] = jnp.broadcast_to(carry, h_ref.shape)
    o_ref[0] = (hcur * _gelu_tanh(gate)).astype(o_ref.dtype)


def _merge_kernel(attn_ref, lru_ref, h_ref, x_ref, wm_ref, wa_ref, wl_ref, wo_ref, g2_ref,
                  x1_ref, h2_ref):
    d = x_ref.shape[1]
    h = h_ref[...]
    mg0 = _sigmoid(jnp.dot(h, wm_ref[:, 0:d], preferred_element_type=F32))
    mg1 = _sigmoid(jnp.dot(h, wm_ref[:, d:2 * d], preferred_element_type=F32))
    ya = jnp.dot(attn_ref[...], wa_ref[...], preferred_element_type=F32)
    yl = jnp.dot(lru_ref[...], wl_ref[...], preferred_element_type=F32)
    merged = mg0 * ya + mg1 * yl
    x1 = x_ref[...] + jnp.dot(merged.astype(MXU_DTYPE), wo_ref[...], preferred_element_type=F32)
    x1_ref[...] = x1
    ms = jnp.mean(x1 * x1, axis=-1, keepdims=True)
    h2_ref[...] = (x1 * lax.rsqrt(ms + EPS) * g2_ref[...]).astype(h2_ref.dtype)


def _merge(attn, lru, h2d, x2d, w_mg, wa, wl, wo, g2, tm=512):
    m, d = x2d.shape
    row = pl.BlockSpec((tm, d), lambda i: (i, 0))
    full = lambda a: pl.BlockSpec(a.shape, lambda i: (0, 0))
    g2 = g2.reshape(1, d)
    return pl.pallas_call(
        _merge_kernel,
        grid=(m // tm,),
        in_specs=[row, row, row, row, full(w_mg), full(wa), full(wl), full(wo), full(g2)],
        out_specs=[row, row],
        out_shape=[jax.ShapeDtypeStruct((m, d), F32), jax.ShapeDtypeStruct((m, d), MXU_DTYPE)],
        compiler_params=_params("parallel"),
        name="merge_out",
    )(attn, lru, h2d, x2d, w_mg, wa, wl, wo, g2)


FFN_CHUNK = V7X_MXU_DIM


def _ffn_kernel(h_ref, x1_ref, wg_ref, wu_ref, wd_ref, o_ref, act_ref):
    h = h_ref[...]
    f = wg_ref.shape[1]
    for c0 in range(0, f, FFN_CHUNK):
        cols = slice(c0, min(c0 + FFN_CHUNK, f))
        g = jnp.dot(h, wg_ref[:, cols], preferred_element_type=F32)
        u = jnp.dot(h, wu_ref[:, cols], preferred_element_type=F32)
        act_ref[:, cols] = (g * _sigmoid(g) * u).astype(act_ref.dtype)
    o_ref[...] = x1_ref[...] + jnp.dot(act_ref[...], wd_ref[...], preferred_element_type=F32)


def _ffn(h2, x1, wg, wu, wd, tm=512):
    m, d = x1.shape
    f = wg.shape[1]
    resident = lambda a: pl.BlockSpec(a.shape, lambda i: (0, 0), pipeline_mode=pl.Buffered(1))
    row = pl.BlockSpec((tm, d), lambda i: (i, 0))
    return pl.pallas_call(
        _ffn_kernel,
        grid=(m // tm,),
        in_specs=[row, row, resident(wg), resident(wu), resident(wd)],
        out_specs=row,
        out_shape=jax.ShapeDtypeStruct((m, d), F32),
        scratch_shapes=[pltpu.VMEM((tm, f), MXU_DTYPE)],
        compiler_params=_params("parallel"),
        name="swiglu_ffn",
    )(h2, x1, wg, wu, wd)


def _pad_last(a, n):
    return jnp.pad(a, [(0, 0)] * (a.ndim - 1) + [(0, n - a.shape[-1])])


def _layer(x, norm1_g, w_in, q_norm_g, k_norm_g, cmp_pos_k, cmp_w1_k, cmp_b1_k, cmp_w2_k, cmp_b2_k,
           cmp_pos_v, cmp_w1_v, cmp_b1_v, cmp_w2_v, cmp_b2_v, conv_w, conv_b, lru_wa, lru_ba,
           lru_wi, lru_bi, lru_lambda, w_o_attn, w_o_lru, w_out, norm2_g, w_gate, w_up, w_down):
    b, t, d = x.shape
    m = b * t
    attn_dim = N_HEADS * HEAD_DIM
    kv_dim = N_KV * HEAD_DIM
    lru_w = lru_lambda.shape[0]
    assert t % 512 == 0
    o1 = attn_dim
    o2 = o1 + 6 * kv_dim
    o3 = o2 + 3 * N_HEADS
    o4 = o3 + lru_w
    o5 = o4 + lru_w
    cast = lambda a: a.astype(MXU_DTYPE)

    wq_t = cast(w_in[:, :o1].T)
    w_kv = w_in[:, o1:o2].reshape(d, 6, N_KV, HEAD_DIM)
    w_cmp_src = cast(w_kv[:, 0:2].reshape(d, 2 * kv_dim))
    w_k2 = cast(jnp.stack([w_kv[:, 2], w_kv[:, 4]], axis=1).reshape(d, 2 * kv_dim))
    k_gain2 = jnp.stack([jnp.tile(k_norm_g[1], N_KV), jnp.tile(k_norm_g[2], N_KV)])
    lane_grp = np.arange(kv_dim) // HEAD_DIM
    grp_avg = cast(jnp.asarray((lane_grp[:, None] == lane_grp[None, :]) / HEAD_DIM, F32))
    place_np = np.zeros((kv_dim, N_KV * FEAT), np.float32)
    place_np[np.arange(kv_dim), lane_grp * FEAT + np.arange(kv_dim) % HEAD_DIM] = 1.0
    place = cast(jnp.asarray(place_np))
    w_v = jnp.stack([w_kv[:, 3], w_kv[:, 5]], axis=1)
    w_v_t = cast(_pad_last(w_v, V_ROWS).reshape(d, 2 * N_KV * V_ROWS).T)
    v_ones_col = jnp.tile(jnp.arange(V_ROWS) == HEAD_DIM, 2 * N_KV).astype(F32).reshape(-1, 1)
    w_g = w_in[:, o2:o3].reshape(d, N_KV, HPG, 3).transpose(0, 1, 3, 2).reshape(d, N_KV, 3 * HPG)
    w_g_t = cast(_pad_last(w_g, 16).reshape(d, N_KV * 16).T)
    w_lru = cast(w_in[:, o3:o5])
    w_mg = cast(w_in[:, o5:])
    q_gain_col = jnp.tile(q_norm_g * (HEAD_DIM ** -0.5 * LOG2E), N_HEADS).reshape(attn_dim, 1)
    gain_pad = lambda g: _pad_last(g.reshape(1, HEAD_DIM), FEAT)

    per_slab = V7X_MXU_DIM // (lru_w // LRU_BLOCKS)
    eye = jnp.eye(per_slab, dtype=F32)
    slabs = lambda wgt: cast(jnp.einsum('snkj,nm->snkmj', wgt.reshape(LRU_BLOCKS // per_slab, per_slab,
                                                                       *wgt.shape[1:]), eye)
                             .reshape(LRU_BLOCKS // per_slab, V7X_MXU_DIM, V7X_MXU_DIM))
    vec = lambda v: v.reshape(1, lru_w)

    h3d, q_t, v_t, gates_t, k_sel, k_win, cmp_src, lru = _in_proj(
        x,
        (norm1_g.reshape(1, d), wq_t, q_gain_col, w_v_t, v_ones_col, w_g_t, w_k2, k_gain2, grp_avg, place,
         w_cmp_src),
        (w_lru, conv_w.reshape(CONV_W, lru_w), vec(conv_b), slabs(lru_wa), vec(lru_ba), slabs(lru_wi),
         vec(lru_bi), vec(lru_lambda)))
    h2d = h3d.reshape(m, d)

    k_cmp = _compress(cmp_src, 0, cmp_pos_k, cmp_w1_k, cmp_b1_k, cast(_pad_last(cmp_w2_k, FEAT)),
                      _pad_last(cmp_b2_k.reshape(1, HEAD_DIM), FEAT), gain_pad(k_norm_g[0]))
    v_cmp_t = _compress(cmp_src, N_KV // CMP_GROUPS, cmp_pos_v, cmp_w1_v, cmp_b1_v,
                        cast(_pad_last(cmp_w2_v, V_ROWS).T),
                        jnp.concatenate([cmp_b2_v, v_ones_col[HEAD_DIM:V_ROWS, 0]]).reshape(V_ROWS, 1))

    attn = _attention_static(q_t, gates_t, k_cmp, v_cmp_t, k_sel, k_win, v_t)

    x1, h2 = _merge(attn.reshape(m, attn_dim), lru.reshape(m, lru_w), h2d, x.reshape(m, d), w_mg,
                    cast(w_o_attn), cast(w_o_lru), cast(w_out), norm2_g)
    out = _ffn(h2, x1, cast(w_gate), cast(w_up), cast(w_down))
    return out.reshape(b, t, d)


def kernel(x, norm1_g, w_in, q_norm_g, k_norm_g, cmp_pos_k, cmp_w1_k, cmp_b1_k, cmp_w2_k, cmp_b2_k,
           cmp_pos_v, cmp_w1_v, cmp_b1_v, cmp_w2_v, cmp_b2_v, conv_w, conv_b, lru_wa, lru_ba,
           lru_wi, lru_bi, lru_lambda, w_o_attn, w_o_lru, w_out, norm2_g, w_gate, w_up, w_down):
    for l in range(norm1_g.shape[0]):
        x = _layer(x, norm1_g[l], w_in[l], q_norm_g[l], k_norm_g[l], cmp_pos_k[l], cmp_w1_k[l],
                   cmp_b1_k[l], cmp_w2_k[l], cmp_b2_k[l], cmp_pos_v[l], cmp_w1_v[l], cmp_b1_v[l],
                   cmp_w2_v[l], cmp_b2_v[l], conv_w[l], conv_b[l], lru_wa[l], lru_ba[l], lru_wi[l],
                   lru_bi[l], lru_lambda[l], w_o_attn[l], w_o_lru[l], w_out[l], norm2_g[l],
                   w_gate[l], w_up[l], w_down[l])
    return x
```

```python
import functools

import numpy as np
import jax
import jax.numpy as jnp
from jax import lax
from jax.experimental import pallas as pl
from jax.experimental.pallas import tpu as pltpu

N_HEADS = 16
HEAD_DIM = 64
N_KV = 4
HPG = N_HEADS // N_KV
CMP_BLOCK = 32
CMP_STRIDE = 16
CMP_HIDDEN = 256
SEL_BLOCK = 64
N_SEL = 16
WINDOW = 512
FORCE_BONUS = 1e4
LRU_BLOCKS = 16
CONV_W = 4
LRU_C = 8.0
EPS = 1e-6

MXU_DTYPE = jnp.bfloat16
F32 = jnp.float32

V7X_LANES = 128
V7X_SUBLANES = 8
V7X_MXU_DIM = 256
V7X_VMEM_LIMIT_BYTES = 48 * 1024 * 1024

TQ = 256
TK_SEL = 256
TILES_PER_CALL = 2
V_ROWS = 80
LOG2E = 1.4426950408889634
FEAT = V7X_MXU_DIM
F_SEL = HEAD_DIM
F_POS = 2 * HEAD_DIM
F_CMP = F_POS + 6
NEG_MASK = -1e30
NEG_BLOCK = -(2.0 ** 100)


def _params(*sem):
    return pltpu.CompilerParams(dimension_semantics=sem, vmem_limit_bytes=V7X_VMEM_LIMIT_BYTES)


def _gelu_tanh(x):
    return 0.5 * x * (1.0 + jnp.tanh(0.7978845608028654 * (x + 0.044715 * (x * x * x))))


def _sigmoid(x):
    return 1.0 / (1.0 + jnp.exp(-x))


def _one_minus_sq(a, log_a):
    series = (-2.0 * log_a) * (1.0 + log_a * (1.0 + log_a * (2.0 / 3)))
    return jnp.where(log_a > -1.0 / 128, series, 1.0 - a * a)


def _nt_dot(wt, h):
    return lax.dot_general(wt, h, (((1,), (1,)), ((), ())), preferred_element_type=F32)


def _token_features(pos, col, with_block_mask):
    blk = pos >> 6
    off = pos & (SEL_BLOCK - 1)
    feat = jnp.where((col >= F_POS) & (col < F_POS + 3), blk.astype(F32),
                     jnp.where((col >= F_POS + 3) & (col < F_POS + 6), off.astype(F32), 0.0))
    if with_block_mask:
        feat = jnp.where((col >= F_SEL) & (col - F_SEL == blk) & (col < F_POS), NEG_BLOCK, feat)
    return feat


def _in_proj_kernel(x_ref, g1_ref, wq_ref, qg_ref, wv_ref, vb_ref, wg_ref, wk_ref, kg_ref, grp_ref,
                    place_ref, wc_ref, wl_ref, cw_ref, cb_ref, wa_ref, ba_ref, wi_ref, bi_ref, lam_ref,
                    h_ref, qt_ref, vt_ref, gt_ref, ks_ref, kw_ref, cs_ref, lru_ref, tail_ref, state_ref):
    x = x_ref[0]
    tm = x.shape[0]
    ms = jnp.mean(x * x, axis=-1, keepdims=True)
    h = (x * lax.rsqrt(ms + EPS) * g1_ref[...]).astype(MXU_DTYPE)
    h_ref[0] = h

    lru_mid = _lru_gates(h, wl_ref, cw_ref, cb_ref, wa_ref, wi_ref, tail_ref, state_ref)

    r3 = _nt_dot(wq_ref[...], h).reshape(N_HEADS, HEAD_DIM, tm)
    qn = r3 * lax.rsqrt(jnp.mean(r3 * r3, axis=1, keepdims=True) + EPS)
    qt_ref[0] = (qn.reshape(N_HEADS * HEAD_DIM, tm) * qg_ref[...]).astype(qt_ref.dtype)

    vt_ref[0] = (_nt_dot(wv_ref[...], h) + vb_ref[...]).astype(vt_ref.dtype)
    gt_ref[0] = _sigmoid(_nt_dot(wg_ref[...], h))

    k = jnp.dot(h, wk_ref[...], preferred_element_type=F32)
    kk = k * k
    kk_hi = kk.astype(MXU_DTYPE)
    kk_lo = (kk - kk_hi.astype(F32)).astype(MXU_DTYPE)
    pos = pl.program_id(1) * tm + lax.broadcasted_iota(jnp.int32, (tm, FEAT), 0)
    col = lax.broadcasted_iota(jnp.int32, (tm, FEAT), 1)
    kv_dim = N_KV * HEAD_DIM
    for branch, (o_ref, with_block_mask) in enumerate(((ks_ref, True), (kw_ref, False))):
        sl = slice(branch * kv_dim, (branch + 1) * kv_dim)
        msq = (jnp.dot(kk_hi[:, sl], grp_ref[...], preferred_element_type=F32)
               + jnp.dot(kk_lo[:, sl], grp_ref[...], preferred_element_type=F32))
        kn = (k[:, sl] * lax.rsqrt(msq + EPS) * kg_ref[branch:branch + 1, :]).astype(MXU_DTYPE)
        placed = jnp.dot(kn, place_ref[...], preferred_element_type=F32)
        feat = _token_features(pos, col, with_block_mask)
        for g in range(N_KV):
            o_ref[0, g] = (placed[:, g * FEAT:(g + 1) * FEAT] + feat).astype(o_ref.dtype)

    cs_ref[0] = jnp.dot(h, wc_ref[...], preferred_element_type=F32)

    _lru_scan(*lru_mid, ba_ref, bi_ref, lam_ref, lru_ref, state_ref)


def _in_proj(x, attn_weights, lru_weights, tm=512):
    b, t, d = x.shape
    kv_dim = N_KV * HEAD_DIM
    wq_t, w_v_t, w_g_t = attn_weights[1], attn_weights[3], attn_weights[5]
    lru_w = lru_weights[-1].shape[1]
    full = lambda a: pl.BlockSpec(a.shape, lambda bi, i: (0,) * a.ndim)
    rowblk = lambda n: pl.BlockSpec((1, tm, n), lambda bi, i: (bi, i, 0))
    colblk = lambda n: pl.BlockSpec((1, n, tm), lambda bi, i: (bi, 0, i))
    kblk = pl.BlockSpec((1, N_KV, tm, FEAT), lambda bi, i: (bi, 0, i, 0))
    weights = tuple(attn_weights) + tuple(lru_weights)
    return pl.pallas_call(
        _in_proj_kernel,
        grid=(b, t // tm),
        in_specs=[rowblk(d)] + [full(a) for a in weights],
        out_specs=[rowblk(d), colblk(wq_t.shape[0]), colblk(w_v_t.shape[0]), colblk(w_g_t.shape[0]),
                   kblk, kblk, rowblk(2 * kv_dim), rowblk(lru_w)],
        out_shape=[jax.ShapeDtypeStruct((b, t, d), MXU_DTYPE),
                   jax.ShapeDtypeStruct((b, wq_t.shape[0], t), MXU_DTYPE),
                   jax.ShapeDtypeStruct((b, w_v_t.shape[0], t), MXU_DTYPE),
                   jax.ShapeDtypeStruct((b, w_g_t.shape[0], t), F32),
                   jax.ShapeDtypeStruct((b, N_KV, t, FEAT), MXU_DTYPE),
                   jax.ShapeDtypeStruct((b, N_KV, t, FEAT), MXU_DTYPE),
                   jax.ShapeDtypeStruct((b, t, 2 * kv_dim), F32),
                   jax.ShapeDtypeStruct((b, t, lru_w), MXU_DTYPE)],
        scratch_shapes=[pltpu.VMEM((V7X_SUBLANES, lru_w), F32),
                        pltpu.VMEM((V7X_SUBLANES, lru_w), F32)],
        compiler_params=_params("parallel", "arbitrary"),
        name="in_proj_lru",
    )(x, *weights)


CMP_GROUPS = V7X_LANES // HEAD_DIM


def _cmp_hidden(src_ref, pos_ref, w1_ref, b1_ref):
    ncp = src_ref.shape[1] // CMP_STRIDE
    lo, hi = [], []
    for l in range(CMP_STRIDE):
        x = src_ref[0, pl.ds(l, ncp, stride=CMP_STRIDE), :]
        lo.append((x + pos_ref[l:l + 1, :]).astype(MXU_DTYPE))
        hi.append((x + pos_ref[CMP_STRIDE + l:CMP_STRIDE + l + 1, :]).astype(MXU_DTYPE))
    kdim = CMP_STRIDE * V7X_LANES
    first = jnp.dot(jnp.concatenate(lo, axis=1), w1_ref[0:CMP_STRIDE].reshape(kdim, -1),
                    preferred_element_type=F32)
    second = jnp.dot(jnp.concatenate(hi, axis=1), w1_ref[CMP_STRIDE:CMP_BLOCK].reshape(kdim, -1),
                     preferred_element_type=F32)
    hid = first + pltpu.roll(second, ncp - 1, 0) + b1_ref[...]
    return _gelu_tanh(hid).astype(MXU_DTYPE)


def _cmp_k_kernel(src_ref, pos_ref, w1_ref, b1_ref, w2_ref, b2_ref, g_ref, o_ref):
    hid = _cmp_hidden(src_ref, pos_ref, w1_ref, b1_ref)
    ncp = hid.shape[0]
    idx = lax.broadcasted_iota(jnp.int32, (ncp, FEAT), 0)
    col = lax.broadcasted_iota(jnp.int32, (ncp, FEAT), 1)
    feat = jnp.where((col >= F_CMP) & (col < F_CMP + 3), (idx >> 6).astype(F32),
                     jnp.where((col >= F_CMP + 3) & (col < F_CMP + 6), (idx & 63).astype(F32), 0.0))
    for gl in range(CMP_GROUPS):
        r = jnp.dot(hid[:, gl * CMP_HIDDEN:(gl + 1) * CMP_HIDDEN], w2_ref[...],
                    preferred_element_type=F32) + b2_ref[...]
        ms = jnp.sum(r * r, axis=-1, keepdims=True) * (1.0 / HEAD_DIM)
        o_ref[0, gl] = (r * lax.rsqrt(ms + EPS) * g_ref[...] + feat).astype(o_ref.dtype)


def _cmp_v_kernel(src_ref, pos_ref, w1_ref, b1_ref, w2t_ref, b2_ref, o_ref):
    hid = _cmp_hidden(src_ref, pos_ref, w1_ref, b1_ref)
    for gl in range(CMP_GROUPS):
        r = _nt_dot(w2t_ref[...], hid[:, gl * CMP_HIDDEN:(gl + 1) * CMP_HIDDEN]) + b2_ref[...]
        o_ref[0, gl] = r.astype(o_ref.dtype)


def _compress(cmp_src, lane_block0, pos, w1, b1, w2, b2, gain_pad=None):
    b, t, _ = cmp_src.shape
    ncp = t // CMP_STRIDE
    hid_w = CMP_GROUPS * CMP_HIDDEN
    w1c = w1.astype(MXU_DTYPE)
    zero = jnp.zeros_like(w1c)
    w1_bd = jnp.concatenate([jnp.concatenate([w1c if h == g else zero for h in range(CMP_GROUPS)], axis=2)
                             for g in range(CMP_GROUPS)], axis=1)
    pos_t = jnp.tile(pos, (1, CMP_GROUPS))
    b1_t = jnp.tile(b1.reshape(1, CMP_HIDDEN), (1, CMP_GROUPS))
    full = lambda a: pl.BlockSpec(a.shape, lambda bi, p: (0,) * a.ndim)
    src_spec = pl.BlockSpec((1, t, V7X_LANES), lambda bi, p: (bi, 0, lane_block0 + p))
    grid = (b, N_KV // CMP_GROUPS)
    if gain_pad is not None:
        args = (pos_t, w1_bd, b1_t, w2, b2, gain_pad)
        return pl.pallas_call(
            _cmp_k_kernel,
            grid=grid,
            in_specs=[src_spec] + [full(a) for a in args],
            out_specs=pl.BlockSpec((1, CMP_GROUPS, ncp, FEAT), lambda bi, p: (bi, p, 0, 0)),
            out_shape=jax.ShapeDtypeStruct((b, N_KV, ncp, FEAT), MXU_DTYPE),
            compiler_params=_params("parallel", "parallel"),
            name="compress_k",
        )(cmp_src, *args)
    args = (pos_t, w1_bd, b1_t, w2, b2)
    return pl.pallas_call(
        _cmp_v_kernel,
        grid=grid,
        in_specs=[src_spec] + [full(a) for a in args],
        out_specs=pl.BlockSpec((1, CMP_GROUPS, V_ROWS, ncp), lambda bi, p: (bi, p, 0, 0)),
        out_shape=jax.ShapeDtypeStruct((b, N_KV, V_ROWS, ncp), MXU_DTYPE),
        compiler_params=_params("parallel", "parallel"),
        name="compress_v",
    )(cmp_src, *args)


def _split3(v):
    parts = []
    rest = np.asarray(v, np.float64)
    for _ in range(3):
        p = rest.astype(np.float32).astype(jnp.bfloat16).astype(np.float64)
        parts.append(p)
        rest = rest - p
    return parts


def _alibi_query_features():
    tab = np.zeros((N_KV, FEAT - F_POS, HPG * TQ), np.float64)
    for g in range(N_KV):
        for h in range(HPG):
            slope = 2.0 ** (-8.0 * (g * HPG + h + 1) / N_HEADS)
            parts = _split3(slope * LOG2E)
            lanes = slice(h * TQ, (h + 1) * TQ)
            for i, p in enumerate(parts):
                tab[g, i, lanes] = SEL_BLOCK * p
                tab[g, 3 + i, lanes] = p
                tab[g, 6 + i, lanes] = CMP_STRIDE * 64 * p
                tab[g, 9 + i, lanes] = CMP_STRIDE * p
    return jnp.asarray(tab, F32).astype(MXU_DTYPE)


def _block_map_t(n_cmp_pad, n_blk):
    cs = np.arange(n_cmp_pad) * CMP_STRIDE
    ce = cs + CMP_BLOCK - 1
    bs = np.arange(n_blk) * SEL_BLOCK
    be = bs + SEL_BLOCK - 1
    return jnp.asarray(((cs[None, :] <= be[:, None]) & (ce[None, :] >= bs[:, None])).astype(np.float32))


def _prob(s, m):
    return jnp.exp2(s - m).astype(MXU_DTYPE)


def _attn_tiles_kernel(*refs, tiles, win_blk0, n_blk, n_sel):
    (qt_ref, gate_ref, kc_ref, vc_ref, ks_ref, vs_ref, kd_ref, vd_ref, kwa_ref, kwb_ref, vwa_ref, vwb_ref,
     alibi_ref, map_ref) = refs[:14]
    o_ref, qb_ref, qs_ref, imp_ref = refs[15:]
    lanes = HPG * TQ
    ncp = kc_ref.shape[2]
    blocks_per_tile = TQ // SEL_BLOCK
    win_tiles = WINDOW // TQ
    lane_tok = lax.broadcasted_iota(jnp.int32, (1, lanes), 1) & (TQ - 1)
    row_pos = lax.broadcasted_iota(jnp.int32, (TQ, lanes), 0)
    blk = lax.broadcasted_iota(jnp.int32, (n_blk, TQ), 0)

    def window_tile(qi, j):
        n = len(tiles)
        lt = max(qi - win_tiles, 0) + j - n * win_blk0
        kref, vref = (kwa_ref, vwa_ref) if lt < n else (kwb_ref, vwb_ref)
        sl = slice((lt % n) * TQ, (lt % n + 1) * TQ)
        return kref[0, 0, sl, :], vref[0, :, sl]

    def head(ti, qi):
        cols = slice(ti * TQ, (ti + 1) * TQ)
        t_lane = qi * TQ + lane_tok
        for h in range(HPG):
            qb_ref[ti, 0:HEAD_DIM, h * TQ:(h + 1) * TQ] = qt_ref[0, h * HEAD_DIM:(h + 1) * HEAD_DIM, cols]
        qb_ref[ti, F_SEL:F_POS, :] = jnp.zeros((F_POS - F_SEL, lanes), qb_ref.dtype)
        qb_ref[ti, F_POS:FEAT, :] = alibi_ref[0]
        qb = qb_ref[ti]
        sc = jnp.dot(kc_ref[0, 0], qb, preferred_element_type=F32)
        sw = [jnp.dot(window_tile(qi, j)[0], qb, preferred_element_type=F32) for j in range(3)]
        sd = jnp.dot(kd_ref[0, 0, cols, :], qb, preferred_element_type=F32)
        last_cmp = (t_lane - (CMP_BLOCK - 1)) >> 4
        sc = jnp.where(lax.broadcasted_iota(jnp.int32, (ncp, lanes), 0) <= last_cmp, sc, NEG_MASK)
        ec = jnp.exp2(sc - jnp.max(sc, axis=0, keepdims=True))
        acc_c = jnp.dot(vc_ref[0, 0], ec.astype(MXU_DTYPE), preferred_element_type=F32)
        inv_c = jnp.where(last_cmp >= 0, 1.0 / jnp.maximum(acc_c[HEAD_DIM:HEAD_DIM + 1], 1e-30), 0.0)
        imp = None
        if (qi + 1) * blocks_per_tile > n_sel and qi > 0:
            psum = ec[:, 0:TQ] * inv_c[:, 0:TQ]
            for h in range(1, HPG):
                psum = psum + ec[:, h * TQ:(h + 1) * TQ] * inv_c[:, h * TQ:(h + 1) * TQ]
            imp = jnp.dot(map_ref[...], psum, preferred_element_type=F32)
        return dict(qb=qb, t_lane=t_lane, sw=sw, sd=sd, o_cmp=acc_c[0:HEAD_DIM] * inv_c, imp=imp,
                    causal=qi * TQ + row_pos <= t_lane)

    def rank_init(ti, qi, st):
        first_own_blk = qi * blocks_per_tile
        st.update(ranks=[], k_done=0, k_total=0)
        if st["imp"] is None:
            return
        cur = (qi * TQ + lax.broadcasted_iota(jnp.int32, (n_blk, TQ), 1)) >> 6
        forced = (blk == 0) | (blk == cur) | (blk == cur - 1)
        imp = jnp.where(blk <= cur, st["imp"] + jnp.where(forced, FORCE_BONUS, 0.0), NEG_MASK)
        imp_ref[ti] = imp
        n_rank_chunks = -(-first_own_blk // V7X_SUBLANES)
        st["chunks"] = [imp[c * V7X_SUBLANES:(c + 1) * V7X_SUBLANES] for c in range(n_rank_chunks)]
        st["ranks"] = [jnp.zeros((V7X_SUBLANES, TQ), jnp.int32) for _ in range(n_rank_chunks)]
        st["k_total"] = min(first_own_blk + blocks_per_tile, n_blk)

    def rank_rounds(ti, st, n):
        sub = lax.broadcasted_iota(jnp.int32, (V7X_SUBLANES, TQ), 0)
        stop = min(st["k_done"] + n, st["k_total"])
        for k in range(st["k_done"], stop):
            row = imp_ref[ti, k:k + 1, :]
            for c, mine in enumerate(st["chunks"]):
                lo = c * V7X_SUBLANES
                if lo > k:
                    one = jnp.where(row >= mine, 1, 0)
                elif lo + V7X_SUBLANES - 1 <= k:
                    one = jnp.where(row > mine, 1, 0)
                else:
                    one = jnp.where(sub + lo > k, jnp.where(row >= mine, 1, 0), jnp.where(row > mine, 1, 0))
                st["ranks"][c] = st["ranks"][c] + one
        st["k_done"] = stop

    def chunk_setup(ti, qi, st):
        rank_rounds(ti, st, st["k_total"])
        first_own_blk = qi * blocks_per_tile
        n_chunks = -(-qi * TQ // TK_SEL)
        st.update(n_chunks=n_chunks, m=jnp.full((1, lanes), NEG_MASK, F32), acc=jnp.zeros((V_ROWS, lanes), F32),
                  p_prev=None)
        if n_chunks == 0:
            return
        rank = jnp.zeros((n_blk, TQ), jnp.int32)
        if st["ranks"]:
            pad = [jnp.zeros((n_blk - len(st["ranks"]) * V7X_SUBLANES, TQ), jnp.int32)]
            rank = jnp.concatenate(st["ranks"] + (pad if pad[0].shape[0] else []), axis=0)
        not_sel = jnp.where((rank < n_sel) & (blk < first_own_blk), 0.0, 1.0).astype(qs_ref.dtype)
        qs_ref[ti] = st["qb"]
        for h in range(HPG):
            qs_ref[ti, F_SEL:F_SEL + n_blk, h * TQ:(h + 1) * TQ] = not_sel
        st["qs"] = qs_ref[ti]
        st["s_next"] = chunk_qk(st, 0)

    def chunk_qk(st, k):
        return jnp.dot(ks_ref[0, 0, k * TK_SEL:(k + 1) * TK_SEL, :], st["qs"], preferred_element_type=F32)

    def chunk_pv(st, k):
        return jnp.dot(vs_ref[0, :, k * TK_SEL:(k + 1) * TK_SEL], st["p_prev"], preferred_element_type=F32)

    def chunk_stage(st, k):
        s_cur = st["s_next"]
        if k + 1 < st["n_chunks"]:
            st["s_next"] = chunk_qk(st, k + 1)
        if st["p_prev"] is not None:
            st["acc"] = st["acc"] + chunk_pv(st, k - 1)
        m_new = jnp.maximum(st["m"], jnp.max(s_cur, axis=0, keepdims=True))
        st["p_prev"] = _prob(s_cur, m_new)
        st["acc"] = jnp.exp2(st["m"] - m_new) * st["acc"]
        st["m"] = m_new

    def chunk_finish(st):
        if st["n_chunks"] > 0:
            st["acc"] = st["acc"] + chunk_pv(st, st["n_chunks"] - 1)

    def diagonal(ti, qi, st):
        cols = slice(ti * TQ, (ti + 1) * TQ)
        sd = jnp.where(st["causal"], st["sd"], NEG_MASK)
        m_d = jnp.max(sd, axis=0, keepdims=True)
        acc_d = jnp.dot(vd_ref[0, :, cols], _prob(sd, m_d), preferred_element_type=F32)
        m_all = jnp.maximum(st["m"], m_d)
        acc_s = jnp.exp2(st["m"] - m_all) * st["acc"] + jnp.exp2(m_d - m_all) * acc_d
        st["o_sel"] = acc_s[0:HEAD_DIM] * (1.0 / acc_s[HEAD_DIM:HEAD_DIM + 1])

    def window_order(qi):
        wb = max(qi - win_tiles, 0)
        return [qi - wb] + [j for j in range(qi - wb - 1, -1, -1)]

    def window_stage(qi, st, j, last):
        wb = max(qi - win_tiles, 0)
        s = st["sw"][j]
        if wb + j == qi:
            s = jnp.where(st["causal"], s, NEG_MASK)
        elif wb + j != qi - 1:
            dj = (st["t_lane"] - (wb + j) * TQ) - row_pos
            s = jnp.where(lax.bitcast_convert_type(dj, jnp.uint32) < WINDOW, s, NEG_MASK)
        m_j = jnp.max(s, axis=0, keepdims=True)
        v_j = window_tile(qi, j)[1]
        if "m_w" not in st:
            m_new = m_j
            acc_w = jnp.dot(v_j, _prob(s, m_new), preferred_element_type=F32)
        else:
            m_new = jnp.maximum(st["m_w"], m_j)
            acc_w = jnp.exp2(st["m_w"] - m_new) * st["acc_w"] + jnp.dot(v_j, _prob(s, m_new),
                                                                          preferred_element_type=F32)
        st["m_w"], st["acc_w"] = m_new, acc_w
        if last:
            st["o_win"] = acc_w[0:HEAD_DIM] * (1.0 / acc_w[HEAD_DIM:HEAD_DIM + 1])

    def output(ti, st):
        cols = slice(ti * TQ, (ti + 1) * TQ)
        gates = gate_ref[0, :, cols]
        def gate_row(j):
            return jnp.concatenate([gates[j * HPG + h:j * HPG + h + 1, :] for h in range(HPG)], axis=1)
        o_t = gate_row(0) * st["o_cmp"] + gate_row(1) * st["o_sel"] + gate_row(2) * st["o_win"]
        for hp in range(HPG // 2):
            pair = jnp.concatenate([o_t[:, (2 * hp) * TQ:(2 * hp + 1) * TQ],
                                    o_t[:, (2 * hp + 1) * TQ:(2 * hp + 2) * TQ]], axis=0)
            o_ref[0, cols, hp * 2 * HEAD_DIM:(hp + 1) * 2 * HEAD_DIM] = pair.T.astype(o_ref.dtype)

    def tail_phases(ti, qi, st):
        return [lambda: diagonal(ti, qi, st), lambda: output(ti, st)]

    cur = head(0, tiles[0])
    rank_init(0, tiles[0], cur)
    chunk_setup(0, tiles[0], cur)
    pending = []
    for ti, qi in enumerate(tiles):
        nxt = None
        if ti + 1 < len(tiles):
            nxt = head(ti + 1, tiles[ti + 1])
            rank_init(ti + 1, tiles[ti + 1], nxt)
            rounds_per_stage = -(-nxt["k_total"] // max(cur["n_chunks"], 1))
        order = window_order(qi)
        for j in order:
            pending.append(lambda qi=qi, st=cur, j=j, last=(j == order[-1]): window_stage(qi, st, j, last))
        for k in range(cur["n_chunks"]):
            chunk_stage(cur, k)
            if nxt is not None:
                rank_rounds(ti + 1, nxt, rounds_per_stage)
            if pending:
                pending.pop(0)()
        for phase in pending:
            phase()
        chunk_finish(cur)
        pending = tail_phases(ti, qi, cur)
        if nxt is not None:
            chunk_setup(ti + 1, tiles[ti + 1], nxt)
            cur = nxt
    for phase in pending:
        phase()


def _attention_static(q_t, gates_t, k_cmp, v_cmp_t, k_sel, k_win, v_t):
    b, _, t = q_t.shape
    n_blk = t // SEL_BLOCK
    tiles_per_call = TILES_PER_CALL
    tb = tiles_per_call * TQ
    assert tb >= WINDOW and t % max(tb, TK_SEL) == 0 and t >= 2 * tb and F_SEL + n_blk <= F_POS, \
        "unsupported sequence length"
    n_sel = min(N_SEL, n_blk)
    lanes = HPG * TQ
    rows = HPG * HEAD_DIM
    alibi = _alibi_query_features()
    out_shape = jax.ShapeDtypeStruct((b, t, N_HEADS * HEAD_DIM), MXU_DTYPE)

    attn = jnp.zeros(out_shape.shape, out_shape.dtype)
    for m in range(t // tb):
        tiles = tuple(range(m * tiles_per_call, (m + 1) * tiles_per_call))
        ncp = min(t // CMP_STRIDE, -(-((m + 1) * tb // CMP_STRIDE) // V7X_LANES) * V7X_LANES)
        blk_map_t = _block_map_t(ncp, n_blk)
        kc = -(-tiles[-1] * TQ // TK_SEL) * TK_SEL
        wb0 = max(m - 1, 0)
        in_specs = [
            pl.BlockSpec((1, rows, tb), lambda bi, g, m=m: (bi, g, m)),
            pl.BlockSpec((1, 16, tb), lambda bi, g, m=m: (bi, g, m)),
            pl.BlockSpec((1, 1, ncp, FEAT), lambda bi, g: (bi, g, 0, 0)),
            pl.BlockSpec((1, 1, V_ROWS, ncp), lambda bi, g: (bi, g, 0, 0)),
            pl.BlockSpec((1, 1, kc, FEAT), lambda bi, g: (bi, g, 0, 0)),
            pl.BlockSpec((1, V_ROWS, kc), lambda bi, g: (bi, g, 0)),
            pl.BlockSpec((1, 1, tb, FEAT), lambda bi, g, m=m: (bi, g, m, 0)),
            pl.BlockSpec((1, V_ROWS, tb), lambda bi, g, m=m: (bi, g, m)),
            pl.BlockSpec((1, 1, tb, FEAT), lambda bi, g, w=wb0: (bi, g, w, 0)),
            pl.BlockSpec((1, 1, tb, FEAT), lambda bi, g, w=wb0: (bi, g, w + 1, 0)),
            pl.BlockSpec((1, V_ROWS, tb), lambda bi, g, w=wb0: (bi, N_KV + g, w)),
            pl.BlockSpec((1, V_ROWS, tb), lambda bi, g, w=wb0: (bi, N_KV + g, w + 1)),
            pl.BlockSpec((1, FEAT - F_POS, lanes), lambda bi, g: (g, 0, 0)),
            pl.BlockSpec((n_blk, ncp), lambda bi, g: (0, 0)),
        ]
        in_specs.append(pl.BlockSpec(memory_space=pl.ANY))
        args = [q_t, gates_t, k_cmp, v_cmp_t, k_sel, v_t, k_sel, v_t, k_win, k_win, v_t, v_t, alibi, blk_map_t,
                attn]
        kernel = functools.partial(_attn_tiles_kernel, tiles=tiles, win_blk0=wb0, n_blk=n_blk, n_sel=n_sel)
        attn = pl.pallas_call(
            kernel,
            grid=(b, N_KV),
            in_specs=in_specs,
            out_specs=pl.BlockSpec((1, tb, rows), lambda bi, g, m=m: (bi, m, g)),
            out_shape=out_shape,
            scratch_shapes=[pltpu.VMEM((tiles_per_call, FEAT, lanes), MXU_DTYPE),
                            pltpu.VMEM((tiles_per_call, FEAT, lanes), MXU_DTYPE),
                            pltpu.VMEM((tiles_per_call, n_blk, TQ), F32)],
            input_output_aliases={len(args) - 1: 0},
            compiler_params=_params("parallel", "parallel"),
            name=f"nsa_attention_{m}",
        )(*args)
    return attn


def _lru_gates(hin, wl_ref, cw_ref, cb_ref, wa_ref, wi_ref, tail_ref, h_ref):
    tt = hin.shape[0]
    w = cb_ref.shape[1]

    @pl.when(pl.program_id(1) == 0)
    def _():
        tail_ref[...] = jnp.zeros_like(tail_ref)
        h_ref[...] = jnp.zeros_like(h_ref)

    def gate_matmul(xb, w_ref):
        n = w_ref.shape[1]
        return jnp.concatenate([jnp.dot(xb[:, j * n:(j + 1) * n], w_ref[j], preferred_element_type=F32)
                                for j in range(w_ref.shape[0])], axis=1)

    x = jnp.dot(hin, wl_ref[:, 0:w], preferred_element_type=F32)
    gate = jnp.dot(hin, wl_ref[:, w:2 * w], preferred_element_type=F32)
    ng = tt // V7X_SUBLANES
    sub = lax.broadcasted_iota(jnp.int32, (ng, V7X_SUBLANES, w), 1)
    x3 = x.reshape(ng, V7X_SUBLANES, w)
    xprev3 = jnp.concatenate([tail_ref[...][None], x3], axis=0)
    tail_ref[...] = x[tt - V7X_SUBLANES:tt]
    xc = x * cw_ref[CONV_W - 1:CONV_W, :] + cb_ref[...]
    for s in range(1, CONV_W):
        rot = pltpu.roll(xprev3, s, 1)
        xs = jnp.where(sub >= s, rot[1:], rot[:-1])
        xc = xc + xs.reshape(tt, w) * cw_ref[CONV_W - 1 - s:CONV_W - s, :]

    xb = xc.astype(MXU_DTYPE)
    return xc, gate_matmul(xb, wa_ref), gate_matmul(xb, wi_ref), gate


def _lru_scan(xc, r_pre, i_pre, gate, ba_ref, bi_ref, lam_ref, o_ref, h_ref):
    tt, w = xc.shape
    ng = tt // V7X_SUBLANES
    sub = lax.broadcasted_iota(jnp.int32, (ng, V7X_SUBLANES, w), 1)
    r = _sigmoid(r_pre + ba_ref[...])
    i = _sigmoid(i_pre + bi_ref[...])
    z = -lam_ref[...]
    softplus = jnp.maximum(z, 0.0) + jnp.log1p(jnp.exp(-jnp.abs(z)))
    log_a = -LRU_C * r * softplus
    a = jnp.exp(log_a)
    bb = jnp.sqrt(_one_minus_sq(a, log_a)) * (i * xc)

    a3 = a.reshape(ng, V7X_SUBLANES, w)
    b3 = bb.reshape(ng, V7X_SUBLANES, w)
    for d in (1, 2, 4):
        ok = sub >= d
        a_sh = pltpu.roll(a3, d, 1)
        b_sh = pltpu.roll(b3, d, 1)
        b3 = jnp.where(ok, a3 * b_sh + b3, b3)
        a3 = jnp.where(ok, a3 * a_sh, a3)
    carry = h_ref[0:1, :]
    groups = []
    for g in range(ng):
        hg = b3[g] + a3[g] * carry
        groups.append(hg)
        carry = hg[V7X_SUBLANES - 1:V7X_SUBLANES, :]
    hcur = jnp.concatenate(groups, axis=0)
    h_ref[...] = jnp.broadcast_to(carry, h_ref.shape)
    o_ref[0] = (hcur * _gelu_tanh(gate)).astype(o_ref.dtype)


def _merge_kernel(attn_ref, lru_ref, h_ref, x_ref, wm_ref, wa_ref, wl_ref, wo_ref, g2_ref,
                  x1_ref, h2_ref):
    d = x_ref.shape[1]
    h = h_ref[...]
    mg0 = _sigmoid(jnp.dot(h, wm_ref[:, 0:d], preferred_element_type=F32))
    mg1 = _sigmoid(jnp.dot(h, wm_ref[:, d:2 * d], preferred_element_type=F32))
    ya = jnp.dot(attn_ref[...], wa_ref[...], preferred_element_type=F32)
    yl = jnp.dot(lru_ref[...], wl_ref[...], preferred_element_type=F32)
    merged = mg0 * ya + mg1 * yl
    x1 = x_ref[...] + jnp.dot(merged.astype(MXU_DTYPE), wo_ref[...], preferred_element_type=F32)
    x1_ref[...] = x1
    ms = jnp.mean(x1 * x1, axis=-1, keepdims=True)
    h2_ref[...] = (x1 * lax.rsqrt(ms + EPS) * g2_ref[...]).astype(h2_ref.dtype)


def _merge(attn, lru, h2d, x2d, w_mg, wa, wl, wo, g2, tm=512):
    m, d = x2d.shape
    row = pl.BlockSpec((tm, d), lambda i: (i, 0))
    full = lambda a: pl.BlockSpec(a.shape, lambda i: (0, 0))
    g2 = g2.reshape(1, d)
    return pl.pallas_call(
        _merge_kernel,
        grid=(m // tm,),
        in_specs=[row, row, row, row, full(w_mg), full(wa), full(wl), full(wo), full(g2)],
        out_specs=[row, row],
        out_shape=[jax.ShapeDtypeStruct((m, d), F32), jax.ShapeDtypeStruct((m, d), MXU_DTYPE)],
        compiler_params=_params("parallel"),
        name="merge_out",
    )(attn, lru, h2d, x2d, w_mg, wa, wl, wo, g2)


FFN_CHUNK = V7X_MXU_DIM


def _ffn_kernel(h_ref, x1_ref, wg_ref, wu_ref, wd_ref, o_ref, act_ref):
    h = h_ref[...]
    f = wg_ref.shape[1]
    for c0 in range(0, f, FFN_CHUNK):
        cols = slice(c0, min(c0 + FFN_CHUNK, f))
        g = jnp.dot(h, wg_ref[:, cols], preferred_element_type=F32)
        u = jnp.dot(h, wu_ref[:, cols], preferred_element_type=F32)
        act_ref[:, cols] = (g * _sigmoid(g) * u).astype(act_ref.dtype)
    o_ref[...] = x1_ref[...] + jnp.dot(act_ref[...], wd_ref[...], preferred_element_type=F32)


def _ffn(h2, x1, wg, wu, wd, tm=512):
    m, d = x1.shape
    f = wg.shape[1]
    resident = lambda a: pl.BlockSpec(a.shape, lambda i: (0, 0), pipeline_mode=pl.Buffered(1))
    row = pl.BlockSpec((tm, d), lambda i: (i, 0))
    return pl.pallas_call(
        _ffn_kernel,
        grid=(m // tm,),
        in_specs=[row, row, resident(wg), resident(wu), resident(wd)],
        out_specs=row,
        out_shape=jax.ShapeDtypeStruct((m, d), F32),
        scratch_shapes=[pltpu.VMEM((tm, f), MXU_DTYPE)],
        compiler_params=_params("parallel"),
        name="swiglu_ffn",
    )(h2, x1, wg, wu, wd)


def _pad_last(a, n):
    return jnp.pad(a, [(0, 0)] * (a.ndim - 1) + [(0, n - a.shape[-1])])


def _layer(x, norm1_g, w_in, q_norm_g, k_norm_g, cmp_pos_k, cmp_w1_k, cmp_b1_k, cmp_w2_k, cmp_b2_k,
           cmp_pos_v, cmp_w1_v, cmp_b1_v, cmp_w2_v, cmp_b2_v, conv_w, conv_b, lru_wa, lru_ba,
           lru_wi, lru_bi, lru_lambda, w_o_attn, w_o_lru, w_out, norm2_g, w_gate, w_up, w_down):
    b, t, d = x.shape
    m = b * t
    attn_dim = N_HEADS * HEAD_DIM
    kv_dim = N_KV * HEAD_DIM
    lru_w = lru_lambda.shape[0]
    assert t % 512 == 0
    o1 = attn_dim
    o2 = o1 + 6 * kv_dim
    o3 = o2 + 3 * N_HEADS
    o4 = o3 + lru_w
    o5 = o4 + lru_w
    cast = lambda a: a.astype(MXU_DTYPE)

    wq_t = cast(w_in[:, :o1].T)
    w_kv = w_in[:, o1:o2].reshape(d, 6, N_KV, HEAD_DIM)
    w_cmp_src = cast(w_kv[:, 0:2].reshape(d, 2 * kv_dim))
    w_k2 = cast(jnp.stack([w_kv[:, 2], w_kv[:, 4]], axis=1).reshape(d, 2 * kv_dim))
    k_gain2 = jnp.stack([jnp.tile(k_norm_g[1], N_KV), jnp.tile(k_norm_g[2], N_KV)])
    lane_grp = np.arange(kv_dim) // HEAD_DIM
    grp_avg = cast(jnp.asarray((lane_grp[:, None] == lane_grp[None, :]) / HEAD_DIM, F32))
    place_np = np.zeros((kv_dim, N_KV * FEAT), np.float32)
    place_np[np.arange(kv_dim), lane_grp * FEAT + np.arange(kv_dim) % HEAD_DIM] = 1.0
    place = cast(jnp.asarray(place_np))
    w_v = jnp.stack([w_kv[:, 3], w_kv[:, 5]], axis=1)
    w_v_t = cast(_pad_last(w_v, V_ROWS).reshape(d, 2 * N_KV * V_ROWS).T)
    v_ones_col = jnp.tile(jnp.arange(V_ROWS) == HEAD_DIM, 2 * N_KV).astype(F32).reshape(-1, 1)
    w_g = w_in[:, o2:o3].reshape(d, N_KV, HPG, 3).transpose(0, 1, 3, 2).reshape(d, N_KV, 3 * HPG)
    w_g_t = cast(_pad_last(w_g, 16).reshape(d, N_KV * 16).T)
    w_lru = cast(w_in[:, o3:o5])
    w_mg = cast(w_in[:, o5:])
    q_gain_col = jnp.tile(q_norm_g * (HEAD_DIM ** -0.5 * LOG2E), N_HEADS).reshape(attn_dim, 1)
    gain_pad = lambda g: _pad_last(g.reshape(1, HEAD_DIM), FEAT)

    per_slab = V7X_MXU_DIM // (lru_w // LRU_BLOCKS)
    eye = jnp.eye(per_slab, dtype=F32)
    slabs = lambda wgt: cast(jnp.einsum('snkj,nm->snkmj', wgt.reshape(LRU_BLOCKS // per_slab, per_slab,
                                                                       *wgt.shape[1:]), eye)
                             .reshape(LRU_BLOCKS // per_slab, V7X_MXU_DIM, V7X_MXU_DIM))
    vec = lambda v: v.reshape(1, lru_w)

    h3d, q_t, v_t, gates_t, k_sel, k_win, cmp_src, lru = _in_proj(
        x,
        (norm1_g.reshape(1, d), wq_t, q_gain_col, w_v_t, v_ones_col, w_g_t, w_k2, k_gain2, grp_avg, place,
         w_cmp_src),
        (w_lru, conv_w.reshape(CONV_W, lru_w), vec(conv_b), slabs(lru_wa), vec(lru_ba), slabs(lru_wi),
         vec(lru_bi), vec(lru_lambda)))
    h2d = h3d.reshape(m, d)

    k_cmp = _compress(cmp_src, 0, cmp_pos_k, cmp_w1_k, cmp_b1_k, cast(_pad_last(cmp_w2_k, FEAT)),
                      _pad_last(cmp_b2_k.reshape(1, HEAD_DIM), FEAT), gain_pad(k_norm_g[0]))
    v_cmp_t = _compress(cmp_src, N_KV // CMP_GROUPS, cmp_pos_v, cmp_w1_v, cmp_b1_v,
                        cast(_pad_last(cmp_w2_v, V_ROWS).T),
                        jnp.concatenate([cmp_b2_v, v_ones_col[HEAD_DIM:V_ROWS, 0]]).reshape(V_ROWS, 1))

    attn = _attention_static(q_t, gates_t, k_cmp, v_cmp_t, k_sel, k_win, v_t)

    x1, h2 = _merge(attn.reshape(m, attn_dim), lru.reshape(m, lru_w), h2d, x.reshape(m, d), w_mg,
                    cast(w_o_attn), cast(w_o_lru), cast(w_out), norm2_g)
    out = _ffn(h2, x1, cast(w_gate), cast(w_up), cast(w_down))
    return out.reshape(b, t, d)


def kernel(x, norm1_g, w_in, q_norm_g, k_norm_g, cmp_pos_k, cmp_w1_k, cmp_b1_k, cmp_w2_k, cmp_b2_k,
           cmp_pos_v, cmp_w1_v, cmp_b1_v, cmp_w2_v, cmp_b2_v, conv_w, conv_b, lru_wa, lru_ba,
           lru_wi, lru_bi, lru_lambda, w_o_attn, w_o_lru, w_out, norm2_g, w_gate, w_up, w_down):
    for l in range(norm1_g.shape[0]):
        x = _layer(x, norm1_g[l], w_in[l], q_norm_g[l], k_norm_g[l], cmp_pos_k[l], cmp_w1_k[l],
                   cmp_b1_k[l], cmp_w2_k[l], cmp_b2_k[l], cmp_pos_v[l], cmp_w1_v[l], cmp_b1_v[l],
                   cmp_w2_v[l], cmp_b2_v[l], conv_w[l], conv_b[l], lru_wa[l], lru_ba[l], lru_wi[l],
                   lru_bi[l], lru_lambda[l], w_o_attn[l], w_o_lru[l], w_out[l], norm2_g[l],
                   w_gate[l], w_up[l], w_down[l])
    return x
```

```python
import functools

import numpy as np
import jax
import jax.numpy as jnp
from jax import lax
from jax.experimental import pallas as pl
from jax.experimental.pallas import tpu as pltpu

N_HEADS = 16
HEAD_DIM = 64
N_KV = 4
HPG = N_HEADS // N_KV
CMP_BLOCK = 32
CMP_STRIDE = 16
CMP_HIDDEN = 256
SEL_BLOCK = 64
N_SEL = 16
WINDOW = 512
FORCE_BONUS = 1e4
LRU_BLOCKS = 16
CONV_W = 4
LRU_C = 8.0
EPS = 1e-6

MXU_DTYPE = jnp.bfloat16
F32 = jnp.float32

V7X_LANES = 128
V7X_SUBLANES = 8
V7X_MXU_DIM = 256
V7X_VMEM_LIMIT_BYTES = 48 * 1024 * 1024

TQ = 256
TK_SEL = 256
TILES_PER_CALL = 2
V_ROWS = 80
LOG2E = 1.4426950408889634
FEAT = V7X_MXU_DIM
F_SEL = HEAD_DIM
F_POS = 2 * HEAD_DIM
F_CMP = F_POS + 6
NEG_MASK = -1e30
NEG_BLOCK = -(2.0 ** 100)


def _params(*sem):
    return pltpu.CompilerParams(dimension_semantics=sem, vmem_limit_bytes=V7X_VMEM_LIMIT_BYTES)


def _gelu_tanh(x):
    return 0.5 * x * (1.0 + jnp.tanh(0.7978845608028654 * (x + 0.044715 * (x * x * x))))


def _sigmoid(x):
    return 1.0 / (1.0 + jnp.exp(-x))


def _one_minus_sq(a, log_a):
    series = (-2.0 * log_a) * (1.0 + log_a * (1.0 + log_a * (2.0 / 3)))
    return jnp.where(log_a > -1.0 / 128, series, 1.0 - a * a)


def _nt_dot(wt, h):
    return lax.dot_general(wt, h, (((1,), (1,)), ((), ())), preferred_element_type=F32)


def _token_features(pos, col, with_block_mask):
    blk = pos >> 6
    off = pos & (SEL_BLOCK - 1)
    feat = jnp.where((col >= F_POS) & (col < F_POS + 3), blk.astype(F32),
                     jnp.where((col >= F_POS + 3) & (col < F_POS + 6), off.astype(F32), 0.0))
    if with_block_mask:
        feat = jnp.where((col >= F_SEL) & (col - F_SEL == blk) & (col < F_POS), NEG_BLOCK, feat)
    return feat


def _in_proj_kernel(x_ref, g1_ref, wq_ref, qg_ref, wv_ref, vb_ref, wg_ref, wk_ref, kg_ref, grp_ref,
                    place_ref, wc_ref, wl_ref, cw_ref, cb_ref, wa_ref, ba_ref, wi_ref, bi_ref, lam_ref,
                    h_ref, qt_ref, vt_ref, gt_ref, ks_ref, kw_ref, cs_ref, lru_ref, tail_ref, state_ref):
    x = x_ref[0]
    tm = x.shape[0]
    ms = jnp.mean(x * x, axis=-1, keepdims=True)
    h = (x * lax.rsqrt(ms + EPS) * g1_ref[...]).astype(MXU_DTYPE)
    h_ref[0] = h

    lru_mid = _lru_gates(h, wl_ref, cw_ref, cb_ref, wa_ref, wi_ref, tail_ref, state_ref)

    r3 = _nt_dot(wq_ref[...], h).reshape(N_HEADS, HEAD_DIM, tm)
    qn = r3 * lax.rsqrt(jnp.mean(r3 * r3, axis=1, keepdims=True) + EPS)
    qt_ref[0] = (qn.reshape(N_HEADS * HEAD_DIM, tm) * qg_ref[...]).astype(qt_ref.dtype)

    vt_ref[0] = (_nt_dot(wv_ref[...], h) + vb_ref[...]).astype(vt_ref.dtype)
    gt_ref[0] = _sigmoid(_nt_dot(wg_ref[...], h))

    k = jnp.dot(h, wk_ref[...], preferred_element_type=F32)
    kk = k * k
    kk_hi = kk.astype(MXU_DTYPE)
    kk_lo = (kk - kk_hi.astype(F32)).astype(MXU_DTYPE)
    pos = pl.program_id(1) * tm + lax.broadcasted_iota(jnp.int32, (tm, FEAT), 0)
    col = lax.broadcasted_iota(jnp.int32, (tm, FEAT), 1)
    kv_dim = N_KV * HEAD_DIM
    for branch, (o_ref, with_block_mask) in enumerate(((ks_ref, True), (kw_ref, False))):
        sl = slice(branch * kv_dim, (branch + 1) * kv_dim)
        msq = (jnp.dot(kk_hi[:, sl], grp_ref[...], preferred_element_type=F32)
               + jnp.dot(kk_lo[:, sl], grp_ref[...], preferred_element_type=F32))
        kn = (k[:, sl] * lax.rsqrt(msq + EPS) * kg_ref[branch:branch + 1, :]).astype(MXU_DTYPE)
        placed = jnp.dot(kn, place_ref[...], preferred_element_type=F32)
        feat = _token_features(pos, col, with_block_mask)
        for g in range(N_KV):
            o_ref[0, g] = (placed[:, g * FEAT:(g + 1) * FEAT] + feat).astype(o_ref.dtype)

    cs_ref[0] = jnp.dot(h, wc_ref[...], preferred_element_type=F32)

    _lru_scan(*lru_mid, ba_ref, bi_ref, lam_ref, lru_ref, state_ref)


def _in_proj(x, attn_weights, lru_weights, tm=512):
    b, t, d = x.shape
    kv_dim = N_KV * HEAD_DIM
    wq_t, w_v_t, w_g_t = attn_weights[1], attn_weights[3], attn_weights[5]
    lru_w = lru_weights[-1].shape[1]
    full = lambda a: pl.BlockSpec(a.shape, lambda bi, i: (0,) * a.ndim)
    rowblk = lambda n: pl.BlockSpec((1, tm, n), lambda bi, i: (bi, i, 0))
    colblk = lambda n: pl.BlockSpec((1, n, tm), lambda bi, i: (bi, 0, i))
    kblk = pl.BlockSpec((1, N_KV, tm, FEAT), lambda bi, i: (bi, 0, i, 0))
    weights = tuple(attn_weights) + tuple(lru_weights)
    return pl.pallas_call(
        _in_proj_kernel,
        grid=(b, t // tm),
        in_specs=[rowblk(d)] + [full(a) for a in weights],
        out_specs=[rowblk(d), colblk(wq_t.shape[0]), colblk(w_v_t.shape[0]), colblk(w_g_t.shape[0]),
                   kblk, kblk, rowblk(2 * kv_dim), rowblk(lru_w)],
        out_shape=[jax.ShapeDtypeStruct((b, t, d), MXU_DTYPE),
                   jax.ShapeDtypeStruct((b, wq_t.shape[0], t), MXU_DTYPE),
                   jax.ShapeDtypeStruct((b, w_v_t.shape[0], t), MXU_DTYPE),
                   jax.ShapeDtypeStruct((b, w_g_t.shape[0], t), F32),
                   jax.ShapeDtypeStruct((b, N_KV, t, FEAT), MXU_DTYPE),
                   jax.ShapeDtypeStruct((b, N_KV, t, FEAT), MXU_DTYPE),
                   jax.ShapeDtypeStruct((b, t, 2 * kv_dim), F32),
                   jax.ShapeDtypeStruct((b, t, lru_w), MXU_DTYPE)],
        scratch_shapes=[pltpu.VMEM((V7X_SUBLANES, lru_w), F32),
                        pltpu.VMEM((V7X_SUBLANES, lru_w), F32)],
        compiler_params=_params("parallel", "arbitrary"),
        name="in_proj_lru",
    )(x, *weights)


CMP_GROUPS = V7X_LANES // HEAD_DIM


def _cmp_hidden(src_ref, pos_ref, w1_ref, b1_ref):
    ncp = src_ref.shape[1] // CMP_STRIDE
    lo, hi = [], []
    for l in range(CMP_STRIDE):
        x = src_ref[0, pl.ds(l, ncp, stride=CMP_STRIDE), :]
        lo.append((x + pos_ref[l:l + 1, :]).astype(MXU_DTYPE))
        hi.append((x + pos_ref[CMP_STRIDE + l:CMP_STRIDE + l + 1, :]).astype(MXU_DTYPE))
    kdim = CMP_STRIDE * V7X_LANES
    first = jnp.dot(jnp.concatenate(lo, axis=1), w1_ref[0:CMP_STRIDE].reshape(kdim, -1),
                    preferred_element_type=F32)
    second = jnp.dot(jnp.concatenate(hi, axis=1), w1_ref[CMP_STRIDE:CMP_BLOCK].reshape(kdim, -1),
                     preferred_element_type=F32)
    hid = first + pltpu.roll(second, ncp - 1, 0) + b1_ref[...]
    return _gelu_tanh(hid).astype(MXU_DTYPE)


def _cmp_k_kernel(src_ref, pos_ref, w1_ref, b1_ref, w2_ref, b2_ref, g_ref, o_ref):
    hid = _cmp_hidden(src_ref, pos_ref, w1_ref, b1_ref)
    ncp = hid.shape[0]
    idx = lax.broadcasted_iota(jnp.int32, (ncp, FEAT), 0)
    col = lax.broadcasted_iota(jnp.int32, (ncp, FEAT), 1)
    feat = jnp.where((col >= F_CMP) & (col < F_CMP + 3), (idx >> 6).astype(F32),
                     jnp.where((col >= F_CMP + 3) & (col < F_CMP + 6), (idx & 63).astype(F32), 0.0))
    for gl in range(CMP_GROUPS):
        r = jnp.dot(hid[:, gl * CMP_HIDDEN:(gl + 1) * CMP_HIDDEN], w2_ref[...],
                    preferred_element_type=F32) + b2_ref[...]
        ms = jnp.sum(r * r, axis=-1, keepdims=True) * (1.0 / HEAD_DIM)
        o_ref[0, gl] = (r * lax.rsqrt(ms + EPS) * g_ref[...] + feat).astype(o_ref.dtype)


def _cmp_v_kernel(src_ref, pos_ref, w1_ref, b1_ref, w2t_ref, b2_ref, o_ref):
    hid = _cmp_hidden(src_ref, pos_ref, w1_ref, b1_ref)
    for gl in range(CMP_GROUPS):
        r = _nt_dot(w2t_ref[...], hid[:, gl * CMP_HIDDEN:(gl + 1) * CMP_HIDDEN]) + b2_ref[...]
        o_ref[0, gl] = r.astype(o_ref.dtype)


def _compress(cmp_src, lane_block0, pos, w1, b1, w2, b2, gain_pad=None):
    b, t, _ = cmp_src.shape
    ncp = t // CMP_STRIDE
    hid_w = CMP_GROUPS * CMP_HIDDEN
    w1c = w1.astype(MXU_DTYPE)
    zero = jnp.zeros_like(w1c)
    w1_bd = jnp.concatenate([jnp.concatenate([w1c if h == g else zero for h in range(CMP_GROUPS)], axis=2)
                             for g in range(CMP_GROUPS)], axis=1)
    pos_t = jnp.tile(pos, (1, CMP_GROUPS))
    b1_t = jnp.tile(b1.reshape(1, CMP_HIDDEN), (1, CMP_GROUPS))
    full = lambda a: pl.BlockSpec(a.shape, lambda bi, p: (0,) * a.ndim)
    src_spec = pl.BlockSpec((1, t, V7X_LANES), lambda bi, p: (bi, 0, lane_block0 + p))
    grid = (b, N_KV // CMP_GROUPS)
    if gain_pad is not None:
        args = (pos_t, w1_bd, b1_t, w2, b2, gain_pad)
        return pl.pallas_call(
            _cmp_k_kernel,
            grid=grid,
            in_specs=[src_spec] + [full(a) for a in args],
            out_specs=pl.BlockSpec((1, CMP_GROUPS, ncp, FEAT), lambda bi, p: (bi, p, 0, 0)),
            out_shape=jax.ShapeDtypeStruct((b, N_KV, ncp, FEAT), MXU_DTYPE),
            compiler_params=_params("parallel", "parallel"),
            name="compress_k",
        )(cmp_src, *args)
    args = (pos_t, w1_bd, b1_t, w2, b2)
    return pl.pallas_call(
        _cmp_v_kernel,
        grid=grid,
        in_specs=[src_spec] + [full(a) for a in args],
        out_specs=pl.BlockSpec((1, CMP_GROUPS, V_ROWS, ncp), lambda bi, p: (bi, p, 0, 0)),
        out_shape=jax.ShapeDtypeStruct((b, N_KV, V_ROWS, ncp), MXU_DTYPE),
        compiler_params=_params("parallel", "parallel"),
        name="compress_v",
    )(cmp_src, *args)


def _split3(v):
    parts = []
    rest = np.asarray(v, np.float64)
    for _ in range(3):
        p = rest.astype(np.float32).astype(jnp.bfloat16).astype(np.float64)
        parts.append(p)
        rest = rest - p
    return parts


def _alibi_query_features():
    tab = np.zeros((N_KV, FEAT - F_POS, HPG * TQ), np.float64)
    for g in range(N_KV):
        for h in range(HPG):
            slope = 2.0 ** (-8.0 * (g * HPG + h + 1) / N_HEADS)
            parts = _split3(slope * LOG2E)
            lanes = slice(h * TQ, (h + 1) * TQ)
            for i, p in enumerate(parts):
                tab[g, i, lanes] = SEL_BLOCK * p
                tab[g, 3 + i, lanes] = p
                tab[g, 6 + i, lanes] = CMP_STRIDE * 64 * p
                tab[g, 9 + i, lanes] = CMP_STRIDE * p
    return jnp.asarray(tab, F32).astype(MXU_DTYPE)


def _block_map_t(n_cmp_pad, n_blk):
    cs = np.arange(n_cmp_pad) * CMP_STRIDE
    ce = cs + CMP_BLOCK - 1
    bs = np.arange(n_blk) * SEL_BLOCK
    be = bs + SEL_BLOCK - 1
    return jnp.asarray(((cs[None, :] <= be[:, None]) & (ce[None, :] >= bs[:, None])).astype(np.float32))


def _prob(s, m):
    return jnp.exp2(s - m).astype(MXU_DTYPE)


def _attn_tiles_kernel(*refs, tiles, win_blk0, n_blk, n_sel):
    (qt_ref, gate_ref, kc_ref, vc_ref, ks_ref, vs_ref, kd_ref, vd_ref, kwa_ref, kwb_ref, vwa_ref, vwb_ref,
     alibi_ref, map_ref) = refs[:14]
    o_ref, qb_ref, qs_ref, imp_ref = refs[-4:]
    lanes = HPG * TQ
    ncp = kc_ref.shape[2]
    blocks_per_tile = TQ // SEL_BLOCK
    win_tiles = WINDOW // TQ
    lane_tok = lax.broadcasted_iota(jnp.int32, (1, lanes), 1) & (TQ - 1)
    row_pos = lax.broadcasted_iota(jnp.int32, (TQ, lanes), 0)
    blk = lax.broadcasted_iota(jnp.int32, (n_blk, TQ), 0)

    def window_tile(qi, j):
        n = len(tiles)
        lt = max(qi - win_tiles, 0) + j - n * win_blk0
        kref, vref = (kwa_ref, vwa_ref) if lt < n else (kwb_ref, vwb_ref)
        sl = slice((lt % n) * TQ, (lt % n + 1) * TQ)
        return kref[0, 0, sl, :], vref[0, :, sl]

    def head(ti, qi):
        cols = slice(ti * TQ, (ti + 1) * TQ)
        t_lane = qi * TQ + lane_tok
        for h in range(HPG):
            qb_ref[ti, 0:HEAD_DIM, h * TQ:(h + 1) * TQ] = qt_ref[0, h * HEAD_DIM:(h + 1) * HEAD_DIM, cols]
        qb_ref[ti, F_SEL:F_POS, :] = jnp.zeros((F_POS - F_SEL, lanes), qb_ref.dtype)
        qb_ref[ti, F_POS:FEAT, :] = alibi_ref[0]
        qb = qb_ref[ti]
        sc = jnp.dot(kc_ref[0, 0], qb, preferred_element_type=F32)
        sw = {j: jnp.dot(window_tile(qi, j)[0], qb, preferred_element_type=F32) for j in sorted(window_order(qi))}
        sd = jnp.dot(kd_ref[0, 0, cols, :], qb, preferred_element_type=F32)
        last_cmp = (t_lane - (CMP_BLOCK - 1)) >> 4
        sc = jnp.where(lax.broadcasted_iota(jnp.int32, (ncp, lanes), 0) <= last_cmp, sc, NEG_MASK)
        ec = jnp.exp2(sc - jnp.max(sc, axis=0, keepdims=True))
        acc_c = jnp.dot(vc_ref[0, 0], ec.astype(MXU_DTYPE), preferred_element_type=F32)
        inv_c = jnp.where(last_cmp >= 0, 1.0 / jnp.maximum(acc_c[HEAD_DIM:HEAD_DIM + 1], 1e-30), 0.0)
        imp = None
        if (qi + 1) * blocks_per_tile > n_sel and qi > 0:
            psum = ec[:, 0:TQ] * inv_c[:, 0:TQ]
            for h in range(1, HPG):
                psum = psum + ec[:, h * TQ:(h + 1) * TQ] * inv_c[:, h * TQ:(h + 1) * TQ]
            imp = jnp.dot(map_ref[...], psum, preferred_element_type=F32)
        return dict(qb=qb, t_lane=t_lane, sw=sw, sd=sd, o_cmp=acc_c[0:HEAD_DIM] * inv_c, imp=imp,
                    causal=qi * TQ + row_pos <= t_lane)

    def rank_init(ti, qi, st):
        first_own_blk = qi * blocks_per_tile
        st.update(ranks=[], k_done=0, k_total=0)
        if st["imp"] is None:
            return
        cur = (qi * TQ + lax.broadcasted_iota(jnp.int32, (n_blk, TQ), 1)) >> 6
        forced = (blk == 0) | (blk == cur) | (blk == cur - 1)
        imp = jnp.where(blk <= cur, st["imp"] + jnp.where(forced, FORCE_BONUS, 0.0), NEG_MASK)
        imp_ref[ti] = imp
        n_rank_chunks = -(-first_own_blk // V7X_SUBLANES)
        st["chunks"] = [imp[c * V7X_SUBLANES:(c + 1) * V7X_SUBLANES] for c in range(n_rank_chunks)]
        st["ranks"] = [jnp.zeros((V7X_SUBLANES, TQ), jnp.int32) for _ in range(n_rank_chunks)]
        st["k_total"] = min(first_own_blk + blocks_per_tile, n_blk)

    def rank_rounds(ti, st, n):
        sub = lax.broadcasted_iota(jnp.int32, (V7X_SUBLANES, TQ), 0)
        stop = min(st["k_done"] + n, st["k_total"])
        for k in range(st["k_done"], stop):
            row = imp_ref[ti, k:k + 1, :]
            for c, mine in enumerate(st["chunks"]):
                lo = c * V7X_SUBLANES
                if lo > k:
                    one = jnp.where(row >= mine, 1, 0)
                elif lo + V7X_SUBLANES - 1 <= k:
                    one = jnp.where(row > mine, 1, 0)
                else:
                    one = jnp.where(sub + lo > k, jnp.where(row >= mine, 1, 0), jnp.where(row > mine, 1, 0))
                st["ranks"][c] = st["ranks"][c] + one
        st["k_done"] = stop

    def chunk_setup(ti, qi, st):
        rank_rounds(ti, st, st["k_total"])
        first_own_blk = qi * blocks_per_tile
        n_chunks = -(-qi * TQ // TK_SEL)
        st.update(n_chunks=n_chunks, m=jnp.full((1, lanes), NEG_MASK, F32), acc=jnp.zeros((V_ROWS, lanes), F32),
                  p_prev=None)
        if n_chunks == 0:
            return
        rank = jnp.zeros((n_blk, TQ), jnp.int32)
        if st["ranks"]:
            pad = [jnp.zeros((n_blk - len(st["ranks"]) * V7X_SUBLANES, TQ), jnp.int32)]
            rank = jnp.concatenate(st["ranks"] + (pad if pad[0].shape[0] else []), axis=0)
        not_sel = jnp.where((rank < n_sel) & (blk < first_own_blk), 0.0, 1.0).astype(qs_ref.dtype)
        qs_ref[ti] = st["qb"]
        for h in range(HPG):
            qs_ref[ti, F_SEL:F_SEL + n_blk, h * TQ:(h + 1) * TQ] = not_sel
        st["qs"] = qs_ref[ti]
        st["s_next"] = chunk_qk(st, 0)

    def chunk_qk(st, k):
        return jnp.dot(ks_ref[0, 0, k * TK_SEL:(k + 1) * TK_SEL, :], st["qs"], preferred_element_type=F32)

    def chunk_pv(st, k):
        return jnp.dot(vs_ref[0, :, k * TK_SEL:(k + 1) * TK_SEL], st["p_prev"], preferred_element_type=F32)

    def chunk_stage(st, k):
        s_cur = st["s_next"]
        if k + 1 < st["n_chunks"]:
            st["s_next"] = chunk_qk(st, k + 1)
        if st["p_prev"] is not None:
            st["acc"] = st["acc"] + chunk_pv(st, k - 1)
        m_new = jnp.maximum(st["m"], jnp.max(s_cur, axis=0, keepdims=True))
        st["p_prev"] = _prob(s_cur, m_new)
        st["acc"] = jnp.exp2(st["m"] - m_new) * st["acc"]
        st["m"] = m_new

    def chunk_finish(st):
        if st["n_chunks"] > 0:
            st["acc"] = st["acc"] + chunk_pv(st, st["n_chunks"] - 1)

    def diagonal(ti, qi, st):
        cols = slice(ti * TQ, (ti + 1) * TQ)
        sd = jnp.where(st["causal"], st["sd"], NEG_MASK)
        m_d = jnp.max(sd, axis=0, keepdims=True)
        acc_d = jnp.dot(vd_ref[0, :, cols], _prob(sd, m_d), preferred_element_type=F32)
        m_all = jnp.maximum(st["m"], m_d)
        acc_s = jnp.exp2(st["m"] - m_all) * st["acc"] + jnp.exp2(m_d - m_all) * acc_d
        st["o_sel"] = acc_s[0:HEAD_DIM] * (1.0 / acc_s[HEAD_DIM:HEAD_DIM + 1])

    def window_order(qi):
        wb = max(qi - win_tiles, 0)
        return [qi - wb] + [j for j in range(qi - wb - 1, -1, -1)]

    def window_stage(qi, st, j, last):
        wb = max(qi - win_tiles, 0)
        s = st["sw"][j]
        if wb + j == qi:
            s = jnp.where(st["causal"], s, NEG_MASK)
        elif wb + j != qi - 1:
            dj = (st["t_lane"] - (wb + j) * TQ) - row_pos
            s = jnp.where(lax.bitcast_convert_type(dj, jnp.uint32) < WINDOW, s, NEG_MASK)
        m_j = jnp.max(s, axis=0, keepdims=True)
        v_j = window_tile(qi, j)[1]
        if "m_w" not in st:
            m_new = m_j
            acc_w = jnp.dot(v_j, _prob(s, m_new), preferred_element_type=F32)
        else:
            m_new = jnp.maximum(st["m_w"], m_j)
            acc_w = jnp.exp2(st["m_w"] - m_new) * st["acc_w"] + jnp.dot(v_j, _prob(s, m_new),
                                                                          preferred_element_type=F32)
        st["m_w"], st["acc_w"] = m_new, acc_w
        if last:
            st["o_win"] = acc_w[0:HEAD_DIM] * (1.0 / acc_w[HEAD_DIM:HEAD_DIM + 1])

    def output(ti, st):
        cols = slice(ti * TQ, (ti + 1) * TQ)
        gates = gate_ref[0, :, cols]
        def gate_row(j):
            return jnp.concatenate([gates[j * HPG + h:j * HPG + h + 1, :] for h in range(HPG)], axis=1)
        o_t = gate_row(0) * st["o_cmp"] + gate_row(1) * st["o_sel"] + gate_row(2) * st["o_win"]
        for hp in range(HPG // 2):
            pair = jnp.concatenate([o_t[:, (2 * hp) * TQ:(2 * hp + 1) * TQ],
                                    o_t[:, (2 * hp + 1) * TQ:(2 * hp + 2) * TQ]], axis=0)
            o_ref[0, cols, hp * 2 * HEAD_DIM:(hp + 1) * 2 * HEAD_DIM] = pair.T.astype(o_ref.dtype)

    def tail_phases(ti, qi, st):
        return [lambda: diagonal(ti, qi, st), lambda: output(ti, st)]

    cur = head(0, tiles[0])
    rank_init(0, tiles[0], cur)
    chunk_setup(0, tiles[0], cur)
    pending = []
    for ti, qi in enumerate(tiles):
        nxt = None
        if ti + 1 < len(tiles):
            nxt = head(ti + 1, tiles[ti + 1])
            rank_init(ti + 1, tiles[ti + 1], nxt)
            rounds_per_stage = -(-nxt["k_total"] // max(cur["n_chunks"], 1))
        order = window_order(qi)
        for j in order:
            pending.append(lambda qi=qi, st=cur, j=j, last=(j == order[-1]): window_stage(qi, st, j, last))
        for k in range(cur["n_chunks"]):
            chunk_stage(cur, k)
            if nxt is not None:
                rank_rounds(ti + 1, nxt, rounds_per_stage)
            if pending:
                pending.pop(0)()
        for phase in pending:
            phase()
        chunk_finish(cur)
        pending = tail_phases(ti, qi, cur)
        if nxt is not None:
            chunk_setup(ti + 1, tiles[ti + 1], nxt)
            cur = nxt
    for phase in pending:
        phase()


def _attention_static(q_t, gates_t, k_cmp, v_cmp_t, k_sel, k_win, v_t):
    b, _, t = q_t.shape
    n_blk = t // SEL_BLOCK
    tiles_per_call = TILES_PER_CALL
    tb = tiles_per_call * TQ
    assert tb >= WINDOW and t % max(tb, TK_SEL) == 0 and t >= 2 * tb and F_SEL + n_blk <= F_POS, \
        "unsupported sequence length"
    n_sel = min(N_SEL, n_blk)
    lanes = HPG * TQ
    rows = HPG * HEAD_DIM
    alibi = _alibi_query_features()
    out_shape = jax.ShapeDtypeStruct((b, t, N_HEADS * HEAD_DIM), MXU_DTYPE)

    attn = None
    for m in range(t // tb):
        tiles = tuple(range(m * tiles_per_call, (m + 1) * tiles_per_call))
        ncp = min(t // CMP_STRIDE, -(-((m + 1) * tb // CMP_STRIDE) // V7X_LANES) * V7X_LANES)
        blk_map_t = _block_map_t(ncp, n_blk)
        kc = -(-tiles[-1] * TQ // TK_SEL) * TK_SEL
        wb0 = max(m - 1, 0)
        in_specs = [
            pl.BlockSpec((1, rows, tb), lambda bi, g, m=m: (bi, g, m)),
            pl.BlockSpec((1, 16, tb), lambda bi, g, m=m: (bi, g, m)),
            pl.BlockSpec((1, 1, ncp, FEAT), lambda bi, g: (bi, g, 0, 0)),
            pl.BlockSpec((1, 1, V_ROWS, ncp), lambda bi, g: (bi, g, 0, 0)),
            pl.BlockSpec((1, 1, kc, FEAT), lambda bi, g: (bi, g, 0, 0)),
            pl.BlockSpec((1, V_ROWS, kc), lambda bi, g: (bi, g, 0)),
            pl.BlockSpec((1, 1, tb, FEAT), lambda bi, g, m=m: (bi, g, m, 0)),
            pl.BlockSpec((1, V_ROWS, tb), lambda bi, g, m=m: (bi, g, m)),
            pl.BlockSpec((1, 1, tb, FEAT), lambda bi, g, w=wb0: (bi, g, w, 0)),
            pl.BlockSpec((1, 1, tb, FEAT), lambda bi, g, w=wb0: (bi, g, w + 1, 0)),
            pl.BlockSpec((1, V_ROWS, tb), lambda bi, g, w=wb0: (bi, N_KV + g, w)),
            pl.BlockSpec((1, V_ROWS, tb), lambda bi, g, w=wb0: (bi, N_KV + g, w + 1)),
            pl.BlockSpec((1, FEAT - F_POS, lanes), lambda bi, g: (g, 0, 0)),
            pl.BlockSpec((n_blk, ncp), lambda bi, g: (0, 0)),
        ]
        args = [q_t, gates_t, k_cmp, v_cmp_t, k_sel, v_t, k_sel, v_t, k_win, k_win, v_t, v_t, alibi, blk_map_t]
        aliases = {}
        if attn is not None:
            in_specs.append(pl.BlockSpec(memory_space=pl.ANY))
            args.append(attn)
            aliases = {len(args) - 1: 0}
        kernel = functools.partial(_attn_tiles_kernel, tiles=tiles, win_blk0=wb0, n_blk=n_blk, n_sel=n_sel)
        attn = pl.pallas_call(
            kernel,
            grid=(b, N_KV),
            in_specs=in_specs,
            out_specs=pl.BlockSpec((1, tb, rows), lambda bi, g, m=m: (bi, m, g)),
            out_shape=out_shape,
            scratch_shapes=[pltpu.VMEM((tiles_per_call, FEAT, lanes), MXU_DTYPE),
                            pltpu.VMEM((tiles_per_call, FEAT, lanes), MXU_DTYPE),
                            pltpu.VMEM((tiles_per_call, n_blk, TQ), F32)],
            input_output_aliases=aliases,
            compiler_params=_params("parallel", "parallel"),
            name=f"nsa_attention_{m}",
        )(*args)
    return attn


def _lru_gates(hin, wl_ref, cw_ref, cb_ref, wa_ref, wi_ref, tail_ref, h_ref):
    tt = hin.shape[0]
    w = cb_ref.shape[1]

    @pl.when(pl.program_id(1) == 0)
    def _():
        tail_ref[...] = jnp.zeros_like(tail_ref)
        h_ref[...] = jnp.zeros_like(h_ref)

    def gate_matmul(xb, w_ref):
        n = w_ref.shape[1]
        return jnp.concatenate([jnp.dot(xb[:, j * n:(j + 1) * n], w_ref[j], preferred_element_type=F32)
                                for j in range(w_ref.shape[0])], axis=1)

    x = jnp.dot(hin, wl_ref[:, 0:w], preferred_element_type=F32)
    gate = jnp.dot(hin, wl_ref[:, w:2 * w], preferred_element_type=F32)
    ng = tt // V7X_SUBLANES
    sub = lax.broadcasted_iota(jnp.int32, (ng, V7X_SUBLANES, w), 1)
    x3 = x.reshape(ng, V7X_SUBLANES, w)
    xprev3 = jnp.concatenate([tail_ref[...][None], x3], axis=0)
    tail_ref[...] = x[tt - V7X_SUBLANES:tt]
    xc = x * cw_ref[CONV_W - 1:CONV_W, :] + cb_ref[...]
    for s in range(1, CONV_W):
        rot = pltpu.roll(xprev3, s, 1)
        xs = jnp.where(sub >= s, rot[1:], rot[:-1])
        xc = xc + xs.reshape(tt, w) * cw_ref[CONV_W - 1 - s:CONV_W - s, :]

    xb = xc.astype(MXU_DTYPE)
    return xc, gate_matmul(xb, wa_ref), gate_matmul(xb, wi_ref), gate


def _lru_scan(xc, r_pre, i_pre, gate, ba_ref, bi_ref, lam_ref, o_ref, h_ref):
    tt, w = xc.shape
    ng = tt // V7X_SUBLANES
    sub = lax.broadcasted_iota(jnp.int32, (ng, V7X_SUBLANES, w), 1)
    r = _sigmoid(r_pre + ba_ref[...])
    i = _sigmoid(i_pre + bi_ref[...])
    z = -lam_ref[...]
    softplus = jnp.maximum(z, 0.0) + jnp.log1p(jnp.exp(-jnp.abs(z)))
    log_a = -LRU_C * r * softplus
    a = jnp.exp(log_a)
    bb = jnp.sqrt(_one_minus_sq(a, log_a)) * (i * xc)

    a3 = a.reshape(ng, V7X_SUBLANES, w)
    b3 = bb.reshape(ng, V7X_SUBLANES, w)
    for d in (1, 2, 4):
        ok = sub >= d
        a_sh = pltpu.roll(a3, d, 1)
        b_sh = pltpu.roll(b3, d, 1)
        b3 = jnp.where(ok, a3 * b_sh + b3, b3)
        a3 = jnp.where(ok, a3 * a_sh, a3)
    carry = h_ref[0:1, :]
    groups = []
    for g in range(ng):
        hg = b3[g] + a3[g] * carry
        groups.append(hg)
        carry = hg[V7X_SUBLANES - 1:V7X_SUBLANES, :]
    hcur = jnp.concatenate(groups, axis=0)
    h_ref[...] = jnp.broadcast_to(carry, h_ref.shape)
    o_ref[0] = (hcur * _gelu_tanh(gate)).astype(o_ref.dtype)


def _merge_kernel(attn_ref, lru_ref, h_ref, x_ref, wm_ref, wa_ref, wl_ref, wo_ref, g2_ref,
                  x1_ref, h2_ref):
    d = x_ref.shape[1]
    h = h_ref[...]
    mg0 = _sigmoid(jnp.dot(h, wm_ref[:, 0:d], preferred_element_type=F32))
    mg1 = _sigmoid(jnp.dot(h, wm_ref[:, d:2 * d], preferred_element_type=F32))
    ya = jnp.dot(attn_ref[...], wa_ref[...], preferred_element_type=F32)
    yl = jnp.dot(lru_ref[...], wl_ref[...], preferred_element_type=F32)
    merged = mg0 * ya + mg1 * yl
    x1 = x_ref[...] + jnp.dot(merged.astype(MXU_DTYPE), wo_ref[...], preferred_element_type=F32)
    x1_ref[...] = x1
    ms = jnp.mean(x1 * x1, axis=-1, keepdims=True)
    h2_ref[...] = (x1 * lax.rsqrt(ms + EPS) * g2_ref[...]).astype(h2_ref.dtype)


def _merge(attn, lru, h2d, x2d, w_mg, wa, wl, wo, g2, tm=512):
    m, d = x2d.shape
    row = pl.BlockSpec((tm, d), lambda i: (i, 0))
    full = lambda a: pl.BlockSpec(a.shape, lambda i: (0, 0))
    g2 = g2.reshape(1, d)
    return pl.pallas_call(
        _merge_kernel,
        grid=(m // tm,),
        in_specs=[row, row, row, row, full(w_mg), full(wa), full(wl), full(wo), full(g2)],
        out_specs=[row, row],
        out_shape=[jax.ShapeDtypeStruct((m, d), F32), jax.ShapeDtypeStruct((m, d), MXU_DTYPE)],
        compiler_params=_params("parallel"),
        name="merge_out",
    )(attn, lru, h2d, x2d, w_mg, wa, wl, wo, g2)


FFN_CHUNK = V7X_MXU_DIM


def _ffn_kernel(h_ref, x1_ref, wg_ref, wu_ref, wd_ref, o_ref, act_ref):
    h = h_ref[...]
    f = wg_ref.shape[1]
    for c0 in range(0, f, FFN_CHUNK):
        cols = slice(c0, min(c0 + FFN_CHUNK, f))
        g = jnp.dot(h, wg_ref[:, cols], preferred_element_type=F32)
        u = jnp.dot(h, wu_ref[:, cols], preferred_element_type=F32)
        act_ref[:, cols] = (g * _sigmoid(g) * u).astype(act_ref.dtype)
    o_ref[...] = x1_ref[...] + jnp.dot(act_ref[...], wd_ref[...], preferred_element_type=F32)


def _ffn(h2, x1, wg, wu, wd, tm=512):
    m, d = x1.shape
    f = wg.shape[1]
    resident = lambda a: pl.BlockSpec(a.shape, lambda i: (0, 0), pipeline_mode=pl.Buffered(1))
    row = pl.BlockSpec((tm, d), lambda i: (i, 0))
    return pl.pallas_call(
        _ffn_kernel,
        grid=(m // tm,),
        in_specs=[row, row, resident(wg), resident(wu), resident(wd)],
        out_specs=row,
        out_shape=jax.ShapeDtypeStruct((m, d), F32),
        scratch_shapes=[pltpu.VMEM((tm, f), MXU_DTYPE)],
        compiler_params=_params("parallel"),
        name="swiglu_ffn",
    )(h2, x1, wg, wu, wd)


def _pad_last(a, n):
    return jnp.pad(a, [(0, 0)] * (a.ndim - 1) + [(0, n - a.shape[-1])])


def _layer(x, norm1_g, w_in, q_norm_g, k_norm_g, cmp_pos_k, cmp_w1_k, cmp_b1_k, cmp_w2_k, cmp_b2_k,
           cmp_pos_v, cmp_w1_v, cmp_b1_v, cmp_w2_v, cmp_b2_v, conv_w, conv_b, lru_wa, lru_ba,
           lru_wi, lru_bi, lru_lambda, w_o_attn, w_o_lru, w_out, norm2_g, w_gate, w_up, w_down):
    b, t, d = x.shape
    m = b * t
    attn_dim = N_HEADS * HEAD_DIM
    kv_dim = N_KV * HEAD_DIM
    lru_w = lru_lambda.shape[0]
    assert t % 512 == 0
    o1 = attn_dim
    o2 = o1 + 6 * kv_dim
    o3 = o2 + 3 * N_HEADS
    o4 = o3 + lru_w
    o5 = o4 + lru_w
    cast = lambda a: a.astype(MXU_DTYPE)

    wq_t = cast(w_in[:, :o1].T)
    w_kv = w_in[:, o1:o2].reshape(d, 6, N_KV, HEAD_DIM)
    w_cmp_src = cast(w_kv[:, 0:2].reshape(d, 2 * kv_dim))
    w_k2 = cast(jnp.stack([w_kv[:, 2], w_kv[:, 4]], axis=1).reshape(d, 2 * kv_dim))
    k_gain2 = jnp.stack([jnp.tile(k_norm_g[1], N_KV), jnp.tile(k_norm_g[2], N_KV)])
    lane_grp = np.arange(kv_dim) // HEAD_DIM
    grp_avg = cast(jnp.asarray((lane_grp[:, None] == lane_grp[None, :]) / HEAD_DIM, F32))
    place_np = np.zeros((kv_dim, N_KV * FEAT), np.float32)
    place_np[np.arange(kv_dim), lane_grp * FEAT + np.arange(kv_dim) % HEAD_DIM] = 1.0
    place = cast(jnp.asarray(place_np))
    w_v = jnp.stack([w_kv[:, 3], w_kv[:, 5]], axis=1)
    w_v_t = cast(_pad_last(w_v, V_ROWS).reshape(d, 2 * N_KV * V_ROWS).T)
    v_ones_col = jnp.tile(jnp.arange(V_ROWS) == HEAD_DIM, 2 * N_KV).astype(F32).reshape(-1, 1)
    w_g = w_in[:, o2:o3].reshape(d, N_KV, HPG, 3).transpose(0, 1, 3, 2).reshape(d, N_KV, 3 * HPG)
    w_g_t = cast(_pad_last(w_g, 16).reshape(d, N_KV * 16).T)
    w_lru = cast(w_in[:, o3:o5])
    w_mg = cast(w_in[:, o5:])
    q_gain_col = jnp.tile(q_norm_g * (HEAD_DIM ** -0.5 * LOG2E), N_HEADS).reshape(attn_dim, 1)
    gain_pad = lambda g: _pad_last(g.reshape(1, HEAD_DIM), FEAT)

    per_slab = V7X_MXU_DIM // (lru_w // LRU_BLOCKS)
    eye = jnp.eye(per_slab, dtype=F32)
    slabs = lambda wgt: cast(jnp.einsum('snkj,nm->snkmj', wgt.reshape(LRU_BLOCKS // per_slab, per_slab,
                                                                       *wgt.shape[1:]), eye)
                             .reshape(LRU_BLOCKS // per_slab, V7X_MXU_DIM, V7X_MXU_DIM))
    vec = lambda v: v.reshape(1, lru_w)

    h3d, q_t, v_t, gates_t, k_sel, k_win, cmp_src, lru = _in_proj(
        x,
        (norm1_g.reshape(1, d), wq_t, q_gain_col, w_v_t, v_ones_col, w_g_t, w_k2, k_gain2, grp_avg, place,
         w_cmp_src),
        (w_lru, conv_w.reshape(CONV_W, lru_w), vec(conv_b), slabs(lru_wa), vec(lru_ba), slabs(lru_wi),
         vec(lru_bi), vec(lru_lambda)))
    h2d = h3d.reshape(m, d)

    k_cmp = _compress(cmp_src, 0, cmp_pos_k, cmp_w1_k, cmp_b1_k, cast(_pad_last(cmp_w2_k, FEAT)),
                      _pad_last(cmp_b2_k.reshape(1, HEAD_DIM), FEAT), gain_pad(k_norm_g[0]))
    v_cmp_t = _compress(cmp_src, N_KV // CMP_GROUPS, cmp_pos_v, cmp_w1_v, cmp_b1_v,
                        cast(_pad_last(cmp_w2_v, V_ROWS).T),
                        jnp.concatenate([cmp_b2_v, v_ones_col[HEAD_DIM:V_ROWS, 0]]).reshape(V_ROWS, 1))

    attn = _attention_static(q_t, gates_t, k_cmp, v_cmp_t, k_sel, k_win, v_t)

    x1, h2 = _merge(attn.reshape(m, attn_dim), lru.reshape(m, lru_w), h2d, x.reshape(m, d), w_mg,
                    cast(w_o_attn), cast(w_o_lru), cast(w_out), norm2_g)
    out = _ffn(h2, x1, cast(w_gate), cast(w_up), cast(w_down))
    return out.reshape(b, t, d)


def kernel(x, norm1_g, w_in, q_norm_g, k_norm_g, cmp_pos_k, cmp_w1_k, cmp_b1_k, cmp_w2_k, cmp_b2_k,
           cmp_pos_v, cmp_w1_v, cmp_b1_v, cmp_w2_v, cmp_b2_v, conv_w, conv_b, lru_wa, lru_ba,
           lru_wi, lru_bi, lru_lambda, w_o_attn, w_o_lru, w_out, norm2_g, w_gate, w_up, w_down):
    for l in range(norm1_g.shape[0]):
        x = _layer(x, norm1_g[l], w_in[l], q_norm_g[l], k_norm_g[l], cmp_pos_k[l], cmp_w1_k[l],
                   cmp_b1_k[l], cmp_w2_k[l], cmp_b2_k[l], cmp_pos_v[l], cmp_w1_v[l], cmp_b1_v[l],
                   cmp_w2_v[l], cmp_b2_v[l], conv_w[l], conv_b[l], lru_wa[l], lru_ba[l], lru_wi[l],
                   lru_bi[l], lru_lambda[l], w_o_attn[l], w_o_lru[l], w_out[l], norm2_g[l],
                   w_gate[l], w_up[l], w_down[l])
    return x
```

```python
import functools

import numpy as np
import jax
import jax.numpy as jnp
from jax import lax
from jax.experimental import pallas as pl
from jax.experimental.pallas import tpu as pltpu

N_HEADS = 16
HEAD_DIM = 64
N_KV = 4
HPG = N_HEADS // N_KV
CMP_BLOCK = 32
CMP_STRIDE = 16
CMP_HIDDEN = 256
SEL_BLOCK = 64
N_SEL = 16
WINDOW = 512
FORCE_BONUS = 1e4
LRU_BLOCKS = 16
CONV_W = 4
LRU_C = 8.0
EPS = 1e-6

MXU_DTYPE = jnp.bfloat16
F32 = jnp.float32

V7X_LANES = 128
V7X_SUBLANES = 8
V7X_MXU_DIM = 256
V7X_VMEM_LIMIT_BYTES = 48 * 1024 * 1024

TQ = 256
TK_SEL = 256
TILES_PER_CALL = 2
V_ROWS = 80
LOG2E = 1.4426950408889634
FEAT = V7X_MXU_DIM
F_SEL = HEAD_DIM
F_POS = 2 * HEAD_DIM
F_CMP = F_POS + 6
NEG_MASK = -1e30
NEG_BLOCK = -(2.0 ** 100)


def _params(*sem):
    return pltpu.CompilerParams(dimension_semantics=sem, vmem_limit_bytes=V7X_VMEM_LIMIT_BYTES)


def _gelu_tanh(x):
    return 0.5 * x * (1.0 + jnp.tanh(0.7978845608028654 * (x + 0.044715 * (x * x * x))))


def _sigmoid(x):
    return 1.0 / (1.0 + jnp.exp(-x))


def _one_minus_sq(a, log_a):
    series = (-2.0 * log_a) * (1.0 + log_a * (1.0 + log_a * (2.0 / 3)))
    return jnp.where(log_a > -1.0 / 128, series, 1.0 - a * a)


def _nt_dot(wt, h):
    return lax.dot_general(wt, h, (((1,), (1,)), ((), ())), preferred_element_type=F32)


def _token_features(pos, col, with_block_mask):
    blk = pos >> 6
    off = pos & (SEL_BLOCK - 1)
    feat = jnp.where((col >= F_POS) & (col < F_POS + 3), blk.astype(F32),
                     jnp.where((col >= F_POS + 3) & (col < F_POS + 6), off.astype(F32), 0.0))
    if with_block_mask:
        feat = jnp.where((col >= F_SEL) & (col - F_SEL == blk) & (col < F_POS), NEG_BLOCK, feat)
    return feat


def _in_proj_kernel(x_ref, g1_ref, wq_ref, qg_ref, wv_ref, vb_ref, wg_ref, wk_ref, kg_ref, grp_ref,
                    place_ref, wc_ref, wl_ref, cw_ref, cb_ref, wa_ref, ba_ref, wi_ref, bi_ref, lam_ref,
                    h_ref, qt_ref, vt_ref, gt_ref, ks_ref, kw_ref, cs_ref, lru_ref, tail_ref, state_ref):
    x = x_ref[0]
    tm = x.shape[0]
    ms = jnp.mean(x * x, axis=-1, keepdims=True)
    h = (x * lax.rsqrt(ms + EPS) * g1_ref[...]).astype(MXU_DTYPE)
    h_ref[0] = h

    lru_mid = _lru_gates(h, wl_ref, cw_ref, cb_ref, wa_ref, wi_ref, tail_ref, state_ref)

    r3 = _nt_dot(wq_ref[...], h).reshape(N_HEADS, HEAD_DIM, tm)
    qn = r3 * lax.rsqrt(jnp.mean(r3 * r3, axis=1, keepdims=True) + EPS)
    qt_ref[0] = (qn.reshape(N_HEADS * HEAD_DIM, tm) * qg_ref[...]).astype(qt_ref.dtype)

    vt_ref[0] = (_nt_dot(wv_ref[...], h) + vb_ref[...]).astype(vt_ref.dtype)
    gt_ref[0] = _sigmoid(_nt_dot(wg_ref[...], h))

    k = jnp.dot(h, wk_ref[...], preferred_element_type=F32)
    kk = k * k
    kk_hi = kk.astype(MXU_DTYPE)
    kk_lo = (kk - kk_hi.astype(F32)).astype(MXU_DTYPE)
    pos = pl.program_id(1) * tm + lax.broadcasted_iota(jnp.int32, (tm, FEAT), 0)
    col = lax.broadcasted_iota(jnp.int32, (tm, FEAT), 1)
    kv_dim = N_KV * HEAD_DIM
    for branch, (o_ref, with_block_mask) in enumerate(((ks_ref, True), (kw_ref, False))):
        sl = slice(branch * kv_dim, (branch + 1) * kv_dim)
        msq = (jnp.dot(kk_hi[:, sl], grp_ref[...], preferred_element_type=F32)
               + jnp.dot(kk_lo[:, sl], grp_ref[...], preferred_element_type=F32))
        kn = (k[:, sl] * lax.rsqrt(msq + EPS) * kg_ref[branch:branch + 1, :]).astype(MXU_DTYPE)
        placed = jnp.dot(kn, place_ref[...], preferred_element_type=F32)
        feat = _token_features(pos, col, with_block_mask)
        for g in range(N_KV):
            o_ref[0, g] = (placed[:, g * FEAT:(g + 1) * FEAT] + feat).astype(o_ref.dtype)

    cs_ref[0] = jnp.dot(h, wc_ref[...], preferred_element_type=F32)

    _lru_scan(*lru_mid, ba_ref, bi_ref, lam_ref, lru_ref, state_ref)


def _in_proj(x, attn_weights, lru_weights, tm=512):
    b, t, d = x.shape
    kv_dim = N_KV * HEAD_DIM
    wq_t, w_v_t, w_g_t = attn_weights[1], attn_weights[3], attn_weights[5]
    lru_w = lru_weights[-1].shape[1]
    full = lambda a: pl.BlockSpec(a.shape, lambda bi, i: (0,) * a.ndim)
    rowblk = lambda n: pl.BlockSpec((1, tm, n), lambda bi, i: (bi, i, 0))
    colblk = lambda n: pl.BlockSpec((1, n, tm), lambda bi, i: (bi, 0, i))
    kblk = pl.BlockSpec((1, N_KV, tm, FEAT), lambda bi, i: (bi, 0, i, 0))
    weights = tuple(attn_weights) + tuple(lru_weights)
    return pl.pallas_call(
        _in_proj_kernel,
        grid=(b, t // tm),
        in_specs=[rowblk(d)] + [full(a) for a in weights],
        out_specs=[rowblk(d), colblk(wq_t.shape[0]), colblk(w_v_t.shape[0]), colblk(w_g_t.shape[0]),
                   kblk, kblk, rowblk(2 * kv_dim), rowblk(lru_w)],
        out_shape=[jax.ShapeDtypeStruct((b, t, d), MXU_DTYPE),
                   jax.ShapeDtypeStruct((b, wq_t.shape[0], t), MXU_DTYPE),
                   jax.ShapeDtypeStruct((b, w_v_t.shape[0], t), MXU_DTYPE),
                   jax.ShapeDtypeStruct((b, w_g_t.shape[0], t), F32),
                   jax.ShapeDtypeStruct((b, N_KV, t, FEAT), MXU_DTYPE),
                   jax.ShapeDtypeStruct((b, N_KV, t, FEAT), MXU_DTYPE),
                   jax.ShapeDtypeStruct((b, t, 2 * kv_dim), F32),
                   jax.ShapeDtypeStruct((b, t, lru_w), MXU_DTYPE)],
        scratch_shapes=[pltpu.VMEM((V7X_SUBLANES, lru_w), F32),
                        pltpu.VMEM((V7X_SUBLANES, lru_w), F32)],
        compiler_params=_params("parallel", "arbitrary"),
        name="in_proj_lru",
    )(x, *weights)


CMP_GROUPS = V7X_LANES // HEAD_DIM


def _cmp_hidden(src_ref, pos_ref, w1_ref, b1_ref):
    ncp = src_ref.shape[1] // CMP_STRIDE
    lo, hi = [], []
    for l in range(CMP_STRIDE):
        x = src_ref[0, pl.ds(l, ncp, stride=CMP_STRIDE), :]
        lo.append((x + pos_ref[l:l + 1, :]).astype(MXU_DTYPE))
        hi.append((x + pos_ref[CMP_STRIDE + l:CMP_STRIDE + l + 1, :]).astype(MXU_DTYPE))
    kdim = CMP_STRIDE * V7X_LANES
    first = jnp.dot(jnp.concatenate(lo, axis=1), w1_ref[0:CMP_STRIDE].reshape(kdim, -1),
                    preferred_element_type=F32)
    second = jnp.dot(jnp.concatenate(hi, axis=1), w1_ref[CMP_STRIDE:CMP_BLOCK].reshape(kdim, -1),
                     preferred_element_type=F32)
    hid = first + pltpu.roll(second, ncp - 1, 0) + b1_ref[...]
    return _gelu_tanh(hid).astype(MXU_DTYPE)


def _cmp_k_kernel(src_ref, pos_ref, w1_ref, b1_ref, w2_ref, b2_ref, g_ref, o_ref):
    hid = _cmp_hidden(src_ref, pos_ref, w1_ref, b1_ref)
    ncp = hid.shape[0]
    idx = lax.broadcasted_iota(jnp.int32, (ncp, FEAT), 0)
    col = lax.broadcasted_iota(jnp.int32, (ncp, FEAT), 1)
    feat = jnp.where((col >= F_CMP) & (col < F_CMP + 3), (idx >> 6).astype(F32),
                     jnp.where((col >= F_CMP + 3) & (col < F_CMP + 6), (idx & 63).astype(F32), 0.0))
    for gl in range(CMP_GROUPS):
        r = jnp.dot(hid[:, gl * CMP_HIDDEN:(gl + 1) * CMP_HIDDEN], w2_ref[...],
                    preferred_element_type=F32) + b2_ref[...]
        ms = jnp.sum(r * r, axis=-1, keepdims=True) * (1.0 / HEAD_DIM)
        o_ref[0, gl] = (r * lax.rsqrt(ms + EPS) * g_ref[...] + feat).astype(o_ref.dtype)


def _cmp_v_kernel(src_ref, pos_ref, w1_ref, b1_ref, w2t_ref, b2_ref, o_ref):
    hid = _cmp_hidden(src_ref, pos_ref, w1_ref, b1_ref)
    for gl in range(CMP_GROUPS):
        r = _nt_dot(w2t_ref[...], hid[:, gl * CMP_HIDDEN:(gl + 1) * CMP_HIDDEN]) + b2_ref[...]
        o_ref[0, gl] = r.astype(o_ref.dtype)


def _compress(cmp_src, lane_block0, pos, w1, b1, w2, b2, gain_pad=None):
    b, t, _ = cmp_src.shape
    ncp = t // CMP_STRIDE
    hid_w = CMP_GROUPS * CMP_HIDDEN
    w1c = w1.astype(MXU_DTYPE)
    zero = jnp.zeros_like(w1c)
    w1_bd = jnp.concatenate([jnp.concatenate([w1c if h == g else zero for h in range(CMP_GROUPS)], axis=2)
                             for g in range(CMP_GROUPS)], axis=1)
    pos_t = jnp.tile(pos, (1, CMP_GROUPS))
    b1_t = jnp.tile(b1.reshape(1, CMP_HIDDEN), (1, CMP_GROUPS))
    full = lambda a: pl.BlockSpec(a.shape, lambda bi, p: (0,) * a.ndim)
    src_spec = pl.BlockSpec((1, t, V7X_LANES), lambda bi, p: (bi, 0, lane_block0 + p))
    grid = (b, N_KV // CMP_GROUPS)
    if gain_pad is not None:
        args = (pos_t, w1_bd, b1_t, w2, b2, gain_pad)
        return pl.pallas_call(
            _cmp_k_kernel,
            grid=grid,
            in_specs=[src_spec] + [full(a) for a in args],
            out_specs=pl.BlockSpec((1, CMP_GROUPS, ncp, FEAT), lambda bi, p: (bi, p, 0, 0)),
            out_shape=jax.ShapeDtypeStruct((b, N_KV, ncp, FEAT), MXU_DTYPE),
            compiler_params=_params("parallel", "parallel"),
            name="compress_k",
        )(cmp_src, *args)
    args = (pos_t, w1_bd, b1_t, w2, b2)
    return pl.pallas_call(
        _cmp_v_kernel,
        grid=grid,
        in_specs=[src_spec] + [full(a) for a in args],
        out_specs=pl.BlockSpec((1, CMP_GROUPS, V_ROWS, ncp), lambda bi, p: (bi, p, 0, 0)),
        out_shape=jax.ShapeDtypeStruct((b, N_KV, V_ROWS, ncp), MXU_DTYPE),
        compiler_params=_params("parallel", "parallel"),
        name="compress_v",
    )(cmp_src, *args)


def _split3(v):
    parts = []
    rest = np.asarray(v, np.float64)
    for _ in range(3):
        p = rest.astype(np.float32).astype(jnp.bfloat16).astype(np.float64)
        parts.append(p)
        rest = rest - p
    return parts


def _alibi_query_features():
    tab = np.zeros((N_KV, FEAT - F_POS, HPG * TQ), np.float64)
    for g in range(N_KV):
        for h in range(HPG):
            slope = 2.0 ** (-8.0 * (g * HPG + h + 1) / N_HEADS)
            parts = _split3(slope * LOG2E)
            lanes = slice(h * TQ, (h + 1) * TQ)
            for i, p in enumerate(parts):
                tab[g, i, lanes] = SEL_BLOCK * p
                tab[g, 3 + i, lanes] = p
                tab[g, 6 + i, lanes] = CMP_STRIDE * 64 * p
                tab[g, 9 + i, lanes] = CMP_STRIDE * p
    return jnp.asarray(tab, F32).astype(MXU_DTYPE)


def _block_map_t(n_cmp_pad, n_blk):
    cs = np.arange(n_cmp_pad) * CMP_STRIDE
    ce = cs + CMP_BLOCK - 1
    bs = np.arange(n_blk) * SEL_BLOCK
    be = bs + SEL_BLOCK - 1
    return jnp.asarray(((cs[None, :] <= be[:, None]) & (ce[None, :] >= bs[:, None])).astype(np.float32))


def _prob(s, m):
    return jnp.exp2(s - m).astype(MXU_DTYPE)


def _attn_tiles_kernel(*refs, tiles, win_blk0, n_blk, n_sel):
    (qt_ref, gate_ref, kc_ref, vc_ref, ks_ref, vs_ref, kd_ref, vd_ref, kwa_ref, kwb_ref, vwa_ref, vwb_ref,
     alibi_ref, map_ref) = refs[:14]
    o_ref, qb_ref, qs_ref, imp_ref = refs[-4:]
    lanes = HPG * TQ
    ncp = kc_ref.shape[2]
    blocks_per_tile = TQ // SEL_BLOCK
    win_tiles = WINDOW // TQ
    lane_tok = lax.broadcasted_iota(jnp.int32, (1, lanes), 1) & (TQ - 1)
    row_pos = lax.broadcasted_iota(jnp.int32, (TQ, lanes), 0)
    blk = lax.broadcasted_iota(jnp.int32, (n_blk, TQ), 0)

    def window_tile(qi, j):
        n = len(tiles)
        lt = max(qi - win_tiles, 0) + j - n * win_blk0
        kref, vref = (kwa_ref, vwa_ref) if lt < n else (kwb_ref, vwb_ref)
        sl = slice((lt % n) * TQ, (lt % n + 1) * TQ)
        return kref[0, 0, sl, :], vref[0, :, sl]

    def head(ti, qi):
        cols = slice(ti * TQ, (ti + 1) * TQ)
        t_lane = qi * TQ + lane_tok
        for h in range(HPG):
            qb_ref[ti, 0:HEAD_DIM, h * TQ:(h + 1) * TQ] = qt_ref[0, h * HEAD_DIM:(h + 1) * HEAD_DIM, cols]
        qb_ref[ti, F_SEL:F_POS, :] = jnp.zeros((F_POS - F_SEL, lanes), qb_ref.dtype)
        qb_ref[ti, F_POS:FEAT, :] = alibi_ref[0]
        qb = qb_ref[ti]
        sc = jnp.dot(kc_ref[0, 0], qb, preferred_element_type=F32)
        sw = {j: jnp.dot(window_tile(qi, j)[0], qb, preferred_element_type=F32) for j in sorted(window_order(qi))}
        sd = jnp.dot(kd_ref[0, 0, cols, :], qb, preferred_element_type=F32)
        last_cmp = (t_lane - (CMP_BLOCK - 1)) >> 4
        sc = jnp.where(lax.broadcasted_iota(jnp.int32, (ncp, lanes), 0) <= last_cmp, sc, NEG_MASK)
        ec = jnp.exp2(sc - jnp.max(sc, axis=0, keepdims=True))
        acc_c = jnp.dot(vc_ref[0, 0], ec.astype(MXU_DTYPE), preferred_element_type=F32)
        inv_c = jnp.where(last_cmp >= 0, 1.0 / jnp.maximum(acc_c[HEAD_DIM:HEAD_DIM + 1], 1e-30), 0.0)
        imp = None
        if (qi + 1) * blocks_per_tile > n_sel and qi > 0:
            psum = ec[:, 0:TQ] * inv_c[:, 0:TQ]
            for h in range(1, HPG):
                psum = psum + ec[:, h * TQ:(h + 1) * TQ] * inv_c[:, h * TQ:(h + 1) * TQ]
            imp = jnp.dot(map_ref[...], psum, preferred_element_type=F32)
        return dict(qb=qb, t_lane=t_lane, sw=sw, sd=sd, o_cmp=acc_c[0:HEAD_DIM] * inv_c, imp=imp,
                    causal=qi * TQ + row_pos <= t_lane)

    def rank_init(ti, qi, st):
        first_own_blk = qi * blocks_per_tile
        st.update(ranks=[], k_done=0, k_total=0)
        if st["imp"] is None:
            return
        cur = (qi * TQ + lax.broadcasted_iota(jnp.int32, (n_blk, TQ), 1)) >> 6
        forced = (blk == 0) | (blk == cur) | (blk == cur - 1)
        imp = jnp.where(blk <= cur, st["imp"] + jnp.where(forced, FORCE_BONUS, 0.0), NEG_MASK)
        imp_ref[ti] = imp
        n_rank_chunks = -(-first_own_blk // V7X_SUBLANES)
        st["chunks"] = [imp[c * V7X_SUBLANES:(c + 1) * V7X_SUBLANES] for c in range(n_rank_chunks)]
        st["ranks"] = [jnp.zeros((V7X_SUBLANES, TQ), jnp.int32) for _ in range(n_rank_chunks)]
        st["k_total"] = min(first_own_blk + blocks_per_tile, n_blk)

    def rank_rounds(ti, st, n):
        sub = lax.broadcasted_iota(jnp.int32, (V7X_SUBLANES, TQ), 0)
        stop = min(st["k_done"] + n, st["k_total"])
        for k in range(st["k_done"], stop):
            row = imp_ref[ti, k:k + 1, :]
            for c, mine in enumerate(st["chunks"]):
                lo = c * V7X_SUBLANES
                if lo > k:
                    one = jnp.where(row >= mine, 1, 0)
                elif lo + V7X_SUBLANES - 1 <= k:
                    one = jnp.where(row > mine, 1, 0)
                else:
                    one = jnp.where(sub + lo > k, jnp.where(row >= mine, 1, 0), jnp.where(row > mine, 1, 0))
                st["ranks"][c] = st["ranks"][c] + one
        st["k_done"] = stop

    def chunk_setup(ti, qi, st):
        rank_rounds(ti, st, st["k_total"])
        first_own_blk = qi * blocks_per_tile
        n_chunks = -(-qi * TQ // TK_SEL)
        st.update(n_chunks=n_chunks, m=jnp.full((1, lanes), NEG_MASK, F32), acc=jnp.zeros((V_ROWS, lanes), F32),
                  p_prev=None)
        if n_chunks == 0:
            return
        rank = jnp.zeros((n_blk, TQ), jnp.int32)
        if st["ranks"]:
            pad = [jnp.zeros((n_blk - len(st["ranks"]) * V7X_SUBLANES, TQ), jnp.int32)]
            rank = jnp.concatenate(st["ranks"] + (pad if pad[0].shape[0] else []), axis=0)
        not_sel = jnp.where((rank < n_sel) & (blk < first_own_blk), 0.0, 1.0).astype(qs_ref.dtype)
        qs_ref[ti] = st["qb"]
        for h in range(HPG):
            qs_ref[ti, F_SEL:F_SEL + n_blk, h * TQ:(h + 1) * TQ] = not_sel
        st["qs"] = qs_ref[ti]
        st["s_next"] = chunk_qk(st, 0)

    def chunk_qk(st, k):
        return jnp.dot(ks_ref[0, 0, k * TK_SEL:(k + 1) * TK_SEL, :], st["qs"], preferred_element_type=F32)

    def chunk_pv(st, k):
        return jnp.dot(vs_ref[0, :, k * TK_SEL:(k + 1) * TK_SEL], st["p_prev"], preferred_element_type=F32)

    def chunk_stage(st, k):
        s_cur = st["s_next"]
        if k + 1 < st["n_chunks"]:
            st["s_next"] = chunk_qk(st, k + 1)
        if st["p_prev"] is not None:
            st["acc"] = st["acc"] + chunk_pv(st, k - 1)
        m_new = jnp.maximum(st["m"], jnp.max(s_cur, axis=0, keepdims=True))
        st["p_prev"] = _prob(s_cur, m_new)
        st["acc"] = jnp.exp2(st["m"] - m_new) * st["acc"]
        st["m"] = m_new

    def chunk_finish(st):
        if st["n_chunks"] > 0:
            st["acc"] = st["acc"] + chunk_pv(st, st["n_chunks"] - 1)

    def diagonal(ti, qi, st):
        cols = slice(ti * TQ, (ti + 1) * TQ)
        sd = jnp.where(st["causal"], st["sd"], NEG_MASK)
        m_d = jnp.max(sd, axis=0, keepdims=True)
        acc_d = jnp.dot(vd_ref[0, :, cols], _prob(sd, m_d), preferred_element_type=F32)
        m_all = jnp.maximum(st["m"], m_d)
        acc_s = jnp.exp2(st["m"] - m_all) * st["acc"] + jnp.exp2(m_d - m_all) * acc_d
        st["o_sel"] = acc_s[0:HEAD_DIM] * (1.0 / acc_s[HEAD_DIM:HEAD_DIM + 1])

    def window_order(qi):
        wb = max(qi - win_tiles, 0)
        return [qi - wb] + [j for j in range(qi - wb - 1, -1, -1)]

    def window_stage(qi, st, j, last):
        wb = max(qi - win_tiles, 0)
        s = st["sw"][j]
        if wb + j == qi:
            s = jnp.where(st["causal"], s, NEG_MASK)
        elif wb + j != qi - 1:
            dj = (st["t_lane"] - (wb + j) * TQ) - row_pos
            s = jnp.where(lax.bitcast_convert_type(dj, jnp.uint32) < WINDOW, s, NEG_MASK)
        m_j = jnp.max(s, axis=0, keepdims=True)
        v_j = window_tile(qi, j)[1]
        if "m_w" not in st:
            m_new = m_j
            acc_w = jnp.dot(v_j, _prob(s, m_new), preferred_element_type=F32)
        else:
            m_new = jnp.maximum(st["m_w"], m_j)
            acc_w = jnp.exp2(st["m_w"] - m_new) * st["acc_w"] + jnp.dot(v_j, _prob(s, m_new),
                                                                          preferred_element_type=F32)
        st["m_w"], st["acc_w"] = m_new, acc_w
        if last:
            st["o_win"] = acc_w[0:HEAD_DIM] * (1.0 / acc_w[HEAD_DIM:HEAD_DIM + 1])

    def output(ti, st):
        cols = slice(ti * TQ, (ti + 1) * TQ)
        gates = gate_ref[0, :, cols]
        def gate_row(j):
            return jnp.concatenate([gates[j * HPG + h:j * HPG + h + 1, :] for h in range(HPG)], axis=1)
        o_t = gate_row(0) * st["o_cmp"] + gate_row(1) * st["o_sel"] + gate_row(2) * st["o_win"]
        for hp in range(HPG // 2):
            pair = jnp.concatenate([o_t[:, (2 * hp) * TQ:(2 * hp + 1) * TQ],
                                    o_t[:, (2 * hp + 1) * TQ:(2 * hp + 2) * TQ]], axis=0)
            o_ref[0, cols, hp * 2 * HEAD_DIM:(hp + 1) * 2 * HEAD_DIM] = pair.T.astype(o_ref.dtype)

    def tail_phases(ti, qi, st):
        return [lambda: diagonal(ti, qi, st), lambda: output(ti, st)]

    cur = head(0, tiles[0])
    rank_init(0, tiles[0], cur)
    chunk_setup(0, tiles[0], cur)
    pending = []
    for ti, qi in enumerate(tiles):
        nxt = None
        if ti + 1 < len(tiles):
            nxt = head(ti + 1, tiles[ti + 1])
            rank_init(ti + 1, tiles[ti + 1], nxt)
            rounds_per_stage = -(-nxt["k_total"] // max(cur["n_chunks"], 1))
        order = window_order(qi)
        for j in order:
            pending.append(lambda qi=qi, st=cur, j=j, last=(j == order[-1]): window_stage(qi, st, j, last))
        for k in range(cur["n_chunks"]):
            chunk_stage(cur, k)
            if nxt is not None:
                rank_rounds(ti + 1, nxt, rounds_per_stage)
            if pending:
                pending.pop(0)()
        for phase in pending:
            phase()
        chunk_finish(cur)
        pending = tail_phases(ti, qi, cur)
        if nxt is not None:
            chunk_setup(ti + 1, tiles[ti + 1], nxt)
            cur = nxt
    for phase in pending:
        phase()


def _attention_static(q_t, gates_t, k_cmp, v_cmp_t, k_sel, k_win, v_t):
    b, _, t = q_t.shape
    n_blk = t // SEL_BLOCK
    tiles_per_call = TILES_PER_CALL
    tb = tiles_per_call * TQ
    assert tb >= WINDOW and t % max(tb, TK_SEL) == 0 and t >= 2 * tb and F_SEL + n_blk <= F_POS, \
        "unsupported sequence length"
    n_sel = min(N_SEL, n_blk)
    lanes = HPG * TQ
    rows = HPG * HEAD_DIM
    alibi = _alibi_query_features()
    out_shape = jax.ShapeDtypeStruct((b, t, N_HEADS * HEAD_DIM), MXU_DTYPE)

    attn = None
    for m in range(t // tb):
        tiles = tuple(range(m * tiles_per_call, (m + 1) * tiles_per_call))
        ncp = min(t // CMP_STRIDE, -(-((m + 1) * tb // CMP_STRIDE) // V7X_LANES) * V7X_LANES)
        blk_map_t = _block_map_t(ncp, n_blk)
        kc = -(-tiles[-1] * TQ // TK_SEL) * TK_SEL
        wb0 = max(m - 1, 0)
        in_specs = [
            pl.BlockSpec((1, rows, tb), lambda bi, g, m=m: (bi, g, m)),
            pl.BlockSpec((1, 16, tb), lambda bi, g, m=m: (bi, g, m)),
            pl.BlockSpec((1, 1, ncp, FEAT), lambda bi, g: (bi, g, 0, 0)),
            pl.BlockSpec((1, 1, V_ROWS, ncp), lambda bi, g: (bi, g, 0, 0)),
            pl.BlockSpec((1, 1, kc, FEAT), lambda bi, g: (bi, g, 0, 0)),
            pl.BlockSpec((1, V_ROWS, kc), lambda bi, g: (bi, g, 0)),
            pl.BlockSpec((1, 1, tb, FEAT), lambda bi, g, m=m: (bi, g, m, 0)),
            pl.BlockSpec((1, V_ROWS, tb), lambda bi, g, m=m: (bi, g, m)),
            pl.BlockSpec((1, 1, tb, FEAT), lambda bi, g, w=wb0: (bi, g, w, 0)),
            pl.BlockSpec((1, 1, tb, FEAT), lambda bi, g, w=wb0: (bi, g, w + 1, 0)),
            pl.BlockSpec((1, V_ROWS, tb), lambda bi, g, w=wb0: (bi, N_KV + g, w)),
            pl.BlockSpec((1, V_ROWS, tb), lambda bi, g, w=wb0: (bi, N_KV + g, w + 1)),
            pl.BlockSpec((1, FEAT - F_POS, lanes), lambda bi, g: (g, 0, 0)),
            pl.BlockSpec((n_blk, ncp), lambda bi, g: (0, 0)),
        ]
        args = [q_t, gates_t, k_cmp, v_cmp_t, k_sel, v_t, k_sel, v_t, k_win, k_win, v_t, v_t, alibi, blk_map_t]
        aliases = {}
        if attn is not None:
            in_specs.append(pl.BlockSpec(memory_space=pl.ANY))
            args.append(attn)
            aliases = {len(args) - 1: 0}
        kernel = functools.partial(_attn_tiles_kernel, tiles=tiles, win_blk0=wb0, n_blk=n_blk, n_sel=n_sel)
        attn = pl.pallas_call(
            kernel,
            grid=(b, N_KV),
            in_specs=in_specs,
            out_specs=pl.BlockSpec((1, tb, rows), lambda bi, g, m=m: (bi, m, g)),
            out_shape=out_shape,
            scratch_shapes=[pltpu.VMEM((tiles_per_call, FEAT, lanes), MXU_DTYPE),
                            pltpu.VMEM((tiles_per_call, FEAT, lanes), MXU_DTYPE),
                            pltpu.VMEM((tiles_per_call, n_blk, TQ), F32)],
            input_output_aliases=aliases,
            compiler_params=_params("parallel", "parallel"),
            name=f"nsa_attention_{m}",
        )(*args)
    return attn


def _lru_gates(hin, wl_ref, cw_ref, cb_ref, wa_ref, wi_ref, tail_ref, h_ref):
    tt = hin.shape[0]
    w = cb_ref.shape[1]

    @pl.when(pl.program_id(1) == 0)
    def _():
        tail_ref[...] = jnp.zeros_like(tail_ref)
        h_ref[...] = jnp.zeros_like(h_ref)

    def gate_matmul(xb, w_ref):
        n = w_ref.shape[1]
        return jnp.concatenate([jnp.dot(xb[:, j * n:(j + 1) * n], w_ref[j], preferred_element_type=F32)
                                for j in range(w_ref.shape[0])], axis=1)

    x = jnp.dot(hin, wl_ref[:, 0:w], preferred_element_type=F32)
    gate = jnp.dot(hin, wl_ref[:, w:2 * w], preferred_element_type=F32)
    ng = tt // V7X_SUBLANES
    sub = lax.broadcasted_iota(jnp.int32, (ng, V7X_SUBLANES, w), 1)
    x3 = x.reshape(ng, V7X_SUBLANES, w)
    xprev3 = jnp.concatenate([tail_ref[...][None], x3], axis=0)
    tail_ref[...] = x[tt - V7X_SUBLANES:tt]
    xc = x * cw_ref[CONV_W - 1:CONV_W, :] + cb_ref[...]
    for s in range(1, CONV_W):
        rot = pltpu.roll(xprev3, s, 1)
        xs = jnp.where(sub >= s, rot[1:], rot[:-1])
        xc = xc + xs.reshape(tt, w) * cw_ref[CONV_W - 1 - s:CONV_W - s, :]

    xb = xc.astype(MXU_DTYPE)
    return xc, gate_matmul(xb, wa_ref), gate_matmul(xb, wi_ref), gate


def _lru_scan(xc, r_pre, i_pre, gate, ba_ref, bi_ref, lam_ref, o_ref, h_ref):
    tt, w = xc.shape
    ng = tt // V7X_SUBLANES
    sub = lax.broadcasted_iota(jnp.int32, (ng, V7X_SUBLANES, w), 1)
    r = _sigmoid(r_pre + ba_ref[...])
    i = _sigmoid(i_pre + bi_ref[...])
    z = -lam_ref[...]
    softplus = jnp.maximum(z, 0.0) + jnp.log1p(jnp.exp(-jnp.abs(z)))
    log_a = -LRU_C * r * softplus
    a = jnp.exp(log_a)
    bb = jnp.sqrt(_one_minus_sq(a, log_a)) * (i * xc)

    a3 = a.reshape(ng, V7X_SUBLANES, w)
    b3 = bb.reshape(ng, V7X_SUBLANES, w)
    for d in (1, 2, 4):
        ok = sub >= d
        a_sh = pltpu.roll(a3, d, 1)
        b_sh = pltpu.roll(b3, d, 1)
        b3 = jnp.where(ok, a3 * b_sh + b3, b3)
        a3 = jnp.where(ok, a3 * a_sh, a3)
    carry = h_ref[0:1, :]
    groups = []
    for g in range(ng):
        hg = b3[g] + a3[g] * carry
        groups.append(hg)
        carry = hg[V7X_SUBLANES - 1:V7X_SUBLANES, :]
    hcur = jnp.concatenate(groups, axis=0)
    h_ref[...] = jnp.broadcast_to(carry, h_ref.shape)
    o_ref[0] = (hcur * _gelu_tanh(gate)).astype(o_ref.dtype)


def _merge_kernel(attn_ref, lru_ref, h_ref, x_ref, wm_ref, wa_ref, wl_ref, wo_ref, g2_ref,
                  x1_ref, h2_ref):
    d = x_ref.shape[1]
    h = h_ref[...]
    mg0 = _sigmoid(jnp.dot(h, wm_ref[:, 0:d], preferred_element_type=F32))
    mg1 = _sigmoid(jnp.dot(h, wm_ref[:, d:2 * d], preferred_element_type=F32))
    ya = jnp.dot(attn_ref[...], wa_ref[...], preferred_element_type=F32)
    yl = jnp.dot(lru_ref[...], wl_ref[...], preferred_element_type=F32)
    merged = mg0 * ya + mg1 * yl
    x1 = x_ref[...] + jnp.dot(merged.astype(MXU_DTYPE), wo_ref[...], preferred_element_type=F32)
    x1_ref[...] = x1
    ms = jnp.mean(x1 * x1, axis=-1, keepdims=True)
    h2_ref[...] = (x1 * lax.rsqrt(ms + EPS) * g2_ref[...]).astype(h2_ref.dtype)


def _merge(attn, lru, h2d, x2d, w_mg, wa, wl, wo, g2, tm=512):
    m, d = x2d.shape
    row = pl.BlockSpec((tm, d), lambda i: (i, 0))
    full = lambda a: pl.BlockSpec(a.shape, lambda i: (0, 0))
    g2 = g2.reshape(1, d)
    return pl.pallas_call(
        _merge_kernel,
        grid=(m // tm,),
        in_specs=[row, row, row, row, full(w_mg), full(wa), full(wl), full(wo), full(g2)],
        out_specs=[row, row],
        out_shape=[jax.ShapeDtypeStruct((m, d), F32), jax.ShapeDtypeStruct((m, d), MXU_DTYPE)],
        compiler_params=_params("parallel"),
        name="merge_out",
    )(attn, lru, h2d, x2d, w_mg, wa, wl, wo, g2)


FFN_CHUNK = V7X_MXU_DIM


def _ffn_kernel(h_ref, x1_ref, wg_ref, wu_ref, wd_ref, o_ref, act_ref):
    h = h_ref[...]
    f = wg_ref.shape[1]
    for c0 in range(0, f, FFN_CHUNK):
        cols = slice(c0, min(c0 + FFN_CHUNK, f))
        g = jnp.dot(h, wg_ref[:, cols], preferred_element_type=F32)
        u = jnp.dot(h, wu_ref[:, cols], preferred_element_type=F32)
        act_ref[:, cols] = (g * _sigmoid(g) * u).astype(act_ref.dtype)
    o_ref[...] = x1_ref[...] + jnp.dot(act_ref[...], wd_ref[...], preferred_element_type=F32)


def _ffn(h2, x1, wg, wu, wd, tm=512):
    m, d = x1.shape
    f = wg.shape[1]
    resident = lambda a: pl.BlockSpec(a.shape, lambda i: (0, 0), pipeline_mode=pl.Buffered(1))
    row = pl.BlockSpec((tm, d), lambda i: (i, 0))
    return pl.pallas_call(
        _ffn_kernel,
        grid=(m // tm,),
        in_specs=[row, row, resident(wg), resident(wu), resident(wd)],
        out_specs=row,
        out_shape=jax.ShapeDtypeStruct((m, d), F32),
        scratch_shapes=[pltpu.VMEM((tm, f), MXU_DTYPE)],
        compiler_params=_params("parallel"),
        name="swiglu_ffn",
    )(h2, x1, wg, wu, wd)


def _pad_last(a, n):
    return jnp.pad(a, [(0, 0)] * (a.ndim - 1) + [(0, n - a.shape[-1])])


def _layer(x, norm1_g, w_in, q_norm_g, k_norm_g, cmp_pos_k, cmp_w1_k, cmp_b1_k, cmp_w2_k, cmp_b2_k,
           cmp_pos_v, cmp_w1_v, cmp_b1_v, cmp_w2_v, cmp_b2_v, conv_w, conv_b, lru_wa, lru_ba,
           lru_wi, lru_bi, lru_lambda, w_o_attn, w_o_lru, w_out, norm2_g, w_gate, w_up, w_down):
    b, t, d = x.shape
    m = b * t
    attn_dim = N_HEADS * HEAD_DIM
    kv_dim = N_KV * HEAD_DIM
    lru_w = lru_lambda.shape[0]
    assert t % 512 == 0
    o1 = attn_dim
    o2 = o1 + 6 * kv_dim
    o3 = o2 + 3 * N_HEADS
    o4 = o3 + lru_w
    o5 = o4 + lru_w
    cast = lambda a: a.astype(MXU_DTYPE)

    w_in = cast(w_in)
    wq_t = cast(w_in[:, :o1].T)
    w_kv = w_in[:, o1:o2].reshape(d, 6, N_KV, HEAD_DIM)
    w_cmp_src = cast(w_kv[:, 0:2].reshape(d, 2 * kv_dim))
    w_k2 = cast(jnp.stack([w_kv[:, 2], w_kv[:, 4]], axis=1).reshape(d, 2 * kv_dim))
    k_gain2 = jnp.stack([jnp.tile(k_norm_g[1], N_KV), jnp.tile(k_norm_g[2], N_KV)])
    lane_grp = np.arange(kv_dim) // HEAD_DIM
    grp_avg = cast(jnp.asarray((lane_grp[:, None] == lane_grp[None, :]) / HEAD_DIM, F32))
    place_np = np.zeros((kv_dim, N_KV * FEAT), np.float32)
    place_np[np.arange(kv_dim), lane_grp * FEAT + np.arange(kv_dim) % HEAD_DIM] = 1.0
    place = cast(jnp.asarray(place_np))
    w_v = jnp.stack([w_kv[:, 3], w_kv[:, 5]], axis=1)
    w_v_t = cast(_pad_last(w_v, V_ROWS).reshape(d, 2 * N_KV * V_ROWS).T)
    v_ones_col = jnp.tile(jnp.arange(V_ROWS) == HEAD_DIM, 2 * N_KV).astype(F32).reshape(-1, 1)
    w_g = w_in[:, o2:o3].reshape(d, N_KV, HPG, 3).transpose(0, 1, 3, 2).reshape(d, N_KV, 3 * HPG)
    w_g_t = cast(_pad_last(w_g, 16).reshape(d, N_KV * 16).T)
    w_lru = cast(w_in[:, o3:o5])
    w_mg = cast(w_in[:, o5:])
    q_gain_col = jnp.tile(q_norm_g * (HEAD_DIM ** -0.5 * LOG2E), N_HEADS).reshape(attn_dim, 1)
    gain_pad = lambda g: _pad_last(g.reshape(1, HEAD_DIM), FEAT)

    per_slab = V7X_MXU_DIM // (lru_w // LRU_BLOCKS)
    eye = jnp.eye(per_slab, dtype=F32)
    slabs = lambda wgt: cast(jnp.einsum('snkj,nm->snkmj', wgt.reshape(LRU_BLOCKS // per_slab, per_slab,
                                                                       *wgt.shape[1:]), eye)
                             .reshape(LRU_BLOCKS // per_slab, V7X_MXU_DIM, V7X_MXU_DIM))
    vec = lambda v: v.reshape(1, lru_w)

    h3d, q_t, v_t, gates_t, k_sel, k_win, cmp_src, lru = _in_proj(
        x,
        (norm1_g.reshape(1, d), wq_t, q_gain_col, w_v_t, v_ones_col, w_g_t, w_k2, k_gain2, grp_avg, place,
         w_cmp_src),
        (w_lru, conv_w.reshape(CONV_W, lru_w), vec(conv_b), slabs(lru_wa), vec(lru_ba), slabs(lru_wi),
         vec(lru_bi), vec(lru_lambda)))
    h2d = h3d.reshape(m, d)

    k_cmp = _compress(cmp_src, 0, cmp_pos_k, cmp_w1_k, cmp_b1_k, cast(_pad_last(cmp_w2_k, FEAT)),
                      _pad_last(cmp_b2_k.reshape(1, HEAD_DIM), FEAT), gain_pad(k_norm_g[0]))
    v_cmp_t = _compress(cmp_src, N_KV // CMP_GROUPS, cmp_pos_v, cmp_w1_v, cmp_b1_v,
                        cast(_pad_last(cmp_w2_v, V_ROWS).T),
                        jnp.concatenate([cmp_b2_v, v_ones_col[HEAD_DIM:V_ROWS, 0]]).reshape(V_ROWS, 1))

    attn = _attention_static(q_t, gates_t, k_cmp, v_cmp_t, k_sel, k_win, v_t)

    x1, h2 = _merge(attn.reshape(m, attn_dim), lru.reshape(m, lru_w), h2d, x.reshape(m, d), w_mg,
                    cast(w_o_attn), cast(w_o_lru), cast(w_out), norm2_g)
    out = _ffn(h2, x1, cast(w_gate), cast(w_up), cast(w_down))
    return out.reshape(b, t, d)


def kernel(x, norm1_g, w_in, q_norm_g, k_norm_g, cmp_pos_k, cmp_w1_k, cmp_b1_k, cmp_w2_k, cmp_b2_k,
           cmp_pos_v, cmp_w1_v, cmp_b1_v, cmp_w2_v, cmp_b2_v, conv_w, conv_b, lru_wa, lru_ba,
           lru_wi, lru_bi, lru_lambda, w_o_attn, w_o_lru, w_out, norm2_g, w_gate, w_up, w_down):
    for l in range(norm1_g.shape[0]):
        x = _layer(x, norm1_g[l], w_in[l], q_norm_g[l], k_norm_g[l], cmp_pos_k[l], cmp_w1_k[l],
                   cmp_b1_k[l], cmp_w2_k[l], cmp_b2_k[l], cmp_pos_v[l], cmp_w1_v[l], cmp_b1_v[l],
                   cmp_w2_v[l], cmp_b2_v[l], conv_w[l], conv_b[l], lru_wa[l], lru_ba[l], lru_wi[l],
                   lru_bi[l], lru_lambda[l], w_o_attn[l], w_o_lru[l], w_out[l], norm2_g[l],
                   w_gate[l], w_up[l], w_down[l])
    return x
```
